```python
import jax, jax.numpy as jnp
from jax import lax
import numpy as np

D_MODEL = 1024
BATCH = 8
SEQ = 2048
DEPTH = 2
DEC_BATCH = 32
DEC_SEQ = 16
PAST_LEN = 2048

CHUNK = 64
N_AB = (DEPTH + 1) // 2
N_C = DEPTH // 2
D_FF = 2816
D_MIX = D_MODEL
H_A = 4
DH_A = D_MIX // 2 // H_A
A_WIDTH = H_A * DH_A
H_B = 8
DH_B = D_MIX // 2 // H_B
B_WIDTH = H_B * DH_B
BAND_CHUNKS = 8
BAND_PAST = BAND_CHUNKS * CHUNK
MAX_REL = 128
PROJ_AB = 4 * A_WIDTH + 2 * H_A + 3 * B_WIDTH
R_WIDTH = D_MODEL
N_BLOCKS_C = 8
BW_C = R_WIDTH // N_BLOCKS_C
CONV_W = 4
LRU_C = 8.0
EPS = 1e-6
N_ADA = 9

kernel_name = 'hybrid_mlstm_band_rglru_stream_step'

F32 = jnp.float32


def rmsnorm(x, g):
    xf = x.astype(F32)
    y = xf * lax.rsqrt(jnp.mean(xf * xf, axis=-1, keepdims=True) + EPS)
    return (y * g.astype(F32)).astype(x.dtype)


def swiglu(h, w_in, w_out):
    gate, up = jnp.split(h @ w_in, 2, axis=-1)
    return (jax.nn.silu(gate) * up) @ w_out


def rel_bias(table, rel):
    return table[:, jnp.clip(rel, -MAX_REL, MAX_REL) + MAX_REL].astype(F32)


def mlstm_block(q, k, v, ig, lf, C, n, m):
    L = q.shape[1]
    bt = jnp.swapaxes(jnp.cumsum(lf, axis=1), 1, 2)
    igt = jnp.swapaxes(ig, 1, 2)
    causal = jnp.tril(jnp.ones((L, L), bool))
    logd = jnp.where(causal, bt[..., :, None] - bt[..., None, :] + igt[..., None, :], -jnp.inf)
    inter = bt + m[..., None]
    m_t = jnp.maximum(inter, jnp.max(logd, axis=-1))
    dw = jnp.exp(logd - m_t[..., None])
    iw = jnp.exp(inter - m_t)
    s = jnp.einsum('blhd,bshd->bhls', q, k) * dw
    num = iw[..., None] * jnp.einsum('blhd,bhde->bhle', q, C) + jnp.einsum('bhls,bshe->bhle', s, v)
    den = iw * jnp.einsum('blhd,bhd->bhl', q, n) + jnp.sum(s, axis=-1)
    h = num / jnp.maximum(jnp.abs(den), jnp.exp(-m_t))[..., None]
    b_last = bt[..., -1]
    g = b_last[..., None] - bt + igt
    m_new = jnp.maximum(b_last + m, jnp.max(g, axis=-1))
    w_prev = jnp.exp(b_last + m - m_new)
    w_s = jnp.exp(g - m_new[..., None])
    C_new = w_prev[..., None, None] * C + jnp.einsum('bhs,bshd,bshe->bhde', w_s, k, v)
    n_new = w_prev[..., None] * n + jnp.einsum('bhs,bshd->bhd', w_s, k)
    return jnp.swapaxes(h, 1, 2), C_new, n_new, m_new


def mlstm_chunked(q, k, v, ig, lf, C, n, m):
    Bn, S = q.shape[:2]
    nc = S // CHUNK

    def to_chunks(t):
        return jnp.moveaxis(t.reshape((Bn, nc, CHUNK) + t.shape[2:]), 1, 0)

    def step(carry, inp):
        h, C2, n2, m2 = mlstm_block(*inp, *carry)
        return (C2, n2, m2), h

    (C, n, m), hs = lax.scan(step, (C, n, m), (to_chunks(q), to_chunks(k), to_chunks(v), to_chunks(ig), to_chunks(lf)))
    return jnp.moveaxis(hs, 0, 1).reshape(q.shape[:3] + (v.shape[-1],)), C, n, m


def band_prompt(q, k, v, table):
    Bn, S, H, d = q.shape
    nc = S // CHUNK
    nb = BAND_CHUNKS + 1
    qc = q.reshape(Bn, nc, CHUNK, H, d)
    pad = ((0, 0), (BAND_CHUNKS, 0), (0, 0), (0, 0), (0, 0))
    kp = jnp.pad(k.reshape(Bn, nc, CHUNK, H, d), pad)
    vp = jnp.pad(v.reshape(Bn, nc, CHUNK, H, d), pad)
    kband = jnp.stack([kp[:, o:o + nc] for o in range(nb)], axis=2).reshape(Bn, nc, nb * CHUNK, H, d)
    vband = jnp.stack([vp[:, o:o + nc] for o in range(nb)], axis=2).reshape(Bn, nc, nb * CHUNK, H, d)
    key_off = jnp.arange(nb * CHUNK) - BAND_PAST
    bias = rel_bias(table, key_off[None, :] - jnp.arange(CHUNK)[:, None])
    valid = (jnp.arange(nc)[:, None] - BAND_CHUNKS + (jnp.arange(nb * CHUNK) // CHUNK)[None, :]) >= 0
    s = jnp.einsum('bnqhd,bnkhd->bnhqk', qc, kband).astype(F32) * (DH_B ** -0.5) + bias[None, None]
    s = jnp.where(valid[None, :, None, None, :], s, -jnp.inf)
    pr = jax.nn.softmax(s, axis=-1).astype(v.dtype)
    return jnp.einsum('bnhqk,bnkhd->bnqhd', pr, vband).reshape(Bn, S, H, d)


def band_sample(q, k, v, ck, cv, table):
    W, T = ck.shape[1], q.shape[1]
    kk = jnp.concatenate([ck.astype(k.dtype), k], axis=1)
    vv = jnp.concatenate([cv.astype(v.dtype), v], axis=1)
    bias = rel_bias(table, (jnp.arange(W + T) - W)[None, :] - jnp.arange(T)[:, None])
    s = jnp.einsum('bqhd,bkhd->bhqk', q, kk).astype(F32) * (DH_B ** -0.5) + bias[None]
    pr = jax.nn.softmax(s, axis=-1).astype(v.dtype)
    return jnp.einsum('bhqk,bkhd->bqhd', pr, vv)


def ab_mixer(h, p, i, a_state, b_cache):
    Bn, L, _ = h.shape
    u = h @ p['ab_w_in'][i]
    cuts = [A_WIDTH, 2 * A_WIDTH, 3 * A_WIDTH, 4 * A_WIDTH, 4 * A_WIDTH + 2 * H_A,
            4 * A_WIDTH + 2 * H_A + B_WIDTH, 4 * A_WIDTH + 2 * H_A + 2 * B_WIDTH]
    qa, ka, va, oa, ga, qb, kb, vb = jnp.split(u, cuts, axis=-1)
    qa = qa.reshape(Bn, L, H_A, DH_A).astype(F32)
    ka = ka.reshape(Bn, L, H_A, DH_A).astype(F32) * (DH_A ** -0.5)
    va = va.reshape(Bn, L, H_A, DH_A).astype(F32)
    ga = ga.astype(F32) + p['ab_gate_bias'][i].astype(F32)
    ig, lf = ga[..., :H_A], jax.nn.log_sigmoid(ga[..., H_A:])
    C0, n0, m0 = a_state
    if b_cache is None:
        ha, C1, n1, m1 = mlstm_chunked(qa, ka, va, ig, lf, C0, n0, m0)
    else:
        ha, C1, n1, m1 = mlstm_block(qa, ka, va, ig, lf, C0, n0, m0)
    ha = rmsnorm(ha, p['a_out_norm'][i]) * jax.nn.sigmoid(oa.reshape(Bn, L, H_A, DH_A).astype(F32))
    qb = rmsnorm(qb.reshape(Bn, L, H_B, DH_B), p['b_q_norm'][i])
    kb = rmsnorm(kb.reshape(Bn, L, H_B, DH_B), p['b_k_norm'][i])
    vb = vb.reshape(Bn, L, H_B, DH_B)
    table = p['b_rel_bias'][i]
    if b_cache is None:
        hb = band_prompt(qb, kb, vb, table)
        keep = min(BAND_PAST, L)
        new_k, new_v = kb[:, L - keep:], vb[:, L - keep:]
    else:
        hb = band_sample(qb, kb, vb, b_cache[0], b_cache[1], table)
        new_k, new_v = kb, vb
    mixed = jnp.concatenate([ha.reshape(Bn, L, A_WIDTH).astype(h.dtype), hb.reshape(Bn, L, B_WIDTH)], axis=-1)
    return mixed @ p['ab_w_out'][i], (C1, n1, m1, new_k, new_v)


def _lin_combine(left, right):
    a1, b1 = left
    a2, b2 = right
    return a1 * a2, a2 * b1 + b2


def rglru_mixer(h, p, i, conv_buf, h0):
    Bn, L, _ = h.shape
    gb, xb = jnp.split(h @ p['c_w_in'][i], 2, axis=-1)
    xp = jnp.concatenate([conv_buf.astype(xb.dtype), xb], axis=1)
    w = p['c_conv_w'][i]
    xc = p['c_conv_b'][i] + sum(xp[:, j:j + L] * w[j] for j in range(CONV_W))
    new_buf = xp[:, L:]
    gates = jnp.einsum('blnw,nwv->blnv', xc.reshape(Bn, L, N_BLOCKS_C, BW_C), p['c_gate_w'][i]).astype(F32)
    gates = gates.reshape(Bn, L, N_BLOCKS_C, 2, BW_C)
    gbias = p['c_gate_b'][i].astype(F32)
    r = jax.nn.sigmoid(gates[..., 0, :].reshape(Bn, L, R_WIDTH) + gbias[0])
    ii = jax.nn.sigmoid(gates[..., 1, :].reshape(Bn, L, R_WIDTH) + gbias[1])
    log_a = -LRU_C * r * jax.nn.softplus(-p['c_lambda'][i].astype(F32))
    a = jnp.exp(log_a)
    upd = jnp.sqrt(-jnp.expm1(2.0 * log_a)) * (ii * xc.astype(F32))
    a_cum, u_cum = lax.associative_scan(_lin_combine, (a, upd), axis=1)
    hs = a_cum * h0.astype(F32)[:, None] + u_cum
    y = (jax.nn.gelu(gb.astype(F32)) * hs).astype(h.dtype) @ p['c_w_out'][i]
    return y, (new_buf, hs[:, -1])


def run_trunk(x, c, p, st):
    prompt = st is None
    Bn = x.shape[0]
    out = {name: [] for name in ('a_C', 'a_n', 'a_m', 'b_k', 'b_v', 'c_conv', 'c_h')}
    for l in range(DEPTH):
        ada = (c @ p['ada_w'][l] + p['ada_b'][l]).reshape(Bn, N_ADA, 1, D_MODEL)

        def mod(z, g, j):
            return rmsnorm(z, g) * (1.0 + ada[:, 3 * j + 1]) + ada[:, 3 * j]

        x = x + 0.5 * ada[:, 2] * swiglu(mod(x, p['ffn1_norm'][l], 0), p['ffn1_w_in'][l], p['ffn1_w_out'][l])
        hm = mod(x, p['mix_norm'][l], 1)
        i = l // 2
        if l % 2 == 0:
            if prompt:
                a_state = (jnp.zeros((Bn, H_A, DH_A, DH_A), F32), jnp.zeros((Bn, H_A, DH_A), F32),
                           jnp.zeros((Bn, H_A), F32))
                b_cache = None
            else:
                a_state = (st['a_C'][i].astype(F32), st['a_n'][i].astype(F32), st['a_m'][i].astype(F32))
                b_cache = (st['b_k'][i], st['b_v'][i])
            y, (C1, n1, m1, nk, nv) = ab_mixer(hm, p, i, a_state, b_cache)
            out['a_C'].append(C1)
            out['a_n'].append(n1)
            out['a_m'].append(m1)
            out['b_k'].append(nk)
            out['b_v'].append(nv)
        else:
            if prompt:
                conv_buf = jnp.zeros((Bn, CONV_W - 1, R_WIDTH), x.dtype)
                h0 = jnp.zeros((Bn, R_WIDTH), F32)
            else:
                conv_buf, h0 = st['c_conv'][i], st['c_h'][i]
            y, (nb_, nh) = rglru_mixer(hm, p, i, conv_buf, h0)
            out['c_conv'].append(nb_)
            out['c_h'].append(nh)
        x = x + ada[:, 5] * y
        x = x + 0.5 * ada[:, 8] * swiglu(mod(x, p['ffn2_norm'][l], 2), p['ffn2_w_in'][l], p['ffn2_w_out'][l])
    return x, {name: jnp.stack(v_list, axis=0) for name, v_list in out.items()}


def setup_inputs(seed: int = 0) -> dict:
    key = jax.random.key(seed)
    ks = iter(jax.random.split(key, 48))
    nrm = lambda shape, s=1.0: jax.random.normal(next(ks), shape, F32) * s
    W_B = min(BAND_PAST, PAST_LEN)
    u_lam = jax.random.uniform(next(ks), (N_C, R_WIDTH), F32, 0.9, 0.999) ** (1.0 / LRU_C)
    return {
        'x_prompt': nrm((BATCH, SEQ, D_MODEL)),
        'x_sample': nrm((DEC_BATCH, DEC_SEQ, D_MODEL)),
        'state_a_C': nrm((N_AB, DEC_BATCH, H_A, DH_A, DH_A), 0.1),
        'state_a_n': nrm((N_AB, DEC_BATCH, H_A, DH_A), 0.1),
        'state_a_m': nrm((N_AB, DEC_BATCH, H_A)),
        'cache_b_k': nrm((N_AB, DEC_BATCH, W_B, H_B, DH_B)),
        'cache_b_v': nrm((N_AB, DEC_BATCH, W_B, H_B, DH_B)),
        'state_c_conv': nrm((N_C, DEC_BATCH, CONV_W - 1, R_WIDTH)),
        'state_c_h': nrm((N_C, DEC_BATCH, R_WIDTH), 0.5),
        'c_prompt': nrm((BATCH, D_MODEL)),
        'c_sample': nrm((DEC_BATCH, D_MODEL)),
        'ffn1_norm': 1.0 + nrm((DEPTH, D_MODEL), 0.01),
        'ffn1_w_in': nrm((DEPTH, D_MODEL, 2 * D_FF), D_MODEL ** -0.5),
        'ffn1_w_out': nrm((DEPTH, D_FF, D_MODEL), D_FF ** -0.5),
        'mix_norm': 1.0 + nrm((DEPTH, D_MODEL), 0.01),
        'ffn2_norm': 1.0 + nrm((DEPTH, D_MODEL), 0.01),
        'ffn2_w_in': nrm((DEPTH, D_MODEL, 2 * D_FF), D_MODEL ** -0.5),
        'ffn2_w_out': nrm((DEPTH, D_FF, D_MODEL), D_FF ** -0.5),
        'ada_w': nrm((DEPTH, D_MODEL, N_ADA * D_MODEL), 0.5 * D_MODEL ** -0.5),
        'ada_b': nrm((DEPTH, N_ADA * D_MODEL), 0.01),
        'ab_w_in': nrm((N_AB, D_MODEL, PROJ_AB), D_MODEL ** -0.5),
        'ab_gate_bias': jnp.concatenate([nrm((N_AB, H_A), 0.1), 3.0 + nrm((N_AB, H_A), 0.5)], axis=-1),
        'a_out_norm': 1.0 + nrm((N_AB, H_A, DH_A), 0.01),
        'b_q_norm': 1.0 + nrm((N_AB, DH_B), 0.01),
        'b_k_norm': 1.0 + nrm((N_AB, DH_B), 0.01),
        'b_rel_bias': nrm((N_AB, H_B, 2 * MAX_REL + 1), 0.1),
        'ab_w_out': nrm((N_AB, D_MIX, D_MODEL), D_MIX ** -0.5),
        'c_w_in': nrm((N_C, D_MODEL, 2 * R_WIDTH), D_MODEL ** -0.5),
        'c_conv_w': nrm((N_C, CONV_W, R_WIDTH), CONV_W ** -0.5),
        'c_conv_b': nrm((N_C, R_WIDTH), 0.01),
        'c_gate_w': nrm((N_C, N_BLOCKS_C, BW_C, 2 * BW_C), BW_C ** -0.5),
        'c_gate_b': nrm((N_C, 2, R_WIDTH), 0.01),
        'c_lambda': jnp.log(u_lam) - jnp.log1p(-u_lam),
        'c_w_out': nrm((N_C, R_WIDTH, D_MODEL), R_WIDTH ** -0.5),
    }


def reference(x_prompt, x_sample, state_a_C, state_a_n, state_a_m, cache_b_k, cache_b_v,
              state_c_conv, state_c_h, c_prompt, c_sample,
              ffn1_norm, ffn1_w_in, ffn1_w_out, mix_norm, ffn2_norm, ffn2_w_in, ffn2_w_out,
              ada_w, ada_b, ab_w_in, ab_gate_bias, a_out_norm, b_q_norm, b_k_norm, b_rel_bias,
              ab_w_out, c_w_in, c_conv_w, c_conv_b, c_gate_w, c_gate_b, c_lambda, c_w_out):
    p = dict(ffn1_norm=ffn1_norm, ffn1_w_in=ffn1_w_in, ffn1_w_out=ffn1_w_out, mix_norm=mix_norm,
             ffn2_norm=ffn2_norm, ffn2_w_in=ffn2_w_in, ffn2_w_out=ffn2_w_out, ada_w=ada_w, ada_b=ada_b,
             ab_w_in=ab_w_in, ab_gate_bias=ab_gate_bias, a_out_norm=a_out_norm, b_q_norm=b_q_norm,
             b_k_norm=b_k_norm, b_rel_bias=b_rel_bias, ab_w_out=ab_w_out, c_w_in=c_w_in,
             c_conv_w=c_conv_w, c_conv_b=c_conv_b, c_gate_w=c_gate_w, c_gate_b=c_gate_b,
             c_lambda=c_lambda, c_w_out=c_w_out)
    y_prompt, ps = run_trunk(x_prompt, c_prompt, p, None)
    st = dict(a_C=state_a_C, a_n=state_a_n, a_m=state_a_m, b_k=cache_b_k, b_v=cache_b_v,
              c_conv=state_c_conv, c_h=state_c_h)
    y_sample, ss = run_trunk(x_sample, c_sample, p, st)
    return (y_prompt, y_sample,
            ps['a_C'], ps['a_n'], ps['a_m'], ps['b_k'], ps['b_v'], ps['c_conv'], ps['c_h'],
            ss['a_C'], ss['a_n'], ss['a_m'], ss['b_k'], ss['b_v'], ss['c_conv'], ss['c_h'])
```

```python
import functools

import jax
import jax.numpy as jnp
from jax import lax
from jax.experimental import pallas as pl
from jax.experimental.pallas import tpu as pltpu

F32 = jnp.float32
BF16 = jnp.bfloat16

EPS = 1e-6
CHUNK = 64
LRU_C = 8.0
LANES = 128
SUBLANES = 8
MIB = 1024 * 1024


def _cparams(semantics, vmem_mib):
    return pltpu.CompilerParams(dimension_semantics=semantics, vmem_limit_bytes=vmem_mib * MIB)


def _const_spec(shape):
    nd = len(shape)
    return pl.BlockSpec(shape, lambda *_: (0,) * nd, pipeline_mode=pl.Buffered(1))


def _dot(a, b):
    return jnp.dot(a, b, preferred_element_type=F32)


def _dot_nt(a, b):
    return lax.dot_general(a, b, (((1,), (1,)), ((), ())), preferred_element_type=F32)


def _dot_tn(a, b):
    return lax.dot_general(a, b, (((0,), (0,)), ((), ())), preferred_element_type=F32)


def _rms_mod(x, g, shift, scale):
    ms = jnp.mean(x * x, axis=-1, keepdims=True)
    return (x * lax.rsqrt(ms + EPS) * g) * (1.0 + scale) + shift


def _softplus(x):
    return jnp.maximum(x, 0.0) + jnp.log1p(jnp.exp(-jnp.abs(x)))


def _gelu_tanh(x):
    return x * (0.5 * (1.0 + jnp.tanh(0.7978845608028654 * (x + 0.044715 * (x * x * x)))))


def _ada_kernel(c_ref, w_ref, b_ref, o_ref):
    c = c_ref[...].astype(BF16)
    w = w_ref[0].astype(BF16)
    o_ref[0] = _dot(c, w) + b_ref[0]


def _ada_call(c_all, ada_w, ada_b):
    depth, d, n = ada_w.shape
    m = c_all.shape[0]
    tn = d
    return pl.pallas_call(
        _ada_kernel,
        grid=(depth, n // tn),
        in_specs=[pl.BlockSpec((m, d), lambda l, j: (0, 0)),
                  pl.BlockSpec((1, d, tn), lambda l, j: (l, 0, j)),
                  pl.BlockSpec((1, 1, tn), lambda l, j: (l, 0, j))],
        out_specs=pl.BlockSpec((1, m, tn), lambda l, j: (l, 0, j)),
        out_shape=jax.ShapeDtypeStruct((depth, m, n), F32),
        compiler_params=_cparams(("arbitrary", "arbitrary"), 32),
        name="ada_proj",
    )(c_all, ada_w, ada_b.reshape(depth, 1, n))


def _ffn_kernel(x_ref, sh_ref, sc_ref, gt_ref, g_ref, wgu_ref, wo_ref, o_ref, h_scr, acc_scr, *, nc, tf):
    x = x_ref[...]
    bb, tl, d = x.shape
    h = _rms_mod(x, g_ref[...], sh_ref[...], sc_ref[...])
    h_scr[...] = h.reshape(bb * tl, d).astype(BF16)
    acc_scr[...] = jnp.zeros_like(acc_scr)

    def body(c, carry):
        gu = _dot(h_scr[...], wgu_ref[c])
        gate = gu[:, :tf]
        act = (gate * jax.nn.sigmoid(gate)) * gu[:, tf:]
        acc_scr[...] += _dot(act.astype(BF16), wo_ref[c])
        return carry

    lax.fori_loop(0, nc, body, 0)
    o_ref[...] = x + (0.5 * gt_ref[...]) * acc_scr[...].reshape(bb, tl, d)


def _ffn_call(x, sh, sc, gt, g, wgu, wo, bb, tl):
    nb, length, d = x.shape
    nc, _, tf2 = wgu.shape
    tm = bb * tl
    x_spec = pl.BlockSpec((bb, tl, d), lambda i, t: (i, t, 0))
    ada_spec = pl.BlockSpec((bb, 1, d), lambda i, t: (i, 0, 0))
    return pl.pallas_call(
        functools.partial(_ffn_kernel, nc=nc, tf=tf2 // 2),
        grid=(nb // bb, length // tl),
        in_specs=[x_spec, ada_spec, ada_spec, ada_spec, _const_spec((1, d)),
                  _const_spec(wgu.shape), _const_spec(wo.shape)],
        out_specs=x_spec,
        out_shape=jax.ShapeDtypeStruct(x.shape, F32),
        scratch_shapes=[pltpu.VMEM((tm, d), BF16), pltpu.VMEM((tm, d), F32)],
        compiler_params=_cparams(("arbitrary", "arbitrary"), 48),
        name="ffn",
    )(x, sh, sc, gt, g, wgu, wo)


def _head_rmsnorm(q, e, g, dhb):
    q2 = q * q
    hi = q2.astype(BF16)
    lo = (q2 - hi.astype(F32)).astype(BF16)
    ss = _dot(hi, e) + _dot(lo, e)
    return q * lax.rsqrt(ss * (1.0 / dhb) + EPS) * g


def _proj_body(x, sh, sc, g, wa_ref, wg_ref, wb_ref, gb_ref, qg_ref, kg_ref, e_ref, *, nh, bw, dhb):
    bb, tl, d = x.shape
    h = _rms_mod(x, g, sh, sc).reshape(bb * tl, d).astype(BF16)
    ua = _dot(h, wa_ref[...])
    gg = _dot(h, wg_ref[...]) + gb_ref[...]
    lane = lax.broadcasted_iota(jnp.int32, gg.shape, 1)
    gates = jnp.where(lane < nh, gg, -_softplus(-gg))
    ub = _dot(h, wb_ref[...])
    e = e_ref[...]
    qn = _head_rmsnorm(ub[:, :bw], e, qg_ref[...], dhb)
    kn = _head_rmsnorm(ub[:, bw:2 * bw], e, kg_ref[...], dhb)
    vb = ub[:, 2 * bw:]
    return ua, gates, qn, kn, vb


def _proj_prompt_kernel(x_ref, sh_ref, sc_ref, g_ref, wa_ref, wg_ref, wb_ref, gb_ref, qg_ref, kg_ref, e_ref,
                        ua_ref, gt_ref, qn_ref, kp_ref, vp_ref, kl_ref, vl_ref, *, nh, bw, dhb):
    t = pl.program_id(1)
    nt = pl.num_programs(1)

    @pl.when(t == 0)
    def _():
        kp_ref[...] = jnp.zeros_like(kp_ref)
        vp_ref[...] = jnp.zeros_like(vp_ref)

    @pl.when(t > 0)
    def _():
        ua, gates, qn, kn, vb = _proj_body(x_ref[...], sh_ref[...], sc_ref[...], g_ref[...], wa_ref, wg_ref, wb_ref,
                                           gb_ref, qg_ref, kg_ref, e_ref, nh=nh, bw=bw, dhb=dhb)
        ua_ref[0] = ua
        gt_ref[0] = gates
        qn_ref[0] = qn.astype(BF16)
        kp_ref[0] = kn.astype(BF16)
        vp_ref[0] = vb.astype(BF16)

        @pl.when(t == nt - 1)
        def _():
            kl_ref[0] = kn
            vl_ref[0] = vb


def _proj_prompt_call(x, sh, sc, g, wa, wg, wb, gbias, qg, kg, e, *, nh, dhb, w):
    nb, length, d = x.shape
    tl = w
    nt = length // tl
    bw = wb.shape[1] // 3
    aw4 = wa.shape[1]
    prev = lambda b, t: (b, jnp.maximum(t - 1, 0), 0)
    ada_spec = pl.BlockSpec((1, 1, d), lambda b, t: (b, 0, 0))
    outs = pl.pallas_call(
        functools.partial(_proj_prompt_kernel, nh=nh, bw=bw, dhb=dhb),
        grid=(nb, nt + 1),
        in_specs=[pl.BlockSpec((1, tl, d), prev), ada_spec, ada_spec, _const_spec((1, d)),
                  _const_spec(wa.shape), _const_spec(wg.shape), _const_spec(wb.shape), _const_spec(gbias.shape),
                  _const_spec(qg.shape), _const_spec(kg.shape), _const_spec(e.shape)],
        out_specs=[pl.BlockSpec((1, tl, aw4), prev),
                   pl.BlockSpec((1, tl, LANES), prev),
                   pl.BlockSpec((1, tl, bw), prev),
                   pl.BlockSpec((1, tl, bw), lambda b, t: (b, t, 0)),
                   pl.BlockSpec((1, tl, bw), lambda b, t: (b, t, 0)),
                   pl.BlockSpec((1, tl, bw), lambda b, t: (b, 0, 0)),
                   pl.BlockSpec((1, tl, bw), lambda b, t: (b, 0, 0))],
        out_shape=[jax.ShapeDtypeStruct((nb, length, aw4), F32),
                   jax.ShapeDtypeStruct((nb, length, LANES), F32),
                   jax.ShapeDtypeStruct((nb, length, bw), BF16),
                   jax.ShapeDtypeStruct((nb, length + w, bw), BF16),
                   jax.ShapeDtypeStruct((nb, length + w, bw), BF16),
                   jax.ShapeDtypeStruct((nb, w, bw), F32),
                   jax.ShapeDtypeStruct((nb, w, bw), F32)],
        compiler_params=_cparams(("arbitrary", "arbitrary"), 48),
        name="proj_prompt",
    )(x, sh, sc, g, wa, wg, wb, gbias, qg, kg, e)
    return outs


def _proj_sample_kernel(x_ref, sh_ref, sc_ref, g_ref, wa_ref, wg_ref, wb_ref, gb_ref, qg_ref, kg_ref, e_ref,
                        ua_ref, gt_ref, qn_ref, kn_ref, vb_ref, *, nh, bw, dhb):
    bb, tl, _ = x_ref.shape
    ua, gates, qn, kn, vb = _proj_body(x_ref[...], sh_ref[...], sc_ref[...], g_ref[...], wa_ref, wg_ref, wb_ref,
                                       gb_ref, qg_ref, kg_ref, e_ref, nh=nh, bw=bw, dhb=dhb)
    ua_ref[...] = ua.reshape(bb, tl, -1)
    gt_ref[...] = gates.reshape(bb, tl, -1)
    qn_ref[...] = qn.reshape(bb, tl, -1).astype(BF16)
    kn_ref[...] = kn.reshape(bb, tl, -1)
    vb_ref[...] = vb.reshape(bb, tl, -1)


def _proj_sample_call(x, sh, sc, g, wa, wg, wb, gbias, qg, kg, e, *, nh, dhb):
    nb, length, d = x.shape
    bw = wb.shape[1] // 3
    aw4 = wa.shape[1]
    full = lambda n: pl.BlockSpec((nb, length, n), lambda i: (0, 0, 0))
    ada_spec = pl.BlockSpec((nb, 1, d), lambda i: (0, 0, 0))
    return pl.pallas_call(
        functools.partial(_proj_sample_kernel, nh=nh, bw=bw, dhb=dhb),
        grid=(1,),
        in_specs=[full(d), ada_spec, ada_spec, _const_spec((1, d)),
                  _const_spec(wa.shape), _const_spec(wg.shape), _const_spec(wb.shape), _const_spec(gbias.shape),
                  _const_spec(qg.shape), _const_spec(kg.shape), _const_spec(e.shape)],
        out_specs=[full(aw4), full(LANES), full(bw), full(bw), full(bw)],
        out_shape=[jax.ShapeDtypeStruct((nb, length, aw4), F32),
                   jax.ShapeDtypeStruct((nb, length, LANES), F32),
                   jax.ShapeDtypeStruct((nb, length, bw), BF16),
                   jax.ShapeDtypeStruct((nb, length, bw), F32),
                   jax.ShapeDtypeStruct((nb, length, bw), F32)],
        compiler_params=_cparams(("arbitrary",), 48),
        name="proj_sample",
    )(x, sh, sc, g, wa, wg, wb, gbias, qg, kg, e)


def _mlstm_chunk(q, k, v, ig_col, bt_col, ig_row, bt_row, c_mem, n, m, causal):
    length = q.shape[0]
    logd = jnp.where(causal, bt_col - bt_row + ig_row, -jnp.inf)
    inter = bt_col + m
    m_t = jnp.maximum(inter, jnp.max(logd, axis=-1, keepdims=True))
    dw = jnp.exp(logd - m_t)
    iw = jnp.exp(inter - m_t)
    qb, kb, vb = q.astype(BF16), k.astype(BF16), v.astype(BF16)
    s = _dot_nt(qb, kb) * dw
    num = iw * _dot(qb, c_mem.astype(BF16)) + _dot(s.astype(BF16), vb)
    den = iw * jnp.sum(q * n, axis=-1, keepdims=True) + jnp.sum(s, axis=-1, keepdims=True)
    h = num / jnp.maximum(jnp.abs(den), jnp.exp(-m_t))
    b_last = bt_col[length - 1:length, :]
    g_row = b_last - bt_row + ig_row
    g_col = b_last - bt_col + ig_col
    m_new = jnp.maximum(b_last + m, jnp.max(g_row, axis=-1, keepdims=True))
    w_prev = jnp.exp(b_last + m - m_new)
    kw = k * jnp.exp(g_col - m_new)
    c_new = w_prev * c_mem + _dot_tn(kw.astype(BF16), vb)
    n_new = w_prev * n + jnp.sum(kw, axis=0, keepdims=True)
    return h, c_new, n_new, m_new


def _mlstm_kernel(ua_ref, g_ref, c0_ref, n0_ref, m0_ref, go_ref, ha_ref, c_ref, n_ref, m_ref, *, seg, nh, dh):
    t = pl.program_id(1)

    @pl.when(t == 0)
    def _():
        c_ref[...] = c0_ref[...]
        n_ref[...] = n0_ref[...]
        m_ref[...] = m0_ref[...]

    tq = ua_ref.shape[1]
    aw = nh * dh
    gates = g_ref[0]
    pos = lax.broadcasted_iota(jnp.int32, gates.shape, 0) % seg
    lane = lax.broadcasted_iota(jnp.int32, gates.shape, 1)
    bt = gates
    s = 1
    while s < seg:
        bt = bt + jnp.where(pos >= s, pltpu.roll(bt, s, 0), 0.0)
        s *= 2
    gc = jnp.where(lane < nh, gates, bt)
    if tq % LANES:
        gsq = jnp.concatenate([gc, jnp.zeros((LANES - tq % LANES, LANES), F32)], axis=0)
    else:
        gsq = gc
    gtr = gsq.T
    ri = lax.broadcasted_iota(jnp.int32, (seg, seg), 0)
    ci = lax.broadcasted_iota(jnp.int32, (seg, seg), 1)
    causal = ri >= ci
    for jc in range(tq // seg):
        r0 = jc * seg
        rows = slice(r0, r0 + seg)
        for h in range(nh):
            q = ua_ref[0, rows, h * dh:(h + 1) * dh]
            k = ua_ref[0, rows, aw + h * dh:aw + (h + 1) * dh] * (dh ** -0.5)
            v = ua_ref[0, rows, 2 * aw + h * dh:2 * aw + (h + 1) * dh]
            oa = ua_ref[0, rows, 3 * aw + h * dh:3 * aw + (h + 1) * dh]
            hh, c_new, n_new, m_new = _mlstm_chunk(
                q, k, v,
                gc[rows, h:h + 1], gc[rows, nh + h:nh + h + 1],
                gtr[h:h + 1, rows], gtr[nh + h:nh + h + 1, rows],
                c_ref[0, h], n_ref[0, h:h + 1, :], m_ref[0, h:h + 1, :], causal)
            c_ref[0, h] = c_new
            n_ref[0, h:h + 1, :] = n_new
            m_ref[0, h:h + 1, :] = m_new
            ms = jnp.mean(hh * hh, axis=-1, keepdims=True)
            hn = (hh * lax.rsqrt(ms + EPS) * go_ref[h:h + 1, :]) * jax.nn.sigmoid(oa)
            ha_ref[0, rows, h * dh:(h + 1) * dh] = hn.astype(BF16)


def _mlstm_call(ua, gates, c0, n0, m0, gout, *, tq, seg):
    nb, length, aw4 = ua.shape
    _, nh, dh, _ = c0.shape
    st = lambda shape: pl.BlockSpec((1,) + shape, lambda b, t: (b,) + (0,) * len(shape))
    tile = lambda n: pl.BlockSpec((1, tq, n), lambda b, t: (b, t, 0))
    return pl.pallas_call(
        functools.partial(_mlstm_kernel, seg=seg, nh=nh, dh=dh),
        grid=(nb, length // tq),
        in_specs=[tile(aw4), tile(LANES), st((nh, dh, dh)), st((nh, dh)), st((nh, 1)), _const_spec(gout.shape)],
        out_specs=[tile(nh * dh), st((nh, dh, dh)), st((nh, dh)), st((nh, 1))],
        out_shape=[jax.ShapeDtypeStruct((nb, length, nh * dh), BF16),
                   jax.ShapeDtypeStruct((nb, nh, dh, dh), F32),
                   jax.ShapeDtypeStruct((nb, nh, dh), F32),
                   jax.ShapeDtypeStruct((nb, nh, 1), F32)],
        compiler_params=_cparams(("arbitrary", "arbitrary"), 32),
        name="mlstm",
    )(ua, gates, c0, n0, m0, gout)


def _softmax_rows(parts):
    mx = None
    for s in parts:
        pm = jnp.max(s, axis=-1, keepdims=True)
        mx = pm if mx is None else jnp.maximum(mx, pm)
    es = [jnp.exp(s - mx) for s in parts]
    den = None
    for e in es:
        ps = jnp.sum(e, axis=-1, keepdims=True)
        den = ps if den is None else den + ps
    return [e / den for e in es]


def _band_prompt_kernel(q_ref, k_ref, v_ref, bias_ref, o_ref, *, nhb, w):
    c = pl.program_id(1)
    start = pl.multiple_of(c * CHUNK, CHUNK)
    nk = w + CHUNK
    kwin = k_ref[0, pl.ds(start, nk), :]
    vwin = v_ref[0, pl.ds(start, nk), :]
    q = q_ref[0]
    col = lax.broadcasted_iota(jnp.int32, (CHUNK, nk), 1)
    valid = col + start >= w
    lane = lax.broadcasted_iota(jnp.int32, (CHUNK, LANES), 1)
    low = lane < LANES // 2
    zero = jnp.zeros((CHUNK, LANES), BF16)
    for p in range(nhb // 2):
        sl = slice(p * LANES, (p + 1) * LANES)
        qp, kp, vp = q[:, sl], kwin[:, sl], vwin[:, sl]
        outs = []
        for sub in range(2):
            qh = jnp.where(low if sub == 0 else ~low, qp, zero)
            s = _dot_nt(qh, kp) * 0.125 + bias_ref[2 * p + sub]
            s = jnp.where(valid, s, -jnp.inf)
            (pr,) = _softmax_rows([s])
            outs.append(_dot(pr.astype(BF16), vp))
        o_ref[0, :, sl] = jnp.where(low, outs[0], outs[1]).astype(BF16)


def _band_prompt_call(qn, kpad, vpad, bias, *, w):
    nb, length, bw = qn.shape
    nhb = bias.shape[0]
    lp = kpad.shape[1]
    return pl.pallas_call(
        functools.partial(_band_prompt_kernel, nhb=nhb, w=w),
        grid=(nb, length // CHUNK),
        in_specs=[pl.BlockSpec((1, CHUNK, bw), lambda b, c: (b, c, 0)),
                  pl.BlockSpec((1, lp, bw), lambda b, c: (b, 0, 0)),
                  pl.BlockSpec((1, lp, bw), lambda b, c: (b, 0, 0)),
                  _const_spec(bias.shape)],
        out_specs=pl.BlockSpec((1, CHUNK, bw), lambda b, c: (b, c, 0)),
        out_shape=jax.ShapeDtypeStruct((nb, length, bw), BF16),
        compiler_params=_cparams(("arbitrary", "arbitrary"), 32),
        name="band_prompt",
    )(qn, kpad, vpad, bias)


def _band_sample_kernel(q_ref, kn_ref, vn_ref, ck_ref, cv_ref, bc_ref, bn_ref, o_ref, *, nhb):
    q = q_ref[0]
    tq = q.shape[0]
    lane = lax.broadcasted_iota(jnp.int32, (tq, LANES), 1)
    low = lane < LANES // 2
    zero = jnp.zeros((tq, LANES), BF16)
    for p in range(nhb // 2):
        sl = slice(p * LANES, (p + 1) * LANES)
        qp = q[:, sl]
        kc = ck_ref[0, :, sl].astype(BF16)
        vc = cv_ref[0, :, sl].astype(BF16)
        kn = kn_ref[0, :, sl].astype(BF16)
        vn = vn_ref[0, :, sl].astype(BF16)
        outs = []
        for sub in range(2):
            h = 2 * p + sub
            qh = jnp.where(low if sub == 0 else ~low, qp, zero)
            s_c = _dot_nt(qh, kc) * 0.125 + bc_ref[h]
            s_n = _dot_nt(qh, kn) * 0.125 + bn_ref[h]
            p_c, p_n = _softmax_rows([s_c, s_n])
            outs.append(_dot(p_c.astype(BF16), vc) + _dot(p_n.astype(BF16), vn))
        o_ref[0, :, sl] = jnp.where(low, outs[0], outs[1]).astype(BF16)


def _band_sample_call(qn, kn, vn, ck, cv, bias_c, bias_n):
    nb, tq, bw = qn.shape
    w = ck.shape[1]
    nhb = bias_c.shape[0]
    new = pl.BlockSpec((1, tq, bw), lambda b: (b, 0, 0))
    cache = pl.BlockSpec((1, w, bw), lambda b: (b, 0, 0))
    return pl.pallas_call(
        functools.partial(_band_sample_kernel, nhb=nhb),
        grid=(nb,),
        in_specs=[new, new, new, cache, cache, _const_spec(bias_c.shape), _const_spec(bias_n.shape)],
        out_specs=new,
        out_shape=jax.ShapeDtypeStruct((nb, tq, bw), BF16),
        compiler_params=_cparams(("arbitrary",), 32),
        name="band_sample",
    )(qn, kn, vn, ck, cv, bias_c, bias_n)


def _mixout_kernel(x_ref, ha_ref, hb_ref, gt_ref, woa_ref, wob_ref, o_ref):
    bb, tl, d = x_ref.shape
    ha = ha_ref[...].reshape(bb * tl, -1)
    hb = hb_ref[...].reshape(bb * tl, -1)
    y = _dot(ha, woa_ref[...]) + _dot(hb, wob_ref[...])
    o_ref[...] = x_ref[...] + gt_ref[...] * y.reshape(bb, tl, d)


def _mixout_call(x, ha, hb, gt, woa, wob, bb, tl):
    nb, length, d = x.shape
    tile = lambda n: pl.BlockSpec((bb, tl, n), lambda i, t: (i, t, 0))
    return pl.pallas_call(
        _mixout_kernel,
        grid=(nb // bb, length // tl),
        in_specs=[tile(d), tile(ha.shape[-1]), tile(hb.shape[-1]),
                  pl.BlockSpec((bb, 1, d), lambda i, t: (i, 0, 0)),
                  _const_spec(woa.shape), _const_spec(wob.shape)],
        out_specs=tile(d),
        out_shape=jax.ShapeDtypeStruct(x.shape, F32),
        compiler_params=_cparams(("arbitrary", "arbitrary"), 32),
        name="mix_out",
    )(x, ha, hb, gt, woa, wob)


def _rglru_gates(xc, gw_ref, rb, ib, lam, nblk):
    bwc = xc.shape[1] // nblk
    r_parts, i_parts = [], []
    for n in range(nblk):
        gn = _dot(xc[:, n * bwc:(n + 1) * bwc].astype(BF16), gw_ref[n])
        r_parts.append(gn[:, :bwc])
        i_parts.append(gn[:, bwc:])
    r = jax.nn.sigmoid(jnp.concatenate(r_parts, axis=1) + rb)
    ii = jax.nn.sigmoid(jnp.concatenate(i_parts, axis=1) + ib)
    log_a = (-LRU_C * r) * _softplus(-lam)
    a = jnp.exp(log_a)
    th = jnp.tanh(log_a)
    upd = jnp.sqrt(-2.0 * th / (1.0 - th)) * (ii * xc)
    return a, upd


def _rglru_prompt_kernel(x_ref, sh_ref, sc_ref, gt_ref, g_ref, win_ref, cw_ref, cb_ref, gw_ref, rb_ref, ib_ref,
                         lam_ref, wout_ref, conv0_ref, h0_ref, o_ref, conv_ref, hl_ref, xp_scr, a_scr, b_scr, *, nblk):
    t = pl.program_id(1)
    tq, d = x_ref.shape[1], x_ref.shape[2]
    r_w = lam_ref.shape[1]
    ncw = cw_ref.shape[0]

    @pl.when(t == 0)
    def _():
        xp_scr[0:SUBLANES, :] = conv0_ref[0]
        hl_ref[...] = h0_ref[...]

    x = x_ref[0]
    hm = _rms_mod(x, g_ref[...], sh_ref[0], sc_ref[0]).astype(BF16)
    u = _dot(hm, win_ref[...])
    gb = u[:, :r_w]
    xp_scr[SUBLANES:SUBLANES + tq, :] = u[:, r_w:]
    xc = cb_ref[...]
    for j in range(ncw):
        off = SUBLANES - (ncw - 1 - j)
        xc = xc + xp_scr[off:off + tq, :] * cw_ref[j:j + 1, :]
    conv_ref[0] = xp_scr[tq:tq + SUBLANES, :]
    xp_scr[0:SUBLANES, :] = xp_scr[tq:tq + SUBLANES, :]
    a, upd = _rglru_gates(xc, gw_ref, rb_ref[...], ib_ref[...], lam_ref[...], nblk)
    a_scr[...] = a
    b_scr[...] = upd
    row8 = lax.broadcasted_iota(jnp.int32, (SUBLANES, r_w), 0)

    def scan_body(i, h):
        rows = pl.ds(pl.multiple_of(i * SUBLANES, SUBLANES), SUBLANES)
        ai = a_scr[rows, :]
        bi = b_scr[rows, :]
        s = 1
        while s < SUBLANES:
            m = row8 >= s
            bi = jnp.where(m, ai * pltpu.roll(bi, s, 0) + bi, bi)
            ai = jnp.where(m, ai * pltpu.roll(ai, s, 0), ai)
            s *= 2
        hs = ai * h + bi
        a_scr[rows, :] = hs
        return hs[SUBLANES - 1:SUBLANES, :]

    h_fin = lax.fori_loop(0, tq // SUBLANES, scan_body, hl_ref[0])
    hl_ref[0] = h_fin
    y = _dot((_gelu_tanh(gb) * a_scr[...]).astype(BF16), wout_ref[...])
    o_ref[0] = x + gt_ref[0] * y


def _rglru_prompt_call(x, sh, sc, gt, g, win, cw, cb, gw, rb, ib, lam, wout, conv0, h0, *, tq):
    nb, length, d = x.shape
    r_w = lam.shape[1]
    nblk = gw.shape[0]
    ada_spec = pl.BlockSpec((1, 1, d), lambda b, t: (b, 0, 0))
    tile = pl.BlockSpec((1, tq, d), lambda b, t: (b, t, 0))
    conv_spec = pl.BlockSpec((1, SUBLANES, r_w), lambda b, t: (b, 0, 0))
    h_spec = pl.BlockSpec((1, 1, r_w), lambda b, t: (b, 0, 0))
    consts = [g, win, cw, cb, gw, rb, ib, lam, wout]
    return pl.pallas_call(
        functools.partial(_rglru_prompt_kernel, nblk=nblk),
        grid=(nb, length // tq),
        in_specs=[tile, ada_spec, ada_spec, ada_spec] + [_const_spec(a.shape) for a in consts] + [conv_spec, h_spec],
        out_specs=[tile, conv_spec, h_spec],
        out_shape=[jax.ShapeDtypeStruct(x.shape, F32),
                   jax.ShapeDtypeStruct((nb, SUBLANES, r_w), F32),
                   jax.ShapeDtypeStruct((nb, 1, r_w), F32)],
        scratch_shapes=[pltpu.VMEM((tq + SUBLANES, r_w), F32), pltpu.VMEM((tq, r_w), F32), pltpu.VMEM((tq, r_w), F32)],
        compiler_params=_cparams(("arbitrary", "arbitrary"), 48),
        name="rglru_prompt",
    )(x, sh, sc, gt, *consts, conv0, h0)


def _rglru_sample_kernel(x_ref, sh_ref, sc_ref, gt_ref, g_ref, win_ref, cw_ref, cb_ref, gw_ref, rb_ref, ib_ref,
                         lam_ref, wout_ref, conv0_ref, h0_ref, o_ref, conv_ref, hl_ref, xp_scr, *, nblk):
    bb, tl, d = x_ref.shape
    tm = bb * tl
    r_w = lam_ref.shape[1]
    ncw = cw_ref.shape[0]
    x = x_ref[...]
    hm = _rms_mod(x, g_ref[...], sh_ref[...], sc_ref[...]).reshape(tm, d).astype(BF16)
    u = _dot(hm, win_ref[...])
    gb = u[:, :r_w]
    xp_scr[:, 0:SUBLANES, :] = conv0_ref[...]
    xp_scr[:, SUBLANES:SUBLANES + tl, :] = u[:, r_w:].reshape(bb, tl, r_w)
    xc = jnp.broadcast_to(cb_ref[...], (bb, tl, r_w))
    for j in range(ncw):
        off = SUBLANES - (ncw - 1 - j)
        xc = xc + xp_scr[:, off:off + tl, :] * cw_ref[j:j + 1, :]
    conv_ref[...] = xp_scr[:, tl:tl + SUBLANES, :]
    a, b = _rglru_gates(xc.reshape(tm, r_w), gw_ref, rb_ref[...], ib_ref[...], lam_ref[...], nblk)
    pos = lax.broadcasted_iota(jnp.int32, (tm, r_w), 0) % tl
    s = 1
    while s < tl:
        m = pos >= s
        b = jnp.where(m, a * pltpu.roll(b, s, 0) + b, b)
        a = jnp.where(m, a * pltpu.roll(a, s, 0), a)
        s *= 2
    hs = a.reshape(bb, tl, r_w) * h0_ref[...] + b.reshape(bb, tl, r_w)
    hl_ref[...] = hs[:, tl - 1:tl, :]
    y = _dot((_gelu_tanh(gb) * hs.reshape(tm, r_w)).astype(BF16), wout_ref[...])
    o_ref[...] = x + gt_ref[...] * y.reshape(bb, tl, d)


def _rglru_sample_call(x, sh, sc, gt, g, win, cw, cb, gw, rb, ib, lam, wout, conv0, h0):
    nb, tl, d = x.shape
    r_w = lam.shape[1]
    nblk = gw.shape[0]
    full = lambda a, b: pl.BlockSpec((nb, a, b), lambda i: (0, 0, 0))
    consts = [g, win, cw, cb, gw, rb, ib, lam, wout]
    return pl.pallas_call(
        functools.partial(_rglru_sample_kernel, nblk=nblk),
        grid=(1,),
        in_specs=[full(tl, d), full(1, d), full(1, d), full(1, d)] + [_const_spec(a.shape) for a in consts]
                 + [full(SUBLANES, r_w), full(1, r_w)],
        out_specs=[full(tl, d), full(SUBLANES, r_w), full(1, r_w)],
        out_shape=[jax.ShapeDtypeStruct(x.shape, F32),
                   jax.ShapeDtypeStruct((nb, SUBLANES, r_w), F32),
                   jax.ShapeDtypeStruct((nb, 1, r_w), F32)],
        scratch_shapes=[pltpu.VMEM((nb, tl + SUBLANES, r_w), F32)],
        compiler_params=_cparams(("arbitrary",), 48),
        name="rglru_sample",
    )(x, sh, sc, gt, *consts, conv0, h0)


FFN_TF = 256


def _prep_ffn(w_in, w_out):
    d, two_ff = w_in.shape
    dff = two_ff // 2
    nc = dff // FFN_TF
    wg = w_in[:, :dff].reshape(d, nc, FFN_TF)
    wu = w_in[:, dff:].reshape(d, nc, FFN_TF)
    wgu = jnp.concatenate([wg, wu], axis=-1).transpose(1, 0, 2).astype(BF16)
    wo = w_out.reshape(nc, FFN_TF, d).astype(BF16)
    return wgu, wo


def _band_bias(table, nq, nk, w):
    max_rel = (table.shape[-1] - 1) // 2
    rel = (jnp.arange(nk) - w)[None, :] - jnp.arange(nq)[:, None]
    return table[:, jnp.clip(rel, -max_rel, max_rel) + max_rel].astype(F32)


def _pad_rows_front(a, rows):
    return jnp.pad(a, ((0, 0), (rows - a.shape[1], 0), (0, 0)))


def kernel(x_prompt, x_sample, state_a_C, state_a_n, state_a_m, cache_b_k, cache_b_v, state_c_conv, state_c_h,
           c_prompt, c_sample, ffn1_norm, ffn1_w_in, ffn1_w_out, mix_norm, ffn2_norm, ffn2_w_in, ffn2_w_out,
           ada_w, ada_b, ab_w_in, ab_gate_bias, a_out_norm, b_q_norm, b_k_norm, b_rel_bias, ab_w_out,
           c_w_in, c_conv_w, c_conv_b, c_gate_w, c_gate_b, c_lambda, c_w_out):
    nbp, seq, d = x_prompt.shape
    nbs, tdec, _ = x_sample.shape
    depth = ada_w.shape[0]
    n_ada = ada_w.shape[2] // d
    _, _, nh, dh, _ = state_a_C.shape
    _, _, w_band, nhb, dhb = cache_b_k.shape
    aw, bw = nh * dh, nhb * dhb
    ncw = c_conv_w.shape[1]
    assert 2 * dhb == LANES and dh == LANES and w_band % CHUNK == 0 and seq % w_band == 0

    ada = _ada_call(jnp.concatenate([c_prompt, c_sample], axis=0), ada_w, ada_b)
    ada = ada.reshape(depth, nbp + nbs, n_ada, 1, d)
    ada_p = [[ada[l, :nbp, k] for k in range(n_ada)] for l in range(depth)]
    ada_s = [[ada[l, nbp:, k] for k in range(n_ada)] for l in range(depth)]

    ffn1 = [_prep_ffn(ffn1_w_in[l], ffn1_w_out[l]) for l in range(depth)]
    ffn2 = [_prep_ffn(ffn2_w_in[l], ffn2_w_out[l]) for l in range(depth)]

    tl_p = 512
    xp, xs = x_prompt, x_sample
    outs_p, outs_s = {}, {}
    for l in range(depth):
        ap, as_ = ada_p[l], ada_s[l]
        i = l // 2
        g1 = ffn1_norm[l].reshape(1, d)
        gm = mix_norm[l].reshape(1, d)
        g2 = ffn2_norm[l].reshape(1, d)
        xp = _ffn_call(xp, ap[0], ap[1], ap[2], g1, *ffn1[l], 1, tl_p)
        xs = _ffn_call(xs, as_[0], as_[1], as_[2], g1, *ffn1[l], nbs, tdec)
        if l % 2 == 0:
            w_in = ab_w_in[i]
            wa = w_in[:, :4 * aw].astype(BF16)
            wg = jnp.pad(w_in[:, 4 * aw:4 * aw + 2 * nh], ((0, 0), (0, LANES - 2 * nh))).astype(BF16)
            wb = w_in[:, 4 * aw + 2 * nh:].astype(BF16)
            gbias = jnp.pad(ab_gate_bias[i], (0, LANES - 2 * nh)).reshape(1, LANES)
            qg = jnp.tile(b_q_norm[i], nhb).reshape(1, bw)
            kg = jnp.tile(b_k_norm[i], nhb).reshape(1, bw)
            head = jnp.arange(bw) // dhb
            e = (head[:, None] == head[None, :]).astype(BF16)
            woa = ab_w_out[i][:aw].astype(BF16)
            wob = ab_w_out[i][aw:].astype(BF16)
            gout = a_out_norm[i]
            bias = _band_bias(b_rel_bias[i], CHUNK, w_band + CHUNK, w_band)

            ua, gts, qn, kpad, vpad, klast, vlast = _proj_prompt_call(
                xp, ap[3], ap[4], gm, wa, wg, wb, gbias, qg, kg, e, nh=nh, dhb=dhb, w=w_band)
            zc = jnp.zeros((nbp, nh, dh, dh), F32)
            ha, c1, n1, m1 = _mlstm_call(ua, gts, zc, zc[:, :, 0], zc[:, :, 0, :1], gout, tq=256, seg=CHUNK)
            hb = _band_prompt_call(qn, kpad, vpad, bias, w=w_band)
            xp = _mixout_call(xp, ha, hb, ap[5], woa, wob, 1, tl_p)
            outs_p.setdefault('a_C', []).append(c1)
            outs_p.setdefault('a_n', []).append(n1)
            outs_p.setdefault('a_m', []).append(m1.reshape(nbp, nh))
            outs_p.setdefault('b_k', []).append(klast.reshape(nbp, w_band, nhb, dhb))
            outs_p.setdefault('b_v', []).append(vlast.reshape(nbp, w_band, nhb, dhb))

            ua, gts, qn, kn, vn = _proj_sample_call(
                xs, as_[3], as_[4], gm, wa, wg, wb, gbias, qg, kg, e, nh=nh, dhb=dhb)
            ha, c1, n1, m1 = _mlstm_call(ua, gts, state_a_C[i], state_a_n[i], state_a_m[i][..., None], gout,
                                         tq=tdec, seg=tdec)
            hb = _band_sample_call(qn, kn, vn, cache_b_k[i].reshape(nbs, w_band, bw),
                                   cache_b_v[i].reshape(nbs, w_band, bw),
                                   bias[:, :tdec, :w_band], bias[:, :tdec, w_band:w_band + tdec])
            xs = _mixout_call(xs, ha, hb, as_[5], woa, wob, nbs, tdec)
            outs_s.setdefault('a_C', []).append(c1)
            outs_s.setdefault('a_n', []).append(n1)
            outs_s.setdefault('a_m', []).append(m1.reshape(nbs, nh))
            outs_s.setdefault('b_k', []).append(kn.reshape(nbs, tdec, nhb, dhb))
            outs_s.setdefault('b_v', []).append(vn.reshape(nbs, tdec, nhb, dhb))
        else:
            r_w = c_lambda.shape[1]
            consts = (gm, c_w_in[i].astype(BF16), c_conv_w[i], c_conv_b[i].reshape(1, r_w), c_gate_w[i].astype(BF16),
                      c_gate_b[i][0].reshape(1, r_w), c_gate_b[i][1].reshape(1, r_w), c_lambda[i].reshape(1, r_w),
                      c_w_out[i].astype(BF16))
            xp, conv_p, h_p = _rglru_prompt_call(
                xp, ap[3], ap[4], ap[5], *consts,
                jnp.zeros((nbp, SUBLANES, r_w), F32), jnp.zeros((nbp, 1, r_w), F32), tq=256)
            xs, conv_s, h_s = _rglru_sample_call(
                xs, as_[3], as_[4], as_[5], *consts,
                _pad_rows_front(state_c_conv[i], SUBLANES), state_c_h[i][:, None, :])
            outs_p.setdefault('c_conv', []).append(conv_p[:, SUBLANES - (ncw - 1):])
            outs_p.setdefault('c_h', []).append(h_p[:, 0])
            outs_s.setdefault('c_conv', []).append(conv_s[:, SUBLANES - (ncw - 1):])
            outs_s.setdefault('c_h', []).append(h_s[:, 0])
        xp = _ffn_call(xp, ap[6], ap[7], ap[8], g2, *ffn2[l], 1, tl_p)
        xs = _ffn_call(xs, as_[6], as_[7], as_[8], g2, *ffn2[l], nbs, tdec)

    names = ('a_C', 'a_n', 'a_m', 'b_k', 'b_v', 'c_conv', 'c_h')
    ps = [jnp.stack(outs_p[n], axis=0) for n in names]
    ss = [jnp.stack(outs_s[n], axis=0) for n in names]
    return (xp, xs, *ps, *ss)
```

```python
import functools

import jax
import jax.numpy as jnp
from jax import lax
from jax.experimental import pallas as pl
from jax.experimental.pallas import tpu as pltpu

F32 = jnp.float32
BF16 = jnp.bfloat16

EPS = 1e-6
CHUNK = 64
LRU_C = 8.0
LANES = 128
SUBLANES = 8
MIB = 1024 * 1024


def _cparams(semantics, vmem_mib):
    return pltpu.CompilerParams(dimension_semantics=semantics, vmem_limit_bytes=vmem_mib * MIB)


def _const_spec(shape):
    nd = len(shape)
    return pl.BlockSpec(shape, lambda *_: (0,) * nd, pipeline_mode=pl.Buffered(1))


def _dot(a, b):
    return jnp.dot(a, b, preferred_element_type=F32)


def _dot_nt(a, b):
    return lax.dot_general(a, b, (((1,), (1,)), ((), ())), preferred_element_type=F32)


def _dot_tn(a, b):
    return lax.dot_general(a, b, (((0,), (0,)), ((), ())), preferred_element_type=F32)


def _rms_mod(x, g, shift, scale):
    ms = jnp.mean(x * x, axis=-1, keepdims=True)
    return (x * lax.rsqrt(ms + EPS) * g) * (1.0 + scale) + shift


def _softplus(x):
    return jnp.maximum(x, 0.0) + jnp.log1p(jnp.exp(-jnp.abs(x)))


def _gelu_tanh(x):
    return x * (0.5 * (1.0 + jnp.tanh(0.7978845608028654 * (x + 0.044715 * (x * x * x)))))


def _ada_kernel(c_ref, w_ref, b_ref, o_ref):
    c = c_ref[...].astype(BF16)
    w = w_ref[0].astype(BF16)
    o_ref[0] = _dot(c, w) + b_ref[0]


def _ada_call(c_all, ada_w, ada_b):
    depth, d, n = ada_w.shape
    m = c_all.shape[0]
    tn = d
    return pl.pallas_call(
        _ada_kernel,
        grid=(depth, n // tn),
        in_specs=[pl.BlockSpec((m, d), lambda l, j: (0, 0)),
                  pl.BlockSpec((1, d, tn), lambda l, j: (l, 0, j)),
                  pl.BlockSpec((1, 1, tn), lambda l, j: (l, 0, j))],
        out_specs=pl.BlockSpec((1, m, tn), lambda l, j: (l, 0, j)),
        out_shape=jax.ShapeDtypeStruct((depth, m, n), F32),
        compiler_params=_cparams(("arbitrary", "arbitrary"), 32),
        name="ada_proj",
    )(c_all, ada_w, ada_b.reshape(depth, 1, n))


def _ffn_kernel(x_ref, sh_ref, sc_ref, gt_ref, g_ref, wgu_ref, wo_ref, o_ref, h_scr, acc_scr, *, nc, tf):
    x = x_ref[...]
    bb, tl, d = x.shape
    h = _rms_mod(x, g_ref[...], sh_ref[...], sc_ref[...])
    h_scr[...] = h.reshape(bb * tl, d).astype(BF16)
    acc_scr[...] = jnp.zeros_like(acc_scr)

    def body(c, carry):
        gu = _dot(h_scr[...], wgu_ref[c])
        gate = gu[:, :tf]
        act = (gate * jax.nn.sigmoid(gate)) * gu[:, tf:]
        acc_scr[...] += _dot(act.astype(BF16), wo_ref[c])
        return carry

    lax.fori_loop(0, nc, body, 0)
    o_ref[...] = x + (0.5 * gt_ref[...]) * acc_scr[...].reshape(bb, tl, d)


def _ffn_call(x, sh, sc, gt, g, wgu, wo, bb, tl):
    nb, length, d = x.shape
    nc, _, tf2 = wgu.shape
    tm = bb * tl
    x_spec = pl.BlockSpec((bb, tl, d), lambda i, t: (i, t, 0))
    ada_spec = pl.BlockSpec((bb, 1, d), lambda i, t: (i, 0, 0))
    return pl.pallas_call(
        functools.partial(_ffn_kernel, nc=nc, tf=tf2 // 2),
        grid=(nb // bb, length // tl),
        in_specs=[x_spec, ada_spec, ada_spec, ada_spec, _const_spec((1, d)),
                  _const_spec(wgu.shape), _const_spec(wo.shape)],
        out_specs=x_spec,
        out_shape=jax.ShapeDtypeStruct(x.shape, F32),
        scratch_shapes=[pltpu.VMEM((tm, d), BF16), pltpu.VMEM((tm, d), F32)],
        compiler_params=_cparams(("arbitrary", "arbitrary"), 48),
        name="ffn",
    )(x, sh, sc, gt, g, wgu, wo)


def _head_rmsnorm(q, e, g, dhb):
    q2 = q * q
    hi = q2.astype(BF16)
    lo = (q2 - hi.astype(F32)).astype(BF16)
    ss = _dot(hi, e) + _dot(lo, e)
    return q * lax.rsqrt(ss * (1.0 / dhb) + EPS) * g


def _proj_body(x, sh, sc, g, wa_ref, wg_ref, wb_ref, gb_ref, qg_ref, kg_ref, e_ref, *, nh, bw, dhb):
    bb, tl, d = x.shape
    h = _rms_mod(x, g, sh, sc).reshape(bb * tl, d).astype(BF16)
    ua = _dot(h, wa_ref[...])
    gg = _dot(h, wg_ref[...]) + gb_ref[...]
    lane = lax.broadcasted_iota(jnp.int32, gg.shape, 1)
    gates = jnp.where(lane < nh, gg, -_softplus(-gg))
    ub = _dot(h, wb_ref[...])
    e = e_ref[...]
    qn = _head_rmsnorm(ub[:, :bw], e, qg_ref[...], dhb) * (dhb ** -0.5)
    kn = _head_rmsnorm(ub[:, bw:2 * bw], e, kg_ref[...], dhb)
    vb = ub[:, 2 * bw:]
    return ua, gates, qn, kn, vb


def _proj_prompt_kernel(x_ref, sh_ref, sc_ref, g_ref, wa_ref, wg_ref, wb_ref, gb_ref, qg_ref, kg_ref, e_ref,
                        ua_ref, gt_ref, qn_ref, kp_ref, vp_ref, kl_ref, vl_ref, *, nh, bw, dhb):
    t = pl.program_id(1)
    nt = pl.num_programs(1)

    @pl.when(t == 0)
    def _():
        kp_ref[...] = jnp.zeros_like(kp_ref)
        vp_ref[...] = jnp.zeros_like(vp_ref)

    @pl.when(t > 0)
    def _():
        ua, gates, qn, kn, vb = _proj_body(x_ref[...], sh_ref[...], sc_ref[...], g_ref[...], wa_ref, wg_ref, wb_ref,
                                           gb_ref, qg_ref, kg_ref, e_ref, nh=nh, bw=bw, dhb=dhb)
        ua_ref[0] = ua
        gt_ref[0] = gates
        qn_ref[0] = qn.astype(BF16)
        kp_ref[0] = kn.astype(BF16)
        vp_ref[0] = vb.astype(BF16)

        @pl.when(t == nt - 1)
        def _():
            kl_ref[0] = kn
            vl_ref[0] = vb


def _proj_prompt_call(x, sh, sc, g, wa, wg, wb, gbias, qg, kg, e, *, nh, dhb, w):
    nb, length, d = x.shape
    tl = w
    nt = length // tl
    bw = wb.shape[1] // 3
    aw4 = wa.shape[1]
    prev = lambda b, t: (b, jnp.maximum(t - 1, 0), 0)
    ada_spec = pl.BlockSpec((1, 1, d), lambda b, t: (b, 0, 0))
    outs = pl.pallas_call(
        functools.partial(_proj_prompt_kernel, nh=nh, bw=bw, dhb=dhb),
        grid=(nb, nt + 1),
        in_specs=[pl.BlockSpec((1, tl, d), prev), ada_spec, ada_spec, _const_spec((1, d)),
                  _const_spec(wa.shape), _const_spec(wg.shape), _const_spec(wb.shape), _const_spec(gbias.shape),
                  _const_spec(qg.shape), _const_spec(kg.shape), _const_spec(e.shape)],
        out_specs=[pl.BlockSpec((1, tl, aw4), prev),
                   pl.BlockSpec((1, tl, LANES), prev),
                   pl.BlockSpec((1, tl, bw), prev),
                   pl.BlockSpec((1, tl, bw), lambda b, t: (b, t, 0)),
                   pl.BlockSpec((1, tl, bw), lambda b, t: (b, t, 0)),
                   pl.BlockSpec((1, tl, bw), lambda b, t: (b, 0, 0)),
                   pl.BlockSpec((1, tl, bw), lambda b, t: (b, 0, 0))],
        out_shape=[jax.ShapeDtypeStruct((nb, length, aw4), F32),
                   jax.ShapeDtypeStruct((nb, length, LANES), F32),
                   jax.ShapeDtypeStruct((nb, length, bw), BF16),
                   jax.ShapeDtypeStruct((nb, length + w, bw), BF16),
                   jax.ShapeDtypeStruct((nb, length + w, bw), BF16),
                   jax.ShapeDtypeStruct((nb, w, bw), F32),
                   jax.ShapeDtypeStruct((nb, w, bw), F32)],
        compiler_params=_cparams(("arbitrary", "arbitrary"), 48),
        name="proj_prompt",
    )(x, sh, sc, g, wa, wg, wb, gbias, qg, kg, e)
    return outs


def _proj_sample_kernel(x_ref, sh_ref, sc_ref, g_ref, wa_ref, wg_ref, wb_ref, gb_ref, qg_ref, kg_ref, e_ref,
                        ua_ref, gt_ref, qn_ref, kn_ref, vb_ref, *, nh, bw, dhb):
    bb, tl, _ = x_ref.shape
    ua, gates, qn, kn, vb = _proj_body(x_ref[...], sh_ref[...], sc_ref[...], g_ref[...], wa_ref, wg_ref, wb_ref,
                                       gb_ref, qg_ref, kg_ref, e_ref, nh=nh, bw=bw, dhb=dhb)
    ua_ref[...] = ua.reshape(bb, tl, -1)
    gt_ref[...] = gates.reshape(bb, tl, -1)
    qn_ref[...] = qn.reshape(bb, tl, -1).astype(BF16)
    kn_ref[...] = kn.reshape(bb, tl, -1)
    vb_ref[...] = vb.reshape(bb, tl, -1)


def _proj_sample_call(x, sh, sc, g, wa, wg, wb, gbias, qg, kg, e, *, nh, dhb):
    nb, length, d = x.shape
    bw = wb.shape[1] // 3
    aw4 = wa.shape[1]
    full = lambda n: pl.BlockSpec((nb, length, n), lambda i: (0, 0, 0))
    ada_spec = pl.BlockSpec((nb, 1, d), lambda i: (0, 0, 0))
    return pl.pallas_call(
        functools.partial(_proj_sample_kernel, nh=nh, bw=bw, dhb=dhb),
        grid=(1,),
        in_specs=[full(d), ada_spec, ada_spec, _const_spec((1, d)),
                  _const_spec(wa.shape), _const_spec(wg.shape), _const_spec(wb.shape), _const_spec(gbias.shape),
                  _const_spec(qg.shape), _const_spec(kg.shape), _const_spec(e.shape)],
        out_specs=[full(aw4), full(LANES), full(bw), full(bw), full(bw)],
        out_shape=[jax.ShapeDtypeStruct((nb, length, aw4), F32),
                   jax.ShapeDtypeStruct((nb, length, LANES), F32),
                   jax.ShapeDtypeStruct((nb, length, bw), BF16),
                   jax.ShapeDtypeStruct((nb, length, bw), F32),
                   jax.ShapeDtypeStruct((nb, length, bw), F32)],
        compiler_params=_cparams(("arbitrary",), 48),
        name="proj_sample",
    )(x, sh, sc, g, wa, wg, wb, gbias, qg, kg, e)


def _mlstm_kernel(ua_ref, g_ref, c0_ref, n0_ref, m0_ref, go_ref, ha_ref, c_ref, n_ref, m_ref,
                  pv_scr, kv_scr, ks_scr, h_scr, *, seg, nh, dh):
    t = pl.program_id(1)

    @pl.when(t == 0)
    def _():
        c_ref[...] = c0_ref[...]
        n_ref[...] = n0_ref[...]
        m_ref[...] = m0_ref[...]

    tq = ua_ref.shape[1]
    nck = tq // seg
    aw = nh * dh
    gates = g_ref[0]
    pos = lax.broadcasted_iota(jnp.int32, gates.shape, 0) % seg
    bt = gates
    s = 1
    while s < seg:
        bt = bt + jnp.where(pos >= s, pltpu.roll(bt, s, 0), 0.0)
        s *= 2
    dmb = pltpu.roll(gates, nh, 1) - bt
    pm = dmb
    s = 1
    while s < seg:
        pm = jnp.maximum(pm, jnp.where(pos >= s, pltpu.roll(pm, s, 0), -jnp.inf))
        s *= 2
    if tq % LANES:
        dsq = jnp.concatenate([dmb, jnp.zeros((LANES - tq % LANES, LANES), F32)], axis=0)
    else:
        dsq = dmb
    dtr = dsq.T
    ri = lax.broadcasted_iota(jnp.int32, (seg, seg), 0)
    ci = lax.broadcasted_iota(jnp.int32, (seg, seg), 1)
    causal = ri >= ci
    ones = jnp.ones((seg, dh), BF16)

    def cols(jc, h):
        rows = slice(jc * seg, (jc + 1) * seg)
        ln = slice(nh + h, nh + h + 1)
        return rows, bt[rows, ln], dmb[rows, ln], pm[rows, ln]

    for jc in range(nck):
        for h in range(nh):
            g = jc * nh + h
            rows, _, d_col, p_col = cols(jc, h)
            d_row = dtr[nh + h:nh + h + 1, rows]
            q = ua_ref[0, rows, h * dh:(h + 1) * dh].astype(BF16)
            k = ua_ref[0, rows, aw + h * dh:aw + (h + 1) * dh] * (dh ** -0.5)
            v = ua_ref[0, rows, 2 * aw + h * dh:2 * aw + (h + 1) * dh].astype(BF16)
            dloc = jnp.exp(jnp.where(causal, d_row - p_col, -jnp.inf))
            sl = (_dot_nt(q, k.astype(BF16)) * dloc).astype(BF16)
            pv_scr[g] = _dot(sl, jnp.concatenate([v, ones], axis=1))
            kw = k * jnp.exp(d_col - p_col[seg - 1:seg, :])
            kv_scr[g] = _dot_tn(kw.astype(BF16), v)
            ks_scr[g] = jnp.sum(kw, axis=0, keepdims=True)

    state = [(c_ref[0, h], n_ref[0, h:h + 1, :], m_ref[0, h:h + 1, :]) for h in range(nh)]
    for jc in range(nck):
        for h in range(nh):
            c_mem, n, m = state[h]
            g = jc * nh + h
            rows, b_col, _, p_col = cols(jc, h)
            q = ua_ref[0, rows, h * dh:(h + 1) * dh]
            mm = jnp.maximum(m, p_col)
            iw = jnp.exp(m - mm)
            fl = jnp.exp(p_col - mm)
            pv = pv_scr[g]
            num = iw * _dot(q.astype(BF16), c_mem.astype(BF16)) + fl * pv[:, :dh]
            den = iw * jnp.sum(q * n, axis=-1, keepdims=True) + fl * pv[:, dh:]
            h_scr[g] = num / jnp.maximum(jnp.abs(den), jnp.exp(-(b_col + mm)))
            p_last = p_col[seg - 1:seg, :]
            mml = jnp.maximum(m, p_last)
            w_prev = jnp.exp(m - mml)
            f_new = jnp.exp(p_last - mml)
            state[h] = (w_prev * c_mem + f_new * kv_scr[g],
                        w_prev * n + f_new * ks_scr[g],
                        b_col[seg - 1:seg, :] + mml)
    for h in range(nh):
        c_ref[0, h], n_ref[0, h:h + 1, :], m_ref[0, h:h + 1, :] = state[h]

    for jc in range(nck):
        for h in range(nh):
            g = jc * nh + h
            rows = slice(jc * seg, (jc + 1) * seg)
            hh = h_scr[g]
            oa = ua_ref[0, rows, 3 * aw + h * dh:3 * aw + (h + 1) * dh]
            ms = jnp.mean(hh * hh, axis=-1, keepdims=True)
            hn = (hh * lax.rsqrt(ms + EPS) * go_ref[h:h + 1, :]) * jax.nn.sigmoid(oa)
            ha_ref[0, rows, h * dh:(h + 1) * dh] = hn.astype(BF16)


def _mlstm_call(ua, gates, c0, n0, m0, gout, *, tq, seg):
    nb, length, aw4 = ua.shape
    _, nh, dh, _ = c0.shape
    groups = (tq // seg) * nh
    st = lambda shape: pl.BlockSpec((1,) + shape, lambda b, t: (b,) + (0,) * len(shape))
    tile = lambda n: pl.BlockSpec((1, tq, n), lambda b, t: (b, t, 0))
    return pl.pallas_call(
        functools.partial(_mlstm_kernel, seg=seg, nh=nh, dh=dh),
        grid=(nb, length // tq),
        in_specs=[tile(aw4), tile(LANES), st((nh, dh, dh)), st((nh, dh)), st((nh, 1)), _const_spec(gout.shape)],
        out_specs=[tile(nh * dh), st((nh, dh, dh)), st((nh, dh)), st((nh, 1))],
        out_shape=[jax.ShapeDtypeStruct((nb, length, nh * dh), BF16),
                   jax.ShapeDtypeStruct((nb, nh, dh, dh), F32),
                   jax.ShapeDtypeStruct((nb, nh, dh), F32),
                   jax.ShapeDtypeStruct((nb, nh, 1), F32)],
        scratch_shapes=[pltpu.VMEM((groups, seg, 2 * dh), F32), pltpu.VMEM((groups, dh, dh), F32),
                        pltpu.VMEM((groups, 1, dh), F32), pltpu.VMEM((groups, seg, dh), F32)],
        compiler_params=_cparams(("arbitrary", "arbitrary"), 32),
        name="mlstm",
    )(ua, gates, c0, n0, m0, gout)


def _softmax_rows(parts):
    mx = None
    for s in parts:
        pm = jnp.max(s, axis=-1, keepdims=True)
        mx = pm if mx is None else jnp.maximum(mx, pm)
    es = [jnp.exp(s - mx) for s in parts]
    den = None
    for e in es:
        ps = jnp.sum(e, axis=-1, keepdims=True)
        den = ps if den is None else den + ps
    return [e / den for e in es]


def _relbias_kernel(b0_ref, o_ref):
    nhb, nq, nk = o_ref.shape
    for h in range(nhb):
        x = jnp.broadcast_to(b0_ref[h:h + 1, :], (nq, b0_ref.shape[1]))
        o_ref[h] = pltpu.roll(x, 0, 1, stride=1, stride_axis=0)[:, :nk]


def _relbias_call(table, w):
    nhb = table.shape[0]
    max_rel = (table.shape[1] - 1) // 2
    assert CHUNK - 1 <= max_rel <= w
    first = jnp.broadcast_to(table[:, :1], (nhb, w - max_rel))
    wrap = jnp.broadcast_to(table[:, :1], (nhb, CHUNK))
    b0 = jnp.concatenate([first, table[:, :max_rel + CHUNK], wrap], axis=1).astype(F32)
    return pl.pallas_call(
        _relbias_kernel,
        out_shape=jax.ShapeDtypeStruct((nhb, CHUNK, w + CHUNK), F32),
        name="rel_bias",
    )(b0)


def _band_prompt_kernel(q_ref, k_ref, v_ref, bias_ref, o_ref, s_scr, m_scr, e_scr, *, npair, w, nck):
    c4 = pl.program_id(1)
    nk = w + CHUNK
    lane = lax.broadcasted_iota(jnp.int32, (CHUNK, LANES), 1)
    low = lane < LANES // 2
    zero = jnp.zeros((CHUNK, LANES), BF16)
    ones = jnp.ones((nk, LANES), BF16)

    def run(masked):
        starts = [pl.multiple_of((c4 * nck + jc) * CHUNK, CHUNK) for jc in range(nck)]
        for jc in range(nck):
            for p in range(npair):
                g = jc * npair + p
                sl = slice(p * LANES, (p + 1) * LANES)
                qp = q_ref[0, jc * CHUNK:(jc + 1) * CHUNK, sl]
                q2 = jnp.concatenate([jnp.where(low, qp, zero), jnp.where(low, zero, qp)], axis=0)
                s = _dot_nt(q2, k_ref[0, pl.ds(starts[jc], nk), sl]) + bias_ref[p]
                if masked:
                    col = lax.broadcasted_iota(jnp.int32, s.shape, 1)
                    s = jnp.where(col + starts[jc] >= w, s, -jnp.inf)
                s_scr[g] = s
                m_scr[g] = jnp.max(s, axis=-1, keepdims=True)
        for g in range(nck * npair):
            e_scr[g] = jnp.exp(s_scr[g] - m_scr[g]).astype(BF16)
        for jc in range(nck):
            for p in range(npair):
                g = jc * npair + p
                sl = slice(p * LANES, (p + 1) * LANES)
                vx = jnp.concatenate([v_ref[0, pl.ds(starts[jc], nk), sl], ones], axis=1)
                r = _dot(e_scr[g], vx)
                o_lo = r[:CHUNK, :LANES] / r[:CHUNK, LANES:]
                o_hi = r[CHUNK:, :LANES] / r[CHUNK:, LANES:]
                o_ref[0, jc * CHUNK:(jc + 1) * CHUNK, sl] = jnp.where(low, o_lo, o_hi).astype(BF16)

    first_full = w // (CHUNK * nck)

    @pl.when(c4 < first_full)
    def _():
        run(True)

    @pl.when(c4 >= first_full)
    def _():
        run(False)


def _band_prompt_call(qs, kpad, vpad, bias2, *, w, nck):
    nb, length, bw = qs.shape
    npair = bias2.shape[0]
    lp = kpad.shape[1]
    tq = nck * CHUNK
    nk = w + CHUNK
    groups = nck * npair
    assert w % tq == 0
    return pl.pallas_call(
        functools.partial(_band_prompt_kernel, npair=npair, w=w, nck=nck),
        grid=(nb, length // tq),
        in_specs=[pl.BlockSpec((1, tq, bw), lambda b, c: (b, c, 0)),
                  pl.BlockSpec((1, lp, bw), lambda b, c: (b, 0, 0)),
                  pl.BlockSpec((1, lp, bw), lambda b, c: (b, 0, 0)),
                  _const_spec(bias2.shape)],
        out_specs=pl.BlockSpec((1, tq, bw), lambda b, c: (b, c, 0)),
        out_shape=jax.ShapeDtypeStruct((nb, length, bw), BF16),
        scratch_shapes=[pltpu.VMEM((groups, 2 * CHUNK, nk), F32),
                        pltpu.VMEM((groups, 2 * CHUNK, 1), F32),
                        pltpu.VMEM((groups, 2 * CHUNK, nk), BF16)],
        compiler_params=_cparams(("arbitrary", "arbitrary"), 40),
        name="band_prompt",
    )(qs, kpad, vpad, bias2)


def _band_sample_kernel(q_ref, kn_ref, vn_ref, ck_ref, cv_ref, bc_ref, bn_ref, o_ref, *, nhb):
    q = q_ref[0]
    tq = q.shape[0]
    lane = lax.broadcasted_iota(jnp.int32, (tq, LANES), 1)
    low = lane < LANES // 2
    zero = jnp.zeros((tq, LANES), BF16)
    for p in range(nhb // 2):
        sl = slice(p * LANES, (p + 1) * LANES)
        qp = q[:, sl]
        kc = ck_ref[0, :, sl].astype(BF16)
        vc = cv_ref[0, :, sl].astype(BF16)
        kn = kn_ref[0, :, sl].astype(BF16)
        vn = vn_ref[0, :, sl].astype(BF16)
        outs = []
        for sub in range(2):
            h = 2 * p + sub
            qh = jnp.where(low if sub == 0 else ~low, qp, zero)
            s_c = _dot_nt(qh, kc) + bc_ref[h]
            s_n = _dot_nt(qh, kn) + bn_ref[h]
            p_c, p_n = _softmax_rows([s_c, s_n])
            outs.append(_dot(p_c.astype(BF16), vc) + _dot(p_n.astype(BF16), vn))
        o_ref[0, :, sl] = jnp.where(low, outs[0], outs[1]).astype(BF16)


def _band_sample_call(qn, kn, vn, ck, cv, bias_c, bias_n):
    nb, tq, bw = qn.shape
    w = ck.shape[1]
    nhb = bias_c.shape[0]
    new = pl.BlockSpec((1, tq, bw), lambda b: (b, 0, 0))
    cache = pl.BlockSpec((1, w, bw), lambda b: (b, 0, 0))
    return pl.pallas_call(
        functools.partial(_band_sample_kernel, nhb=nhb),
        grid=(nb,),
        in_specs=[new, new, new, cache, cache, _const_spec(bias_c.shape), _const_spec(bias_n.shape)],
        out_specs=new,
        out_shape=jax.ShapeDtypeStruct((nb, tq, bw), BF16),
        compiler_params=_cparams(("arbitrary",), 32),
        name="band_sample",
    )(qn, kn, vn, ck, cv, bias_c, bias_n)


def _mixout_kernel(x_ref, ha_ref, hb_ref, gt_ref, woa_ref, wob_ref, o_ref):
    bb, tl, d = x_ref.shape
    ha = ha_ref[...].reshape(bb * tl, -1)
    hb = hb_ref[...].reshape(bb * tl, -1)
    y = _dot(ha, woa_ref[...]) + _dot(hb, wob_ref[...])
    o_ref[...] = x_ref[...] + gt_ref[...] * y.reshape(bb, tl, d)


def _mixout_call(x, ha, hb, gt, woa, wob, bb, tl):
    nb, length, d = x.shape
    tile = lambda n: pl.BlockSpec((bb, tl, n), lambda i, t: (i, t, 0))
    return pl.pallas_call(
        _mixout_kernel,
        grid=(nb // bb, length // tl),
        in_specs=[tile(d), tile(ha.shape[-1]), tile(hb.shape[-1]),
                  pl.BlockSpec((bb, 1, d), lambda i, t: (i, 0, 0)),
                  _const_spec(woa.shape), _const_spec(wob.shape)],
        out_specs=tile(d),
        out_shape=jax.ShapeDtypeStruct(x.shape, F32),
        compiler_params=_cparams(("arbitrary", "arbitrary"), 32),
        name="mix_out",
    )(x, ha, hb, gt, woa, wob)


def _rglru_gates(xc, gw_ref, rb, ib, lam, nblk):
    bwc = xc.shape[1] // nblk
    r_parts, i_parts = [], []
    for n in range(nblk):
        gn = _dot(xc[:, n * bwc:(n + 1) * bwc].astype(BF16), gw_ref[n])
        r_parts.append(gn[:, :bwc])
        i_parts.append(gn[:, bwc:])
    r = jax.nn.sigmoid(jnp.concatenate(r_parts, axis=1) + rb)
    ii = jax.nn.sigmoid(jnp.concatenate(i_parts, axis=1) + ib)
    log_a = (-LRU_C * r) * _softplus(-lam)
    a = jnp.exp(log_a)
    th = jnp.tanh(log_a)
    upd = jnp.sqrt(-2.0 * th / (1.0 - th)) * (ii * xc)
    return a, upd


def _rglru_prompt_kernel(x_ref, sh_ref, sc_ref, gt_ref, g_ref, win_ref, cw_ref, cb_ref, gw_ref, rb_ref, ib_ref,
                         lam_ref, wout_ref, conv0_ref, h0_ref, o_ref, conv_ref, hl_ref, xp_scr, a_scr, b_scr, *, nblk):
    t = pl.program_id(1)
    tq, d = x_ref.shape[1], x_ref.shape[2]
    r_w = lam_ref.shape[1]
    ncw = cw_ref.shape[0]

    @pl.when(t == 0)
    def _():
        xp_scr[0:SUBLANES, :] = conv0_ref[0]
        hl_ref[...] = h0_ref[...]

    x = x_ref[0]
    hm = _rms_mod(x, g_ref[...], sh_ref[0], sc_ref[0]).astype(BF16)
    u = _dot(hm, win_ref[...])
    gb = u[:, :r_w]
    xp_scr[SUBLANES:SUBLANES + tq, :] = u[:, r_w:]
    xc = cb_ref[...]
    for j in range(ncw):
        off = SUBLANES - (ncw - 1 - j)
        xc = xc + xp_scr[off:off + tq, :] * cw_ref[j:j + 1, :]
    conv_ref[0] = xp_scr[tq:tq + SUBLANES, :]
    xp_scr[0:SUBLANES, :] = xp_scr[tq:tq + SUBLANES, :]
    a, upd = _rglru_gates(xc, gw_ref, rb_ref[...], ib_ref[...], lam_ref[...], nblk)
    a_scr[...] = a
    b_scr[...] = upd
    row8 = lax.broadcasted_iota(jnp.int32, (SUBLANES, r_w), 0)

    def scan_body(i, h):
        rows = pl.ds(pl.multiple_of(i * SUBLANES, SUBLANES), SUBLANES)
        ai = a_scr[rows, :]
        bi = b_scr[rows, :]
        s = 1
        while s < SUBLANES:
            m = row8 >= s
            bi = jnp.where(m, ai * pltpu.roll(bi, s, 0) + bi, bi)
            ai = jnp.where(m, ai * pltpu.roll(ai, s, 0), ai)
            s *= 2
        hs = ai * h + bi
        a_scr[rows, :] = hs
        return hs[SUBLANES - 1:SUBLANES, :]

    h_fin = lax.fori_loop(0, tq // SUBLANES, scan_body, hl_ref[0])
    hl_ref[0] = h_fin
    y = _dot((_gelu_tanh(gb) * a_scr[...]).astype(BF16), wout_ref[...])
    o_ref[0] = x + gt_ref[0] * y


def _rglru_prompt_call(x, sh, sc, gt, g, win, cw, cb, gw, rb, ib, lam, wout, conv0, h0, *, tq):
    nb, length, d = x.shape
    r_w = lam.shape[1]
    nblk = gw.shape[0]
    ada_spec = pl.BlockSpec((1, 1, d), lambda b, t: (b, 0, 0))
    tile = pl.BlockSpec((1, tq, d), lambda b, t: (b, t, 0))
    conv_spec = pl.BlockSpec((1, SUBLANES, r_w), lambda b, t: (b, 0, 0))
    h_spec = pl.BlockSpec((1, 1, r_w), lambda b, t: (b, 0, 0))
    consts = [g, win, cw, cb, gw, rb, ib, lam, wout]
    return pl.pallas_call(
        functools.partial(_rglru_prompt_kernel, nblk=nblk),
        grid=(nb, length // tq),
        in_specs=[tile, ada_spec, ada_spec, ada_spec] + [_const_spec(a.shape) for a in consts] + [conv_spec, h_spec],
        out_specs=[tile, conv_spec, h_spec],
        out_shape=[jax.ShapeDtypeStruct(x.shape, F32),
                   jax.ShapeDtypeStruct((nb, SUBLANES, r_w), F32),
                   jax.ShapeDtypeStruct((nb, 1, r_w), F32)],
        scratch_shapes=[pltpu.VMEM((tq + SUBLANES, r_w), F32), pltpu.VMEM((tq, r_w), F32), pltpu.VMEM((tq, r_w), F32)],
        compiler_params=_cparams(("arbitrary", "arbitrary"), 48),
        name="rglru_prompt",
    )(x, sh, sc, gt, *consts, conv0, h0)


def _rglru_sample_kernel(x_ref, sh_ref, sc_ref, gt_ref, g_ref, win_ref, cw_ref, cb_ref, gw_ref, rb_ref, ib_ref,
                         lam_ref, wout_ref, conv0_ref, h0_ref, o_ref, conv_ref, hl_ref, xp_scr, *, nblk):
    bb, tl, d = x_ref.shape
    tm = bb * tl
    r_w = lam_ref.shape[1]
    ncw = cw_ref.shape[0]
    x = x_ref[...]
    hm = _rms_mod(x, g_ref[...], sh_ref[...], sc_ref[...]).reshape(tm, d).astype(BF16)
    u = _dot(hm, win_ref[...])
    gb = u[:, :r_w]
    xp_scr[:, 0:SUBLANES, :] = conv0_ref[...]
    xp_scr[:, SUBLANES:SUBLANES + tl, :] = u[:, r_w:].reshape(bb, tl, r_w)
    xc = jnp.broadcast_to(cb_ref[...], (bb, tl, r_w))
    for j in range(ncw):
        off = SUBLANES - (ncw - 1 - j)
        xc = xc + xp_scr[:, off:off + tl, :] * cw_ref[j:j + 1, :]
    conv_ref[...] = xp_scr[:, tl:tl + SUBLANES, :]
    a, b = _rglru_gates(xc.reshape(tm, r_w), gw_ref, rb_ref[...], ib_ref[...], lam_ref[...], nblk)
    pos = lax.broadcasted_iota(jnp.int32, (tm, r_w), 0) % tl
    s = 1
    while s < tl:
        m = pos >= s
        b = jnp.where(m, a * pltpu.roll(b, s, 0) + b, b)
        a = jnp.where(m, a * pltpu.roll(a, s, 0), a)
        s *= 2
    hs = a.reshape(bb, tl, r_w) * h0_ref[...] + b.reshape(bb, tl, r_w)
    hl_ref[...] = hs[:, tl - 1:tl, :]
    y = _dot((_gelu_tanh(gb) * hs.reshape(tm, r_w)).astype(BF16), wout_ref[...])
    o_ref[...] = x + gt_ref[...] * y.reshape(bb, tl, d)


def _rglru_sample_call(x, sh, sc, gt, g, win, cw, cb, gw, rb, ib, lam, wout, conv0, h0):
    nb, tl, d = x.shape
    r_w = lam.shape[1]
    nblk = gw.shape[0]
    full = lambda a, b: pl.BlockSpec((nb, a, b), lambda i: (0, 0, 0))
    consts = [g, win, cw, cb, gw, rb, ib, lam, wout]
    return pl.pallas_call(
        functools.partial(_rglru_sample_kernel, nblk=nblk),
        grid=(1,),
        in_specs=[full(tl, d), full(1, d), full(1, d), full(1, d)] + [_const_spec(a.shape) for a in consts]
                 + [full(SUBLANES, r_w), full(1, r_w)],
        out_specs=[full(tl, d), full(SUBLANES, r_w), full(1, r_w)],
        out_shape=[jax.ShapeDtypeStruct(x.shape, F32),
                   jax.ShapeDtypeStruct((nb, SUBLANES, r_w), F32),
                   jax.ShapeDtypeStruct((nb, 1, r_w), F32)],
        scratch_shapes=[pltpu.VMEM((nb, tl + SUBLANES, r_w), F32)],
        compiler_params=_cparams(("arbitrary",), 48),
        name="rglru_sample",
    )(x, sh, sc, gt, *consts, conv0, h0)


FFN_TF = 256


def _prep_ffn(w_in, w_out):
    d, two_ff = w_in.shape
    dff = two_ff // 2
    nc = dff // FFN_TF
    wg = w_in[:, :dff].reshape(d, nc, FFN_TF)
    wu = w_in[:, dff:].reshape(d, nc, FFN_TF)
    wgu = jnp.concatenate([wg, wu], axis=-1).transpose(1, 0, 2).astype(BF16)
    wo = w_out.reshape(nc, FFN_TF, d).astype(BF16)
    return wgu, wo


def _pad_rows_front(a, rows):
    return jnp.pad(a, ((0, 0), (rows - a.shape[1], 0), (0, 0)))


def kernel(x_prompt, x_sample, state_a_C, state_a_n, state_a_m, cache_b_k, cache_b_v, state_c_conv, state_c_h,
           c_prompt, c_sample, ffn1_norm, ffn1_w_in, ffn1_w_out, mix_norm, ffn2_norm, ffn2_w_in, ffn2_w_out,
           ada_w, ada_b, ab_w_in, ab_gate_bias, a_out_norm, b_q_norm, b_k_norm, b_rel_bias, ab_w_out,
           c_w_in, c_conv_w, c_conv_b, c_gate_w, c_gate_b, c_lambda, c_w_out):
    nbp, seq, d = x_prompt.shape
    nbs, tdec, _ = x_sample.shape
    depth = ada_w.shape[0]
    n_ada = ada_w.shape[2] // d
    _, _, nh, dh, _ = state_a_C.shape
    _, _, w_band, nhb, dhb = cache_b_k.shape
    aw, bw = nh * dh, nhb * dhb
    ncw = c_conv_w.shape[1]
    assert 2 * dhb == LANES and dh == LANES and w_band % CHUNK == 0 and seq % w_band == 0

    ada = _ada_call(jnp.concatenate([c_prompt, c_sample], axis=0), ada_w, ada_b)
    ada = ada.reshape(depth, nbp + nbs, n_ada, 1, d)
    ada_p = [[ada[l, :nbp, k] for k in range(n_ada)] for l in range(depth)]
    ada_s = [[ada[l, nbp:, k] for k in range(n_ada)] for l in range(depth)]

    ffn1 = [_prep_ffn(ffn1_w_in[l], ffn1_w_out[l]) for l in range(depth)]
    ffn2 = [_prep_ffn(ffn2_w_in[l], ffn2_w_out[l]) for l in range(depth)]

    tl_p = 512
    xp, xs = x_prompt, x_sample
    outs_p, outs_s = {}, {}
    for l in range(depth):
        ap, as_ = ada_p[l], ada_s[l]
        i = l // 2
        g1 = ffn1_norm[l].reshape(1, d)
        gm = mix_norm[l].reshape(1, d)
        g2 = ffn2_norm[l].reshape(1, d)
        xp = _ffn_call(xp, ap[0], ap[1], ap[2], g1, *ffn1[l], 1, tl_p)
        xs = _ffn_call(xs, as_[0], as_[1], as_[2], g1, *ffn1[l], nbs, tdec)
        if l % 2 == 0:
            w_in = ab_w_in[i]
            wa = w_in[:, :4 * aw].astype(BF16)
            wg = jnp.pad(w_in[:, 4 * aw:4 * aw + 2 * nh], ((0, 0), (0, LANES - 2 * nh))).astype(BF16)
            wb = w_in[:, 4 * aw + 2 * nh:].astype(BF16)
            gbias = jnp.pad(ab_gate_bias[i], (0, LANES - 2 * nh)).reshape(1, LANES)
            qg = jnp.tile(b_q_norm[i], nhb).reshape(1, bw)
            kg = jnp.tile(b_k_norm[i], nhb).reshape(1, bw)
            head = jnp.arange(bw) // dhb
            e = (head[:, None] == head[None, :]).astype(BF16)
            woa = ab_w_out[i][:aw].astype(BF16)
            wob = ab_w_out[i][aw:].astype(BF16)
            gout = a_out_norm[i]
            bias = _relbias_call(b_rel_bias[i], w_band)
            bias2 = bias.reshape(nhb // 2, 2 * CHUNK, w_band + CHUNK)

            ua, gts, qn, kpad, vpad, klast, vlast = _proj_prompt_call(
                xp, ap[3], ap[4], gm, wa, wg, wb, gbias, qg, kg, e, nh=nh, dhb=dhb, w=w_band)
            zc = jnp.zeros((nbp, nh, dh, dh), F32)
            ha, c1, n1, m1 = _mlstm_call(ua, gts, zc, zc[:, :, 0], zc[:, :, 0, :1], gout, tq=256, seg=CHUNK)
            hb = _band_prompt_call(qn, kpad, vpad, bias2, w=w_band, nck=4)
            xp = _mixout_call(xp, ha, hb, ap[5], woa, wob, 1, tl_p)
            outs_p.setdefault('a_C', []).append(c1)
            outs_p.setdefault('a_n', []).append(n1)
            outs_p.setdefault('a_m', []).append(m1.reshape(nbp, nh))
            outs_p.setdefault('b_k', []).append(klast.reshape(nbp, w_band, nhb, dhb))
            outs_p.setdefault('b_v', []).append(vlast.reshape(nbp, w_band, nhb, dhb))

            ua, gts, qn, kn, vn = _proj_sample_call(
                xs, as_[3], as_[4], gm, wa, wg, wb, gbias, qg, kg, e, nh=nh, dhb=dhb)
            ha, c1, n1, m1 = _mlstm_call(ua, gts, state_a_C[i], state_a_n[i], state_a_m[i][..., None], gout,
                                         tq=tdec, seg=tdec)
            hb = _band_sample_call(qn, kn, vn, cache_b_k[i].reshape(nbs, w_band, bw),
                                   cache_b_v[i].reshape(nbs, w_band, bw),
                                   bias[:, :tdec, :w_band], bias[:, :tdec, w_band:w_band + tdec])
            xs = _mixout_call(xs, ha, hb, as_[5], woa, wob, nbs, tdec)
            outs_s.setdefault('a_C', []).append(c1)
            outs_s.setdefault('a_n', []).append(n1)
            outs_s.setdefault('a_m', []).append(m1.reshape(nbs, nh))
            outs_s.setdefault('b_k', []).append(kn.reshape(nbs, tdec, nhb, dhb))
            outs_s.setdefault('b_v', []).append(vn.reshape(nbs, tdec, nhb, dhb))
        else:
            r_w = c_lambda.shape[1]
            consts = (gm, c_w_in[i].astype(BF16), c_conv_w[i], c_conv_b[i].reshape(1, r_w), c_gate_w[i].astype(BF16),
                      c_gate_b[i][0].reshape(1, r_w), c_gate_b[i][1].reshape(1, r_w), c_lambda[i].reshape(1, r_w),
                      c_w_out[i].astype(BF16))
            xp, conv_p, h_p = _rglru_prompt_call(
                xp, ap[3], ap[4], ap[5], *consts,
                jnp.zeros((nbp, SUBLANES, r_w), F32), jnp.zeros((nbp, 1, r_w), F32), tq=256)
            xs, conv_s, h_s = _rglru_sample_call(
                xs, as_[3], as_[4], as_[5], *consts,
                _pad_rows_front(state_c_conv[i], SUBLANES), state_c_h[i][:, None, :])
            outs_p.setdefault('c_conv', []).append(conv_p[:, SUBLANES - (ncw - 1):])
            outs_p.setdefault('c_h', []).append(h_p[:, 0])
            outs_s.setdefault('c_conv', []).append(conv_s[:, SUBLANES - (ncw - 1):])
            outs_s.setdefault('c_h', []).append(h_s[:, 0])
        xp = _ffn_call(xp, ap[6], ap[7], ap[8], g2, *ffn2[l], 1, tl_p)
        xs = _ffn_call(xs, as_[6], as_[7], as_[8], g2, *ffn2[l], nbs, tdec)

    names = ('a_C', 'a_n', 'a_m', 'b_k', 'b_v', 'c_conv', 'c_h')
    ps = [jnp.stack(outs_p[n], axis=0) for n in names]
    ss = [jnp.stack(outs_s[n], axis=0) for n in names]
    return (xp, xs, *ps, *ss)
```

```python
import functools

import jax
import jax.numpy as jnp
from jax import lax
from jax.experimental import pallas as pl
from jax.experimental.pallas import tpu as pltpu

F32 = jnp.float32
BF16 = jnp.bfloat16

EPS = 1e-6
CHUNK = 64
LRU_C = 8.0
LANES = 128
SUBLANES = 8
MIB = 1024 * 1024


def _cparams(semantics, vmem_mib):
    return pltpu.CompilerParams(dimension_semantics=semantics, vmem_limit_bytes=vmem_mib * MIB)


def _const_spec(shape):
    nd = len(shape)
    return pl.BlockSpec(shape, lambda *_: (0,) * nd, pipeline_mode=pl.Buffered(1))


def _dot(a, b):
    return jnp.dot(a, b, preferred_element_type=F32)


def _dot_nt(a, b):
    return lax.dot_general(a, b, (((1,), (1,)), ((), ())), preferred_element_type=F32)


def _dot_tn(a, b):
    return lax.dot_general(a, b, (((0,), (0,)), ((), ())), preferred_element_type=F32)


def _rms_mod(x, g, shift, scale):
    ms = jnp.mean(x * x, axis=-1, keepdims=True)
    return (x * lax.rsqrt(ms + EPS) * g) * (1.0 + scale) + shift


def _softplus(x):
    return jnp.maximum(x, 0.0) + jnp.log1p(jnp.exp(-jnp.abs(x)))


def _gelu_tanh(x):
    return x * (0.5 * (1.0 + jnp.tanh(0.7978845608028654 * (x + 0.044715 * (x * x * x)))))


def _ada_kernel(c_ref, w_ref, b_ref, o_ref):
    c = c_ref[...].astype(BF16)
    w = w_ref[0].astype(BF16)
    o_ref[0] = _dot(c, w) + b_ref[0]


def _ada_call(c_all, ada_w, ada_b):
    depth, d, n = ada_w.shape
    m = c_all.shape[0]
    tn = d
    return pl.pallas_call(
        _ada_kernel,
        grid=(depth, n // tn),
        in_specs=[pl.BlockSpec((m, d), lambda l, j: (0, 0)),
                  pl.BlockSpec((1, d, tn), lambda l, j: (l, 0, j)),
                  pl.BlockSpec((1, 1, tn), lambda l, j: (l, 0, j))],
        out_specs=pl.BlockSpec((1, m, tn), lambda l, j: (l, 0, j)),
        out_shape=jax.ShapeDtypeStruct((depth, m, n), F32),
        compiler_params=_cparams(("arbitrary", "arbitrary"), 32),
        name="ada_proj",
    )(c_all, ada_w, ada_b.reshape(depth, 1, n))


FFN_TF = 256


def _ffn_kernel(x_ref, sh_ref, sc_ref, gt_ref, g_ref, win_ref, wo_ref, o_ref, act_scr):
    x = x_ref[...]
    bb, tl, d = x.shape
    dff = wo_ref.shape[0]
    h = _rms_mod(x, g_ref[...], sh_ref[...], sc_ref[...]).reshape(bb * tl, d).astype(BF16)
    for c0 in range(0, dff, FFN_TF):
        gate = _dot(h, win_ref[:, c0:c0 + FFN_TF])
        up = _dot(h, win_ref[:, dff + c0:dff + c0 + FFN_TF])
        act_scr[:, c0:c0 + FFN_TF] = ((gate * jax.nn.sigmoid(gate)) * up).astype(BF16)
    y = _dot(act_scr[...], wo_ref[...])
    o_ref[...] = x + (0.5 * gt_ref[...]) * y.reshape(bb, tl, d)


def _ffn_call(x, sh, sc, gt, g, win, wo, bb, tl):
    nb, length, d = x.shape
    dff = wo.shape[0]
    assert dff % FFN_TF == 0
    tm = bb * tl
    x_spec = pl.BlockSpec((bb, tl, d), lambda i, t: (i, t, 0))
    ada_spec = pl.BlockSpec((bb, 1, d), lambda i, t: (i, 0, 0))
    return pl.pallas_call(
        _ffn_kernel,
        grid=(nb // bb, length // tl),
        in_specs=[x_spec, ada_spec, ada_spec, ada_spec, _const_spec((1, d)),
                  _const_spec(win.shape), _const_spec(wo.shape)],
        out_specs=x_spec,
        out_shape=jax.ShapeDtypeStruct(x.shape, F32),
        scratch_shapes=[pltpu.VMEM((tm, dff), BF16)],
        compiler_params=_cparams(("arbitrary", "arbitrary"), 48),
        name="ffn",
    )(x, sh, sc, gt, g, win, wo)


def _head_rmsnorm(q, e, g, dhb):
    q2 = q * q
    hi = q2.astype(BF16)
    lo = (q2 - hi.astype(F32)).astype(BF16)
    ss = _dot(hi, e) + _dot(lo, e)
    return q * lax.rsqrt(ss * (1.0 / dhb) + EPS) * g


def _proj_body(x, sh, sc, g, wa_ref, wg_ref, wb_ref, gb_ref, qg_ref, kg_ref, e_ref, *, nh, bw, dhb):
    bb, tl, d = x.shape
    h = _rms_mod(x, g, sh, sc).reshape(bb * tl, d).astype(BF16)
    ua = _dot(h, wa_ref[...])
    gg = _dot(h, wg_ref[...]) + gb_ref[...]
    lane = lax.broadcasted_iota(jnp.int32, gg.shape, 1)
    gates = jnp.where(lane < nh, gg, -_softplus(-gg))
    ub = _dot(h, wb_ref[...])
    e = e_ref[...]
    qn = _head_rmsnorm(ub[:, :bw], e, qg_ref[...], dhb) * (dhb ** -0.5)
    kn = _head_rmsnorm(ub[:, bw:2 * bw], e, kg_ref[...], dhb)
    vb = ub[:, 2 * bw:]
    return ua, gates, qn, kn, vb


def _proj_prompt_kernel(x_ref, sh_ref, sc_ref, g_ref, wa_ref, wg_ref, wb_ref, gb_ref, qg_ref, kg_ref, e_ref,
                        ua_ref, gt_ref, qn_ref, kp_ref, vp_ref, kl_ref, vl_ref, *, nh, bw, dhb):
    t = pl.program_id(1)
    nt = pl.num_programs(1)

    @pl.when(t == 0)
    def _():
        kp_ref[...] = jnp.zeros_like(kp_ref)
        vp_ref[...] = jnp.zeros_like(vp_ref)

    @pl.when(t > 0)
    def _():
        ua, gates, qn, kn, vb = _proj_body(x_ref[...], sh_ref[...], sc_ref[...], g_ref[...], wa_ref, wg_ref, wb_ref,
                                           gb_ref, qg_ref, kg_ref, e_ref, nh=nh, bw=bw, dhb=dhb)
        ua_ref[0] = ua
        gt_ref[0] = gates
        qn_ref[0] = qn.astype(BF16)
        kp_ref[0] = kn.astype(BF16)
        vp_ref[0] = vb.astype(BF16)

        @pl.when(t == nt - 1)
        def _():
            kl_ref[0] = kn
            vl_ref[0] = vb


def _proj_prompt_call(x, sh, sc, g, wa, wg, wb, gbias, qg, kg, e, *, nh, dhb, w):
    nb, length, d = x.shape
    tl = w
    nt = length // tl
    bw = wb.shape[1] // 3
    aw4 = wa.shape[1]
    prev = lambda b, t: (b, jnp.maximum(t - 1, 0), 0)
    ada_spec = pl.BlockSpec((1, 1, d), lambda b, t: (b, 0, 0))
    outs = pl.pallas_call(
        functools.partial(_proj_prompt_kernel, nh=nh, bw=bw, dhb=dhb),
        grid=(nb, nt + 1),
        in_specs=[pl.BlockSpec((1, tl, d), prev), ada_spec, ada_spec, _const_spec((1, d)),
                  _const_spec(wa.shape), _const_spec(wg.shape), _const_spec(wb.shape), _const_spec(gbias.shape),
                  _const_spec(qg.shape), _const_spec(kg.shape), _const_spec(e.shape)],
        out_specs=[pl.BlockSpec((1, tl, aw4), prev),
                   pl.BlockSpec((1, tl, LANES), prev),
                   pl.BlockSpec((1, tl, bw), prev),
                   pl.BlockSpec((1, tl, bw), lambda b, t: (b, t, 0)),
                   pl.BlockSpec((1, tl, bw), lambda b, t: (b, t, 0)),
                   pl.BlockSpec((1, tl, bw), lambda b, t: (b, 0, 0)),
                   pl.BlockSpec((1, tl, bw), lambda b, t: (b, 0, 0))],
        out_shape=[jax.ShapeDtypeStruct((nb, length, aw4), F32),
                   jax.ShapeDtypeStruct((nb, length, LANES), F32),
                   jax.ShapeDtypeStruct((nb, length, bw), BF16),
                   jax.ShapeDtypeStruct((nb, length + w, bw), BF16),
                   jax.ShapeDtypeStruct((nb, length + w, bw), BF16),
                   jax.ShapeDtypeStruct((nb, w, bw), F32),
                   jax.ShapeDtypeStruct((nb, w, bw), F32)],
        compiler_params=_cparams(("arbitrary", "arbitrary"), 48),
        name="proj_prompt",
    )(x, sh, sc, g, wa, wg, wb, gbias, qg, kg, e)
    return outs


def _proj_sample_kernel(x_ref, sh_ref, sc_ref, g_ref, wa_ref, wg_ref, wb_ref, gb_ref, qg_ref, kg_ref, e_ref,
                        ua_ref, gt_ref, qn_ref, kn_ref, vb_ref, *, nh, bw, dhb):
    bb, tl, _ = x_ref.shape
    ua, gates, qn, kn, vb = _proj_body(x_ref[...], sh_ref[...], sc_ref[...], g_ref[...], wa_ref, wg_ref, wb_ref,
                                       gb_ref, qg_ref, kg_ref, e_ref, nh=nh, bw=bw, dhb=dhb)
    ua_ref[...] = ua.reshape(bb, tl, -1)
    gt_ref[...] = gates.reshape(bb, tl, -1)
    qn_ref[...] = qn.reshape(bb, tl, -1).astype(BF16)
    kn_ref[...] = kn.reshape(bb, tl, -1)
    vb_ref[...] = vb.reshape(bb, tl, -1)


def _proj_sample_call(x, sh, sc, g, wa, wg, wb, gbias, qg, kg, e, *, nh, dhb):
    nb, length, d = x.shape
    bw = wb.shape[1] // 3
    aw4 = wa.shape[1]
    full = lambda n: pl.BlockSpec((nb, length, n), lambda i: (0, 0, 0))
    ada_spec = pl.BlockSpec((nb, 1, d), lambda i: (0, 0, 0))
    return pl.pallas_call(
        functools.partial(_proj_sample_kernel, nh=nh, bw=bw, dhb=dhb),
        grid=(1,),
        in_specs=[full(d), ada_spec, ada_spec, _const_spec((1, d)),
                  _const_spec(wa.shape), _const_spec(wg.shape), _const_spec(wb.shape), _const_spec(gbias.shape),
                  _const_spec(qg.shape), _const_spec(kg.shape), _const_spec(e.shape)],
        out_specs=[full(aw4), full(LANES), full(bw), full(bw), full(bw)],
        out_shape=[jax.ShapeDtypeStruct((nb, length, aw4), F32),
                   jax.ShapeDtypeStruct((nb, length, LANES), F32),
                   jax.ShapeDtypeStruct((nb, length, bw), BF16),
                   jax.ShapeDtypeStruct((nb, length, bw), F32),
                   jax.ShapeDtypeStruct((nb, length, bw), F32)],
        compiler_params=_cparams(("arbitrary",), 48),
        name="proj_sample",
    )(x, sh, sc, g, wa, wg, wb, gbias, qg, kg, e)


def _mlstm_kernel(ua_ref, g_ref, c0_ref, n0_ref, m0_ref, go_ref, ha_ref, c_ref, n_ref, m_ref,
                  pv_scr, kv_scr, ks_scr, h_scr, *, seg, nh, dh):
    t = pl.program_id(1)

    @pl.when(t == 0)
    def _():
        c_ref[...] = c0_ref[...]
        n_ref[...] = n0_ref[...]
        m_ref[...] = m0_ref[...]

    tq = ua_ref.shape[1]
    nck = tq // seg
    aw = nh * dh
    gates = g_ref[0]
    pos = lax.broadcasted_iota(jnp.int32, gates.shape, 0) % seg
    bt = gates
    s = 1
    while s < seg:
        bt = bt + jnp.where(pos >= s, pltpu.roll(bt, s, 0), 0.0)
        s *= 2
    dmb = pltpu.roll(gates, nh, 1) - bt
    pm = dmb
    s = 1
    while s < seg:
        pm = jnp.maximum(pm, jnp.where(pos >= s, pltpu.roll(pm, s, 0), -jnp.inf))
        s *= 2
    if tq % LANES:
        dsq = jnp.concatenate([dmb, jnp.zeros((LANES - tq % LANES, LANES), F32)], axis=0)
    else:
        dsq = dmb
    dtr = dsq.T
    ri = lax.broadcasted_iota(jnp.int32, (seg, seg), 0)
    ci = lax.broadcasted_iota(jnp.int32, (seg, seg), 1)
    causal = ri >= ci
    ones = jnp.ones((seg, dh), BF16)

    def cols(jc, h):
        rows = slice(jc * seg, (jc + 1) * seg)
        ln = slice(nh + h, nh + h + 1)
        return rows, bt[rows, ln], dmb[rows, ln], pm[rows, ln]

    for jc in range(nck):
        for h in range(nh):
            g = jc * nh + h
            rows, _, d_col, p_col = cols(jc, h)
            d_row = dtr[nh + h:nh + h + 1, rows]
            q = ua_ref[0, rows, h * dh:(h + 1) * dh].astype(BF16)
            k = ua_ref[0, rows, aw + h * dh:aw + (h + 1) * dh] * (dh ** -0.5)
            v = ua_ref[0, rows, 2 * aw + h * dh:2 * aw + (h + 1) * dh].astype(BF16)
            dloc = jnp.exp(jnp.where(causal, d_row - p_col, -jnp.inf))
            sl = (_dot_nt(q, k.astype(BF16)) * dloc).astype(BF16)
            pv_scr[g] = _dot(sl, jnp.concatenate([v, ones], axis=1))
            kw = k * jnp.exp(d_col - p_col[seg - 1:seg, :])
            kv_scr[g] = _dot_tn(kw.astype(BF16), v)
            ks_scr[g] = jnp.sum(kw, axis=0, keepdims=True)

    state = [(c_ref[0, h], n_ref[0, h:h + 1, :], m_ref[0, h:h + 1, :]) for h in range(nh)]
    for jc in range(nck):
        for h in range(nh):
            c_mem, n, m = state[h]
            g = jc * nh + h
            rows, b_col, _, p_col = cols(jc, h)
            q = ua_ref[0, rows, h * dh:(h + 1) * dh]
            mm = jnp.maximum(m, p_col)
            iw = jnp.exp(m - mm)
            fl = jnp.exp(p_col - mm)
            pv = pv_scr[g]
            num = iw * _dot(q.astype(BF16), c_mem.astype(BF16)) + fl * pv[:, :dh]
            den = iw * jnp.sum(q * n, axis=-1, keepdims=True) + fl * pv[:, dh:]
            h_scr[g] = num / jnp.maximum(jnp.abs(den), jnp.exp(-(b_col + mm)))
            p_last = p_col[seg - 1:seg, :]
            mml = jnp.maximum(m, p_last)
            w_prev = jnp.exp(m - mml)
            f_new = jnp.exp(p_last - mml)
            state[h] = (w_prev * c_mem + f_new * kv_scr[g],
                        w_prev * n + f_new * ks_scr[g],
                        b_col[seg - 1:seg, :] + mml)
    for h in range(nh):
        c_ref[0, h], n_ref[0, h:h + 1, :], m_ref[0, h:h + 1, :] = state[h]

    for jc in range(nck):
        for h in range(nh):
            g = jc * nh + h
            rows = slice(jc * seg, (jc + 1) * seg)
            hh = h_scr[g]
            oa = ua_ref[0, rows, 3 * aw + h * dh:3 * aw + (h + 1) * dh]
            ms = jnp.mean(hh * hh, axis=-1, keepdims=True)
            hn = (hh * lax.rsqrt(ms + EPS) * go_ref[h:h + 1, :]) * jax.nn.sigmoid(oa)
            ha_ref[0, rows, h * dh:(h + 1) * dh] = hn.astype(BF16)


def _mlstm_call(ua, gates, c0, n0, m0, gout, *, tq, seg):
    nb, length, aw4 = ua.shape
    _, nh, dh, _ = c0.shape
    groups = (tq // seg) * nh
    st = lambda shape: pl.BlockSpec((1,) + shape, lambda b, t: (b,) + (0,) * len(shape))
    tile = lambda n: pl.BlockSpec((1, tq, n), lambda b, t: (b, t, 0))
    return pl.pallas_call(
        functools.partial(_mlstm_kernel, seg=seg, nh=nh, dh=dh),
        grid=(nb, length // tq),
        in_specs=[tile(aw4), tile(LANES), st((nh, dh, dh)), st((nh, dh)), st((nh, 1)), _const_spec(gout.shape)],
        out_specs=[tile(nh * dh), st((nh, dh, dh)), st((nh, dh)), st((nh, 1))],
        out_shape=[jax.ShapeDtypeStruct((nb, length, nh * dh), BF16),
                   jax.ShapeDtypeStruct((nb, nh, dh, dh), F32),
                   jax.ShapeDtypeStruct((nb, nh, dh), F32),
                   jax.ShapeDtypeStruct((nb, nh, 1), F32)],
        scratch_shapes=[pltpu.VMEM((groups, seg, 2 * dh), F32), pltpu.VMEM((groups, dh, dh), F32),
                        pltpu.VMEM((groups, 1, dh), F32), pltpu.VMEM((groups, seg, dh), F32)],
        compiler_params=_cparams(("arbitrary", "arbitrary"), 32),
        name="mlstm",
    )(ua, gates, c0, n0, m0, gout)


def _relbias_kernel(b0_ref, o_ref):
    nhb, nq, nk = o_ref.shape
    for h in range(nhb):
        x = jnp.broadcast_to(b0_ref[h:h + 1, :], (nq, b0_ref.shape[1]))
        o_ref[h] = pltpu.roll(x, 0, 1, stride=1, stride_axis=0)[:, :nk]


def _relbias_call(table, w):
    nhb = table.shape[0]
    max_rel = (table.shape[1] - 1) // 2
    assert CHUNK - 1 <= max_rel <= w
    first = jnp.broadcast_to(table[:, :1], (nhb, w - max_rel))
    wrap = jnp.broadcast_to(table[:, :1], (nhb, CHUNK))
    b0 = jnp.concatenate([first, table[:, :max_rel + CHUNK], wrap], axis=1).astype(F32)
    return pl.pallas_call(
        _relbias_kernel,
        out_shape=jax.ShapeDtypeStruct((nhb, CHUNK, w + CHUNK), F32),
        name="rel_bias",
    )(b0)


def _band_prompt_kernel(q_ref, k_ref, v_ref, bias_ref, o_ref, s_scr, m_scr, e_scr, *, npair, w, nck):
    c4 = pl.program_id(1)
    nk = w + CHUNK
    lane = lax.broadcasted_iota(jnp.int32, (CHUNK, LANES), 1)
    low = lane < LANES // 2
    zero = jnp.zeros((CHUNK, LANES), BF16)
    ones = jnp.ones((nk, LANES), BF16)

    def run(masked):
        starts = [pl.multiple_of((c4 * nck + jc) * CHUNK, CHUNK) for jc in range(nck)]
        for jc in range(nck):
            for p in range(npair):
                g = jc * npair + p
                sl = slice(p * LANES, (p + 1) * LANES)
                qp = q_ref[0, jc * CHUNK:(jc + 1) * CHUNK, sl]
                q2 = jnp.concatenate([jnp.where(low, qp, zero), jnp.where(low, zero, qp)], axis=0)
                s = _dot_nt(q2, k_ref[0, pl.ds(starts[jc], nk), sl]) + bias_ref[p]
                if masked:
                    col = lax.broadcasted_iota(jnp.int32, s.shape, 1)
                    s = jnp.where(col + starts[jc] >= w, s, -jnp.inf)
                s_scr[g] = s
                m_scr[g] = jnp.max(s, axis=-1, keepdims=True)
        for g in range(nck * npair):
            e_scr[g] = jnp.exp(s_scr[g] - m_scr[g]).astype(BF16)
        for jc in range(nck):
            for p in range(npair):
                g = jc * npair + p
                sl = slice(p * LANES, (p + 1) * LANES)
                vx = jnp.concatenate([v_ref[0, pl.ds(starts[jc], nk), sl], ones], axis=1)
                r = _dot(e_scr[g], vx)
                o_lo = r[:CHUNK, :LANES] / r[:CHUNK, LANES:]
                o_hi = r[CHUNK:, :LANES] / r[CHUNK:, LANES:]
                o_ref[0, jc * CHUNK:(jc + 1) * CHUNK, sl] = jnp.where(low, o_lo, o_hi).astype(BF16)

    first_full = w // (CHUNK * nck)

    @pl.when(c4 < first_full)
    def _():
        run(True)

    @pl.when(c4 >= first_full)
    def _():
        run(False)


def _band_prompt_call(qs, kpad, vpad, bias2, *, w, nck):
    nb, length, bw = qs.shape
    npair = bias2.shape[0]
    lp = kpad.shape[1]
    tq = nck * CHUNK
    nk = w + CHUNK
    groups = nck * npair
    assert w % tq == 0
    return pl.pallas_call(
        functools.partial(_band_prompt_kernel, npair=npair, w=w, nck=nck),
        grid=(nb, length // tq),
        in_specs=[pl.BlockSpec((1, tq, bw), lambda b, c: (b, c, 0)),
                  pl.BlockSpec((1, lp, bw), lambda b, c: (b, 0, 0)),
                  pl.BlockSpec((1, lp, bw), lambda b, c: (b, 0, 0)),
                  _const_spec(bias2.shape)],
        out_specs=pl.BlockSpec((1, tq, bw), lambda b, c: (b, c, 0)),
        out_shape=jax.ShapeDtypeStruct((nb, length, bw), BF16),
        scratch_shapes=[pltpu.VMEM((groups, 2 * CHUNK, nk), F32),
                        pltpu.VMEM((groups, 2 * CHUNK, 1), F32),
                        pltpu.VMEM((groups, 2 * CHUNK, nk), BF16)],
        compiler_params=_cparams(("arbitrary", "arbitrary"), 40),
        name="band_prompt",
    )(qs, kpad, vpad, bias2)


def _band_sample_kernel(q_ref, kn_ref, vn_ref, ck_ref, cv_ref, bias_ref, o_ref, *, npair):
    tq = q_ref.shape[1]
    nk = ck_ref.shape[1] + tq
    lane = lax.broadcasted_iota(jnp.int32, (tq, LANES), 1)
    low = lane < LANES // 2
    zero = jnp.zeros((tq, LANES), BF16)
    ones = jnp.ones((nk, LANES), BF16)
    scores = []
    for p in range(npair):
        sl = slice(p * LANES, (p + 1) * LANES)
        qp = q_ref[0, :, sl]
        q2 = jnp.concatenate([jnp.where(low, qp, zero), jnp.where(low, zero, qp)], axis=0)
        kx = jnp.concatenate([ck_ref[0, :, sl].astype(BF16), kn_ref[0, :, sl].astype(BF16)], axis=0)
        scores.append(_dot_nt(q2, kx) + bias_ref[p])
    probs = [jnp.exp(s - jnp.max(s, axis=-1, keepdims=True)).astype(BF16) for s in scores]
    for p in range(npair):
        sl = slice(p * LANES, (p + 1) * LANES)
        vx = jnp.concatenate([cv_ref[0, :, sl].astype(BF16), vn_ref[0, :, sl].astype(BF16)], axis=0)
        r = _dot(probs[p], jnp.concatenate([vx, ones], axis=1))
        o_lo = r[:tq, :LANES] / r[:tq, LANES:]
        o_hi = r[tq:, :LANES] / r[tq:, LANES:]
        o_ref[0, :, sl] = jnp.where(low, o_lo, o_hi).astype(BF16)


def _band_sample_call(qn, kn, vn, ck, cv, bias2):
    nb, tq, bw = qn.shape
    w = ck.shape[1]
    npair = bias2.shape[0]
    new = pl.BlockSpec((1, tq, bw), lambda b: (b, 0, 0))
    cache = pl.BlockSpec((1, w, bw), lambda b: (b, 0, 0))
    return pl.pallas_call(
        functools.partial(_band_sample_kernel, npair=npair),
        grid=(nb,),
        in_specs=[new, new, new, cache, cache, _const_spec(bias2.shape)],
        out_specs=new,
        out_shape=jax.ShapeDtypeStruct((nb, tq, bw), BF16),
        compiler_params=_cparams(("arbitrary",), 32),
        name="band_sample",
    )(qn, kn, vn, ck, cv, bias2)


def _mixout_kernel(x_ref, ha_ref, hb_ref, gt_ref, woa_ref, wob_ref, o_ref):
    bb, tl, d = x_ref.shape
    ha = ha_ref[...].reshape(bb * tl, -1)
    hb = hb_ref[...].reshape(bb * tl, -1)
    y = _dot(ha, woa_ref[...]) + _dot(hb, wob_ref[...])
    o_ref[...] = x_ref[...] + gt_ref[...] * y.reshape(bb, tl, d)


def _mixout_call(x, ha, hb, gt, woa, wob, bb, tl):
    nb, length, d = x.shape
    tile = lambda n: pl.BlockSpec((bb, tl, n), lambda i, t: (i, t, 0))
    return pl.pallas_call(
        _mixout_kernel,
        grid=(nb // bb, length // tl),
        in_specs=[tile(d), tile(ha.shape[-1]), tile(hb.shape[-1]),
                  pl.BlockSpec((bb, 1, d), lambda i, t: (i, 0, 0)),
                  _const_spec(woa.shape), _const_spec(wob.shape)],
        out_specs=tile(d),
        out_shape=jax.ShapeDtypeStruct(x.shape, F32),
        compiler_params=_cparams(("arbitrary", "arbitrary"), 32),
        name="mix_out",
    )(x, ha, hb, gt, woa, wob)


def _rglru_gates(xc, gw_ref, rb, ib, lam, nblk):
    bwc = xc.shape[1] // nblk
    r_parts, i_parts = [], []
    for n in range(nblk):
        gn = _dot(xc[:, n * bwc:(n + 1) * bwc].astype(BF16), gw_ref[n])
        r_parts.append(gn[:, :bwc])
        i_parts.append(gn[:, bwc:])
    r = jax.nn.sigmoid(jnp.concatenate(r_parts, axis=1) + rb)
    ii = jax.nn.sigmoid(jnp.concatenate(i_parts, axis=1) + ib)
    log_a = (-LRU_C * r) * _softplus(-lam)
    a = jnp.exp(log_a)
    th = jnp.tanh(log_a)
    upd = jnp.sqrt(-2.0 * th / (1.0 - th)) * (ii * xc)
    return a, upd


def _rglru_prompt_kernel(x_ref, sh_ref, sc_ref, gt_ref, g_ref, win_ref, cw_ref, cb_ref, gw_ref, rb_ref, ib_ref,
                         lam_ref, wout_ref, conv0_ref, h0_ref, o_ref, conv_ref, hl_ref, xp_scr, a_scr, b_scr, *, nblk):
    t = pl.program_id(1)
    tq, d = x_ref.shape[1], x_ref.shape[2]
    r_w = lam_ref.shape[1]
    ncw = cw_ref.shape[0]

    @pl.when(t == 0)
    def _():
        xp_scr[0:SUBLANES, :] = conv0_ref[0]
        hl_ref[...] = h0_ref[...]

    x = x_ref[0]
    hm = _rms_mod(x, g_ref[...], sh_ref[0], sc_ref[0]).astype(BF16)
    u = _dot(hm, win_ref[...])
    gb = u[:, :r_w]
    xp_scr[SUBLANES:SUBLANES + tq, :] = u[:, r_w:]
    xc = cb_ref[...]
    for j in range(ncw):
        off = SUBLANES - (ncw - 1 - j)
        xc = xc + xp_scr[off:off + tq, :] * cw_ref[j:j + 1, :]
    conv_ref[0] = xp_scr[tq:tq + SUBLANES, :]
    xp_scr[0:SUBLANES, :] = xp_scr[tq:tq + SUBLANES, :]
    a, upd = _rglru_gates(xc, gw_ref, rb_ref[...], ib_ref[...], lam_ref[...], nblk)
    a_scr[...] = a
    b_scr[...] = upd
    row8 = lax.broadcasted_iota(jnp.int32, (SUBLANES, r_w), 0)

    def scan_body(i, h):
        rows = pl.ds(pl.multiple_of(i * SUBLANES, SUBLANES), SUBLANES)
        ai = a_scr[rows, :]
        bi = b_scr[rows, :]
        s = 1
        while s < SUBLANES:
            m = row8 >= s
            bi = jnp.where(m, ai * pltpu.roll(bi, s, 0) + bi, bi)
            ai = jnp.where(m, ai * pltpu.roll(ai, s, 0), ai)
            s *= 2
        hs = ai * h + bi
        a_scr[rows, :] = hs
        return hs[SUBLANES - 1:SUBLANES, :]

    h_fin = lax.fori_loop(0, tq // SUBLANES, scan_body, hl_ref[0])
    hl_ref[0] = h_fin
    y = _dot((_gelu_tanh(gb) * a_scr[...]).astype(BF16), wout_ref[...])
    o_ref[0] = x + gt_ref[0] * y


def _rglru_prompt_call(x, sh, sc, gt, g, win, cw, cb, gw, rb, ib, lam, wout, conv0, h0, *, tq):
    nb, length, d = x.shape
    r_w = lam.shape[1]
    nblk = gw.shape[0]
    ada_spec = pl.BlockSpec((1, 1, d), lambda b, t: (b, 0, 0))
    tile = pl.BlockSpec((1, tq, d), lambda b, t: (b, t, 0))
    conv_spec = pl.BlockSpec((1, SUBLANES, r_w), lambda b, t: (b, 0, 0))
    h_spec = pl.BlockSpec((1, 1, r_w), lambda b, t: (b, 0, 0))
    consts = [g, win, cw, cb, gw, rb, ib, lam, wout]
    return pl.pallas_call(
        functools.partial(_rglru_prompt_kernel, nblk=nblk),
        grid=(nb, length // tq),
        in_specs=[tile, ada_spec, ada_spec, ada_spec] + [_const_spec(a.shape) for a in consts] + [conv_spec, h_spec],
        out_specs=[tile, conv_spec, h_spec],
        out_shape=[jax.ShapeDtypeStruct(x.shape, F32),
                   jax.ShapeDtypeStruct((nb, SUBLANES, r_w), F32),
                   jax.ShapeDtypeStruct((nb, 1, r_w), F32)],
        scratch_shapes=[pltpu.VMEM((tq + SUBLANES, r_w), F32), pltpu.VMEM((tq, r_w), F32), pltpu.VMEM((tq, r_w), F32)],
        compiler_params=_cparams(("arbitrary", "arbitrary"), 48),
        name="rglru_prompt",
    )(x, sh, sc, gt, *consts, conv0, h0)


def _rglru_sample_kernel(x_ref, sh_ref, sc_ref, gt_ref, g_ref, win_ref, cw_ref, cb_ref, gw_ref, rb_ref, ib_ref,
                         lam_ref, wout_ref, conv0_ref, h0_ref, o_ref, conv_ref, hl_ref, xp_scr, *, nblk):
    bb, tl, d = x_ref.shape
    tm = bb * tl
    r_w = lam_ref.shape[1]
    ncw = cw_ref.shape[0]
    x = x_ref[...]
    hm = _rms_mod(x, g_ref[...], sh_ref[...], sc_ref[...]).reshape(tm, d).astype(BF16)
    u = _dot(hm, win_ref[...])
    gb = u[:, :r_w]
    xp_scr[:, 0:SUBLANES, :] = conv0_ref[...]
    xp_scr[:, SUBLANES:SUBLANES + tl, :] = u[:, r_w:].reshape(bb, tl, r_w)
    xc = jnp.broadcast_to(cb_ref[...], (bb, tl, r_w))
    for j in range(ncw):
        off = SUBLANES - (ncw - 1 - j)
        xc = xc + xp_scr[:, off:off + tl, :] * cw_ref[j:j + 1, :]
    conv_ref[...] = xp_scr[:, tl:tl + SUBLANES, :]
    a, b = _rglru_gates(xc.reshape(tm, r_w), gw_ref, rb_ref[...], ib_ref[...], lam_ref[...], nblk)
    pos = lax.broadcasted_iota(jnp.int32, (tm, r_w), 0) % tl
    s = 1
    while s < tl:
        m = pos >= s
        b = jnp.where(m, a * pltpu.roll(b, s, 0) + b, b)
        a = jnp.where(m, a * pltpu.roll(a, s, 0), a)
        s *= 2
    hs = a.reshape(bb, tl, r_w) * h0_ref[...] + b.reshape(bb, tl, r_w)
    hl_ref[...] = hs[:, tl - 1:tl, :]
    y = _dot((_gelu_tanh(gb) * hs.reshape(tm, r_w)).astype(BF16), wout_ref[...])
    o_ref[...] = x + gt_ref[...] * y.reshape(bb, tl, d)


def _rglru_sample_call(x, sh, sc, gt, g, win, cw, cb, gw, rb, ib, lam, wout, conv0, h0):
    nb, tl, d = x.shape
    r_w = lam.shape[1]
    nblk = gw.shape[0]
    full = lambda a, b: pl.BlockSpec((nb, a, b), lambda i: (0, 0, 0))
    consts = [g, win, cw, cb, gw, rb, ib, lam, wout]
    return pl.pallas_call(
        functools.partial(_rglru_sample_kernel, nblk=nblk),
        grid=(1,),
        in_specs=[full(tl, d), full(1, d), full(1, d), full(1, d)] + [_const_spec(a.shape) for a in consts]
                 + [full(SUBLANES, r_w), full(1, r_w)],
        out_specs=[full(tl, d), full(SUBLANES, r_w), full(1, r_w)],
        out_shape=[jax.ShapeDtypeStruct(x.shape, F32),
                   jax.ShapeDtypeStruct((nb, SUBLANES, r_w), F32),
                   jax.ShapeDtypeStruct((nb, 1, r_w), F32)],
        scratch_shapes=[pltpu.VMEM((nb, tl + SUBLANES, r_w), F32)],
        compiler_params=_cparams(("arbitrary",), 48),
        name="rglru_sample",
    )(x, sh, sc, gt, *consts, conv0, h0)


def _prep_ffn(w_in, w_out):
    return w_in.astype(BF16), w_out.astype(BF16)


def _pad_rows_front(a, rows):
    return jnp.pad(a, ((0, 0), (rows - a.shape[1], 0), (0, 0)))


def kernel(x_prompt, x_sample, state_a_C, state_a_n, state_a_m, cache_b_k, cache_b_v, state_c_conv, state_c_h,
           c_prompt, c_sample, ffn1_norm, ffn1_w_in, ffn1_w_out, mix_norm, ffn2_norm, ffn2_w_in, ffn2_w_out,
           ada_w, ada_b, ab_w_in, ab_gate_bias, a_out_norm, b_q_norm, b_k_norm, b_rel_bias, ab_w_out,
           c_w_in, c_conv_w, c_conv_b, c_gate_w, c_gate_b, c_lambda, c_w_out):
    nbp, seq, d = x_prompt.shape
    nbs, tdec, _ = x_sample.shape
    depth = ada_w.shape[0]
    n_ada = ada_w.shape[2] // d
    _, _, nh, dh, _ = state_a_C.shape
    _, _, w_band, nhb, dhb = cache_b_k.shape
    aw, bw = nh * dh, nhb * dhb
    ncw = c_conv_w.shape[1]
    assert 2 * dhb == LANES and dh == LANES and w_band % CHUNK == 0 and seq % w_band == 0

    ada = _ada_call(jnp.concatenate([c_prompt, c_sample], axis=0), ada_w, ada_b)
    ada = ada.reshape(depth, nbp + nbs, n_ada, 1, d)
    ada_p = [[ada[l, :nbp, k] for k in range(n_ada)] for l in range(depth)]
    ada_s = [[ada[l, nbp:, k] for k in range(n_ada)] for l in range(depth)]

    ffn1 = [_prep_ffn(ffn1_w_in[l], ffn1_w_out[l]) for l in range(depth)]
    ffn2 = [_prep_ffn(ffn2_w_in[l], ffn2_w_out[l]) for l in range(depth)]

    tl_p = 512
    xp, xs = x_prompt, x_sample
    outs_p, outs_s = {}, {}
    for l in range(depth):
        ap, as_ = ada_p[l], ada_s[l]
        i = l // 2
        g1 = ffn1_norm[l].reshape(1, d)
        gm = mix_norm[l].reshape(1, d)
        g2 = ffn2_norm[l].reshape(1, d)
        xp = _ffn_call(xp, ap[0], ap[1], ap[2], g1, *ffn1[l], 1, tl_p)
        xs = _ffn_call(xs, as_[0], as_[1], as_[2], g1, *ffn1[l], nbs, tdec)
        if l % 2 == 0:
            w_in = ab_w_in[i]
            wa = w_in[:, :4 * aw].astype(BF16)
            wg = jnp.pad(w_in[:, 4 * aw:4 * aw + 2 * nh], ((0, 0), (0, LANES - 2 * nh))).astype(BF16)
            wb = w_in[:, 4 * aw + 2 * nh:].astype(BF16)
            gbias = jnp.pad(ab_gate_bias[i], (0, LANES - 2 * nh)).reshape(1, LANES)
            qg = jnp.tile(b_q_norm[i], nhb).reshape(1, bw)
            kg = jnp.tile(b_k_norm[i], nhb).reshape(1, bw)
            head = jnp.arange(bw) // dhb
            e = (head[:, None] == head[None, :]).astype(BF16)
            woa = ab_w_out[i][:aw].astype(BF16)
            wob = ab_w_out[i][aw:].astype(BF16)
            gout = a_out_norm[i]
            bias = _relbias_call(b_rel_bias[i], w_band)
            bias2 = bias.reshape(nhb // 2, 2 * CHUNK, w_band + CHUNK)

            ua, gts, qn, kpad, vpad, klast, vlast = _proj_prompt_call(
                xp, ap[3], ap[4], gm, wa, wg, wb, gbias, qg, kg, e, nh=nh, dhb=dhb, w=w_band)
            zc = jnp.zeros((nbp, nh, dh, dh), F32)
            ha, c1, n1, m1 = _mlstm_call(ua, gts, zc, zc[:, :, 0], zc[:, :, 0, :1], gout, tq=256, seg=CHUNK)
            hb = _band_prompt_call(qn, kpad, vpad, bias2, w=w_band, nck=4)
            xp = _mixout_call(xp, ha, hb, ap[5], woa, wob, 1, tl_p)
            outs_p.setdefault('a_C', []).append(c1)
            outs_p.setdefault('a_n', []).append(n1)
            outs_p.setdefault('a_m', []).append(m1.reshape(nbp, nh))
            outs_p.setdefault('b_k', []).append(klast.reshape(nbp, w_band, nhb, dhb))
            outs_p.setdefault('b_v', []).append(vlast.reshape(nbp, w_band, nhb, dhb))

            ua, gts, qn, kn, vn = _proj_sample_call(
                xs, as_[3], as_[4], gm, wa, wg, wb, gbias, qg, kg, e, nh=nh, dhb=dhb)
            ha, c1, n1, m1 = _mlstm_call(ua, gts, state_a_C[i], state_a_n[i], state_a_m[i][..., None], gout,
                                         tq=tdec, seg=tdec)
            hb = _band_sample_call(qn, kn, vn, cache_b_k[i].reshape(nbs, w_band, bw),
                                   cache_b_v[i].reshape(nbs, w_band, bw),
                                   bias[:, :tdec, :w_band + tdec].reshape(nhb // 2, 2 * tdec, w_band + tdec))
            xs = _mixout_call(xs, ha, hb, as_[5], woa, wob, nbs, tdec)
            outs_s.setdefault('a_C', []).append(c1)
            outs_s.setdefault('a_n', []).append(n1)
            outs_s.setdefault('a_m', []).append(m1.reshape(nbs, nh))
            outs_s.setdefault('b_k', []).append(kn.reshape(nbs, tdec, nhb, dhb))
            outs_s.setdefault('b_v', []).append(vn.reshape(nbs, tdec, nhb, dhb))
        else:
            r_w = c_lambda.shape[1]
            consts = (gm, c_w_in[i].astype(BF16), c_conv_w[i], c_conv_b[i].reshape(1, r_w), c_gate_w[i].astype(BF16),
                      c_gate_b[i][0].reshape(1, r_w), c_gate_b[i][1].reshape(1, r_w), c_lambda[i].reshape(1, r_w),
                      c_w_out[i].astype(BF16))
            xp, conv_p, h_p = _rglru_prompt_call(
                xp, ap[3], ap[4], ap[5], *consts,
                jnp.zeros((nbp, SUBLANES, r_w), F32), jnp.zeros((nbp, 1, r_w), F32), tq=256)
            xs, conv_s, h_s = _rglru_sample_call(
                xs, as_[3], as_[4], as_[5], *consts,
                _pad_rows_front(state_c_conv[i], SUBLANES), state_c_h[i][:, None, :])
            outs_p.setdefault('c_conv', []).append(conv_p[:, SUBLANES - (ncw - 1):])
            outs_p.setdefault('c_h', []).append(h_p[:, 0])
            outs_s.setdefault('c_conv', []).append(conv_s[:, SUBLANES - (ncw - 1):])
            outs_s.setdefault('c_h', []).append(h_s[:, 0])
        xp = _ffn_call(xp, ap[6], ap[7], ap[8], g2, *ffn2[l], 1, tl_p)
        xs = _ffn_call(xs, as_[6], as_[7], as_[8], g2, *ffn2[l], nbs, tdec)

    names = ('a_C', 'a_n', 'a_m', 'b_k', 'b_v', 'c_conv', 'c_h')
    ps = [jnp.stack(outs_p[n], axis=0) for n in names]
    ss = [jnp.stack(outs_s[n], axis=0) for n in names]
    return (xp, xs, *ps, *ss)
```

```python
import functools

import jax
import jax.numpy as jnp
from jax import lax
from jax.experimental import pallas as pl
from jax.experimental.pallas import tpu as pltpu

F32 = jnp.float32
BF16 = jnp.bfloat16

EPS = 1e-6
CHUNK = 64
LRU_C = 8.0
LANES = 128
SUBLANES = 8
MIB = 1024 * 1024


def _cparams(semantics, vmem_mib):
    return pltpu.CompilerParams(dimension_semantics=semantics, vmem_limit_bytes=vmem_mib * MIB)


def _const_spec(shape):
    nd = len(shape)
    return pl.BlockSpec(shape, lambda *_: (0,) * nd, pipeline_mode=pl.Buffered(1))


def _dot(a, b):
    return jnp.dot(a, b, preferred_element_type=F32)


def _dot_nt(a, b):
    return lax.dot_general(a, b, (((1,), (1,)), ((), ())), preferred_element_type=F32)


def _dot_tn(a, b):
    return lax.dot_general(a, b, (((0,), (0,)), ((), ())), preferred_element_type=F32)


def _rms_mod(x, g, shift, scale):
    ms = jnp.mean(x * x, axis=-1, keepdims=True)
    return (x * lax.rsqrt(ms + EPS) * g) * (1.0 + scale) + shift


def _softplus(x):
    return jnp.maximum(x, 0.0) + jnp.log1p(jnp.exp(-jnp.abs(x)))


def _gelu_tanh(x):
    return x * (0.5 * (1.0 + jnp.tanh(0.7978845608028654 * (x + 0.044715 * (x * x * x)))))


def _ada_kernel(c_ref, w_ref, b_ref, o_ref):
    c = c_ref[...].astype(BF16)
    w = w_ref[0].astype(BF16)
    o_ref[0] = _dot(c, w) + b_ref[0]


def _ada_call(c_all, ada_w, ada_b):
    depth, d, n = ada_w.shape
    m = c_all.shape[0]
    tn = d
    return pl.pallas_call(
        _ada_kernel,
        grid=(depth, n // tn),
        in_specs=[pl.BlockSpec((m, d), lambda l, j: (0, 0)),
                  pl.BlockSpec((1, d, tn), lambda l, j: (l, 0, j)),
                  pl.BlockSpec((1, 1, tn), lambda l, j: (l, 0, j))],
        out_specs=pl.BlockSpec((1, m, tn), lambda l, j: (l, 0, j)),
        out_shape=jax.ShapeDtypeStruct((depth, m, n), F32),
        compiler_params=_cparams(("arbitrary", "arbitrary"), 32),
        name="ada_proj",
    )(c_all, ada_w, ada_b.reshape(depth, 1, n))


FFN_TF = 256


def _ffn_kernel(x_ref, sh_ref, sc_ref, gt_ref, g_ref, win_ref, wo_ref, o_ref, act_scr):
    x = x_ref[...]
    bb, tl, d = x.shape
    dff = wo_ref.shape[0]
    h = _rms_mod(x, g_ref[...], sh_ref[...], sc_ref[...]).reshape(bb * tl, d).astype(BF16)
    for c0 in range(0, dff, FFN_TF):
        gate = _dot(h, win_ref[:, c0:c0 + FFN_TF].astype(BF16))
        up = _dot(h, win_ref[:, dff + c0:dff + c0 + FFN_TF].astype(BF16))
        act_scr[:, c0:c0 + FFN_TF] = ((gate * jax.nn.sigmoid(gate)) * up).astype(BF16)
    y = _dot(act_scr[...], wo_ref[...].astype(BF16))
    o_ref[...] = x + (0.5 * gt_ref[...]) * y.reshape(bb, tl, d)


def _ffn_call(x, sh, sc, gt, g, w_in, w_out, layer, bb, tl):
    nb, length, d = x.shape
    dff = w_out.shape[1]
    assert dff % FFN_TF == 0
    tm = bb * tl
    x_spec = pl.BlockSpec((bb, tl, d), lambda i, t: (i, t, 0))
    ada_spec = pl.BlockSpec((bb, 1, d), lambda i, t: (i, 0, 0))
    layer_spec = lambda shape: pl.BlockSpec((None,) + shape, lambda i, t: (layer, 0, 0),
                                            pipeline_mode=pl.Buffered(1))
    return pl.pallas_call(
        _ffn_kernel,
        grid=(nb // bb, length // tl),
        in_specs=[x_spec, ada_spec, ada_spec, ada_spec, _const_spec((1, d)),
                  layer_spec(w_in.shape[1:]), layer_spec(w_out.shape[1:])],
        out_specs=x_spec,
        out_shape=jax.ShapeDtypeStruct(x.shape, F32),
        scratch_shapes=[pltpu.VMEM((tm, dff), BF16)],
        compiler_params=_cparams(("arbitrary", "arbitrary"), 58),
        name="ffn",
    )(x, sh, sc, gt, g, w_in, w_out)


def _head_rmsnorm(q, e, g, dhb):
    q2 = q * q
    hi = q2.astype(BF16)
    lo = (q2 - hi.astype(F32)).astype(BF16)
    ss = _dot(hi, e) + _dot(lo, e)
    return q * lax.rsqrt(ss * (1.0 / dhb) + EPS) * g


def _proj_body(x, sh, sc, g, w_ref, gb_ref, qg_ref, kg_ref, e_ref, *, nh, bw, dhb):
    bb, tl, d = x.shape
    na = w_ref.shape[1] - 3 * bw - LANES
    h = _rms_mod(x, g, sh, sc).reshape(bb * tl, d).astype(BF16)
    ua = _dot(h, w_ref[:, :na])
    gg = _dot(h, w_ref[:, na + 3 * bw:]) + gb_ref[...]
    lane = lax.broadcasted_iota(jnp.int32, gg.shape, 1)
    gates = jnp.where(lane < nh, gg, -_softplus(-gg))
    ub = _dot(h, w_ref[:, na:na + 3 * bw])
    e = e_ref[...]
    qn = _head_rmsnorm(ub[:, :bw], e, qg_ref[...], dhb) * (dhb ** -0.5)
    kn = _head_rmsnorm(ub[:, bw:2 * bw], e, kg_ref[...], dhb)
    vb = ub[:, 2 * bw:]
    return ua, gates, qn, kn, vb


def _proj_prompt_kernel(x_ref, sh_ref, sc_ref, g_ref, w_ref, gb_ref, qg_ref, kg_ref, e_ref,
                        ua_ref, gt_ref, qn_ref, kp_ref, vp_ref, kl_ref, vl_ref, *, nh, bw, dhb):
    t = pl.program_id(1)
    nt = pl.num_programs(1)

    @pl.when(t == 0)
    def _():
        kp_ref[...] = jnp.zeros_like(kp_ref)
        vp_ref[...] = jnp.zeros_like(vp_ref)

    @pl.when(t > 0)
    def _():
        ua, gates, qn, kn, vb = _proj_body(x_ref[...], sh_ref[...], sc_ref[...], g_ref[...], w_ref,
                                           gb_ref, qg_ref, kg_ref, e_ref, nh=nh, bw=bw, dhb=dhb)
        ua_ref[0] = ua
        gt_ref[0] = gates
        qn_ref[0] = qn.astype(BF16)
        kp_ref[0] = kn.astype(BF16)
        vp_ref[0] = vb.astype(BF16)

        @pl.when(t == nt - 1)
        def _():
            kl_ref[0] = kn
            vl_ref[0] = vb


def _proj_prompt_call(x, sh, sc, g, wab, gbias, qg, kg, e, *, nh, dhb, w):
    nb, length, d = x.shape
    tl = w
    nt = length // tl
    bw = e.shape[0]
    aw4 = wab.shape[1] - 3 * bw - LANES
    prev = lambda b, t: (b, jnp.maximum(t - 1, 0), 0)
    ada_spec = pl.BlockSpec((1, 1, d), lambda b, t: (b, 0, 0))
    outs = pl.pallas_call(
        functools.partial(_proj_prompt_kernel, nh=nh, bw=bw, dhb=dhb),
        grid=(nb, nt + 1),
        in_specs=[pl.BlockSpec((1, tl, d), prev), ada_spec, ada_spec, _const_spec((1, d)),
                  _const_spec(wab.shape), _const_spec(gbias.shape),
                  _const_spec(qg.shape), _const_spec(kg.shape), _const_spec(e.shape)],
        out_specs=[pl.BlockSpec((1, tl, aw4), prev),
                   pl.BlockSpec((1, tl, LANES), prev),
                   pl.BlockSpec((1, tl, bw), prev),
                   pl.BlockSpec((1, tl, bw), lambda b, t: (b, t, 0)),
                   pl.BlockSpec((1, tl, bw), lambda b, t: (b, t, 0)),
                   pl.BlockSpec((1, tl, bw), lambda b, t: (b, 0, 0)),
                   pl.BlockSpec((1, tl, bw), lambda b, t: (b, 0, 0))],
        out_shape=[jax.ShapeDtypeStruct((nb, length, aw4), F32),
                   jax.ShapeDtypeStruct((nb, length, LANES), F32),
                   jax.ShapeDtypeStruct((nb, length, bw), BF16),
                   jax.ShapeDtypeStruct((nb, length + w, bw), BF16),
                   jax.ShapeDtypeStruct((nb, length + w, bw), BF16),
                   jax.ShapeDtypeStruct((nb, w, bw), F32),
                   jax.ShapeDtypeStruct((nb, w, bw), F32)],
        compiler_params=_cparams(("arbitrary", "arbitrary"), 48),
        name="proj_prompt",
    )(x, sh, sc, g, wab, gbias, qg, kg, e)
    return outs


def _proj_sample_kernel(x_ref, sh_ref, sc_ref, g_ref, w_ref, gb_ref, qg_ref, kg_ref, e_ref,
                        ua_ref, gt_ref, qn_ref, kn_ref, vb_ref, *, nh, bw, dhb):
    bb, tl, _ = x_ref.shape
    ua, gates, qn, kn, vb = _proj_body(x_ref[...], sh_ref[...], sc_ref[...], g_ref[...], w_ref,
                                       gb_ref, qg_ref, kg_ref, e_ref, nh=nh, bw=bw, dhb=dhb)
    ua_ref[...] = ua.reshape(bb, tl, -1)
    gt_ref[...] = gates.reshape(bb, tl, -1)
    qn_ref[...] = qn.reshape(bb, tl, -1).astype(BF16)
    kn_ref[...] = kn.reshape(bb, tl, -1)
    vb_ref[...] = vb.reshape(bb, tl, -1)


def _proj_sample_call(x, sh, sc, g, wab, gbias, qg, kg, e, *, nh, dhb):
    nb, length, d = x.shape
    bw = e.shape[0]
    aw4 = wab.shape[1] - 3 * bw - LANES
    full = lambda n: pl.BlockSpec((nb, length, n), lambda i: (0, 0, 0))
    ada_spec = pl.BlockSpec((nb, 1, d), lambda i: (0, 0, 0))
    return pl.pallas_call(
        functools.partial(_proj_sample_kernel, nh=nh, bw=bw, dhb=dhb),
        grid=(1,),
        in_specs=[full(d), ada_spec, ada_spec, _const_spec((1, d)),
                  _const_spec(wab.shape), _const_spec(gbias.shape),
                  _const_spec(qg.shape), _const_spec(kg.shape), _const_spec(e.shape)],
        out_specs=[full(aw4), full(LANES), full(bw), full(bw), full(bw)],
        out_shape=[jax.ShapeDtypeStruct((nb, length, aw4), F32),
                   jax.ShapeDtypeStruct((nb, length, LANES), F32),
                   jax.ShapeDtypeStruct((nb, length, bw), BF16),
                   jax.ShapeDtypeStruct((nb, length, bw), F32),
                   jax.ShapeDtypeStruct((nb, length, bw), F32)],
        compiler_params=_cparams(("arbitrary",), 48),
        name="proj_sample",
    )(x, sh, sc, g, wab, gbias, qg, kg, e)


def _mlstm_kernel(ua_ref, g_ref, c0_ref, n0_ref, m0_ref, go_ref, ha_ref, c_ref, n_ref, m_ref,
                  nrep_scr, rep_scr, s_scr, pv_scr, kv_scr, qc_scr, *, seg, nh, dh):
    t = pl.program_id(1)

    @pl.when(t == 0)
    def _():
        c_ref[...] = c0_ref[...]
        nrep_scr[...] = n0_ref[0]
        m_ref[...] = m0_ref[...]

    tq = ua_ref.shape[1]
    nck = tq // seg
    aw = nh * dh
    gates = g_ref[0]
    pos = lax.broadcasted_iota(jnp.int32, gates.shape, 0) % seg
    bt = gates
    s = 1
    while s < seg:
        bt = bt + jnp.where(pos >= s, pltpu.roll(bt, s, 0), 0.0)
        s *= 2
    dmb = pltpu.roll(gates, nh, 1) - bt
    pm = dmb
    s = 1
    while s < seg:
        pm = jnp.maximum(pm, jnp.where(pos >= s, pltpu.roll(pm, s, 0), -jnp.inf))
        s *= 2
    if tq % LANES:
        dsq = jnp.concatenate([dmb, jnp.zeros((LANES - tq % LANES, LANES), F32)], axis=0)
    else:
        dsq = dmb
    dtr = dsq.T
    ri = lax.broadcasted_iota(jnp.int32, (seg, seg), 0)
    ci = lax.broadcasted_iota(jnp.int32, (seg, seg), 1)
    causal = ri >= ci
    ones = jnp.ones((seg, LANES), BF16)
    ones_dh = jnp.ones((dh, LANES), BF16)
    for h in range(nh):
        ln = slice(nh + h, nh + h + 1)
        for j, arr in enumerate((bt, dmb, pm)):
            rep_scr[3 * h + j] = jnp.broadcast_to(arr[:, ln], (tq, LANES))

    def cols(jc, h):
        rows = slice(jc * seg, (jc + 1) * seg)
        return rows, rep_scr[3 * h, rows, :], rep_scr[3 * h + 1, rows, :], rep_scr[3 * h + 2, rows, :]

    def last(jc, h, j):
        r = (jc + 1) * seg - 1
        return rep_scr[3 * h + j, r:r + 1, :]

    groups = [(jc, h) for jc in range(nck) for h in range(nh)]

    def qkv(jc, h, which):
        rows = slice(jc * seg, (jc + 1) * seg)
        return ua_ref[0, rows, which * aw + h * dh:which * aw + (h + 1) * dh]

    for g, (jc, h) in enumerate(groups):
        k = qkv(jc, h, 1) * (dh ** -0.5)
        s_scr[g] = _dot_nt(qkv(jc, h, 0).astype(BF16), k.astype(BF16))
    for g, (jc, h) in enumerate(groups):
        rows, _, _, p_col = cols(jc, h)
        d_row = dtr[nh + h:nh + h + 1, rows]
        dloc = jnp.exp(jnp.where(causal, d_row - p_col[:, :seg], -jnp.inf))
        sl = (s_scr[g] * dloc).astype(BF16)
        v = qkv(jc, h, 2).astype(BF16)
        pv_scr[g] = _dot(sl, jnp.concatenate([v, ones], axis=1))
    for g, (jc, h) in enumerate(groups):
        _, _, d_col, _ = cols(jc, h)
        kw = (qkv(jc, h, 1) * (dh ** -0.5)) * jnp.exp(d_col - last(jc, h, 2))
        vx = jnp.concatenate([qkv(jc, h, 2).astype(BF16), ones], axis=1)
        kv_scr[g] = _dot_tn(kw.astype(BF16), vx)
    state = [(c_ref[0, h], nrep_scr[h], m_ref[0, h:h + 1, :]) for h in range(nh)]
    before = []
    for g, (jc, h) in enumerate(groups):
        c_mem, n_rep, m = state[h]
        cn = jnp.concatenate([c_mem.astype(BF16), n_rep.astype(BF16)], axis=1)
        qc_scr[g] = _dot(qkv(jc, h, 0).astype(BF16), cn)
        before.append(m)
        p_last = last(jc, h, 2)
        mml = jnp.maximum(m, p_last)
        w_prev = jnp.exp(m - mml)
        f_new = jnp.exp(p_last - mml)
        kvx = kv_scr[g]
        state[h] = (w_prev * c_mem + f_new * kvx[:, :dh],
                    w_prev * n_rep + f_new * kvx[:, dh:],
                    last(jc, h, 0) + mml)
    for h in range(nh):
        c_ref[0, h], nrep_scr[h], m_ref[0, h:h + 1, :] = state[h]
        n_ref[0, h:h + 1, :] = state[h][1].T[0:1, :]
    for g, (jc, h) in enumerate(groups):
        rows, b_col, _, p_col = cols(jc, h)
        m = before[g]
        mm = jnp.maximum(m, p_col)
        iw = jnp.exp(m - mm)
        fl = jnp.exp(p_col - mm)
        pv = pv_scr[g]
        qc = qc_scr[g]
        num = iw * qc[:, :dh] + fl * pv[:, :dh]
        den = iw * qc[:, dh:] + fl * pv[:, dh:]
        hh = num / jnp.maximum(jnp.abs(den), jnp.exp(-(b_col + mm)))
        h2 = hh * hh
        hi = h2.astype(BF16)
        lo = (h2 - hi.astype(F32)).astype(BF16)
        ms = (_dot(hi, ones_dh) + _dot(lo, ones_dh)) * (1.0 / dh)
        hn = (hh * lax.rsqrt(ms + EPS) * go_ref[h:h + 1, :]) * jax.nn.sigmoid(qkv(jc, h, 3))
        ha_ref[0, rows, h * dh:(h + 1) * dh] = hn.astype(BF16)


def _mlstm_call(ua, gates, c0, n0rep, m0rep, gout, *, tq, seg):
    nb, length, aw4 = ua.shape
    _, nh, dh, _ = c0.shape
    assert dh == LANES
    groups = (tq // seg) * nh
    st = lambda shape: pl.BlockSpec((1,) + shape, lambda b, t: (b,) + (0,) * len(shape))
    tile = lambda n: pl.BlockSpec((1, tq, n), lambda b, t: (b, t, 0))
    return pl.pallas_call(
        functools.partial(_mlstm_kernel, seg=seg, nh=nh, dh=dh),
        grid=(nb, length // tq),
        in_specs=[tile(aw4), tile(LANES), st((nh, dh, dh)), st((nh, dh, LANES)), st((nh, LANES)),
                  _const_spec(gout.shape)],
        out_specs=[tile(nh * dh), st((nh, dh, dh)), st((nh, dh)), st((nh, LANES))],
        out_shape=[jax.ShapeDtypeStruct((nb, length, nh * dh), BF16),
                   jax.ShapeDtypeStruct((nb, nh, dh, dh), F32),
                   jax.ShapeDtypeStruct((nb, nh, dh), F32),
                   jax.ShapeDtypeStruct((nb, nh, LANES), F32)],
        scratch_shapes=[pltpu.VMEM((nh, dh, LANES), F32), pltpu.VMEM((3 * nh, tq, LANES), F32),
                        pltpu.VMEM((groups, seg, seg), F32), pltpu.VMEM((groups, seg, dh + LANES), F32),
                        pltpu.VMEM((groups, dh, dh + LANES), F32), pltpu.VMEM((groups, seg, dh + LANES), F32)],
        compiler_params=_cparams(("arbitrary", "arbitrary"), 32),
        name="mlstm",
    )(ua, gates, c0, n0rep, m0rep, gout)


def _relbias_kernel(b0_ref, o_ref):
    nhb, nq, nk = o_ref.shape
    for h in range(nhb):
        x = jnp.broadcast_to(b0_ref[h:h + 1, :], (nq, b0_ref.shape[1]))
        o_ref[h] = pltpu.roll(x, 0, 1, stride=1, stride_axis=0)[:, :nk]


def _relbias_call(table, w):
    nhb = table.shape[0]
    max_rel = (table.shape[1] - 1) // 2
    assert CHUNK - 1 <= max_rel <= w
    first = jnp.broadcast_to(table[:, :1], (nhb, w - max_rel))
    wrap = jnp.broadcast_to(table[:, :1], (nhb, CHUNK))
    b0 = jnp.concatenate([first, table[:, :max_rel + CHUNK], wrap], axis=1).astype(F32)
    return pl.pallas_call(
        _relbias_kernel,
        out_shape=jax.ShapeDtypeStruct((nhb, CHUNK, w + CHUNK), F32),
        name="rel_bias",
    )(b0)


def _band_prompt_kernel(q_ref, k_ref, v_ref, bias_ref, o_ref, s_scr, m_scr, e_scr, *, npair, w, nck):
    c4 = pl.program_id(1)
    nk = w + CHUNK
    lane = lax.broadcasted_iota(jnp.int32, (CHUNK, LANES), 1)
    low = lane < LANES // 2
    zero = jnp.zeros((CHUNK, LANES), BF16)
    ones = jnp.ones((nk, LANES), BF16)

    def run(masked):
        starts = [pl.multiple_of((c4 * nck + jc) * CHUNK, CHUNK) for jc in range(nck)]
        for jc in range(nck):
            for p in range(npair):
                g = jc * npair + p
                sl = slice(p * LANES, (p + 1) * LANES)
                qp = q_ref[0, jc * CHUNK:(jc + 1) * CHUNK, sl]
                q2 = jnp.concatenate([jnp.where(low, qp, zero), jnp.where(low, zero, qp)], axis=0)
                s = _dot_nt(q2, k_ref[0, pl.ds(starts[jc], nk), sl]) + bias_ref[p]
                if masked:
                    col = lax.broadcasted_iota(jnp.int32, s.shape, 1)
                    s = jnp.where(col + starts[jc] >= w, s, -jnp.inf)
                s_scr[g] = s
                m_scr[g] = jnp.max(s, axis=-1, keepdims=True)
        for g in range(nck * npair):
            e_scr[g] = jnp.exp(s_scr[g] - m_scr[g]).astype(BF16)
        for jc in range(nck):
            for p in range(npair):
                g = jc * npair + p
                sl = slice(p * LANES, (p + 1) * LANES)
                vx = jnp.concatenate([v_ref[0, pl.ds(starts[jc], nk), sl], ones], axis=1)
                r = _dot(e_scr[g], vx)
                o_lo = r[:CHUNK, :LANES] / r[:CHUNK, LANES:]
                o_hi = r[CHUNK:, :LANES] / r[CHUNK:, LANES:]
                o_ref[0, jc * CHUNK:(jc + 1) * CHUNK, sl] = jnp.where(low, o_lo, o_hi).astype(BF16)

    first_full = w // (CHUNK * nck)

    @pl.when(c4 < first_full)
    def _():
        run(True)

    @pl.when(c4 >= first_full)
    def _():
        run(False)


def _band_prompt_call(qs, kpad, vpad, bias2, *, w, nck):
    nb, length, bw = qs.shape
    npair = bias2.shape[0]
    lp = kpad.shape[1]
    tq = nck * CHUNK
    nk = w + CHUNK
    groups = nck * npair
    assert w % tq == 0
    return pl.pallas_call(
        functools.partial(_band_prompt_kernel, npair=npair, w=w, nck=nck),
        grid=(nb, length // tq),
        in_specs=[pl.BlockSpec((1, tq, bw), lambda b, c: (b, c, 0)),
                  pl.BlockSpec((1, lp, bw), lambda b, c: (b, 0, 0)),
                  pl.BlockSpec((1, lp, bw), lambda b, c: (b, 0, 0)),
                  _const_spec(bias2.shape)],
        out_specs=pl.BlockSpec((1, tq, bw), lambda b, c: (b, c, 0)),
        out_shape=jax.ShapeDtypeStruct((nb, length, bw), BF16),
        scratch_shapes=[pltpu.VMEM((groups, 2 * CHUNK, nk), F32),
                        pltpu.VMEM((groups, 2 * CHUNK, 1), F32),
                        pltpu.VMEM((groups, 2 * CHUNK, nk), BF16)],
        compiler_params=_cparams(("arbitrary", "arbitrary"), 40),
        name="band_prompt",
    )(qs, kpad, vpad, bias2)


def _band_sample_kernel(q_ref, kn_ref, vn_ref, ck_ref, cv_ref, bias_ref, o_ref, *, npair):
    tq = q_ref.shape[1]
    nk = ck_ref.shape[1] + tq
    lane = lax.broadcasted_iota(jnp.int32, (tq, LANES), 1)
    low = lane < LANES // 2
    zero = jnp.zeros((tq, LANES), BF16)
    ones = jnp.ones((nk, LANES), BF16)
    scores = []
    for p in range(npair):
        sl = slice(p * LANES, (p + 1) * LANES)
        qp = q_ref[0, :, sl]
        q2 = jnp.concatenate([jnp.where(low, qp, zero), jnp.where(low, zero, qp)], axis=0)
        kx = jnp.concatenate([ck_ref[0, :, sl].astype(BF16), kn_ref[0, :, sl].astype(BF16)], axis=0)
        scores.append(_dot_nt(q2, kx) + bias_ref[p])
    probs = [jnp.exp(s - jnp.max(s, axis=-1, keepdims=True)).astype(BF16) for s in scores]
    for p in range(npair):
        sl = slice(p * LANES, (p + 1) * LANES)
        vx = jnp.concatenate([cv_ref[0, :, sl].astype(BF16), vn_ref[0, :, sl].astype(BF16)], axis=0)
        r = _dot(probs[p], jnp.concatenate([vx, ones], axis=1))
        o_lo = r[:tq, :LANES] / r[:tq, LANES:]
        o_hi = r[tq:, :LANES] / r[tq:, LANES:]
        o_ref[0, :, sl] = jnp.where(low, o_lo, o_hi).astype(BF16)


def _band_sample_call(qn, kn, vn, ck, cv, bias2):
    nb, tq, bw = qn.shape
    w = ck.shape[1]
    npair = bias2.shape[0]
    new = pl.BlockSpec((1, tq, bw), lambda b: (b, 0, 0))
    cache = pl.BlockSpec((1, w, bw), lambda b: (b, 0, 0))
    return pl.pallas_call(
        functools.partial(_band_sample_kernel, npair=npair),
        grid=(nb,),
        in_specs=[new, new, new, cache, cache, _const_spec(bias2.shape)],
        out_specs=new,
        out_shape=jax.ShapeDtypeStruct((nb, tq, bw), BF16),
        compiler_params=_cparams(("arbitrary",), 32),
        name="band_sample",
    )(qn, kn, vn, ck, cv, bias2)


def _mixout_kernel(x_ref, ha_ref, hb_ref, gt_ref, woa_ref, wob_ref, o_ref):
    bb, tl, d = x_ref.shape
    ha = ha_ref[...].reshape(bb * tl, -1)
    hb = hb_ref[...].reshape(bb * tl, -1)
    y = _dot(ha, woa_ref[...]) + _dot(hb, wob_ref[...])
    o_ref[...] = x_ref[...] + gt_ref[...] * y.reshape(bb, tl, d)


def _mixout_call(x, ha, hb, gt, woa, wob, bb, tl):
    nb, length, d = x.shape
    tile = lambda n: pl.BlockSpec((bb, tl, n), lambda i, t: (i, t, 0))
    return pl.pallas_call(
        _mixout_kernel,
        grid=(nb // bb, length // tl),
        in_specs=[tile(d), tile(ha.shape[-1]), tile(hb.shape[-1]),
                  pl.BlockSpec((bb, 1, d), lambda i, t: (i, 0, 0)),
                  _const_spec(woa.shape), _const_spec(wob.shape)],
        out_specs=tile(d),
        out_shape=jax.ShapeDtypeStruct(x.shape, F32),
        compiler_params=_cparams(("arbitrary", "arbitrary"), 32),
        name="mix_out",
    )(x, ha, hb, gt, woa, wob)


def _rglru_gates(xc, gw_ref, rb, ib, lam, nblk):
    bwc = xc.shape[1] // nblk
    r_parts, i_parts = [], []
    for n in range(nblk):
        gn = _dot(xc[:, n * bwc:(n + 1) * bwc].astype(BF16), gw_ref[n])
        r_parts.append(gn[:, :bwc])
        i_parts.append(gn[:, bwc:])
    r = jax.nn.sigmoid(jnp.concatenate(r_parts, axis=1) + rb)
    ii = jax.nn.sigmoid(jnp.concatenate(i_parts, axis=1) + ib)
    log_a = (-LRU_C * r) * _softplus(-lam)
    a = jnp.exp(log_a)
    th = jnp.tanh(log_a)
    upd = jnp.sqrt(-2.0 * th / (1.0 - th)) * (ii * xc)
    return a, upd


def _rglru_prompt_kernel(x_ref, sh_ref, sc_ref, gt_ref, g_ref, win_ref, cw_ref, cb_ref, gw_ref, rb_ref, ib_ref,
                         lam_ref, wout_ref, conv0_ref, h0_ref, o_ref, conv_ref, hl_ref, xp_scr, a_scr, b_scr, *, nblk):
    t = pl.program_id(1)
    tq, d = x_ref.shape[1], x_ref.shape[2]
    r_w = lam_ref.shape[1]
    ncw = cw_ref.shape[0]

    @pl.when(t == 0)
    def _():
        xp_scr[0:SUBLANES, :] = conv0_ref[0]
        hl_ref[...] = h0_ref[...]

    x = x_ref[0]
    hm = _rms_mod(x, g_ref[...], sh_ref[0], sc_ref[0]).astype(BF16)
    u = _dot(hm, win_ref[...])
    gb = u[:, :r_w]
    xp_scr[SUBLANES:SUBLANES + tq, :] = u[:, r_w:]
    xc = cb_ref[...]
    for j in range(ncw):
        off = SUBLANES - (ncw - 1 - j)
        xc = xc + xp_scr[off:off + tq, :] * cw_ref[j:j + 1, :]
    conv_ref[0] = xp_scr[tq:tq + SUBLANES, :]
    xp_scr[0:SUBLANES, :] = xp_scr[tq:tq + SUBLANES, :]
    a, upd = _rglru_gates(xc, gw_ref, rb_ref[...], ib_ref[...], lam_ref[...], nblk)
    a_scr[...] = a
    b_scr[...] = upd
    row8 = lax.broadcasted_iota(jnp.int32, (SUBLANES, r_w), 0)

    def scan_body(i, h):
        rows = pl.ds(pl.multiple_of(i * SUBLANES, SUBLANES), SUBLANES)
        ai = a_scr[rows, :]
        bi = b_scr[rows, :]
        s = 1
        while s < SUBLANES:
            m = row8 >= s
            bi = jnp.where(m, ai * pltpu.roll(bi, s, 0) + bi, bi)
            ai = jnp.where(m, ai * pltpu.roll(ai, s, 0), ai)
            s *= 2
        hs = ai * h + bi
        a_scr[rows, :] = hs
        return hs[SUBLANES - 1:SUBLANES, :]

    h_fin = lax.fori_loop(0, tq // SUBLANES, scan_body, hl_ref[0])
    hl_ref[0] = h_fin
    y = _dot((_gelu_tanh(gb) * a_scr[...]).astype(BF16), wout_ref[...])
    o_ref[0] = x + gt_ref[0] * y


def _rglru_prompt_call(x, sh, sc, gt, g, win, cw, cb, gw, rb, ib, lam, wout, conv0, h0, *, tq):
    nb, length, d = x.shape
    r_w = lam.shape[1]
    nblk = gw.shape[0]
    ada_spec = pl.BlockSpec((1, 1, d), lambda b, t: (b, 0, 0))
    tile = pl.BlockSpec((1, tq, d), lambda b, t: (b, t, 0))
    conv_spec = pl.BlockSpec((1, SUBLANES, r_w), lambda b, t: (b, 0, 0))
    h_spec = pl.BlockSpec((1, 1, r_w), lambda b, t: (b, 0, 0))
    consts = [g, win, cw, cb, gw, rb, ib, lam, wout]
    return pl.pallas_call(
        functools.partial(_rglru_prompt_kernel, nblk=nblk),
        grid=(nb, length // tq),
        in_specs=[tile, ada_spec, ada_spec, ada_spec] + [_const_spec(a.shape) for a in consts] + [conv_spec, h_spec],
        out_specs=[tile, conv_spec, h_spec],
        out_shape=[jax.ShapeDtypeStruct(x.shape, F32),
                   jax.ShapeDtypeStruct((nb, SUBLANES, r_w), F32),
                   jax.ShapeDtypeStruct((nb, 1, r_w), F32)],
        scratch_shapes=[pltpu.VMEM((tq + SUBLANES, r_w), F32), pltpu.VMEM((tq, r_w), F32), pltpu.VMEM((tq, r_w), F32)],
        compiler_params=_cparams(("arbitrary", "arbitrary"), 48),
        name="rglru_prompt",
    )(x, sh, sc, gt, *consts, conv0, h0)


def _rglru_sample_kernel(x_ref, sh_ref, sc_ref, gt_ref, g_ref, win_ref, cw_ref, cb_ref, gw_ref, rb_ref, ib_ref,
                         lam_ref, wout_ref, conv0_ref, h0_ref, o_ref, conv_ref, hl_ref, xp_scr, *, nblk):
    bb, tl, d = x_ref.shape
    tm = bb * tl
    r_w = lam_ref.shape[1]
    ncw = cw_ref.shape[0]
    x = x_ref[...]
    hm = _rms_mod(x, g_ref[...], sh_ref[...], sc_ref[...]).reshape(tm, d).astype(BF16)
    u = _dot(hm, win_ref[...])
    gb = u[:, :r_w]
    xp_scr[:, 0:SUBLANES, :] = conv0_ref[...]
    xp_scr[:, SUBLANES:SUBLANES + tl, :] = u[:, r_w:].reshape(bb, tl, r_w)
    xc = jnp.broadcast_to(cb_ref[...], (bb, tl, r_w))
    for j in range(ncw):
        off = SUBLANES - (ncw - 1 - j)
        xc = xc + xp_scr[:, off:off + tl, :] * cw_ref[j:j + 1, :]
    conv_ref[...] = xp_scr[:, tl:tl + SUBLANES, :]
    a, b = _rglru_gates(xc.reshape(tm, r_w), gw_ref, rb_ref[...], ib_ref[...], lam_ref[...], nblk)
    pos = lax.broadcasted_iota(jnp.int32, (tm, r_w), 0) % tl
    s = 1
    while s < tl:
        m = pos >= s
        b = jnp.where(m, a * pltpu.roll(b, s, 0) + b, b)
        a = jnp.where(m, a * pltpu.roll(a, s, 0), a)
        s *= 2
    hs = a.reshape(bb, tl, r_w) * h0_ref[...] + b.reshape(bb, tl, r_w)
    hl_ref[...] = hs[:, tl - 1:tl, :]
    y = _dot((_gelu_tanh(gb) * hs.reshape(tm, r_w)).astype(BF16), wout_ref[...])
    o_ref[...] = x + gt_ref[...] * y.reshape(bb, tl, d)


def _rglru_sample_call(x, sh, sc, gt, g, win, cw, cb, gw, rb, ib, lam, wout, conv0, h0):
    nb, tl, d = x.shape
    r_w = lam.shape[1]
    nblk = gw.shape[0]
    full = lambda a, b: pl.BlockSpec((nb, a, b), lambda i: (0, 0, 0))
    consts = [g, win, cw, cb, gw, rb, ib, lam, wout]
    return pl.pallas_call(
        functools.partial(_rglru_sample_kernel, nblk=nblk),
        grid=(1,),
        in_specs=[full(tl, d), full(1, d), full(1, d), full(1, d)] + [_const_spec(a.shape) for a in consts]
                 + [full(SUBLANES, r_w), full(1, r_w)],
        out_specs=[full(tl, d), full(SUBLANES, r_w), full(1, r_w)],
        out_shape=[jax.ShapeDtypeStruct(x.shape, F32),
                   jax.ShapeDtypeStruct((nb, SUBLANES, r_w), F32),
                   jax.ShapeDtypeStruct((nb, 1, r_w), F32)],
        scratch_shapes=[pltpu.VMEM((nb, tl + SUBLANES, r_w), F32)],
        compiler_params=_cparams(("arbitrary",), 48),
        name="rglru_sample",
    )(x, sh, sc, gt, *consts, conv0, h0)


def _pad_rows_front(a, rows):
    return jnp.pad(a, ((0, 0), (rows - a.shape[1], 0), (0, 0)))


def kernel(x_prompt, x_sample, state_a_C, state_a_n, state_a_m, cache_b_k, cache_b_v, state_c_conv, state_c_h,
           c_prompt, c_sample, ffn1_norm, ffn1_w_in, ffn1_w_out, mix_norm, ffn2_norm, ffn2_w_in, ffn2_w_out,
           ada_w, ada_b, ab_w_in, ab_gate_bias, a_out_norm, b_q_norm, b_k_norm, b_rel_bias, ab_w_out,
           c_w_in, c_conv_w, c_conv_b, c_gate_w, c_gate_b, c_lambda, c_w_out):
    nbp, seq, d = x_prompt.shape
    nbs, tdec, _ = x_sample.shape
    depth = ada_w.shape[0]
    n_ada = ada_w.shape[2] // d
    _, _, nh, dh, _ = state_a_C.shape
    _, _, w_band, nhb, dhb = cache_b_k.shape
    aw, bw = nh * dh, nhb * dhb
    ncw = c_conv_w.shape[1]
    assert 2 * dhb == LANES and dh == LANES and w_band % CHUNK == 0 and seq % w_band == 0

    ada = _ada_call(jnp.concatenate([c_prompt, c_sample], axis=0), ada_w, ada_b)
    ada = ada.reshape(depth, nbp + nbs, n_ada, 1, d)
    ada_p = [[ada[l, :nbp, k] for k in range(n_ada)] for l in range(depth)]
    ada_s = [[ada[l, nbp:, k] for k in range(n_ada)] for l in range(depth)]

    tl_p = 512
    xp, xs = x_prompt, x_sample
    outs_p, outs_s = {}, {}
    for l in range(depth):
        ap, as_ = ada_p[l], ada_s[l]
        i = l // 2
        g1 = ffn1_norm[l].reshape(1, d)
        gm = mix_norm[l].reshape(1, d)
        g2 = ffn2_norm[l].reshape(1, d)
        xp = _ffn_call(xp, ap[0], ap[1], ap[2], g1, ffn1_w_in, ffn1_w_out, l, 1, tl_p)
        xs = _ffn_call(xs, as_[0], as_[1], as_[2], g1, ffn1_w_in, ffn1_w_out, l, nbs, tdec)
        if l % 2 == 0:
            w_in = ab_w_in[i]
            wab = jnp.concatenate(
                [w_in[:, :4 * aw], w_in[:, 4 * aw + 2 * nh:], w_in[:, 4 * aw:4 * aw + 2 * nh],
                 jnp.zeros((d, LANES - 2 * nh), F32)], axis=1).astype(BF16)
            gbias = jnp.pad(ab_gate_bias[i], (0, LANES - 2 * nh)).reshape(1, LANES)
            qg = jnp.tile(b_q_norm[i], nhb).reshape(1, bw)
            kg = jnp.tile(b_k_norm[i], nhb).reshape(1, bw)
            head = jnp.arange(bw) // dhb
            e = (head[:, None] == head[None, :]).astype(BF16)
            woa = ab_w_out[i][:aw].astype(BF16)
            wob = ab_w_out[i][aw:].astype(BF16)
            gout = a_out_norm[i]
            bias = _relbias_call(b_rel_bias[i], w_band)
            bias2 = bias.reshape(nhb // 2, 2 * CHUNK, w_band + CHUNK)

            ua, gts, qn, kpad, vpad, klast, vlast = _proj_prompt_call(
                xp, ap[3], ap[4], gm, wab, gbias, qg, kg, e, nh=nh, dhb=dhb, w=w_band)
            zc = jnp.zeros((nbp, nh, dh, dh), F32)
            ha, c1, n1, m1 = _mlstm_call(ua, gts, zc, zc, zc[:, :, 0], gout, tq=256, seg=CHUNK)
            hb = _band_prompt_call(qn, kpad, vpad, bias2, w=w_band, nck=4)
            xp = _mixout_call(xp, ha, hb, ap[5], woa, wob, 1, tl_p)
            outs_p.setdefault('a_C', []).append(c1)
            outs_p.setdefault('a_n', []).append(n1)
            outs_p.setdefault('a_m', []).append(m1[:, :, 0])
            outs_p.setdefault('b_k', []).append(klast.reshape(nbp, w_band, nhb, dhb))
            outs_p.setdefault('b_v', []).append(vlast.reshape(nbp, w_band, nhb, dhb))

            ua, gts, qn, kn, vn = _proj_sample_call(
                xs, as_[3], as_[4], gm, wab, gbias, qg, kg, e, nh=nh, dhb=dhb)
            ha, c1, n1, m1 = _mlstm_call(
                ua, gts, state_a_C[i],
                jnp.broadcast_to(state_a_n[i][..., None], (nbs, nh, dh, LANES)),
                jnp.broadcast_to(state_a_m[i][..., None], (nbs, nh, LANES)), gout, tq=tdec, seg=tdec)
            hb = _band_sample_call(qn, kn, vn, cache_b_k[i].reshape(nbs, w_band, bw),
                                   cache_b_v[i].reshape(nbs, w_band, bw),
                                   bias[:, :tdec, :w_band + tdec].reshape(nhb // 2, 2 * tdec, w_band + tdec))
            xs = _mixout_call(xs, ha, hb, as_[5], woa, wob, nbs, tdec)
            outs_s.setdefault('a_C', []).append(c1)
            outs_s.setdefault('a_n', []).append(n1)
            outs_s.setdefault('a_m', []).append(m1[:, :, 0])
            outs_s.setdefault('b_k', []).append(kn.reshape(nbs, tdec, nhb, dhb))
            outs_s.setdefault('b_v', []).append(vn.reshape(nbs, tdec, nhb, dhb))
        else:
            r_w = c_lambda.shape[1]
            consts = (gm, c_w_in[i].astype(BF16), c_conv_w[i], c_conv_b[i].reshape(1, r_w), c_gate_w[i].astype(BF16),
                      c_gate_b[i][0].reshape(1, r_w), c_gate_b[i][1].reshape(1, r_w), c_lambda[i].reshape(1, r_w),
                      c_w_out[i].astype(BF16))
            xp, conv_p, h_p = _rglru_prompt_call(
                xp, ap[3], ap[4], ap[5], *consts,
                jnp.zeros((nbp, SUBLANES, r_w), F32), jnp.zeros((nbp, 1, r_w), F32), tq=256)
            xs, conv_s, h_s = _rglru_sample_call(
                xs, as_[3], as_[4], as_[5], *consts,
                _pad_rows_front(state_c_conv[i], SUBLANES), state_c_h[i][:, None, :])
            outs_p.setdefault('c_conv', []).append(conv_p[:, SUBLANES - (ncw - 1):])
            outs_p.setdefault('c_h', []).append(h_p[:, 0])
            outs_s.setdefault('c_conv', []).append(conv_s[:, SUBLANES - (ncw - 1):])
            outs_s.setdefault('c_h', []).append(h_s[:, 0])
        xp = _ffn_call(xp, ap[6], ap[7], ap[8], g2, ffn2_w_in, ffn2_w_out, l, 1, tl_p)
        xs = _ffn_call(xs, as_[6], as_[7], as_[8], g2, ffn2_w_in, ffn2_w_out, l, nbs, tdec)

    names = ('a_C', 'a_n', 'a_m', 'b_k', 'b_v', 'c_conv', 'c_h')
    ps = [jnp.stack(outs_p[n], axis=0) for n in names]
    ss = [jnp.stack(outs_s[n], axis=0) for n in names]
    return (xp, xs, *ps, *ss)
```

```python
import functools

import jax
import jax.numpy as jnp
from jax import lax
from jax.experimental import pallas as pl
from jax.experimental.pallas import tpu as pltpu

F32 = jnp.float32
BF16 = jnp.bfloat16

EPS = 1e-6
CHUNK = 64
LRU_C = 8.0
LANES = 128
SUBLANES = 8
MIB = 1024 * 1024


def _cparams(semantics, vmem_mib):
    return pltpu.CompilerParams(dimension_semantics=semantics, vmem_limit_bytes=vmem_mib * MIB)


def _const_spec(shape):
    nd = len(shape)
    return pl.BlockSpec(shape, lambda *_: (0,) * nd, pipeline_mode=pl.Buffered(1))


def _dot(a, b):
    return jnp.dot(a, b, preferred_element_type=F32)


def _dot_nt(a, b):
    return lax.dot_general(a, b, (((1,), (1,)), ((), ())), preferred_element_type=F32)


def _dot_tn(a, b):
    return lax.dot_general(a, b, (((0,), (0,)), ((), ())), preferred_element_type=F32)


def _rms_mod(x, g, shift, scale):
    ms = jnp.mean(x * x, axis=-1, keepdims=True)
    return (x * lax.rsqrt(ms + EPS)) * (g * (1.0 + scale)) + shift


def _softplus(x):
    return jnp.maximum(x, 0.0) + jnp.log1p(jnp.exp(-jnp.abs(x)))


def _gelu_tanh(x):
    return x * (0.5 * (1.0 + jnp.tanh(0.7978845608028654 * (x + 0.044715 * (x * x * x)))))


def _ada_kernel(c_ref, w_ref, b_ref, o_ref):
    c = c_ref[...].astype(BF16)
    w = w_ref[0].astype(BF16)
    o_ref[0] = _dot(c, w) + b_ref[0]


def _ada_call(c_all, ada_w, ada_b):
    depth, d, n = ada_w.shape
    m = c_all.shape[0]
    tn = d
    return pl.pallas_call(
        _ada_kernel,
        grid=(depth, n // tn),
        in_specs=[pl.BlockSpec((m, d), lambda l, j: (0, 0)),
                  pl.BlockSpec((1, d, tn), lambda l, j: (l, 0, j)),
                  pl.BlockSpec((1, 1, tn), lambda l, j: (l, 0, j))],
        out_specs=pl.BlockSpec((1, m, tn), lambda l, j: (l, 0, j)),
        out_shape=jax.ShapeDtypeStruct((depth, m, n), F32),
        compiler_params=_cparams(("arbitrary", "arbitrary"), 32),
        name="ada_proj",
    )(c_all, ada_w, ada_b.reshape(depth, 1, n))


FFN_TF = 256


def _ffn_kernel(x_ref, sh_ref, sc_ref, gt_ref, g_ref, win_ref, wo_ref, o_ref, act_scr):
    x = x_ref[...]
    bb, tl, d = x.shape
    dff = wo_ref.shape[0]
    h = _rms_mod(x, g_ref[...], sh_ref[...], sc_ref[...]).reshape(bb * tl, d).astype(BF16)
    for c0 in range(0, dff, FFN_TF):
        gate = _dot(h, win_ref[:, c0:c0 + FFN_TF].astype(BF16))
        up = _dot(h, win_ref[:, dff + c0:dff + c0 + FFN_TF].astype(BF16))
        act_scr[:, c0:c0 + FFN_TF] = ((gate * jax.nn.sigmoid(gate)) * up).astype(BF16)
    y = _dot(act_scr[...], wo_ref[...].astype(BF16))
    o_ref[...] = x + (0.5 * gt_ref[...]) * y.reshape(bb, tl, d)


def _ffn_call(x, sh, sc, gt, g, w_in, w_out, layer, bb, tl):
    nb, length, d = x.shape
    dff = w_out.shape[1]
    assert dff % FFN_TF == 0
    tm = bb * tl
    x_spec = pl.BlockSpec((bb, tl, d), lambda i, t: (i, t, 0))
    ada_spec = pl.BlockSpec((bb, 1, d), lambda i, t: (i, 0, 0))
    layer_spec = lambda shape: pl.BlockSpec((None,) + shape, lambda i, t: (layer, 0, 0),
                                            pipeline_mode=pl.Buffered(1))
    return pl.pallas_call(
        _ffn_kernel,
        grid=(nb // bb, length // tl),
        in_specs=[x_spec, ada_spec, ada_spec, ada_spec, _const_spec((1, d)),
                  layer_spec(w_in.shape[1:]), layer_spec(w_out.shape[1:])],
        out_specs=x_spec,
        out_shape=jax.ShapeDtypeStruct(x.shape, F32),
        scratch_shapes=[pltpu.VMEM((tm, dff), BF16)],
        compiler_params=_cparams(("arbitrary", "arbitrary"), 58),
        name="ffn",
    )(x, sh, sc, gt, g, w_in, w_out)


def _ffn_stream_kernel(x_ref, sh_ref, sc_ref, gt_ref, g_ref, wg_ref, wu_ref, wo_ref, o_ref, h_scr, acc_scr):
    c = pl.program_id(0)
    bb, tl, d = x_ref.shape

    @pl.when(c == 0)
    def _():
        h = _rms_mod(x_ref[...], g_ref[...], sh_ref[...], sc_ref[...])
        h_scr[...] = h.reshape(bb * tl, d).astype(BF16)
        acc_scr[...] = jnp.zeros_like(acc_scr)

    h = h_scr[...]
    gate = _dot(h, wg_ref[...].astype(BF16))
    up = _dot(h, wu_ref[...].astype(BF16))
    acc_scr[...] += _dot(((gate * jax.nn.sigmoid(gate)) * up).astype(BF16), wo_ref[...].astype(BF16))

    @pl.when(c == pl.num_programs(0) - 1)
    def _():
        o_ref[...] = x_ref[...] + (0.5 * gt_ref[...]) * acc_scr[...].reshape(bb, tl, d)


def _ffn_stream_call(x, sh, sc, gt, g, w_in, w_out, layer):
    nb, tl, d = x.shape
    dff = w_out.shape[1]
    nchunk = dff // FFN_TF
    tm = nb * tl
    full = pl.BlockSpec((nb, tl, d), lambda c: (0, 0, 0))
    ada_spec = pl.BlockSpec((nb, 1, d), lambda c: (0, 0, 0))
    return pl.pallas_call(
        _ffn_stream_kernel,
        grid=(nchunk,),
        in_specs=[full, ada_spec, ada_spec, ada_spec, _const_spec((1, d)),
                  pl.BlockSpec((None, d, FFN_TF), lambda c: (layer, 0, c)),
                  pl.BlockSpec((None, d, FFN_TF), lambda c: (layer, 0, c + nchunk)),
                  pl.BlockSpec((None, FFN_TF, d), lambda c: (layer, c, 0))],
        out_specs=full,
        out_shape=jax.ShapeDtypeStruct(x.shape, F32),
        scratch_shapes=[pltpu.VMEM((tm, d), BF16), pltpu.VMEM((tm, d), F32)],
        compiler_params=_cparams(("arbitrary",), 32),
        name="ffn_stream",
    )(x, sh, sc, gt, g, w_in, w_in, w_out)


def _head_rmsnorm(q, e, g, dhb):
    q2 = q * q
    hi = q2.astype(BF16)
    lo = (q2 - hi.astype(F32)).astype(BF16)
    ss = _dot(hi, e) + _dot(lo, e)
    return q * lax.rsqrt(ss * (1.0 / dhb) + EPS) * g


def _proj_body(x, sh, sc, g, w_ref, gb_ref, qg_ref, kg_ref, e_ref, *, nh, bw, dhb):
    bb, tl, d = x.shape
    na = w_ref.shape[1] - 3 * bw - LANES
    h = _rms_mod(x, g, sh, sc).reshape(bb * tl, d).astype(BF16)
    ua = _dot(h, w_ref[:, :na])
    gg = _dot(h, w_ref[:, na + 3 * bw:]) + gb_ref[...]
    lane = lax.broadcasted_iota(jnp.int32, gg.shape, 1)
    gates = jnp.where(lane < nh, gg, -_softplus(-gg))
    ub = _dot(h, w_ref[:, na:na + 3 * bw])
    e = e_ref[...]
    qn = _head_rmsnorm(ub[:, :bw], e, qg_ref[...], dhb) * (dhb ** -0.5)
    kn = _head_rmsnorm(ub[:, bw:2 * bw], e, kg_ref[...], dhb)
    vb = ub[:, 2 * bw:]
    return ua, gates, qn, kn, vb


def _proj_prompt_kernel(x_ref, sh_ref, sc_ref, g_ref, w_ref, gb_ref, qg_ref, kg_ref, e_ref,
                        ua_ref, gt_ref, qn_ref, kp_ref, vp_ref, kl_ref, vl_ref, *, nh, bw, dhb):
    t = pl.program_id(1)
    nt = pl.num_programs(1)

    @pl.when(t == 0)
    def _():
        kp_ref[...] = jnp.zeros_like(kp_ref)
        vp_ref[...] = jnp.zeros_like(vp_ref)

    @pl.when(t > 0)
    def _():
        ua, gates, qn, kn, vb = _proj_body(x_ref[...], sh_ref[...], sc_ref[...], g_ref[...], w_ref,
                                           gb_ref, qg_ref, kg_ref, e_ref, nh=nh, bw=bw, dhb=dhb)
        ua_ref[0] = ua
        gt_ref[0] = gates
        qn_ref[0] = qn.astype(BF16)
        kp_ref[0] = kn.astype(BF16)
        vp_ref[0] = vb.astype(BF16)

        @pl.when(t == nt - 1)
        def _():
            kl_ref[0] = kn
            vl_ref[0] = vb


def _proj_prompt_call(x, sh, sc, g, wab, gbias, qg, kg, e, *, nh, dhb, w):
    nb, length, d = x.shape
    tl = w
    nt = length // tl
    bw = e.shape[0]
    aw4 = wab.shape[1] - 3 * bw - LANES
    prev = lambda b, t: (b, jnp.maximum(t - 1, 0), 0)
    ada_spec = pl.BlockSpec((1, 1, d), lambda b, t: (b, 0, 0))
    outs = pl.pallas_call(
        functools.partial(_proj_prompt_kernel, nh=nh, bw=bw, dhb=dhb),
        grid=(nb, nt + 1),
        in_specs=[pl.BlockSpec((1, tl, d), prev), ada_spec, ada_spec, _const_spec((1, d)),
                  _const_spec(wab.shape), _const_spec(gbias.shape),
                  _const_spec(qg.shape), _const_spec(kg.shape), _const_spec(e.shape)],
        out_specs=[pl.BlockSpec((1, tl, aw4), prev),
                   pl.BlockSpec((1, tl, LANES), prev),
                   pl.BlockSpec((1, tl, bw), prev),
                   pl.BlockSpec((1, tl, bw), lambda b, t: (b, t, 0)),
                   pl.BlockSpec((1, tl, bw), lambda b, t: (b, t, 0)),
                   pl.BlockSpec((1, tl, bw), lambda b, t: (b, 0, 0)),
                   pl.BlockSpec((1, tl, bw), lambda b, t: (b, 0, 0))],
        out_shape=[jax.ShapeDtypeStruct((nb, length, aw4), F32),
                   jax.ShapeDtypeStruct((nb, length, LANES), F32),
                   jax.ShapeDtypeStruct((nb, length, bw), BF16),
                   jax.ShapeDtypeStruct((nb, length + w, bw), BF16),
                   jax.ShapeDtypeStruct((nb, length + w, bw), BF16),
                   jax.ShapeDtypeStruct((nb, w, bw), F32),
                   jax.ShapeDtypeStruct((nb, w, bw), F32)],
        compiler_params=_cparams(("arbitrary", "arbitrary"), 48),
        name="proj_prompt",
    )(x, sh, sc, g, wab, gbias, qg, kg, e)
    return outs


def _proj_sample_kernel(x_ref, sh_ref, sc_ref, g_ref, w_ref, gb_ref, qg_ref, kg_ref, e_ref,
                        ua_ref, gt_ref, qn_ref, kn_ref, vb_ref, *, nh, bw, dhb):
    bb, tl, _ = x_ref.shape
    ua, gates, qn, kn, vb = _proj_body(x_ref[...], sh_ref[...], sc_ref[...], g_ref[...], w_ref,
                                       gb_ref, qg_ref, kg_ref, e_ref, nh=nh, bw=bw, dhb=dhb)
    ua_ref[...] = ua.reshape(bb, tl, -1)
    gt_ref[...] = gates.reshape(bb, tl, -1)
    qn_ref[...] = qn.reshape(bb, tl, -1).astype(BF16)
    kn_ref[...] = kn.reshape(bb, tl, -1)
    vb_ref[...] = vb.reshape(bb, tl, -1)


def _proj_sample_call(x, sh, sc, g, wab, gbias, qg, kg, e, *, nh, dhb):
    nb, length, d = x.shape
    bw = e.shape[0]
    aw4 = wab.shape[1] - 3 * bw - LANES
    full = lambda n: pl.BlockSpec((nb, length, n), lambda i: (0, 0, 0))
    ada_spec = pl.BlockSpec((nb, 1, d), lambda i: (0, 0, 0))
    return pl.pallas_call(
        functools.partial(_proj_sample_kernel, nh=nh, bw=bw, dhb=dhb),
        grid=(1,),
        in_specs=[full(d), ada_spec, ada_spec, _const_spec((1, d)),
                  _const_spec(wab.shape), _const_spec(gbias.shape),
                  _const_spec(qg.shape), _const_spec(kg.shape), _const_spec(e.shape)],
        out_specs=[full(aw4), full(LANES), full(bw), full(bw), full(bw)],
        out_shape=[jax.ShapeDtypeStruct((nb, length, aw4), F32),
                   jax.ShapeDtypeStruct((nb, length, LANES), F32),
                   jax.ShapeDtypeStruct((nb, length, bw), BF16),
                   jax.ShapeDtypeStruct((nb, length, bw), F32),
                   jax.ShapeDtypeStruct((nb, length, bw), F32)],
        compiler_params=_cparams(("arbitrary",), 48),
        name="proj_sample",
    )(x, sh, sc, g, wab, gbias, qg, kg, e)


def _mlstm_kernel(ua_ref, g_ref, c0_ref, n0_ref, m0_ref, go_ref, ha_ref, c_ref, n_ref, m_ref,
                  nrep_scr, rep_scr, s_scr, pv_scr, kv_scr, qc_scr, *, seg, nh, dh):
    t = pl.program_id(1)

    @pl.when(t == 0)
    def _():
        c_ref[...] = c0_ref[...]
        nrep_scr[...] = n0_ref[0]
        m_ref[...] = m0_ref[...]

    tq = ua_ref.shape[1]
    nck = tq // seg
    aw = nh * dh
    gates = g_ref[0]
    pos = lax.broadcasted_iota(jnp.int32, gates.shape, 0) % seg
    bt = gates
    s = 1
    while s < seg:
        bt = bt + jnp.where(pos >= s, pltpu.roll(bt, s, 0), 0.0)
        s *= 2
    dmb = pltpu.roll(gates, nh, 1) - bt
    pm = dmb
    s = 1
    while s < seg:
        pm = jnp.maximum(pm, jnp.where(pos >= s, pltpu.roll(pm, s, 0), -jnp.inf))
        s *= 2
    if tq % LANES:
        dsq = jnp.concatenate([dmb, jnp.zeros((LANES - tq % LANES, LANES), F32)], axis=0)
    else:
        dsq = dmb
    dtr = dsq.T
    ri = lax.broadcasted_iota(jnp.int32, (seg, seg), 0)
    ci = lax.broadcasted_iota(jnp.int32, (seg, seg), 1)
    causal = ri >= ci
    ones = jnp.ones((seg, LANES), BF16)
    ones_dh = jnp.ones((dh, LANES), BF16)
    for h in range(nh):
        ln = slice(nh + h, nh + h + 1)
        for j, arr in enumerate((bt, dmb, pm)):
            rep_scr[3 * h + j] = jnp.broadcast_to(arr[:, ln], (tq, LANES))

    def cols(jc, h):
        rows = slice(jc * seg, (jc + 1) * seg)
        return rows, rep_scr[3 * h, rows, :], rep_scr[3 * h + 1, rows, :], rep_scr[3 * h + 2, rows, :]

    def last(jc, h, j):
        r = (jc + 1) * seg - 1
        return rep_scr[3 * h + j, r:r + 1, :]

    groups = [(jc, h) for jc in range(nck) for h in range(nh)]

    def qkv(jc, h, which):
        rows = slice(jc * seg, (jc + 1) * seg)
        return ua_ref[0, rows, which * aw + h * dh:which * aw + (h + 1) * dh]

    for g, (jc, h) in enumerate(groups):
        k = qkv(jc, h, 1) * (dh ** -0.5)
        s_scr[g] = _dot_nt(qkv(jc, h, 0).astype(BF16), k.astype(BF16))
    for g, (jc, h) in enumerate(groups):
        rows, _, _, p_col = cols(jc, h)
        d_row = dtr[nh + h:nh + h + 1, rows]
        dloc = jnp.exp(jnp.where(causal, d_row - p_col[:, :seg], -jnp.inf))
        sl = (s_scr[g] * dloc).astype(BF16)
        v = qkv(jc, h, 2).astype(BF16)
        pv_scr[g] = _dot(sl, jnp.concatenate([v, ones], axis=1))
    for g, (jc, h) in enumerate(groups):
        _, _, d_col, _ = cols(jc, h)
        kw = (qkv(jc, h, 1) * (dh ** -0.5)) * jnp.exp(d_col - last(jc, h, 2))
        vx = jnp.concatenate([qkv(jc, h, 2).astype(BF16), ones], axis=1)
        kv_scr[g] = _dot_tn(kw.astype(BF16), vx)
    state = [(c_ref[0, h], nrep_scr[h], m_ref[0, h:h + 1, :]) for h in range(nh)]
    before = []
    for g, (jc, h) in enumerate(groups):
        c_mem, n_rep, m = state[h]
        cn = jnp.concatenate([c_mem.astype(BF16), n_rep.astype(BF16)], axis=1)
        qc_scr[g] = _dot(qkv(jc, h, 0).astype(BF16), cn)
        before.append(m)
        p_last = last(jc, h, 2)
        mml = jnp.maximum(m, p_last)
        w_prev = jnp.exp(m - mml)
        f_new = jnp.exp(p_last - mml)
        kvx = kv_scr[g]
        state[h] = (w_prev * c_mem + f_new * kvx[:, :dh],
                    w_prev * n_rep + f_new * kvx[:, dh:],
                    last(jc, h, 0) + mml)
    for h in range(nh):
        c_ref[0, h], nrep_scr[h], m_ref[0, h:h + 1, :] = state[h]
        n_ref[0, h:h + 1, :] = state[h][1].T[0:1, :]
    for g, (jc, h) in enumerate(groups):
        rows, b_col, _, p_col = cols(jc, h)
        m = before[g]
        mm = jnp.maximum(m, p_col)
        iw = jnp.exp(m - mm)
        fl = jnp.exp(p_col - mm)
        pv = pv_scr[g]
        qc = qc_scr[g]
        num = iw * qc[:, :dh] + fl * pv[:, :dh]
        den = iw * qc[:, dh:] + fl * pv[:, dh:]
        hh = num / jnp.maximum(jnp.abs(den), jnp.exp(-(b_col + mm)))
        h2 = hh * hh
        hi = h2.astype(BF16)
        lo = (h2 - hi.astype(F32)).astype(BF16)
        ms = (_dot(hi, ones_dh) + _dot(lo, ones_dh)) * (1.0 / dh)
        hn = (hh * lax.rsqrt(ms + EPS) * go_ref[h:h + 1, :]) * jax.nn.sigmoid(qkv(jc, h, 3))
        ha_ref[0, rows, h * dh:(h + 1) * dh] = hn.astype(BF16)


def _mlstm_call(ua, gates, c0, n0rep, m0rep, gout, *, tq, seg):
    nb, length, aw4 = ua.shape
    _, nh, dh, _ = c0.shape
    assert dh == LANES
    groups = (tq // seg) * nh
    st = lambda shape: pl.BlockSpec((1,) + shape, lambda b, t: (b,) + (0,) * len(shape))
    tile = lambda n: pl.BlockSpec((1, tq, n), lambda b, t: (b, t, 0))
    return pl.pallas_call(
        functools.partial(_mlstm_kernel, seg=seg, nh=nh, dh=dh),
        grid=(nb, length // tq),
        in_specs=[tile(aw4), tile(LANES), st((nh, dh, dh)), st((nh, dh, LANES)), st((nh, LANES)),
                  _const_spec(gout.shape)],
        out_specs=[tile(nh * dh), st((nh, dh, dh)), st((nh, dh)), st((nh, LANES))],
        out_shape=[jax.ShapeDtypeStruct((nb, length, nh * dh), BF16),
                   jax.ShapeDtypeStruct((nb, nh, dh, dh), F32),
                   jax.ShapeDtypeStruct((nb, nh, dh), F32),
                   jax.ShapeDtypeStruct((nb, nh, LANES), F32)],
        scratch_shapes=[pltpu.VMEM((nh, dh, LANES), F32), pltpu.VMEM((3 * nh, tq, LANES), F32),
                        pltpu.VMEM((groups, seg, seg), F32), pltpu.VMEM((groups, seg, dh + LANES), F32),
                        pltpu.VMEM((groups, dh, dh + LANES), F32), pltpu.VMEM((groups, seg, dh + LANES), F32)],
        compiler_params=_cparams(("arbitrary", "arbitrary"), 32),
        name="mlstm",
    )(ua, gates, c0, n0rep, m0rep, gout)


def _relbias_kernel(b0_ref, o_ref):
    nhb, nq, nk = o_ref.shape
    for h in range(nhb):
        x = jnp.broadcast_to(b0_ref[h:h + 1, :], (nq, b0_ref.shape[1]))
        o_ref[h] = pltpu.roll(x, 0, 1, stride=1, stride_axis=0)[:, :nk]


def _relbias_call(table, w):
    nhb = table.shape[0]
    max_rel = (table.shape[1] - 1) // 2
    assert CHUNK - 1 <= max_rel <= w
    first = jnp.broadcast_to(table[:, :1], (nhb, w - max_rel))
    wrap = jnp.broadcast_to(table[:, :1], (nhb, CHUNK))
    b0 = jnp.concatenate([first, table[:, :max_rel + CHUNK], wrap], axis=1).astype(F32)
    return pl.pallas_call(
        _relbias_kernel,
        out_shape=jax.ShapeDtypeStruct((nhb, CHUNK, w + CHUNK), F32),
        name="rel_bias",
    )(b0)


def _band_prompt_kernel(q_ref, k_ref, v_ref, bias_ref, o_ref, s_scr, m_scr, e_scr, *, npair, w, nck):
    c4 = pl.program_id(1)
    nk = w + CHUNK
    lane = lax.broadcasted_iota(jnp.int32, (CHUNK, LANES), 1)
    low = lane < LANES // 2
    zero = jnp.zeros((CHUNK, LANES), BF16)
    ones = jnp.ones((nk, LANES), BF16)

    def run(masked):
        starts = [pl.multiple_of((c4 * nck + jc) * CHUNK, CHUNK) for jc in range(nck)]
        for jc in range(nck):
            for p in range(npair):
                g = jc * npair + p
                sl = slice(p * LANES, (p + 1) * LANES)
                qp = q_ref[0, jc * CHUNK:(jc + 1) * CHUNK, sl]
                q2 = jnp.concatenate([jnp.where(low, qp, zero), jnp.where(low, zero, qp)], axis=0)
                s = _dot_nt(q2, k_ref[0, pl.ds(starts[jc], nk), sl]) + bias_ref[p]
                if masked:
                    col = lax.broadcasted_iota(jnp.int32, s.shape, 1)
                    s = jnp.where(col + starts[jc] >= w, s, -jnp.inf)
                s_scr[g] = s
                m_scr[g] = jnp.max(s, axis=-1, keepdims=True)
        for g in range(nck * npair):
            e_scr[g] = jnp.exp(s_scr[g] - m_scr[g]).astype(BF16)
        for jc in range(nck):
            for p in range(npair):
                g = jc * npair + p
                sl = slice(p * LANES, (p + 1) * LANES)
                vx = jnp.concatenate([v_ref[0, pl.ds(starts[jc], nk), sl], ones], axis=1)
                r = _dot(e_scr[g], vx)
                o_lo = r[:CHUNK, :LANES] / r[:CHUNK, LANES:]
                o_hi = r[CHUNK:, :LANES] / r[CHUNK:, LANES:]
                o_ref[0, jc * CHUNK:(jc + 1) * CHUNK, sl] = jnp.where(low, o_lo, o_hi).astype(BF16)

    first_full = w // (CHUNK * nck)

    @pl.when(c4 < first_full)
    def _():
        run(True)

    @pl.when(c4 >= first_full)
    def _():
        run(False)


def _band_prompt_call(qs, kpad, vpad, bias2, *, w, nck):
    nb, length, bw = qs.shape
    npair = bias2.shape[0]
    lp = kpad.shape[1]
    tq = nck * CHUNK
    nk = w + CHUNK
    groups = nck * npair
    assert w % tq == 0
    return pl.pallas_call(
        functools.partial(_band_prompt_kernel, npair=npair, w=w, nck=nck),
        grid=(nb, length // tq),
        in_specs=[pl.BlockSpec((1, tq, bw), lambda b, c: (b, c, 0)),
                  pl.BlockSpec((1, lp, bw), lambda b, c: (b, 0, 0)),
                  pl.BlockSpec((1, lp, bw), lambda b, c: (b, 0, 0)),
                  _const_spec(bias2.shape)],
        out_specs=pl.BlockSpec((1, tq, bw), lambda b, c: (b, c, 0)),
        out_shape=jax.ShapeDtypeStruct((nb, length, bw), BF16),
        scratch_shapes=[pltpu.VMEM((groups, 2 * CHUNK, nk), F32),
                        pltpu.VMEM((groups, 2 * CHUNK, 1), F32),
                        pltpu.VMEM((groups, 2 * CHUNK, nk), BF16)],
        compiler_params=_cparams(("arbitrary", "arbitrary"), 40),
        name="band_prompt",
    )(qs, kpad, vpad, bias2)


def _band_sample_kernel(q_ref, kn_ref, vn_ref, ck_ref, cv_ref, bias_ref, o_ref):
    dhb = q_ref.shape[2]
    kx = jnp.concatenate([ck_ref[0].astype(BF16), kn_ref[0].astype(BF16)], axis=0)
    s = _dot_nt(q_ref[0], kx) + bias_ref[...]
    e = jnp.exp(s - jnp.max(s, axis=-1, keepdims=True)).astype(BF16)
    vx = jnp.concatenate([cv_ref[0].astype(BF16), vn_ref[0].astype(BF16)], axis=0)
    r = _dot(e, jnp.concatenate([vx, jnp.ones_like(vx)], axis=1))
    o_ref[0] = (r[:, :dhb] / r[:, dhb:]).astype(BF16)


def _band_sample_call(q2, kn2, vn2, ck2, cv2, bias_big):
    nb, rows_q, dhb = q2.shape
    blk = lambda a: pl.BlockSpec((1,) + a.shape[1:], lambda b: (b, 0, 0))
    return pl.pallas_call(
        _band_sample_kernel,
        grid=(nb,),
        in_specs=[blk(q2), blk(kn2), blk(vn2), blk(ck2), blk(cv2), _const_spec(bias_big.shape)],
        out_specs=blk(q2),
        out_shape=jax.ShapeDtypeStruct((nb, rows_q, dhb), BF16),
        compiler_params=_cparams(("arbitrary",), 40),
        name="band_sample",
    )(q2, kn2, vn2, ck2, cv2, bias_big)


def _mixout_kernel(x_ref, ha_ref, hb_ref, gt_ref, woa_ref, wob_ref, o_ref):
    bb, tl, d = x_ref.shape
    ha = ha_ref[...].reshape(bb * tl, -1)
    hb = hb_ref[...].reshape(bb * tl, -1)
    y = _dot(ha, woa_ref[...]) + _dot(hb, wob_ref[...])
    o_ref[...] = x_ref[...] + gt_ref[...] * y.reshape(bb, tl, d)


def _mixout_call(x, ha, hb, gt, woa, wob, bb, tl):
    nb, length, d = x.shape
    tile = lambda n: pl.BlockSpec((bb, tl, n), lambda i, t: (i, t, 0))
    return pl.pallas_call(
        _mixout_kernel,
        grid=(nb // bb, length // tl),
        in_specs=[tile(d), tile(ha.shape[-1]), tile(hb.shape[-1]),
                  pl.BlockSpec((bb, 1, d), lambda i, t: (i, 0, 0)),
                  _const_spec(woa.shape), _const_spec(wob.shape)],
        out_specs=tile(d),
        out_shape=jax.ShapeDtypeStruct(x.shape, F32),
        compiler_params=_cparams(("arbitrary", "arbitrary"), 32),
        name="mix_out",
    )(x, ha, hb, gt, woa, wob)


def _rglru_gates(xc, gw_ref, rb, ib, lam, nblk):
    bwc = xc.shape[1] // nblk
    r_parts, i_parts = [], []
    for n in range(nblk):
        gn = _dot(xc[:, n * bwc:(n + 1) * bwc].astype(BF16), gw_ref[n])
        r_parts.append(gn[:, :bwc])
        i_parts.append(gn[:, bwc:])
    r = jax.nn.sigmoid(jnp.concatenate(r_parts, axis=1) + rb)
    ii = jax.nn.sigmoid(jnp.concatenate(i_parts, axis=1) + ib)
    log_a = (-LRU_C * r) * _softplus(-lam)
    a = jnp.exp(log_a)
    th = jnp.tanh(log_a)
    upd = jnp.sqrt(-2.0 * th / (1.0 - th)) * (ii * xc)
    return a, upd


def _rglru_prompt_kernel(x_ref, sh_ref, sc_ref, gt_ref, g_ref, win_ref, cw_ref, cb_ref, gw_ref, rb_ref, ib_ref,
                         lam_ref, wout_ref, conv0_ref, h0_ref, o_ref, conv_ref, hl_ref, xp_scr, a_scr, b_scr, *, nblk):
    t = pl.program_id(1)
    tq, d = x_ref.shape[1], x_ref.shape[2]
    r_w = lam_ref.shape[1]
    ncw = cw_ref.shape[0]

    @pl.when(t == 0)
    def _():
        hl_ref[...] = h0_ref[...]
        for j in range(1, ncw):
            zj = cw_ref[0:1, :] * conv0_ref[0, SUBLANES - j:SUBLANES - j + 1, :]
            for i in range(1, j):
                zj = zj + cw_ref[i:i + 1, :] * conv0_ref[0, SUBLANES - j + i:SUBLANES - j + i + 1, :]
            xp_scr[j - 1:j, :] = zj

    x = x_ref[0]
    hm = _rms_mod(x, g_ref[...], sh_ref[0], sc_ref[0]).astype(BF16)
    u = _dot(hm, win_ref[...])
    gb = u[:, :r_w]
    xb = u[:, r_w:]
    conv_ref[0] = xb[tq - SUBLANES:, :]
    first = lax.broadcasted_iota(jnp.int32, (SUBLANES, r_w), 0) == 0

    def delay(z, j):
        rolled = pltpu.roll(z, 1, 0)
        head = jnp.where(first, xp_scr[j - 1:j, :], rolled[:SUBLANES])
        xp_scr[j - 1:j, :] = z[tq - 1:tq, :]
        return jnp.concatenate([head, rolled[SUBLANES:]], axis=0)

    z = xb * cw_ref[0:1, :]
    for j in range(1, ncw - 1):
        z = xb * cw_ref[j:j + 1, :] + delay(z, j)
    xc = xb * cw_ref[ncw - 1:ncw, :] + delay(z, ncw - 1) + cb_ref[...]
    a, upd = _rglru_gates(xc, gw_ref, rb_ref[...], ib_ref[...], lam_ref[...], nblk)
    a_scr[...] = a
    b_scr[...] = upd
    row8 = lax.broadcasted_iota(jnp.int32, (SUBLANES, r_w), 0)

    def scan_body(i, h):
        rows = pl.ds(pl.multiple_of(i * SUBLANES, SUBLANES), SUBLANES)
        ai = a_scr[rows, :]
        bi = b_scr[rows, :]
        s = 1
        while s < SUBLANES:
            m = row8 >= s
            bi = jnp.where(m, ai * pltpu.roll(bi, s, 0) + bi, bi)
            ai = jnp.where(m, ai * pltpu.roll(ai, s, 0), ai)
            s *= 2
        hs = ai * h + bi
        a_scr[rows, :] = hs
        return hs[SUBLANES - 1:SUBLANES, :]

    h_fin = lax.fori_loop(0, tq // SUBLANES, scan_body, hl_ref[0])
    hl_ref[0] = h_fin
    y = _dot((_gelu_tanh(gb) * a_scr[...]).astype(BF16), wout_ref[...])
    o_ref[0] = x + gt_ref[0] * y


def _rglru_prompt_call(x, sh, sc, gt, g, win, cw, cb, gw, rb, ib, lam, wout, conv0, h0, *, tq):
    nb, length, d = x.shape
    r_w = lam.shape[1]
    nblk = gw.shape[0]
    ada_spec = pl.BlockSpec((1, 1, d), lambda b, t: (b, 0, 0))
    tile = pl.BlockSpec((1, tq, d), lambda b, t: (b, t, 0))
    conv_spec = pl.BlockSpec((1, SUBLANES, r_w), lambda b, t: (b, 0, 0))
    h_spec = pl.BlockSpec((1, 1, r_w), lambda b, t: (b, 0, 0))
    consts = [g, win, cw, cb, gw, rb, ib, lam, wout]
    return pl.pallas_call(
        functools.partial(_rglru_prompt_kernel, nblk=nblk),
        grid=(nb, length // tq),
        in_specs=[tile, ada_spec, ada_spec, ada_spec] + [_const_spec(a.shape) for a in consts] + [conv_spec, h_spec],
        out_specs=[tile, conv_spec, h_spec],
        out_shape=[jax.ShapeDtypeStruct(x.shape, F32),
                   jax.ShapeDtypeStruct((nb, SUBLANES, r_w), F32),
                   jax.ShapeDtypeStruct((nb, 1, r_w), F32)],
        scratch_shapes=[pltpu.VMEM((SUBLANES, r_w), F32), pltpu.VMEM((tq, r_w), F32), pltpu.VMEM((tq, r_w), F32)],
        compiler_params=_cparams(("arbitrary", "arbitrary"), 48),
        name="rglru_prompt",
    )(x, sh, sc, gt, *consts, conv0, h0)


def _rglru_sample_kernel(x_ref, sh_ref, sc_ref, gt_ref, g_ref, win_ref, cw_ref, cb_ref, gw_ref, rb_ref, ib_ref,
                         lam_ref, wout_ref, conv0_ref, h0_ref, o_ref, conv_ref, hl_ref, xp_scr, *, nblk):
    bb, tl, d = x_ref.shape
    tm = bb * tl
    r_w = lam_ref.shape[1]
    ncw = cw_ref.shape[0]
    x = x_ref[...]
    hm = _rms_mod(x, g_ref[...], sh_ref[...], sc_ref[...]).reshape(tm, d).astype(BF16)
    u = _dot(hm, win_ref[...])
    gb = u[:, :r_w]
    xp_scr[:, 0:SUBLANES, :] = conv0_ref[...]
    xp_scr[:, SUBLANES:SUBLANES + tl, :] = u[:, r_w:].reshape(bb, tl, r_w)
    xc = jnp.broadcast_to(cb_ref[...], (bb, tl, r_w))
    for j in range(ncw):
        off = SUBLANES - (ncw - 1 - j)
        xc = xc + xp_scr[:, off:off + tl, :] * cw_ref[j:j + 1, :]
    conv_ref[...] = xp_scr[:, tl:tl + SUBLANES, :]
    a, b = _rglru_gates(xc.reshape(tm, r_w), gw_ref, rb_ref[...], ib_ref[...], lam_ref[...], nblk)
    pos = lax.broadcasted_iota(jnp.int32, (tm, r_w), 0) % tl
    s = 1
    while s < tl:
        m = pos >= s
        b = jnp.where(m, a * pltpu.roll(b, s, 0) + b, b)
        a = jnp.where(m, a * pltpu.roll(a, s, 0), a)
        s *= 2
    hs = a.reshape(bb, tl, r_w) * h0_ref[...] + b.reshape(bb, tl, r_w)
    hl_ref[...] = hs[:, tl - 1:tl, :]
    y = _dot((_gelu_tanh(gb) * hs.reshape(tm, r_w)).astype(BF16), wout_ref[...])
    o_ref[...] = x + gt_ref[...] * y.reshape(bb, tl, d)


def _rglru_sample_call(x, sh, sc, gt, g, win, cw, cb, gw, rb, ib, lam, wout, conv0, h0):
    nb, tl, d = x.shape
    r_w = lam.shape[1]
    nblk = gw.shape[0]
    full = lambda a, b: pl.BlockSpec((nb, a, b), lambda i: (0, 0, 0))
    consts = [g, win, cw, cb, gw, rb, ib, lam, wout]
    return pl.pallas_call(
        functools.partial(_rglru_sample_kernel, nblk=nblk),
        grid=(1,),
        in_specs=[full(tl, d), full(1, d), full(1, d), full(1, d)] + [_const_spec(a.shape) for a in consts]
                 + [full(SUBLANES, r_w), full(1, r_w)],
        out_specs=[full(tl, d), full(SUBLANES, r_w), full(1, r_w)],
        out_shape=[jax.ShapeDtypeStruct(x.shape, F32),
                   jax.ShapeDtypeStruct((nb, SUBLANES, r_w), F32),
                   jax.ShapeDtypeStruct((nb, 1, r_w), F32)],
        scratch_shapes=[pltpu.VMEM((nb, tl + SUBLANES, r_w), F32)],
        compiler_params=_cparams(("arbitrary",), 48),
        name="rglru_sample",
    )(x, sh, sc, gt, *consts, conv0, h0)


def _pad_rows_front(a, rows):
    return jnp.pad(a, ((0, 0), (rows - a.shape[1], 0), (0, 0)))


def kernel(x_prompt, x_sample, state_a_C, state_a_n, state_a_m, cache_b_k, cache_b_v, state_c_conv, state_c_h,
           c_prompt, c_sample, ffn1_norm, ffn1_w_in, ffn1_w_out, mix_norm, ffn2_norm, ffn2_w_in, ffn2_w_out,
           ada_w, ada_b, ab_w_in, ab_gate_bias, a_out_norm, b_q_norm, b_k_norm, b_rel_bias, ab_w_out,
           c_w_in, c_conv_w, c_conv_b, c_gate_w, c_gate_b, c_lambda, c_w_out):
    nbp, seq, d = x_prompt.shape
    nbs, tdec, _ = x_sample.shape
    depth = ada_w.shape[0]
    n_ada = ada_w.shape[2] // d
    _, _, nh, dh, _ = state_a_C.shape
    _, _, w_band, nhb, dhb = cache_b_k.shape
    aw, bw = nh * dh, nhb * dhb
    ncw = c_conv_w.shape[1]
    assert 2 * dhb == LANES and dh == LANES and w_band % CHUNK == 0 and seq % w_band == 0

    ada = _ada_call(jnp.concatenate([c_prompt, c_sample], axis=0), ada_w, ada_b)
    ada = ada.reshape(depth, nbp + nbs, n_ada, 1, d)
    ada_p = [[ada[l, :nbp, k] for k in range(n_ada)] for l in range(depth)]
    ada_s = [[ada[l, nbp:, k] for k in range(n_ada)] for l in range(depth)]

    tl_p = 512
    xp, xs = x_prompt, x_sample
    outs_p, outs_s = {}, {}
    for l in range(depth):
        ap, as_ = ada_p[l], ada_s[l]
        i = l // 2
        g1 = ffn1_norm[l].reshape(1, d)
        gm = mix_norm[l].reshape(1, d)
        g2 = ffn2_norm[l].reshape(1, d)
        xp = _ffn_call(xp, ap[0], ap[1], ap[2], g1, ffn1_w_in, ffn1_w_out, l, 1, tl_p)
        xs = _ffn_stream_call(xs, as_[0], as_[1], as_[2], g1, ffn1_w_in, ffn1_w_out, l)
        if l % 2 == 0:
            w_in = ab_w_in[i]
            wab = jnp.concatenate(
                [w_in[:, :4 * aw], w_in[:, 4 * aw + 2 * nh:], w_in[:, 4 * aw:4 * aw + 2 * nh],
                 jnp.zeros((d, LANES - 2 * nh), F32)], axis=1).astype(BF16)
            gbias = jnp.pad(ab_gate_bias[i], (0, LANES - 2 * nh)).reshape(1, LANES)
            qg = jnp.tile(b_q_norm[i], nhb).reshape(1, bw)
            kg = jnp.tile(b_k_norm[i], nhb).reshape(1, bw)
            head = jnp.arange(bw) // dhb
            e = (head[:, None] == head[None, :]).astype(BF16)
            woa = ab_w_out[i][:aw].astype(BF16)
            wob = ab_w_out[i][aw:].astype(BF16)
            gout = a_out_norm[i]
            bias = _relbias_call(b_rel_bias[i], w_band)
            bias2 = bias.reshape(nhb // 2, 2 * CHUNK, w_band + CHUNK)

            ua, gts, qn, kpad, vpad, klast, vlast = _proj_prompt_call(
                xp, ap[3], ap[4], gm, wab, gbias, qg, kg, e, nh=nh, dhb=dhb, w=w_band)
            zc = jnp.zeros((nbp, nh, dh, dh), F32)
            ha, c1, n1, m1 = _mlstm_call(ua, gts, zc, zc, zc[:, :, 0], gout, tq=256, seg=CHUNK)
            hb = _band_prompt_call(qn, kpad, vpad, bias2, w=w_band, nck=4)
            xp = _mixout_call(xp, ha, hb, ap[5], woa, wob, 1, tl_p)
            outs_p.setdefault('a_C', []).append(c1)
            outs_p.setdefault('a_n', []).append(n1)
            outs_p.setdefault('a_m', []).append(m1[:, :, 0])
            outs_p.setdefault('b_k', []).append(klast.reshape(nbp, w_band, nhb, dhb))
            outs_p.setdefault('b_v', []).append(vlast.reshape(nbp, w_band, nhb, dhb))

            ua, gts, qn, kn, vn = _proj_sample_call(
                xs, as_[3], as_[4], gm, wab, gbias, qg, kg, e, nh=nh, dhb=dhb)
            ha, c1, n1, m1 = _mlstm_call(
                ua, gts, state_a_C[i],
                jnp.broadcast_to(state_a_n[i][..., None], (nbs, nh, dh, LANES)),
                jnp.broadcast_to(state_a_m[i][..., None], (nbs, nh, LANES)), gout, tq=tdec, seg=tdec)
            nks = w_band + tdec
            same_head = jnp.arange(nhb)[:, None, None, None] == jnp.arange(nhb)[None, None, None, :]
            bias_big = jnp.where(same_head, bias[:, :tdec, :nks, None], -jnp.inf).reshape(nhb * tdec, nks * nhb)
            to_rows = lambda a: a.reshape(nbs, tdec * nhb, dhb)
            o2 = _band_sample_call(
                qn.reshape(nbs, tdec, nhb, dhb).transpose(0, 2, 1, 3).reshape(nbs, nhb * tdec, dhb),
                to_rows(kn), to_rows(vn),
                cache_b_k[i].reshape(nbs, w_band * nhb, dhb), cache_b_v[i].reshape(nbs, w_band * nhb, dhb), bias_big)
            hb = o2.reshape(nbs, nhb, tdec, dhb).transpose(0, 2, 1, 3).reshape(nbs, tdec, bw)
            xs = _mixout_call(xs, ha, hb, as_[5], woa, wob, nbs, tdec)
            outs_s.setdefault('a_C', []).append(c1)
            outs_s.setdefault('a_n', []).append(n1)
            outs_s.setdefault('a_m', []).append(m1[:, :, 0])
            outs_s.setdefault('b_k', []).append(kn.reshape(nbs, tdec, nhb, dhb))
            outs_s.setdefault('b_v', []).append(vn.reshape(nbs, tdec, nhb, dhb))
        else:
            r_w = c_lambda.shape[1]
            consts = (gm, c_w_in[i].astype(BF16), c_conv_w[i], c_conv_b[i].reshape(1, r_w), c_gate_w[i].astype(BF16),
                      c_gate_b[i][0].reshape(1, r_w), c_gate_b[i][1].reshape(1, r_w), c_lambda[i].reshape(1, r_w),
                      c_w_out[i].astype(BF16))
            xp, conv_p, h_p = _rglru_prompt_call(
                xp, ap[3], ap[4], ap[5], *consts,
                jnp.zeros((nbp, SUBLANES, r_w), F32), jnp.zeros((nbp, 1, r_w), F32), tq=256)
            xs, conv_s, h_s = _rglru_sample_call(
                xs, as_[3], as_[4], as_[5], *consts,
                _pad_rows_front(state_c_conv[i], SUBLANES), state_c_h[i][:, None, :])
            outs_p.setdefault('c_conv', []).append(conv_p[:, SUBLANES - (ncw - 1):])
            outs_p.setdefault('c_h', []).append(h_p[:, 0])
            outs_s.setdefault('c_conv', []).append(conv_s[:, SUBLANES - (ncw - 1):])
            outs_s.setdefault('c_h', []).append(h_s[:, 0])
        xp = _ffn_call(xp, ap[6], ap[7], ap[8], g2, ffn2_w_in, ffn2_w_out, l, 1, tl_p)
        xs = _ffn_stream_call(xs, as_[6], as_[7], as_[8], g2, ffn2_w_in, ffn2_w_out, l)

    names = ('a_C', 'a_n', 'a_m', 'b_k', 'b_v', 'c_conv', 'c_h')
    ps = [jnp.stack(outs_p[n], axis=0) for n in names]
    ss = [jnp.stack(outs_s[n], axis=0) for n in names]
    return (xp, xs, *ps, *ss)
```

```python
import functools

import jax
import jax.numpy as jnp
from jax import lax
from jax.experimental import pallas as pl
from jax.experimental.pallas import tpu as pltpu

F32 = jnp.float32
BF16 = jnp.bfloat16

EPS = 1e-6
CHUNK = 64
LRU_C = 8.0
LANES = 128
SUBLANES = 8
MIB = 1024 * 1024


def _cparams(semantics, vmem_mib):
    return pltpu.CompilerParams(dimension_semantics=semantics, vmem_limit_bytes=vmem_mib * MIB)


def _const_spec(shape):
    nd = len(shape)
    return pl.BlockSpec(shape, lambda *_: (0,) * nd, pipeline_mode=pl.Buffered(1))


def _dot(a, b):
    return jnp.dot(a, b, preferred_element_type=F32)


def _dot_nt(a, b):
    return lax.dot_general(a, b, (((1,), (1,)), ((), ())), preferred_element_type=F32)


def _dot_tn(a, b):
    return lax.dot_general(a, b, (((0,), (0,)), ((), ())), preferred_element_type=F32)


def _rms_mod(x, g, shift, scale):
    ms = jnp.mean(x * x, axis=-1, keepdims=True)
    return (x * lax.rsqrt(ms + EPS)) * (g * (1.0 + scale)) + shift


def _softplus(x):
    return jnp.maximum(x, 0.0) + jnp.log1p(jnp.exp(-jnp.abs(x)))


def _gelu_tanh(x):
    return x * (0.5 * (1.0 + jnp.tanh(0.7978845608028654 * (x + 0.044715 * (x * x * x)))))


def _ada_kernel(c_ref, w_ref, b_ref, o_ref):
    c = c_ref[...].astype(BF16)
    w = w_ref[0].astype(BF16)
    o_ref[0] = _dot(c, w) + b_ref[0]


def _ada_call(c_all, ada_w, ada_b):
    depth, d, n = ada_w.shape
    m = c_all.shape[0]
    tn = d
    return pl.pallas_call(
        _ada_kernel,
        grid=(depth, n // tn),
        in_specs=[pl.BlockSpec((m, d), lambda l, j: (0, 0)),
                  pl.BlockSpec((1, d, tn), lambda l, j: (l, 0, j)),
                  pl.BlockSpec((1, 1, tn), lambda l, j: (l, 0, j))],
        out_specs=pl.BlockSpec((1, m, tn), lambda l, j: (l, 0, j)),
        out_shape=jax.ShapeDtypeStruct((depth, m, n), F32),
        compiler_params=_cparams(("arbitrary", "arbitrary"), 32),
        name="ada_proj",
    )(c_all, ada_w, ada_b.reshape(depth, 1, n))


FFN_TF = 256


def _ffn_kernel(x_ref, sh_ref, sc_ref, gt_ref, g_ref, win_ref, wo_ref, o_ref, act_scr):
    x = x_ref[...]
    bb, tl, d = x.shape
    dff = wo_ref.shape[0]
    h = _rms_mod(x, g_ref[...], sh_ref[...], sc_ref[...]).reshape(bb * tl, d).astype(BF16)
    for c0 in range(0, dff, FFN_TF):
        gate = _dot(h, win_ref[:, c0:c0 + FFN_TF].astype(BF16))
        up = _dot(h, win_ref[:, dff + c0:dff + c0 + FFN_TF].astype(BF16))
        act_scr[:, c0:c0 + FFN_TF] = ((gate * jax.nn.sigmoid(gate)) * up).astype(BF16)
    y = _dot(act_scr[...], wo_ref[...].astype(BF16))
    o_ref[...] = x + (0.5 * gt_ref[...]) * y.reshape(bb, tl, d)


def _ffn_call(x, sh, sc, gt, g, w_in, w_out, layer, bb, tl):
    nb, length, d = x.shape
    dff = w_out.shape[1]
    assert dff % FFN_TF == 0
    tm = bb * tl
    x_spec = pl.BlockSpec((bb, tl, d), lambda i, t: (i, t, 0))
    ada_spec = pl.BlockSpec((bb, 1, d), lambda i, t: (i, 0, 0))
    layer_spec = lambda shape: pl.BlockSpec((None,) + shape, lambda i, t: (layer, 0, 0),
                                            pipeline_mode=pl.Buffered(1))
    return pl.pallas_call(
        _ffn_kernel,
        grid=(nb // bb, length // tl),
        in_specs=[x_spec, ada_spec, ada_spec, ada_spec, _const_spec((1, d)),
                  layer_spec(w_in.shape[1:]), layer_spec(w_out.shape[1:])],
        out_specs=x_spec,
        out_shape=jax.ShapeDtypeStruct(x.shape, F32),
        scratch_shapes=[pltpu.VMEM((tm, dff), BF16)],
        compiler_params=_cparams(("arbitrary", "arbitrary"), 58),
        name="ffn",
    )(x, sh, sc, gt, g, w_in, w_out)


def _ffn_stream_kernel(x_ref, sh_ref, sc_ref, gt_ref, g_ref, wg_ref, wu_ref, wo_ref, o_ref, h_scr, acc_scr):
    c = pl.program_id(0)
    bb, tl, d = x_ref.shape

    @pl.when(c == 0)
    def _():
        h = _rms_mod(x_ref[...], g_ref[...], sh_ref[...], sc_ref[...])
        h_scr[...] = h.reshape(bb * tl, d).astype(BF16)
        acc_scr[...] = jnp.zeros_like(acc_scr)

    h = h_scr[...]
    gate = _dot(h, wg_ref[...].astype(BF16))
    up = _dot(h, wu_ref[...].astype(BF16))
    acc_scr[...] += _dot(((gate * jax.nn.sigmoid(gate)) * up).astype(BF16), wo_ref[...].astype(BF16))

    @pl.when(c == pl.num_programs(0) - 1)
    def _():
        o_ref[...] = x_ref[...] + (0.5 * gt_ref[...]) * acc_scr[...].reshape(bb, tl, d)


def _ffn_stream_call(x, sh, sc, gt, g, w_in, w_out, layer):
    nb, tl, d = x.shape
    dff = w_out.shape[1]
    nchunk = dff // FFN_TF
    tm = nb * tl
    full = pl.BlockSpec((nb, tl, d), lambda c: (0, 0, 0))
    ada_spec = pl.BlockSpec((nb, 1, d), lambda c: (0, 0, 0))
    return pl.pallas_call(
        _ffn_stream_kernel,
        grid=(nchunk,),
        in_specs=[full, ada_spec, ada_spec, ada_spec, _const_spec((1, d)),
                  pl.BlockSpec((None, d, FFN_TF), lambda c: (layer, 0, c)),
                  pl.BlockSpec((None, d, FFN_TF), lambda c: (layer, 0, c + nchunk)),
                  pl.BlockSpec((None, FFN_TF, d), lambda c: (layer, c, 0))],
        out_specs=full,
        out_shape=jax.ShapeDtypeStruct(x.shape, F32),
        scratch_shapes=[pltpu.VMEM((tm, d), BF16), pltpu.VMEM((tm, d), F32)],
        compiler_params=_cparams(("arbitrary",), 32),
        name="ffn_stream",
    )(x, sh, sc, gt, g, w_in, w_in, w_out)


def _head_rmsnorm(q, e, g, dhb):
    q2 = q * q
    hi = q2.astype(BF16)
    lo = (q2 - hi.astype(F32)).astype(BF16)
    ss = _dot(hi, e) + _dot(lo, e)
    return q * lax.rsqrt(ss * (1.0 / dhb) + EPS) * g


def _proj_body(x, sh, sc, g, w_ref, gb_ref, qg_ref, kg_ref, e_ref, *, nh, bw, dhb):
    bb, tl, d = x.shape
    na = w_ref.shape[1] - 3 * bw - LANES
    h = _rms_mod(x, g, sh, sc).reshape(bb * tl, d).astype(BF16)
    ua = _dot(h, w_ref[:, :na])
    gg = _dot(h, w_ref[:, na + 3 * bw:]) + gb_ref[...]
    lane = lax.broadcasted_iota(jnp.int32, gg.shape, 1)
    gates = jnp.where(lane < nh, gg, -_softplus(-gg))
    ub = _dot(h, w_ref[:, na:na + 3 * bw])
    e = e_ref[...]
    qn = _head_rmsnorm(ub[:, :bw], e, qg_ref[...], dhb) * (dhb ** -0.5)
    kn = _head_rmsnorm(ub[:, bw:2 * bw], e, kg_ref[...], dhb)
    vb = ub[:, 2 * bw:]
    return ua, gates, qn, kn, vb


def _proj_prompt_kernel(x_ref, sh_ref, sc_ref, g_ref, w_ref, gb_ref, qg_ref, kg_ref, e_ref,
                        ua_ref, gt_ref, qn_ref, kp_ref, vp_ref, kl_ref, vl_ref, *, nh, bw, dhb):
    t = pl.program_id(1)
    nt = pl.num_programs(1)

    @pl.when(t == 0)
    def _():
        kp_ref[...] = jnp.zeros_like(kp_ref)
        vp_ref[...] = jnp.zeros_like(vp_ref)

    @pl.when(t > 0)
    def _():
        ua, gates, qn, kn, vb = _proj_body(x_ref[...], sh_ref[...], sc_ref[...], g_ref[...], w_ref,
                                           gb_ref, qg_ref, kg_ref, e_ref, nh=nh, bw=bw, dhb=dhb)
        ua_ref[0] = ua
        gt_ref[0] = gates
        qn_ref[0] = qn.astype(BF16)
        kp_ref[0] = kn.astype(BF16)
        vp_ref[0] = vb.astype(BF16)

        @pl.when(t == nt - 1)
        def _():
            kl_ref[0] = kn
            vl_ref[0] = vb


def _proj_prompt_call(x, sh, sc, g, wab, gbias, qg, kg, e, *, nh, dhb, w):
    nb, length, d = x.shape
    tl = w
    nt = length // tl
    bw = e.shape[0]
    aw4 = wab.shape[1] - 3 * bw - LANES
    prev = lambda b, t: (b, jnp.maximum(t - 1, 0), 0)
    ada_spec = pl.BlockSpec((1, 1, d), lambda b, t: (b, 0, 0))
    outs = pl.pallas_call(
        functools.partial(_proj_prompt_kernel, nh=nh, bw=bw, dhb=dhb),
        grid=(nb, nt + 1),
        in_specs=[pl.BlockSpec((1, tl, d), prev), ada_spec, ada_spec, _const_spec((1, d)),
                  _const_spec(wab.shape), _const_spec(gbias.shape),
                  _const_spec(qg.shape), _const_spec(kg.shape), _const_spec(e.shape)],
        out_specs=[pl.BlockSpec((1, tl, aw4), prev),
                   pl.BlockSpec((1, tl, LANES), prev),
                   pl.BlockSpec((1, tl, bw), prev),
                   pl.BlockSpec((1, tl, bw), lambda b, t: (b, t, 0)),
                   pl.BlockSpec((1, tl, bw), lambda b, t: (b, t, 0)),
                   pl.BlockSpec((1, tl, bw), lambda b, t: (b, 0, 0)),
                   pl.BlockSpec((1, tl, bw), lambda b, t: (b, 0, 0))],
        out_shape=[jax.ShapeDtypeStruct((nb, length, aw4), F32),
                   jax.ShapeDtypeStruct((nb, length, LANES), F32),
                   jax.ShapeDtypeStruct((nb, length, bw), BF16),
                   jax.ShapeDtypeStruct((nb, length + w, bw), BF16),
                   jax.ShapeDtypeStruct((nb, length + w, bw), BF16),
                   jax.ShapeDtypeStruct((nb, w, bw), F32),
                   jax.ShapeDtypeStruct((nb, w, bw), F32)],
        compiler_params=_cparams(("arbitrary", "arbitrary"), 48),
        name="proj_prompt",
    )(x, sh, sc, g, wab, gbias, qg, kg, e)
    return outs


def _proj_sample_kernel(x_ref, sh_ref, sc_ref, g_ref, w_ref, gb_ref, qg_ref, kg_ref, e_ref,
                        ua_ref, gt_ref, qn_ref, kn_ref, vb_ref, *, nh, bw, dhb):
    bb, tl, _ = x_ref.shape
    ua, gates, qn, kn, vb = _proj_body(x_ref[...], sh_ref[...], sc_ref[...], g_ref[...], w_ref,
                                       gb_ref, qg_ref, kg_ref, e_ref, nh=nh, bw=bw, dhb=dhb)
    ua_ref[...] = ua.reshape(bb, tl, -1)
    gt_ref[...] = gates.reshape(bb, tl, -1)
    qn_ref[...] = qn.reshape(bb, tl, -1).astype(BF16)
    kn_ref[...] = kn.reshape(bb, tl, -1)
    vb_ref[...] = vb.reshape(bb, tl, -1)


def _proj_sample_call(x, sh, sc, g, wab, gbias, qg, kg, e, *, nh, dhb):
    nb, length, d = x.shape
    bw = e.shape[0]
    aw4 = wab.shape[1] - 3 * bw - LANES
    full = lambda n: pl.BlockSpec((nb, length, n), lambda i: (0, 0, 0))
    ada_spec = pl.BlockSpec((nb, 1, d), lambda i: (0, 0, 0))
    return pl.pallas_call(
        functools.partial(_proj_sample_kernel, nh=nh, bw=bw, dhb=dhb),
        grid=(1,),
        in_specs=[full(d), ada_spec, ada_spec, _const_spec((1, d)),
                  _const_spec(wab.shape), _const_spec(gbias.shape),
                  _const_spec(qg.shape), _const_spec(kg.shape), _const_spec(e.shape)],
        out_specs=[full(aw4), full(LANES), full(bw), full(bw), full(bw)],
        out_shape=[jax.ShapeDtypeStruct((nb, length, aw4), F32),
                   jax.ShapeDtypeStruct((nb, length, LANES), F32),
                   jax.ShapeDtypeStruct((nb, length, bw), BF16),
                   jax.ShapeDtypeStruct((nb, length, bw), F32),
                   jax.ShapeDtypeStruct((nb, length, bw), F32)],
        compiler_params=_cparams(("arbitrary",), 48),
        name="proj_sample",
    )(x, sh, sc, g, wab, gbias, qg, kg, e)


def _mlstm_kernel(ua_ref, g_ref, c0_ref, n0_ref, m0_ref, go_ref, ha_ref, c_ref, n_ref, m_ref,
                  nrep_scr, rep_scr, s_scr, pv_scr, kv_scr, qc_scr, *, seg, nh, dh):
    t = pl.program_id(1)

    @pl.when(t == 0)
    def _():
        c_ref[...] = c0_ref[...]
        nrep_scr[...] = n0_ref[0]
        m_ref[...] = m0_ref[...]

    tq = ua_ref.shape[1]
    nck = tq // seg
    aw = nh * dh
    gates = g_ref[0]
    pos = lax.broadcasted_iota(jnp.int32, gates.shape, 0) % seg
    bt = gates
    s = 1
    while s < seg:
        bt = bt + jnp.where(pos >= s, pltpu.roll(bt, s, 0), 0.0)
        s *= 2
    dmb = pltpu.roll(gates, nh, 1) - bt
    pm = dmb
    s = 1
    while s < seg:
        pm = jnp.maximum(pm, jnp.where(pos >= s, pltpu.roll(pm, s, 0), -jnp.inf))
        s *= 2
    if tq % LANES:
        dsq = jnp.concatenate([dmb, jnp.zeros((LANES - tq % LANES, LANES), F32)], axis=0)
    else:
        dsq = dmb
    dtr = dsq.T
    ri = lax.broadcasted_iota(jnp.int32, (seg, seg), 0)
    ci = lax.broadcasted_iota(jnp.int32, (seg, seg), 1)
    causal = ri >= ci
    ones = jnp.ones((seg, LANES), BF16)
    ones_dh = jnp.ones((dh, LANES), BF16)
    for h in range(nh):
        ln = slice(nh + h, nh + h + 1)
        for j, arr in enumerate((bt, dmb, pm)):
            rep_scr[3 * h + j] = jnp.broadcast_to(arr[:, ln], (tq, LANES))

    def cols(jc, h):
        rows = slice(jc * seg, (jc + 1) * seg)
        return rows, rep_scr[3 * h, rows, :], rep_scr[3 * h + 1, rows, :], rep_scr[3 * h + 2, rows, :]

    def last(jc, h, j):
        r = (jc + 1) * seg - 1
        return rep_scr[3 * h + j, r:r + 1, :]

    groups = [(jc, h) for jc in range(nck) for h in range(nh)]

    def qkv(jc, h, which):
        rows = slice(jc * seg, (jc + 1) * seg)
        return ua_ref[0, rows, which * aw + h * dh:which * aw + (h + 1) * dh]

    for g, (jc, h) in enumerate(groups):
        k = qkv(jc, h, 1) * (dh ** -0.5)
        s_scr[g] = _dot_nt(qkv(jc, h, 0).astype(BF16), k.astype(BF16))
    for g, (jc, h) in enumerate(groups):
        rows, _, _, p_col = cols(jc, h)
        d_row = dtr[nh + h:nh + h + 1, rows]
        dloc = jnp.exp(jnp.where(causal, d_row - p_col[:, :seg], -jnp.inf))
        sl = (s_scr[g] * dloc).astype(BF16)
        v = qkv(jc, h, 2).astype(BF16)
        pv_scr[g] = _dot(sl, jnp.concatenate([v, ones], axis=1))
    for g, (jc, h) in enumerate(groups):
        _, _, d_col, _ = cols(jc, h)
        kw = (qkv(jc, h, 1) * (dh ** -0.5)) * jnp.exp(d_col - last(jc, h, 2))
        vx = jnp.concatenate([qkv(jc, h, 2).astype(BF16), ones], axis=1)
        kv_scr[g] = _dot_tn(kw.astype(BF16), vx)
    state = [(c_ref[0, h], nrep_scr[h], m_ref[0, h:h + 1, :]) for h in range(nh)]
    before = []
    for g, (jc, h) in enumerate(groups):
        c_mem, n_rep, m = state[h]
        cn = jnp.concatenate([c_mem.astype(BF16), n_rep.astype(BF16)], axis=1)
        qc_scr[g] = _dot(qkv(jc, h, 0).astype(BF16), cn)
        before.append(m)
        p_last = last(jc, h, 2)
        mml = jnp.maximum(m, p_last)
        w_prev = jnp.exp(m - mml)
        f_new = jnp.exp(p_last - mml)
        kvx = kv_scr[g]
        state[h] = (w_prev * c_mem + f_new * kvx[:, :dh],
                    w_prev * n_rep + f_new * kvx[:, dh:],
                    last(jc, h, 0) + mml)
    for h in range(nh):
        c_ref[0, h], nrep_scr[h], m_ref[0, h:h + 1, :] = state[h]
        n_ref[0, h:h + 1, :] = state[h][1].T[0:1, :]
    for g, (jc, h) in enumerate(groups):
        rows, b_col, _, p_col = cols(jc, h)
        m = before[g]
        mm = jnp.maximum(m, p_col)
        iw = jnp.exp(m - mm)
        fl = jnp.exp(p_col - mm)
        pv = pv_scr[g]
        qc = qc_scr[g]
        num = iw * qc[:, :dh] + fl * pv[:, :dh]
        den = iw * qc[:, dh:] + fl * pv[:, dh:]
        hh = num / jnp.maximum(jnp.abs(den), jnp.exp(-(b_col + mm)))
        h2 = hh * hh
        hi = h2.astype(BF16)
        lo = (h2 - hi.astype(F32)).astype(BF16)
        ms = (_dot(hi, ones_dh) + _dot(lo, ones_dh)) * (1.0 / dh)
        hn = (hh * lax.rsqrt(ms + EPS) * go_ref[h:h + 1, :]) * jax.nn.sigmoid(qkv(jc, h, 3))
        ha_ref[0, rows, h * dh:(h + 1) * dh] = hn.astype(BF16)


def _mlstm_call(ua, gates, c0, n0rep, m0rep, gout, *, tq, seg):
    nb, length, aw4 = ua.shape
    _, nh, dh, _ = c0.shape
    assert dh == LANES
    groups = (tq // seg) * nh
    st = lambda shape: pl.BlockSpec((1,) + shape, lambda b, t: (b,) + (0,) * len(shape))
    tile = lambda n: pl.BlockSpec((1, tq, n), lambda b, t: (b, t, 0))
    return pl.pallas_call(
        functools.partial(_mlstm_kernel, seg=seg, nh=nh, dh=dh),
        grid=(nb, length // tq),
        in_specs=[tile(aw4), tile(LANES), st((nh, dh, dh)), st((nh, dh, LANES)), st((nh, LANES)),
                  _const_spec(gout.shape)],
        out_specs=[tile(nh * dh), st((nh, dh, dh)), st((nh, dh)), st((nh, LANES))],
        out_shape=[jax.ShapeDtypeStruct((nb, length, nh * dh), BF16),
                   jax.ShapeDtypeStruct((nb, nh, dh, dh), F32),
                   jax.ShapeDtypeStruct((nb, nh, dh), F32),
                   jax.ShapeDtypeStruct((nb, nh, LANES), F32)],
        scratch_shapes=[pltpu.VMEM((nh, dh, LANES), F32), pltpu.VMEM((3 * nh, tq, LANES), F32),
                        pltpu.VMEM((groups, seg, seg), F32), pltpu.VMEM((groups, seg, dh + LANES), F32),
                        pltpu.VMEM((groups, dh, dh + LANES), F32), pltpu.VMEM((groups, seg, dh + LANES), F32)],
        compiler_params=_cparams(("arbitrary", "arbitrary"), 32),
        name="mlstm",
    )(ua, gates, c0, n0rep, m0rep, gout)


def _relbias_kernel(b0_ref, o_ref):
    nhb, nq, nk = o_ref.shape
    for h in range(nhb):
        x = jnp.broadcast_to(b0_ref[h:h + 1, :], (nq, b0_ref.shape[1]))
        o_ref[h] = pltpu.roll(x, 0, 1, stride=1, stride_axis=0)[:, :nk]


def _relbias_call(table, w):
    nhb = table.shape[0]
    max_rel = (table.shape[1] - 1) // 2
    assert CHUNK - 1 <= max_rel <= w
    first = jnp.broadcast_to(table[:, :1], (nhb, w - max_rel))
    wrap = jnp.broadcast_to(table[:, :1], (nhb, CHUNK))
    b0 = jnp.concatenate([first, table[:, :max_rel + CHUNK], wrap], axis=1).astype(F32)
    return pl.pallas_call(
        _relbias_kernel,
        out_shape=jax.ShapeDtypeStruct((nhb, CHUNK, w + CHUNK), F32),
        name="rel_bias",
    )(b0)


def _band_prompt_kernel(q_ref, k_ref, v_ref, bias_ref, o_ref, s_scr, m_scr, e_scr, *, npair, w, nck):
    c4 = pl.program_id(1)
    nk = w + CHUNK
    lane = lax.broadcasted_iota(jnp.int32, (CHUNK, LANES), 1)
    low = lane < LANES // 2
    zero = jnp.zeros((CHUNK, LANES), BF16)
    ones = jnp.ones((nk, LANES), BF16)

    def run(masked):
        starts = [pl.multiple_of((c4 * nck + jc) * CHUNK, CHUNK) for jc in range(nck)]
        for jc in range(nck):
            for p in range(npair):
                g = jc * npair + p
                sl = slice(p * LANES, (p + 1) * LANES)
                qp = q_ref[0, jc * CHUNK:(jc + 1) * CHUNK, sl]
                q2 = jnp.concatenate([jnp.where(low, qp, zero), jnp.where(low, zero, qp)], axis=0)
                s = _dot_nt(q2, k_ref[0, pl.ds(starts[jc], nk), sl]) + bias_ref[p]
                if masked:
                    col = lax.broadcasted_iota(jnp.int32, s.shape, 1)
                    s = jnp.where(col + starts[jc] >= w, s, -jnp.inf)
                s_scr[g] = s
                m_scr[g] = jnp.max(s, axis=-1, keepdims=True)
        for g in range(nck * npair):
            e_scr[g] = jnp.exp(s_scr[g] - m_scr[g]).astype(BF16)
        for jc in range(nck):
            for p in range(npair):
                g = jc * npair + p
                sl = slice(p * LANES, (p + 1) * LANES)
                vx = jnp.concatenate([v_ref[0, pl.ds(starts[jc], nk), sl], ones], axis=1)
                r = _dot(e_scr[g], vx)
                o_lo = r[:CHUNK, :LANES] / r[:CHUNK, LANES:]
                o_hi = r[CHUNK:, :LANES] / r[CHUNK:, LANES:]
                o_ref[0, jc * CHUNK:(jc + 1) * CHUNK, sl] = jnp.where(low, o_lo, o_hi).astype(BF16)

    first_full = w // (CHUNK * nck)

    @pl.when(c4 < first_full)
    def _():
        run(True)

    @pl.when(c4 >= first_full)
    def _():
        run(False)


def _band_prompt_call(qs, kpad, vpad, bias2, *, w, nck):
    nb, length, bw = qs.shape
    npair = bias2.shape[0]
    lp = kpad.shape[1]
    tq = nck * CHUNK
    nk = w + CHUNK
    groups = nck * npair
    assert w % tq == 0
    return pl.pallas_call(
        functools.partial(_band_prompt_kernel, npair=npair, w=w, nck=nck),
        grid=(nb, length // tq),
        in_specs=[pl.BlockSpec((1, tq, bw), lambda b, c: (b, c, 0)),
                  pl.BlockSpec((1, lp, bw), lambda b, c: (b, 0, 0)),
                  pl.BlockSpec((1, lp, bw), lambda b, c: (b, 0, 0)),
                  _const_spec(bias2.shape)],
        out_specs=pl.BlockSpec((1, tq, bw), lambda b, c: (b, c, 0)),
        out_shape=jax.ShapeDtypeStruct((nb, length, bw), BF16),
        scratch_shapes=[pltpu.VMEM((groups, 2 * CHUNK, nk), F32),
                        pltpu.VMEM((groups, 2 * CHUNK, 1), F32),
                        pltpu.VMEM((groups, 2 * CHUNK, nk), BF16)],
        compiler_params=_cparams(("arbitrary", "arbitrary"), 40),
        name="band_prompt",
    )(qs, kpad, vpad, bias2)


def _band_sample_kernel(q_ref, kn_ref, vn_ref, ck_ref, cv_ref, bias_ref, o_ref):
    dhb = q_ref.shape[2]
    _, w, nhb, _ = ck_ref.shape
    ck = ck_ref[0].reshape(w * nhb, dhb)
    cv = cv_ref[0].reshape(w * nhb, dhb)
    kx = jnp.concatenate([ck.astype(BF16), kn_ref[0].astype(BF16)], axis=0)
    s = _dot_nt(q_ref[0], kx) + bias_ref[...]
    e = jnp.exp(s - jnp.max(s, axis=-1, keepdims=True)).astype(BF16)
    vx = jnp.concatenate([cv.astype(BF16), vn_ref[0].astype(BF16)], axis=0)
    r = _dot(e, jnp.concatenate([vx, jnp.ones_like(vx)], axis=1))
    o_ref[0] = (r[:, :dhb] / r[:, dhb:]).astype(BF16)


def _band_sample_call(q2, kn2, vn2, cache_k, cache_v, layer, bias_big):
    nb, rows_q, dhb = q2.shape
    blk = lambda a: pl.BlockSpec((1,) + a.shape[1:], lambda b: (b, 0, 0))
    cache = pl.BlockSpec((None, 1) + cache_k.shape[2:], lambda b: (layer, b, 0, 0, 0))
    return pl.pallas_call(
        _band_sample_kernel,
        grid=(nb,),
        in_specs=[blk(q2), blk(kn2), blk(vn2), cache, cache, _const_spec(bias_big.shape)],
        out_specs=blk(q2),
        out_shape=jax.ShapeDtypeStruct((nb, rows_q, dhb), BF16),
        compiler_params=_cparams(("arbitrary",), 40),
        name="band_sample",
    )(q2, kn2, vn2, cache_k, cache_v, bias_big)


def _mixout_kernel(x_ref, ha_ref, hb_ref, gt_ref, woa_ref, wob_ref, o_ref):
    bb, tl, d = x_ref.shape
    ha = ha_ref[...].reshape(bb * tl, -1)
    hb = hb_ref[...].reshape(bb * tl, -1)
    y = _dot(ha, woa_ref[...]) + _dot(hb, wob_ref[...])
    o_ref[...] = x_ref[...] + gt_ref[...] * y.reshape(bb, tl, d)


def _mixout_call(x, ha, hb, gt, woa, wob, bb, tl):
    nb, length, d = x.shape
    tile = lambda n: pl.BlockSpec((bb, tl, n), lambda i, t: (i, t, 0))
    return pl.pallas_call(
        _mixout_kernel,
        grid=(nb // bb, length // tl),
        in_specs=[tile(d), tile(ha.shape[-1]), tile(hb.shape[-1]),
                  pl.BlockSpec((bb, 1, d), lambda i, t: (i, 0, 0)),
                  _const_spec(woa.shape), _const_spec(wob.shape)],
        out_specs=tile(d),
        out_shape=jax.ShapeDtypeStruct(x.shape, F32),
        compiler_params=_cparams(("arbitrary", "arbitrary"), 32),
        name="mix_out",
    )(x, ha, hb, gt, woa, wob)


def _rglru_gates(xc, gw_ref, rb, ib, lam, nblk):
    bwc = xc.shape[1] // nblk
    r_parts, i_parts = [], []
    for n in range(nblk):
        gn = _dot(xc[:, n * bwc:(n + 1) * bwc].astype(BF16), gw_ref[n])
        r_parts.append(gn[:, :bwc])
        i_parts.append(gn[:, bwc:])
    r = jax.nn.sigmoid(jnp.concatenate(r_parts, axis=1) + rb)
    ii = jax.nn.sigmoid(jnp.concatenate(i_parts, axis=1) + ib)
    log_a = (-LRU_C * r) * _softplus(-lam)
    a = jnp.exp(log_a)
    th = jnp.tanh(log_a)
    upd = jnp.sqrt(-2.0 * th / (1.0 - th)) * (ii * xc)
    return a, upd


def _rglru_prompt_kernel(x_ref, sh_ref, sc_ref, gt_ref, g_ref, win_ref, cw_ref, cb_ref, gw_ref, rb_ref, ib_ref,
                         lam_ref, wout_ref, conv0_ref, h0_ref, o_ref, conv_ref, hl_ref, xp_scr, a_scr, b_scr, *, nblk):
    t = pl.program_id(1)
    tq, d = x_ref.shape[1], x_ref.shape[2]
    r_w = lam_ref.shape[1]
    ncw = cw_ref.shape[0]

    @pl.when(t == 0)
    def _():
        xp_scr[0:SUBLANES, :] = conv0_ref[0]
        hl_ref[...] = h0_ref[...]

    x = x_ref[0]
    hm = _rms_mod(x, g_ref[...], sh_ref[0], sc_ref[0]).astype(BF16)
    u = _dot(hm, win_ref[...])
    gb = u[:, :r_w]
    xp_scr[SUBLANES:SUBLANES + tq, :] = u[:, r_w:]
    xc = cb_ref[...]
    for j in range(ncw):
        off = SUBLANES - (ncw - 1 - j)
        xc = xc + xp_scr[off:off + tq, :] * cw_ref[j:j + 1, :]
    conv_ref[0] = xp_scr[tq:tq + SUBLANES, :]
    xp_scr[0:SUBLANES, :] = xp_scr[tq:tq + SUBLANES, :]
    a, upd = _rglru_gates(xc, gw_ref, rb_ref[...], ib_ref[...], lam_ref[...], nblk)
    a_scr[...] = a
    b_scr[...] = upd
    row8 = lax.broadcasted_iota(jnp.int32, (SUBLANES, r_w), 0)

    def scan_body(i, h):
        rows = pl.ds(pl.multiple_of(i * SUBLANES, SUBLANES), SUBLANES)
        ai = a_scr[rows, :]
        bi = b_scr[rows, :]
        s = 1
        while s < SUBLANES:
            m = row8 >= s
            bi = jnp.where(m, ai * pltpu.roll(bi, s, 0) + bi, bi)
            ai = jnp.where(m, ai * pltpu.roll(ai, s, 0), ai)
            s *= 2
        hs = ai * h + bi
        a_scr[rows, :] = hs
        return hs[SUBLANES - 1:SUBLANES, :]

    h_fin = lax.fori_loop(0, tq // SUBLANES, scan_body, hl_ref[0])
    hl_ref[0] = h_fin
    y = _dot((_gelu_tanh(gb) * a_scr[...]).astype(BF16), wout_ref[...])
    o_ref[0] = x + gt_ref[0] * y


def _rglru_prompt_call(x, sh, sc, gt, g, win, cw, cb, gw, rb, ib, lam, wout, conv0, h0, *, tq):
    nb, length, d = x.shape
    r_w = lam.shape[1]
    nblk = gw.shape[0]
    ada_spec = pl.BlockSpec((1, 1, d), lambda b, t: (b, 0, 0))
    tile = pl.BlockSpec((1, tq, d), lambda b, t: (b, t, 0))
    conv_spec = pl.BlockSpec((1, SUBLANES, r_w), lambda b, t: (b, 0, 0))
    h_spec = pl.BlockSpec((1, 1, r_w), lambda b, t: (b, 0, 0))
    consts = [g, win, cw, cb, gw, rb, ib, lam, wout]
    return pl.pallas_call(
        functools.partial(_rglru_prompt_kernel, nblk=nblk),
        grid=(nb, length // tq),
        in_specs=[tile, ada_spec, ada_spec, ada_spec] + [_const_spec(a.shape) for a in consts] + [conv_spec, h_spec],
        out_specs=[tile, conv_spec, h_spec],
        out_shape=[jax.ShapeDtypeStruct(x.shape, F32),
                   jax.ShapeDtypeStruct((nb, SUBLANES, r_w), F32),
                   jax.ShapeDtypeStruct((nb, 1, r_w), F32)],
        scratch_shapes=[pltpu.VMEM((tq + SUBLANES, r_w), F32), pltpu.VMEM((tq, r_w), F32), pltpu.VMEM((tq, r_w), F32)],
        compiler_params=_cparams(("arbitrary", "arbitrary"), 48),
        name="rglru_prompt",
    )(x, sh, sc, gt, *consts, conv0, h0)


def _rglru_sample_kernel(x_ref, sh_ref, sc_ref, gt_ref, g_ref, win_ref, cw_ref, cb_ref, gw_ref, rb_ref, ib_ref,
                         lam_ref, wout_ref, conv0_ref, h0_ref, o_ref, conv_ref, hl_ref, xp_scr, *, nblk):
    bb, tl, d = x_ref.shape
    tm = bb * tl
    r_w = lam_ref.shape[1]
    ncw = cw_ref.shape[0]
    x = x_ref[...]
    hm = _rms_mod(x, g_ref[...], sh_ref[...], sc_ref[...]).reshape(tm, d).astype(BF16)
    u = _dot(hm, win_ref[...])
    gb = u[:, :r_w]
    xp_scr[:, 0:SUBLANES, :] = conv0_ref[...]
    xp_scr[:, SUBLANES:SUBLANES + tl, :] = u[:, r_w:].reshape(bb, tl, r_w)
    xc = jnp.broadcast_to(cb_ref[...], (bb, tl, r_w))
    for j in range(ncw):
        off = SUBLANES - (ncw - 1 - j)
        xc = xc + xp_scr[:, off:off + tl, :] * cw_ref[j:j + 1, :]
    conv_ref[...] = xp_scr[:, tl:tl + SUBLANES, :]
    a, b = _rglru_gates(xc.reshape(tm, r_w), gw_ref, rb_ref[...], ib_ref[...], lam_ref[...], nblk)
    pos = lax.broadcasted_iota(jnp.int32, (tm, r_w), 0) % tl
    s = 1
    while s < tl:
        m = pos >= s
        b = jnp.where(m, a * pltpu.roll(b, s, 0) + b, b)
        a = jnp.where(m, a * pltpu.roll(a, s, 0), a)
        s *= 2
    hs = a.reshape(bb, tl, r_w) * h0_ref[...] + b.reshape(bb, tl, r_w)
    hl_ref[...] = hs[:, tl - 1:tl, :]
    y = _dot((_gelu_tanh(gb) * hs.reshape(tm, r_w)).astype(BF16), wout_ref[...])
    o_ref[...] = x + gt_ref[...] * y.reshape(bb, tl, d)


def _rglru_sample_call(x, sh, sc, gt, g, win, cw, cb, gw, rb, ib, lam, wout, conv0, h0):
    nb, tl, d = x.shape
    r_w = lam.shape[1]
    nblk = gw.shape[0]
    full = lambda a, b: pl.BlockSpec((nb, a, b), lambda i: (0, 0, 0))
    consts = [g, win, cw, cb, gw, rb, ib, lam, wout]
    return pl.pallas_call(
        functools.partial(_rglru_sample_kernel, nblk=nblk),
        grid=(1,),
        in_specs=[full(tl, d), full(1, d), full(1, d), full(1, d)] + [_const_spec(a.shape) for a in consts]
                 + [full(SUBLANES, r_w), full(1, r_w)],
        out_specs=[full(tl, d), full(SUBLANES, r_w), full(1, r_w)],
        out_shape=[jax.ShapeDtypeStruct(x.shape, F32),
                   jax.ShapeDtypeStruct((nb, SUBLANES, r_w), F32),
                   jax.ShapeDtypeStruct((nb, 1, r_w), F32)],
        scratch_shapes=[pltpu.VMEM((nb, tl + SUBLANES, r_w), F32)],
        compiler_params=_cparams(("arbitrary",), 48),
        name="rglru_sample",
    )(x, sh, sc, gt, *consts, conv0, h0)


def _pad_rows_front(a, rows):
    return jnp.pad(a, ((0, 0), (rows - a.shape[1], 0), (0, 0)))


def kernel(x_prompt, x_sample, state_a_C, state_a_n, state_a_m, cache_b_k, cache_b_v, state_c_conv, state_c_h,
           c_prompt, c_sample, ffn1_norm, ffn1_w_in, ffn1_w_out, mix_norm, ffn2_norm, ffn2_w_in, ffn2_w_out,
           ada_w, ada_b, ab_w_in, ab_gate_bias, a_out_norm, b_q_norm, b_k_norm, b_rel_bias, ab_w_out,
           c_w_in, c_conv_w, c_conv_b, c_gate_w, c_gate_b, c_lambda, c_w_out):
    nbp, seq, d = x_prompt.shape
    nbs, tdec, _ = x_sample.shape
    depth = ada_w.shape[0]
    n_ada = ada_w.shape[2] // d
    _, _, nh, dh, _ = state_a_C.shape
    _, _, w_band, nhb, dhb = cache_b_k.shape
    aw, bw = nh * dh, nhb * dhb
    ncw = c_conv_w.shape[1]
    assert 2 * dhb == LANES and dh == LANES and w_band % CHUNK == 0 and seq % w_band == 0

    ada = _ada_call(jnp.concatenate([c_prompt, c_sample], axis=0), ada_w, ada_b)
    ada = ada.reshape(depth, nbp + nbs, n_ada, 1, d)
    ada_p = [[ada[l, :nbp, k] for k in range(n_ada)] for l in range(depth)]
    ada_s = [[ada[l, nbp:, k] for k in range(n_ada)] for l in range(depth)]

    tl_p = 512
    xp, xs = x_prompt, x_sample
    outs_p, outs_s = {}, {}
    for l in range(depth):
        ap, as_ = ada_p[l], ada_s[l]
        i = l // 2
        g1 = ffn1_norm[l].reshape(1, d)
        gm = mix_norm[l].reshape(1, d)
        g2 = ffn2_norm[l].reshape(1, d)
        xp = _ffn_call(xp, ap[0], ap[1], ap[2], g1, ffn1_w_in, ffn1_w_out, l, 1, tl_p)
        xs = _ffn_stream_call(xs, as_[0], as_[1], as_[2], g1, ffn1_w_in, ffn1_w_out, l)
        if l % 2 == 0:
            w_in = ab_w_in[i]
            wab = jnp.concatenate(
                [w_in[:, :4 * aw], w_in[:, 4 * aw + 2 * nh:], w_in[:, 4 * aw:4 * aw + 2 * nh],
                 jnp.zeros((d, LANES - 2 * nh), F32)], axis=1).astype(BF16)
            gbias = jnp.pad(ab_gate_bias[i], (0, LANES - 2 * nh)).reshape(1, LANES)
            qg = jnp.tile(b_q_norm[i], nhb).reshape(1, bw)
            kg = jnp.tile(b_k_norm[i], nhb).reshape(1, bw)
            head = jnp.arange(bw) // dhb
            e = (head[:, None] == head[None, :]).astype(BF16)
            woa = ab_w_out[i][:aw].astype(BF16)
            wob = ab_w_out[i][aw:].astype(BF16)
            gout = a_out_norm[i]
            bias = _relbias_call(b_rel_bias[i], w_band)
            bias2 = bias.reshape(nhb // 2, 2 * CHUNK, w_band + CHUNK)

            ua, gts, qn, kpad, vpad, klast, vlast = _proj_prompt_call(
                xp, ap[3], ap[4], gm, wab, gbias, qg, kg, e, nh=nh, dhb=dhb, w=w_band)
            zc = jnp.zeros((nbp, nh, dh, dh), F32)
            ha, c1, n1, m1 = _mlstm_call(ua, gts, zc, zc, zc[:, :, 0], gout, tq=256, seg=CHUNK)
            hb = _band_prompt_call(qn, kpad, vpad, bias2, w=w_band, nck=4)
            xp = _mixout_call(xp, ha, hb, ap[5], woa, wob, 1, tl_p)
            outs_p.setdefault('a_C', []).append(c1)
            outs_p.setdefault('a_n', []).append(n1)
            outs_p.setdefault('a_m', []).append(m1[:, :, 0])
            outs_p.setdefault('b_k', []).append(klast.reshape(nbp, w_band, nhb, dhb))
            outs_p.setdefault('b_v', []).append(vlast.reshape(nbp, w_band, nhb, dhb))

            ua, gts, qn, kn, vn = _proj_sample_call(
                xs, as_[3], as_[4], gm, wab, gbias, qg, kg, e, nh=nh, dhb=dhb)
            ha, c1, n1, m1 = _mlstm_call(
                ua, gts, state_a_C[i],
                jnp.broadcast_to(state_a_n[i][..., None], (nbs, nh, dh, LANES)),
                jnp.broadcast_to(state_a_m[i][..., None], (nbs, nh, LANES)), gout, tq=tdec, seg=tdec)
            nks = w_band + tdec
            same_head = jnp.arange(nhb)[:, None, None, None] == jnp.arange(nhb)[None, None, None, :]
            bias_big = jnp.where(same_head, bias[:, :tdec, :nks, None], -jnp.inf).reshape(nhb * tdec, nks * nhb)
            to_rows = lambda a: a.reshape(nbs, tdec * nhb, dhb)
            o2 = _band_sample_call(
                qn.reshape(nbs, tdec, nhb, dhb).transpose(0, 2, 1, 3).reshape(nbs, nhb * tdec, dhb),
                to_rows(kn), to_rows(vn), cache_b_k, cache_b_v, i, bias_big)
            hb = o2.reshape(nbs, nhb, tdec, dhb).transpose(0, 2, 1, 3).reshape(nbs, tdec, bw)
            xs = _mixout_call(xs, ha, hb, as_[5], woa, wob, nbs, tdec)
            outs_s.setdefault('a_C', []).append(c1)
            outs_s.setdefault('a_n', []).append(n1)
            outs_s.setdefault('a_m', []).append(m1[:, :, 0])
            outs_s.setdefault('b_k', []).append(kn.reshape(nbs, tdec, nhb, dhb))
            outs_s.setdefault('b_v', []).append(vn.reshape(nbs, tdec, nhb, dhb))
        else:
            r_w = c_lambda.shape[1]
            consts = (gm, c_w_in[i].astype(BF16), c_conv_w[i], c_conv_b[i].reshape(1, r_w), c_gate_w[i].astype(BF16),
                      c_gate_b[i][0].reshape(1, r_w), c_gate_b[i][1].reshape(1, r_w), c_lambda[i].reshape(1, r_w),
                      c_w_out[i].astype(BF16))
            xp, conv_p, h_p = _rglru_prompt_call(
                xp, ap[3], ap[4], ap[5], *consts,
                jnp.zeros((nbp, SUBLANES, r_w), F32), jnp.zeros((nbp, 1, r_w), F32), tq=256)
            xs, conv_s, h_s = _rglru_sample_call(
                xs, as_[3], as_[4], as_[5], *consts,
                _pad_rows_front(state_c_conv[i], SUBLANES), state_c_h[i][:, None, :])
            outs_p.setdefault('c_conv', []).append(conv_p[:, SUBLANES - (ncw - 1):])
            outs_p.setdefault('c_h', []).append(h_p[:, 0])
            outs_s.setdefault('c_conv', []).append(conv_s[:, SUBLANES - (ncw - 1):])
            outs_s.setdefault('c_h', []).append(h_s[:, 0])
        xp = _ffn_call(xp, ap[6], ap[7], ap[8], g2, ffn2_w_in, ffn2_w_out, l, 1, tl_p)
        xs = _ffn_stream_call(xs, as_[6], as_[7], as_[8], g2, ffn2_w_in, ffn2_w_out, l)

    names = ('a_C', 'a_n', 'a_m', 'b_k', 'b_v', 'c_conv', 'c_h')
    ps = [jnp.stack(outs_p[n], axis=0) for n in names]
    ss = [jnp.stack(outs_s[n], axis=0) for n in names]
    return (xp, xs, *ps, *ss)
```

```python
import functools

import jax
import jax.numpy as jnp
from jax import lax
from jax.experimental import pallas as pl
from jax.experimental.pallas import tpu as pltpu

F32 = jnp.float32
BF16 = jnp.bfloat16

EPS = 1e-6
CHUNK = 64
LRU_C = 8.0
LANES = 128
SUBLANES = 8
MIB = 1024 * 1024


def _cparams(semantics, vmem_mib):
    return pltpu.CompilerParams(dimension_semantics=semantics, vmem_limit_bytes=vmem_mib * MIB)


def _const_spec(shape):
    nd = len(shape)
    return pl.BlockSpec(shape, lambda *_: (0,) * nd, pipeline_mode=pl.Buffered(1))


def _dot(a, b):
    return jnp.dot(a, b, preferred_element_type=F32)


def _dot_nt(a, b):
    return lax.dot_general(a, b, (((1,), (1,)), ((), ())), preferred_element_type=F32)


def _dot_tn(a, b):
    return lax.dot_general(a, b, (((0,), (0,)), ((), ())), preferred_element_type=F32)


def _rms_mod(x, g, shift, scale):
    ms = jnp.mean(x * x, axis=-1, keepdims=True)
    return (x * lax.rsqrt(ms + EPS)) * (g * (1.0 + scale)) + shift


def _softplus(x):
    return jnp.maximum(x, 0.0) + jnp.log1p(jnp.exp(-jnp.abs(x)))


def _gelu_tanh(x):
    return x * (0.5 * (1.0 + jnp.tanh(0.7978845608028654 * (x + 0.044715 * (x * x * x)))))


def _ada_kernel(c_ref, w_ref, b_ref, o_ref):
    c = c_ref[...].astype(BF16)
    w = w_ref[0].astype(BF16)
    o_ref[0] = _dot(c, w) + b_ref[0]


def _ada_call(c_all, ada_w, ada_b):
    depth, d, n = ada_w.shape
    m = c_all.shape[0]
    tn = d
    return pl.pallas_call(
        _ada_kernel,
        grid=(depth, n // tn),
        in_specs=[pl.BlockSpec((m, d), lambda l, j: (0, 0)),
                  pl.BlockSpec((1, d, tn), lambda l, j: (l, 0, j)),
                  pl.BlockSpec((1, 1, tn), lambda l, j: (l, 0, j))],
        out_specs=pl.BlockSpec((1, m, tn), lambda l, j: (l, 0, j)),
        out_shape=jax.ShapeDtypeStruct((depth, m, n), F32),
        compiler_params=_cparams(("arbitrary", "arbitrary"), 32),
        name="ada_proj",
    )(c_all, ada_w, ada_b.reshape(depth, 1, n))


FFN_TF = 256


def _ffn_kernel(*refs, mixed):
    if mixed:
        ha_ref, hb_ref, gm_ref, woa_ref, wob_ref = refs[:5]
        refs = refs[5:]
    x_ref, sh_ref, sc_ref, gt_ref, g_ref, win_ref, wo_ref, o_ref, act_scr = refs
    x = x_ref[...]
    bb, tl, d = x.shape
    dff = wo_ref.shape[0]
    if mixed:
        ym = (_dot(ha_ref[...].reshape(bb * tl, -1), woa_ref[...])
              + _dot(hb_ref[...].reshape(bb * tl, -1), wob_ref[...]))
        x = x + gm_ref[...] * ym.reshape(bb, tl, d)
    h = _rms_mod(x, g_ref[...], sh_ref[...], sc_ref[...]).reshape(bb * tl, d).astype(BF16)
    for c0 in range(0, dff, FFN_TF):
        gate = _dot(h, win_ref[:, c0:c0 + FFN_TF].astype(BF16))
        up = _dot(h, win_ref[:, dff + c0:dff + c0 + FFN_TF].astype(BF16))
        act_scr[:, c0:c0 + FFN_TF] = ((gate * jax.nn.sigmoid(gate)) * up).astype(BF16)
    y = _dot(act_scr[...], wo_ref[...].astype(BF16))
    o_ref[...] = x + (0.5 * gt_ref[...]) * y.reshape(bb, tl, d)


def _ffn_call(x, sh, sc, gt, g, w_in, w_out, layer, bb, tl, mix=None):
    nb, length, d = x.shape
    dff = w_out.shape[1]
    assert dff % FFN_TF == 0
    tm = bb * tl
    tile = lambda n: pl.BlockSpec((bb, tl, n), lambda i, t: (i, t, 0))
    ada_spec = pl.BlockSpec((bb, 1, d), lambda i, t: (i, 0, 0))
    layer_spec = lambda shape: pl.BlockSpec((None,) + shape, lambda i, t: (layer, 0, 0),
                                            pipeline_mode=pl.Buffered(1))
    mix_args, mix_specs = [], []
    if mix is not None:
        ha, hb, gm, woa, wob = mix
        mix_args = [ha, hb, gm, woa, wob]
        mix_specs = [tile(ha.shape[-1]), tile(hb.shape[-1]), ada_spec, _const_spec(woa.shape), _const_spec(wob.shape)]
    return pl.pallas_call(
        functools.partial(_ffn_kernel, mixed=mix is not None),
        grid=(nb // bb, length // tl),
        in_specs=mix_specs + [tile(d), ada_spec, ada_spec, ada_spec, _const_spec((1, d)),
                              layer_spec(w_in.shape[1:]), layer_spec(w_out.shape[1:])],
        out_specs=tile(d),
        out_shape=jax.ShapeDtypeStruct(x.shape, F32),
        scratch_shapes=[pltpu.VMEM((tm, dff), BF16)],
        compiler_params=_cparams(("arbitrary", "arbitrary"), 58),
        name="ffn",
    )(*mix_args, x, sh, sc, gt, g, w_in, w_out)


def _ffn_stream_kernel(x_ref, sh_ref, sc_ref, gt_ref, g_ref, wg_ref, wu_ref, wo_ref, o_ref, h_scr, acc_scr):
    c = pl.program_id(0)
    bb, tl, d = x_ref.shape

    @pl.when(c == 0)
    def _():
        h = _rms_mod(x_ref[...], g_ref[...], sh_ref[...], sc_ref[...])
        h_scr[...] = h.reshape(bb * tl, d).astype(BF16)
        acc_scr[...] = jnp.zeros_like(acc_scr)

    h = h_scr[...]
    gate = _dot(h, wg_ref[...].astype(BF16))
    up = _dot(h, wu_ref[...].astype(BF16))
    acc_scr[...] += _dot(((gate * jax.nn.sigmoid(gate)) * up).astype(BF16), wo_ref[...].astype(BF16))

    @pl.when(c == pl.num_programs(0) - 1)
    def _():
        o_ref[...] = x_ref[...] + (0.5 * gt_ref[...]) * acc_scr[...].reshape(bb, tl, d)


def _ffn_stream_call(x, sh, sc, gt, g, w_in, w_out, layer):
    nb, tl, d = x.shape
    dff = w_out.shape[1]
    nchunk = dff // FFN_TF
    tm = nb * tl
    full = pl.BlockSpec((nb, tl, d), lambda c: (0, 0, 0))
    ada_spec = pl.BlockSpec((nb, 1, d), lambda c: (0, 0, 0))
    return pl.pallas_call(
        _ffn_stream_kernel,
        grid=(nchunk,),
        in_specs=[full, ada_spec, ada_spec, ada_spec, _const_spec((1, d)),
                  pl.BlockSpec((None, d, FFN_TF), lambda c: (layer, 0, c)),
                  pl.BlockSpec((None, d, FFN_TF), lambda c: (layer, 0, c + nchunk)),
                  pl.BlockSpec((None, FFN_TF, d), lambda c: (layer, c, 0))],
        out_specs=full,
        out_shape=jax.ShapeDtypeStruct(x.shape, F32),
        scratch_shapes=[pltpu.VMEM((tm, d), BF16), pltpu.VMEM((tm, d), F32)],
        compiler_params=_cparams(("arbitrary",), 32),
        name="ffn_stream",
    )(x, sh, sc, gt, g, w_in, w_in, w_out)


def _head_rmsnorm(q, e, g, dhb):
    q2 = q * q
    hi = q2.astype(BF16)
    lo = (q2 - hi.astype(F32)).astype(BF16)
    ss = _dot(hi, e) + _dot(lo, e)
    return q * lax.rsqrt(ss * (1.0 / dhb) + EPS) * g


def _proj_body(x, sh, sc, g, w_ref, gb_ref, qg_ref, kg_ref, e_ref, *, nh, bw, dhb):
    bb, tl, d = x.shape
    na = w_ref.shape[1] - 3 * bw - LANES
    h = _rms_mod(x, g, sh, sc).reshape(bb * tl, d).astype(BF16)
    ua = _dot(h, w_ref[:, :na])
    gg = _dot(h, w_ref[:, na + 3 * bw:]) + gb_ref[...]
    lane = lax.broadcasted_iota(jnp.int32, gg.shape, 1)
    gates = jnp.where(lane < nh, gg, -_softplus(-gg))
    ub = _dot(h, w_ref[:, na:na + 3 * bw])
    e = e_ref[...]
    qn = _head_rmsnorm(ub[:, :bw], e, qg_ref[...], dhb) * (dhb ** -0.5)
    kn = _head_rmsnorm(ub[:, bw:2 * bw], e, kg_ref[...], dhb)
    vb = ub[:, 2 * bw:]
    return ua, gates, qn, kn, vb


def _proj_prompt_kernel(x_ref, sh_ref, sc_ref, g_ref, w_ref, gb_ref, qg_ref, kg_ref, e_ref,
                        ua_ref, gt_ref, qn_ref, kp_ref, vp_ref, kl_ref, vl_ref, *, nh, bw, dhb):
    t = pl.program_id(1)
    nt = pl.num_programs(1)

    @pl.when(t == 0)
    def _():
        kp_ref[...] = jnp.zeros_like(kp_ref)
        vp_ref[...] = jnp.zeros_like(vp_ref)

    @pl.when(t > 0)
    def _():
        ua, gates, qn, kn, vb = _proj_body(x_ref[...], sh_ref[...], sc_ref[...], g_ref[...], w_ref,
                                           gb_ref, qg_ref, kg_ref, e_ref, nh=nh, bw=bw, dhb=dhb)
        ua_ref[0] = ua
        gt_ref[0] = gates
        qn_ref[0] = qn.astype(BF16)
        kp_ref[0] = kn.astype(BF16)
        vp_ref[0] = vb.astype(BF16)

        @pl.when(t == nt - 1)
        def _():
            kl_ref[0] = kn
            vl_ref[0] = vb


def _proj_prompt_call(x, sh, sc, g, wab, gbias, qg, kg, e, *, nh, dhb, w):
    nb, length, d = x.shape
    tl = w
    nt = length // tl
    bw = e.shape[0]
    aw4 = wab.shape[1] - 3 * bw - LANES
    prev = lambda b, t: (b, jnp.maximum(t - 1, 0), 0)
    ada_spec = pl.BlockSpec((1, 1, d), lambda b, t: (b, 0, 0))
    outs = pl.pallas_call(
        functools.partial(_proj_prompt_kernel, nh=nh, bw=bw, dhb=dhb),
        grid=(nb, nt + 1),
        in_specs=[pl.BlockSpec((1, tl, d), prev), ada_spec, ada_spec, _const_spec((1, d)),
                  _const_spec(wab.shape), _const_spec(gbias.shape),
                  _const_spec(qg.shape), _const_spec(kg.shape), _const_spec(e.shape)],
        out_specs=[pl.BlockSpec((1, tl, aw4), prev),
                   pl.BlockSpec((1, tl, LANES), prev),
                   pl.BlockSpec((1, tl, bw), prev),
                   pl.BlockSpec((1, tl, bw), lambda b, t: (b, t, 0)),
                   pl.BlockSpec((1, tl, bw), lambda b, t: (b, t, 0)),
                   pl.BlockSpec((1, tl, bw), lambda b, t: (b, 0, 0)),
                   pl.BlockSpec((1, tl, bw), lambda b, t: (b, 0, 0))],
        out_shape=[jax.ShapeDtypeStruct((nb, length, aw4), F32),
                   jax.ShapeDtypeStruct((nb, length, LANES), F32),
                   jax.ShapeDtypeStruct((nb, length, bw), BF16),
                   jax.ShapeDtypeStruct((nb, length + w, bw), BF16),
                   jax.ShapeDtypeStruct((nb, length + w, bw), BF16),
                   jax.ShapeDtypeStruct((nb, w, bw), F32),
                   jax.ShapeDtypeStruct((nb, w, bw), F32)],
        compiler_params=_cparams(("arbitrary", "arbitrary"), 48),
        name="proj_prompt",
    )(x, sh, sc, g, wab, gbias, qg, kg, e)
    return outs


def _proj_sample_kernel(x_ref, sh_ref, sc_ref, g_ref, w_ref, gb_ref, qg_ref, kg_ref, e_ref,
                        ua_ref, gt_ref, qn_ref, kn_ref, vb_ref, *, nh, bw, dhb):
    bb, tl, _ = x_ref.shape
    ua, gates, qn, kn, vb = _proj_body(x_ref[...], sh_ref[...], sc_ref[...], g_ref[...], w_ref,
                                       gb_ref, qg_ref, kg_ref, e_ref, nh=nh, bw=bw, dhb=dhb)
    ua_ref[...] = ua.reshape(bb, tl, -1)
    gt_ref[...] = gates.reshape(bb, tl, -1)
    qn_ref[...] = qn.reshape(bb, tl, -1).astype(BF16)
    kn_ref[...] = kn.reshape(bb, tl, -1)
    vb_ref[...] = vb.reshape(bb, tl, -1)


def _proj_sample_call(x, sh, sc, g, wab, gbias, qg, kg, e, *, nh, dhb):
    nb, length, d = x.shape
    bw = e.shape[0]
    aw4 = wab.shape[1] - 3 * bw - LANES
    full = lambda n: pl.BlockSpec((nb, length, n), lambda i: (0, 0, 0))
    ada_spec = pl.BlockSpec((nb, 1, d), lambda i: (0, 0, 0))
    return pl.pallas_call(
        functools.partial(_proj_sample_kernel, nh=nh, bw=bw, dhb=dhb),
        grid=(1,),
        in_specs=[full(d), ada_spec, ada_spec, _const_spec((1, d)),
                  _const_spec(wab.shape), _const_spec(gbias.shape),
                  _const_spec(qg.shape), _const_spec(kg.shape), _const_spec(e.shape)],
        out_specs=[full(aw4), full(LANES), full(bw), full(bw), full(bw)],
        out_shape=[jax.ShapeDtypeStruct((nb, length, aw4), F32),
                   jax.ShapeDtypeStruct((nb, length, LANES), F32),
                   jax.ShapeDtypeStruct((nb, length, bw), BF16),
                   jax.ShapeDtypeStruct((nb, length, bw), F32),
                   jax.ShapeDtypeStruct((nb, length, bw), F32)],
        compiler_params=_cparams(("arbitrary",), 48),
        name="proj_sample",
    )(x, sh, sc, g, wab, gbias, qg, kg, e)


def _mlstm_init(c0_ref, n0_ref, m0_ref, c_ref, m_ref, nrep_scr):
    @pl.when(pl.program_id(1) == 0)
    def _():
        c_ref[...] = c0_ref[...]
        nrep_scr[...] = n0_ref[0]
        m_ref[...] = m0_ref[...]


def _mlstm_kernel(ua_ref, g_ref, c0_ref, n0_ref, m0_ref, go_ref, ha_ref, c_ref, n_ref, m_ref,
                  nrep_scr, rep_scr, s_scr, pv_scr, kv_scr, qc_scr, *, seg, nh, dh):
    _mlstm_init(c0_ref, n0_ref, m0_ref, c_ref, m_ref, nrep_scr)
    for steps in _mlstm_tile(ua_ref, g_ref, go_ref, ha_ref, c_ref, n_ref, m_ref,
                             nrep_scr, rep_scr, s_scr, pv_scr, kv_scr, qc_scr, seg=seg, nh=nh, dh=dh):
        _run_passes(steps)


def _mlstm_tile(ua_ref, g_ref, go_ref, ha_ref, c_ref, n_ref, m_ref,
                nrep_scr, rep_scr, s_scr, pv_scr, kv_scr, qc_scr, *, seg, nh, dh):
    tq = ua_ref.shape[1]
    nck = tq // seg
    aw = nh * dh
    gates = g_ref[0]
    pos = lax.broadcasted_iota(jnp.int32, gates.shape, 0) % seg
    bt = gates
    s = 1
    while s < seg:
        bt = bt + jnp.where(pos >= s, pltpu.roll(bt, s, 0), 0.0)
        s *= 2
    dmb = pltpu.roll(gates, nh, 1) - bt
    pm = dmb
    s = 1
    while s < seg:
        pm = jnp.maximum(pm, jnp.where(pos >= s, pltpu.roll(pm, s, 0), -jnp.inf))
        s *= 2
    if tq % LANES:
        dsq = jnp.concatenate([dmb, jnp.zeros((LANES - tq % LANES, LANES), F32)], axis=0)
    else:
        dsq = dmb
    dtr = dsq.T
    ri = lax.broadcasted_iota(jnp.int32, (seg, seg), 0)
    ci = lax.broadcasted_iota(jnp.int32, (seg, seg), 1)
    causal = ri >= ci
    ones = jnp.ones((seg, LANES), BF16)
    ones_dh = jnp.ones((dh, LANES), BF16)
    for h in range(nh):
        ln = slice(nh + h, nh + h + 1)
        for j, arr in enumerate((bt, dmb, pm)):
            rep_scr[3 * h + j] = jnp.broadcast_to(arr[:, ln], (tq, LANES))

    def cols(jc, h):
        rows = slice(jc * seg, (jc + 1) * seg)
        return rows, rep_scr[3 * h, rows, :], rep_scr[3 * h + 1, rows, :], rep_scr[3 * h + 2, rows, :]

    def last(jc, h, j):
        r = (jc + 1) * seg - 1
        return rep_scr[3 * h + j, r:r + 1, :]

    groups = [(jc, h) for jc in range(nck) for h in range(nh)]

    def qkv(jc, h, which):
        rows = slice(jc * seg, (jc + 1) * seg)
        return ua_ref[0, rows, which * aw + h * dh:which * aw + (h + 1) * dh]

    state = []
    before = []

    def score(g, jc, h):
        k = qkv(jc, h, 1) * (dh ** -0.5)
        s_scr[g] = _dot_nt(qkv(jc, h, 0).astype(BF16), k.astype(BF16))

    def local(g, jc, h):
        rows, _, _, p_col = cols(jc, h)
        d_row = dtr[nh + h:nh + h + 1, rows]
        dloc = jnp.exp(jnp.where(causal, d_row - p_col[:, :seg], -jnp.inf))
        sl = (s_scr[g] * dloc).astype(BF16)
        v = qkv(jc, h, 2).astype(BF16)
        pv_scr[g] = _dot(sl, jnp.concatenate([v, ones], axis=1))

    def contrib(g, jc, h):
        _, _, d_col, _ = cols(jc, h)
        kw = (qkv(jc, h, 1) * (dh ** -0.5)) * jnp.exp(d_col - last(jc, h, 2))
        vx = jnp.concatenate([qkv(jc, h, 2).astype(BF16), ones], axis=1)
        kv_scr[g] = _dot_tn(kw.astype(BF16), vx)

    def carry(g, jc, h):
        if not state:
            state.extend((c_ref[0, hh], nrep_scr[hh], m_ref[0, hh:hh + 1, :]) for hh in range(nh))
        c_mem, n_rep, m = state[h]
        cn = jnp.concatenate([c_mem.astype(BF16), n_rep.astype(BF16)], axis=1)
        qc_scr[g] = _dot(qkv(jc, h, 0).astype(BF16), cn)
        before.append(m)
        p_last = last(jc, h, 2)
        mml = jnp.maximum(m, p_last)
        w_prev = jnp.exp(m - mml)
        f_new = jnp.exp(p_last - mml)
        kvx = kv_scr[g]
        state[h] = (w_prev * c_mem + f_new * kvx[:, :dh],
                    w_prev * n_rep + f_new * kvx[:, dh:],
                    last(jc, h, 0) + mml)
        if g == len(groups) - 1:
            for hh in range(nh):
                c_ref[0, hh], nrep_scr[hh], m_ref[0, hh:hh + 1, :] = state[hh]
                n_ref[0, hh:hh + 1, :] = state[hh][1].T[0:1, :]

    def combine(g, jc, h):
        rows, b_col, _, p_col = cols(jc, h)
        m = before[g]
        mm = jnp.maximum(m, p_col)
        iw = jnp.exp(m - mm)
        fl = jnp.exp(p_col - mm)
        pv = pv_scr[g]
        qc = qc_scr[g]
        num = iw * qc[:, :dh] + fl * pv[:, :dh]
        den = iw * qc[:, dh:] + fl * pv[:, dh:]
        hh = num / jnp.maximum(jnp.abs(den), jnp.exp(-(b_col + mm)))
        h2 = hh * hh
        hi = h2.astype(BF16)
        lo = (h2 - hi.astype(F32)).astype(BF16)
        ms = (_dot(hi, ones_dh) + _dot(lo, ones_dh)) * (1.0 / dh)
        hn = (hh * lax.rsqrt(ms + EPS) * go_ref[h:h + 1, :]) * jax.nn.sigmoid(qkv(jc, h, 3))
        ha_ref[0, rows, h * dh:(h + 1) * dh] = hn.astype(BF16)

    return [[functools.partial(fn, g, jc, h) for g, (jc, h) in enumerate(groups)]
            for fn in (score, local, contrib, carry, combine)]


def _run_passes(*pass_lists):
    for steps in zip(*[p + [None] * (max(map(len, pass_lists)) - len(p)) for p in pass_lists]):
        for step in steps:
            if step is not None:
                step()


def _mlstm_call(ua, gates, c0, n0rep, m0rep, gout, *, tq, seg):
    nb, length, aw4 = ua.shape
    _, nh, dh, _ = c0.shape
    assert dh == LANES
    groups = (tq // seg) * nh
    st = lambda shape: pl.BlockSpec((1,) + shape, lambda b, t: (b,) + (0,) * len(shape))
    tile = lambda n: pl.BlockSpec((1, tq, n), lambda b, t: (b, t, 0))
    return pl.pallas_call(
        functools.partial(_mlstm_kernel, seg=seg, nh=nh, dh=dh),
        grid=(nb, length // tq),
        in_specs=[tile(aw4), tile(LANES), st((nh, dh, dh)), st((nh, dh, LANES)), st((nh, LANES)),
                  _const_spec(gout.shape)],
        out_specs=[tile(nh * dh), st((nh, dh, dh)), st((nh, dh)), st((nh, LANES))],
        out_shape=[jax.ShapeDtypeStruct((nb, length, nh * dh), BF16),
                   jax.ShapeDtypeStruct((nb, nh, dh, dh), F32),
                   jax.ShapeDtypeStruct((nb, nh, dh), F32),
                   jax.ShapeDtypeStruct((nb, nh, LANES), F32)],
        scratch_shapes=[pltpu.VMEM((nh, dh, LANES), F32), pltpu.VMEM((3 * nh, tq, LANES), F32),
                        pltpu.VMEM((groups, seg, seg), F32), pltpu.VMEM((groups, seg, dh + LANES), F32),
                        pltpu.VMEM((groups, dh, dh + LANES), F32), pltpu.VMEM((groups, seg, dh + LANES), F32)],
        compiler_params=_cparams(("arbitrary", "arbitrary"), 32),
        name="mlstm",
    )(ua, gates, c0, n0rep, m0rep, gout)


def _relbias_kernel(b0_ref, o_ref):
    nhb, nq, nk = o_ref.shape
    for h in range(nhb):
        x = jnp.broadcast_to(b0_ref[h:h + 1, :], (nq, b0_ref.shape[1]))
        o_ref[h] = pltpu.roll(x, 0, 1, stride=1, stride_axis=0)[:, :nk]


def _relbias_call(table, w):
    nhb = table.shape[0]
    max_rel = (table.shape[1] - 1) // 2
    assert CHUNK - 1 <= max_rel <= w
    first = jnp.broadcast_to(table[:, :1], (nhb, w - max_rel))
    wrap = jnp.broadcast_to(table[:, :1], (nhb, CHUNK))
    b0 = jnp.concatenate([first, table[:, :max_rel + CHUNK], wrap], axis=1).astype(F32)
    return pl.pallas_call(
        _relbias_kernel,
        out_shape=jax.ShapeDtypeStruct((nhb, CHUNK, w + CHUNK), F32),
        name="rel_bias",
    )(b0)


def _band_tile(q_ref, k_ref, v_ref, bias_ref, o_ref, s_scr, m_scr, e_scr, *, masked, npair, w, nck):
    c4 = pl.program_id(1)
    nk = w + CHUNK
    lane = lax.broadcasted_iota(jnp.int32, (CHUNK, LANES), 1)
    low = lane < LANES // 2
    zero = jnp.zeros((CHUNK, LANES), BF16)
    ones = jnp.ones((nk, LANES), BF16)

    starts = [pl.multiple_of((c4 * nck + jc) * CHUNK, CHUNK) for jc in range(nck)]
    groups = [(jc, p) for jc in range(nck) for p in range(npair)]

    def score(g, jc, p):
        sl = slice(p * LANES, (p + 1) * LANES)
        qp = q_ref[0, jc * CHUNK:(jc + 1) * CHUNK, sl]
        q2 = jnp.concatenate([jnp.where(low, qp, zero), jnp.where(low, zero, qp)], axis=0)
        s = _dot_nt(q2, k_ref[0, pl.ds(starts[jc], nk), sl]) + bias_ref[p]
        if masked:
            col = lax.broadcasted_iota(jnp.int32, s.shape, 1)
            s = jnp.where(col + starts[jc] >= w, s, -jnp.inf)
        s_scr[g] = s
        m_scr[g] = jnp.max(s, axis=-1, keepdims=True)

    def expo(g, jc, p):
        e_scr[g] = jnp.exp(s_scr[g] - m_scr[g]).astype(BF16)

    def value(g, jc, p):
        sl = slice(p * LANES, (p + 1) * LANES)
        vx = jnp.concatenate([v_ref[0, pl.ds(starts[jc], nk), sl], ones], axis=1)
        r = _dot(e_scr[g], vx)
        o_lo = r[:CHUNK, :LANES] / r[:CHUNK, LANES:]
        o_hi = r[CHUNK:, :LANES] / r[CHUNK:, LANES:]
        o_ref[0, jc * CHUNK:(jc + 1) * CHUNK, sl] = jnp.where(low, o_lo, o_hi).astype(BF16)

    return [[functools.partial(fn, g, jc, p) for g, (jc, p) in enumerate(groups)] for fn in (score, expo, value)]


def _mixer_prompt_kernel(ua_ref, g_ref, c0_ref, n0_ref, m0_ref, go_ref, q_ref, k_ref, v_ref, bias_ref,
                         ha_ref, c_ref, n_ref, m_ref, hb_ref,
                         nrep_scr, rep_scr, sa_scr, pv_scr, kv_scr, qc_scr, sb_scr, mb_scr, eb_scr,
                         *, seg, nh, dh, npair, w, nck):
    _mlstm_init(c0_ref, n0_ref, m0_ref, c_ref, m_ref, nrep_scr)
    first_full = w // (CHUNK * nck)

    def tile(masked):
        b_score, b_exp, b_value = _band_tile(q_ref, k_ref, v_ref, bias_ref, hb_ref, sb_scr, mb_scr, eb_scr,
                                             masked=masked, npair=npair, w=w, nck=nck)
        a_score, a_local, a_contrib, a_carry, a_combine = _mlstm_tile(
            ua_ref, g_ref, go_ref, ha_ref, c_ref, n_ref, m_ref,
            nrep_scr, rep_scr, sa_scr, pv_scr, kv_scr, qc_scr, seg=seg, nh=nh, dh=dh)
        _run_passes(a_score, b_score)
        _run_passes(a_local, b_exp)
        _run_passes(a_contrib)
        _run_passes(a_carry)
        _run_passes(a_combine, b_value)

    @pl.when(pl.program_id(1) < first_full)
    def _():
        tile(True)

    @pl.when(pl.program_id(1) >= first_full)
    def _():
        tile(False)


def _mixer_prompt_call(ua, gates, c0, n0rep, m0rep, gout, qs, kpad, vpad, bias2, *, w, nck):
    nb, length, aw4 = ua.shape
    _, nh, dh, _ = c0.shape
    bw = qs.shape[2]
    npair = bias2.shape[0]
    lp = kpad.shape[1]
    tq = nck * CHUNK
    nk = w + CHUNK
    ga = nck * nh
    gb = nck * npair
    assert w % tq == 0 and dh == LANES
    st = lambda shape: pl.BlockSpec((1,) + shape, lambda b, t: (b,) + (0,) * len(shape))
    tile = lambda n: pl.BlockSpec((1, tq, n), lambda b, t: (b, t, 0))
    whole = pl.BlockSpec((1, lp, bw), lambda b, t: (b, 0, 0))
    return pl.pallas_call(
        functools.partial(_mixer_prompt_kernel, seg=CHUNK, nh=nh, dh=dh, npair=npair, w=w, nck=nck),
        grid=(nb, length // tq),
        in_specs=[tile(aw4), tile(LANES), st((nh, dh, dh)), st((nh, dh, LANES)), st((nh, LANES)),
                  _const_spec(gout.shape), tile(bw), whole, whole, _const_spec(bias2.shape)],
        out_specs=[tile(nh * dh), st((nh, dh, dh)), st((nh, dh)), st((nh, LANES)), tile(bw)],
        out_shape=[jax.ShapeDtypeStruct((nb, length, nh * dh), BF16),
                   jax.ShapeDtypeStruct((nb, nh, dh, dh), F32),
                   jax.ShapeDtypeStruct((nb, nh, dh), F32),
                   jax.ShapeDtypeStruct((nb, nh, LANES), F32),
                   jax.ShapeDtypeStruct((nb, length, bw), BF16)],
        scratch_shapes=[pltpu.VMEM((nh, dh, LANES), F32), pltpu.VMEM((3 * nh, tq, LANES), F32),
                        pltpu.VMEM((ga, CHUNK, CHUNK), F32), pltpu.VMEM((ga, CHUNK, dh + LANES), F32),
                        pltpu.VMEM((ga, dh, dh + LANES), F32), pltpu.VMEM((ga, CHUNK, dh + LANES), F32),
                        pltpu.VMEM((gb, 2 * CHUNK, nk), F32), pltpu.VMEM((gb, 2 * CHUNK, 1), F32),
                        pltpu.VMEM((gb, 2 * CHUNK, nk), BF16)],
        compiler_params=_cparams(("arbitrary", "arbitrary"), 48),
        name="mixer_prompt",
    )(ua, gates, c0, n0rep, m0rep, gout, qs, kpad, vpad, bias2)


def _band_sample_kernel(q_ref, kn_ref, vn_ref, ck_ref, cv_ref, bias_ref, o_ref, *, npair):
    tq = q_ref.shape[1]
    nk = ck_ref.shape[1] + tq
    lane = lax.broadcasted_iota(jnp.int32, (tq, LANES), 1)
    low = lane < LANES // 2
    zero = jnp.zeros((tq, LANES), BF16)
    ones = jnp.ones((nk, LANES), BF16)
    scores = []
    for p in range(npair):
        sl = slice(p * LANES, (p + 1) * LANES)
        qp = q_ref[0, :, sl]
        q2 = jnp.concatenate([jnp.where(low, qp, zero), jnp.where(low, zero, qp)], axis=0)
        kx = jnp.concatenate([ck_ref[0, :, sl].astype(BF16), kn_ref[0, :, sl].astype(BF16)], axis=0)
        scores.append(_dot_nt(q2, kx) + bias_ref[p])
    probs = [jnp.exp(s - jnp.max(s, axis=-1, keepdims=True)).astype(BF16) for s in scores]
    for p in range(npair):
        sl = slice(p * LANES, (p + 1) * LANES)
        vx = jnp.concatenate([cv_ref[0, :, sl].astype(BF16), vn_ref[0, :, sl].astype(BF16)], axis=0)
        r = _dot(probs[p], jnp.concatenate([vx, ones], axis=1))
        o_lo = r[:tq, :LANES] / r[:tq, LANES:]
        o_hi = r[tq:, :LANES] / r[tq:, LANES:]
        o_ref[0, :, sl] = jnp.where(low, o_lo, o_hi).astype(BF16)


def _band_sample_call(qn, kn, vn, ck, cv, bias2):
    nb, tq, bw = qn.shape
    w = ck.shape[1]
    npair = bias2.shape[0]
    new = pl.BlockSpec((1, tq, bw), lambda b: (b, 0, 0))
    cache = pl.BlockSpec((1, w, bw), lambda b: (b, 0, 0))
    return pl.pallas_call(
        functools.partial(_band_sample_kernel, npair=npair),
        grid=(nb,),
        in_specs=[new, new, new, cache, cache, _const_spec(bias2.shape)],
        out_specs=new,
        out_shape=jax.ShapeDtypeStruct((nb, tq, bw), BF16),
        compiler_params=_cparams(("arbitrary",), 32),
        name="band_sample",
    )(qn, kn, vn, ck, cv, bias2)


def _mixout_kernel(x_ref, ha_ref, hb_ref, gt_ref, woa_ref, wob_ref, o_ref):
    bb, tl, d = x_ref.shape
    ha = ha_ref[...].reshape(bb * tl, -1)
    hb = hb_ref[...].reshape(bb * tl, -1)
    y = _dot(ha, woa_ref[...]) + _dot(hb, wob_ref[...])
    o_ref[...] = x_ref[...] + gt_ref[...] * y.reshape(bb, tl, d)


def _mixout_call(x, ha, hb, gt, woa, wob, bb, tl):
    nb, length, d = x.shape
    tile = lambda n: pl.BlockSpec((bb, tl, n), lambda i, t: (i, t, 0))
    return pl.pallas_call(
        _mixout_kernel,
        grid=(nb // bb, length // tl),
        in_specs=[tile(d), tile(ha.shape[-1]), tile(hb.shape[-1]),
                  pl.BlockSpec((bb, 1, d), lambda i, t: (i, 0, 0)),
                  _const_spec(woa.shape), _const_spec(wob.shape)],
        out_specs=tile(d),
        out_shape=jax.ShapeDtypeStruct(x.shape, F32),
        compiler_params=_cparams(("arbitrary", "arbitrary"), 32),
        name="mix_out",
    )(x, ha, hb, gt, woa, wob)


def _rglru_gates(xc, gw_ref, rb, ib, lam, nblk):
    bwc = xc.shape[1] // nblk
    r_parts, i_parts = [], []
    for n in range(nblk):
        gn = _dot(xc[:, n * bwc:(n + 1) * bwc].astype(BF16), gw_ref[n])
        r_parts.append(gn[:, :bwc])
        i_parts.append(gn[:, bwc:])
    r = jax.nn.sigmoid(jnp.concatenate(r_parts, axis=1) + rb)
    ii = jax.nn.sigmoid(jnp.concatenate(i_parts, axis=1) + ib)
    log_a = (-LRU_C * r) * _softplus(-lam)
    a = jnp.exp(log_a)
    th = jnp.tanh(log_a)
    upd = jnp.sqrt(-2.0 * th / (1.0 - th)) * (ii * xc)
    return a, upd


def _rglru_prompt_kernel(x_ref, sh_ref, sc_ref, gt_ref, g_ref, win_ref, cw_ref, cb_ref, gw_ref, rb_ref, ib_ref,
                         lam_ref, wout_ref, conv0_ref, h0_ref, o_ref, conv_ref, hl_ref, xp_scr, a_scr, b_scr, *, nblk):
    t = pl.program_id(1)
    tq, d = x_ref.shape[1], x_ref.shape[2]
    r_w = lam_ref.shape[1]
    ncw = cw_ref.shape[0]

    @pl.when(t == 0)
    def _():
        xp_scr[0:SUBLANES, :] = conv0_ref[0]
        hl_ref[...] = h0_ref[...]

    x = x_ref[0]
    hm = _rms_mod(x, g_ref[...], sh_ref[0], sc_ref[0]).astype(BF16)
    u = _dot(hm, win_ref[...])
    gb = u[:, :r_w]
    xp_scr[SUBLANES:SUBLANES + tq, :] = u[:, r_w:]
    xc = cb_ref[...]
    for j in range(ncw):
        off = SUBLANES - (ncw - 1 - j)
        xc = xc + xp_scr[off:off + tq, :] * cw_ref[j:j + 1, :]
    conv_ref[0] = xp_scr[tq:tq + SUBLANES, :]
    xp_scr[0:SUBLANES, :] = xp_scr[tq:tq + SUBLANES, :]
    a, upd = _rglru_gates(xc, gw_ref, rb_ref[...], ib_ref[...], lam_ref[...], nblk)
    a_scr[...] = a
    b_scr[...] = upd
    row8 = lax.broadcasted_iota(jnp.int32, (SUBLANES, r_w), 0)

    def scan_body(i, h):
        rows = pl.ds(pl.multiple_of(i * SUBLANES, SUBLANES), SUBLANES)
        ai = a_scr[rows, :]
        bi = b_scr[rows, :]
        s = 1
        while s < SUBLANES:
            m = row8 >= s
            bi = jnp.where(m, ai * pltpu.roll(bi, s, 0) + bi, bi)
            ai = jnp.where(m, ai * pltpu.roll(ai, s, 0), ai)
            s *= 2
        hs = ai * h + bi
        a_scr[rows, :] = hs
        return hs[SUBLANES - 1:SUBLANES, :]

    h_fin = lax.fori_loop(0, tq // SUBLANES, scan_body, hl_ref[0])
    hl_ref[0] = h_fin
    y = _dot((_gelu_tanh(gb) * a_scr[...]).astype(BF16), wout_ref[...])
    o_ref[0] = x + gt_ref[0] * y


def _rglru_prompt_call(x, sh, sc, gt, g, win, cw, cb, gw, rb, ib, lam, wout, conv0, h0, *, tq):
    nb, length, d = x.shape
    r_w = lam.shape[1]
    nblk = gw.shape[0]
    ada_spec = pl.BlockSpec((1, 1, d), lambda b, t: (b, 0, 0))
    tile = pl.BlockSpec((1, tq, d), lambda b, t: (b, t, 0))
    conv_spec = pl.BlockSpec((1, SUBLANES, r_w), lambda b, t: (b, 0, 0))
    h_spec = pl.BlockSpec((1, 1, r_w), lambda b, t: (b, 0, 0))
    consts = [g, win, cw, cb, gw, rb, ib, lam, wout]
    return pl.pallas_call(
        functools.partial(_rglru_prompt_kernel, nblk=nblk),
        grid=(nb, length // tq),
        in_specs=[tile, ada_spec, ada_spec, ada_spec] + [_const_spec(a.shape) for a in consts] + [conv_spec, h_spec],
        out_specs=[tile, conv_spec, h_spec],
        out_shape=[jax.ShapeDtypeStruct(x.shape, F32),
                   jax.ShapeDtypeStruct((nb, SUBLANES, r_w), F32),
                   jax.ShapeDtypeStruct((nb, 1, r_w), F32)],
        scratch_shapes=[pltpu.VMEM((tq + SUBLANES, r_w), F32), pltpu.VMEM((tq, r_w), F32), pltpu.VMEM((tq, r_w), F32)],
        compiler_params=_cparams(("arbitrary", "arbitrary"), 48),
        name="rglru_prompt",
    )(x, sh, sc, gt, *consts, conv0, h0)


def _rglru_sample_kernel(x_ref, sh_ref, sc_ref, gt_ref, g_ref, win_ref, cw_ref, cb_ref, gw_ref, rb_ref, ib_ref,
                         lam_ref, wout_ref, conv0_ref, h0_ref, o_ref, conv_ref, hl_ref, xp_scr, *, nblk):
    bb, tl, d = x_ref.shape
    tm = bb * tl
    r_w = lam_ref.shape[1]
    ncw = cw_ref.shape[0]
    x = x_ref[...]
    hm = _rms_mod(x, g_ref[...], sh_ref[...], sc_ref[...]).reshape(tm, d).astype(BF16)
    u = _dot(hm, win_ref[...])
    gb = u[:, :r_w]
    xp_scr[:, 0:SUBLANES, :] = conv0_ref[...]
    xp_scr[:, SUBLANES:SUBLANES + tl, :] = u[:, r_w:].reshape(bb, tl, r_w)
    xc = jnp.broadcast_to(cb_ref[...], (bb, tl, r_w))
    for j in range(ncw):
        off = SUBLANES - (ncw - 1 - j)
        xc = xc + xp_scr[:, off:off + tl, :] * cw_ref[j:j + 1, :]
    conv_ref[...] = xp_scr[:, tl:tl + SUBLANES, :]
    a, b = _rglru_gates(xc.reshape(tm, r_w), gw_ref, rb_ref[...], ib_ref[...], lam_ref[...], nblk)
    pos = lax.broadcasted_iota(jnp.int32, (tm, r_w), 0) % tl
    s = 1
    while s < tl:
        m = pos >= s
        b = jnp.where(m, a * pltpu.roll(b, s, 0) + b, b)
        a = jnp.where(m, a * pltpu.roll(a, s, 0), a)
        s *= 2
    hs = a.reshape(bb, tl, r_w) * h0_ref[...] + b.reshape(bb, tl, r_w)
    hl_ref[...] = hs[:, tl - 1:tl, :]
    y = _dot((_gelu_tanh(gb) * hs.reshape(tm, r_w)).astype(BF16), wout_ref[...])
    o_ref[...] = x + gt_ref[...] * y.reshape(bb, tl, d)


def _rglru_sample_call(x, sh, sc, gt, g, win, cw, cb, gw, rb, ib, lam, wout, conv0, h0):
    nb, tl, d = x.shape
    r_w = lam.shape[1]
    nblk = gw.shape[0]
    full = lambda a, b: pl.BlockSpec((nb, a, b), lambda i: (0, 0, 0))
    consts = [g, win, cw, cb, gw, rb, ib, lam, wout]
    return pl.pallas_call(
        functools.partial(_rglru_sample_kernel, nblk=nblk),
        grid=(1,),
        in_specs=[full(tl, d), full(1, d), full(1, d), full(1, d)] + [_const_spec(a.shape) for a in consts]
                 + [full(SUBLANES, r_w), full(1, r_w)],
        out_specs=[full(tl, d), full(SUBLANES, r_w), full(1, r_w)],
        out_shape=[jax.ShapeDtypeStruct(x.shape, F32),
                   jax.ShapeDtypeStruct((nb, SUBLANES, r_w), F32),
                   jax.ShapeDtypeStruct((nb, 1, r_w), F32)],
        scratch_shapes=[pltpu.VMEM((nb, tl + SUBLANES, r_w), F32)],
        compiler_params=_cparams(("arbitrary",), 48),
        name="rglru_sample",
    )(x, sh, sc, gt, *consts, conv0, h0)


def _pad_rows_front(a, rows):
    return jnp.pad(a, ((0, 0), (rows - a.shape[1], 0), (0, 0)))


def kernel(x_prompt, x_sample, state_a_C, state_a_n, state_a_m, cache_b_k, cache_b_v, state_c_conv, state_c_h,
           c_prompt, c_sample, ffn1_norm, ffn1_w_in, ffn1_w_out, mix_norm, ffn2_norm, ffn2_w_in, ffn2_w_out,
           ada_w, ada_b, ab_w_in, ab_gate_bias, a_out_norm, b_q_norm, b_k_norm, b_rel_bias, ab_w_out,
           c_w_in, c_conv_w, c_conv_b, c_gate_w, c_gate_b, c_lambda, c_w_out):
    nbp, seq, d = x_prompt.shape
    nbs, tdec, _ = x_sample.shape
    depth = ada_w.shape[0]
    n_ada = ada_w.shape[2] // d
    _, _, nh, dh, _ = state_a_C.shape
    _, _, w_band, nhb, dhb = cache_b_k.shape
    aw, bw = nh * dh, nhb * dhb
    ncw = c_conv_w.shape[1]
    assert 2 * dhb == LANES and dh == LANES and w_band % CHUNK == 0 and seq % w_band == 0

    ada = _ada_call(jnp.concatenate([c_prompt, c_sample], axis=0), ada_w, ada_b)
    ada = ada.reshape(depth, nbp + nbs, n_ada, 1, d)
    ada_p = [[ada[l, :nbp, k] for k in range(n_ada)] for l in range(depth)]
    ada_s = [[ada[l, nbp:, k] for k in range(n_ada)] for l in range(depth)]

    tl_p = 512
    xp, xs = x_prompt, x_sample
    outs_p, outs_s = {}, {}
    for l in range(depth):
        ap, as_ = ada_p[l], ada_s[l]
        i = l // 2
        g1 = ffn1_norm[l].reshape(1, d)
        gm = mix_norm[l].reshape(1, d)
        g2 = ffn2_norm[l].reshape(1, d)
        mix_p = None
        xp = _ffn_call(xp, ap[0], ap[1], ap[2], g1, ffn1_w_in, ffn1_w_out, l, 1, tl_p)
        xs = _ffn_stream_call(xs, as_[0], as_[1], as_[2], g1, ffn1_w_in, ffn1_w_out, l)
        if l % 2 == 0:
            w_in = ab_w_in[i]
            wab = jnp.concatenate(
                [w_in[:, :4 * aw], w_in[:, 4 * aw + 2 * nh:], w_in[:, 4 * aw:4 * aw + 2 * nh],
                 jnp.zeros((d, LANES - 2 * nh), F32)], axis=1).astype(BF16)
            gbias = jnp.pad(ab_gate_bias[i], (0, LANES - 2 * nh)).reshape(1, LANES)
            qg = jnp.tile(b_q_norm[i], nhb).reshape(1, bw)
            kg = jnp.tile(b_k_norm[i], nhb).reshape(1, bw)
            head = jnp.arange(bw) // dhb
            e = (head[:, None] == head[None, :]).astype(BF16)
            woa = ab_w_out[i][:aw].astype(BF16)
            wob = ab_w_out[i][aw:].astype(BF16)
            gout = a_out_norm[i]
            bias = _relbias_call(b_rel_bias[i], w_band)
            bias2 = bias.reshape(nhb // 2, 2 * CHUNK, w_band + CHUNK)

            ua, gts, qn, kpad, vpad, klast, vlast = _proj_prompt_call(
                xp, ap[3], ap[4], gm, wab, gbias, qg, kg, e, nh=nh, dhb=dhb, w=w_band)
            zc = jnp.zeros((nbp, nh, dh, dh), F32)
            ha, c1, n1, m1, hb = _mixer_prompt_call(ua, gts, zc, zc, zc[:, :, 0], gout, qn, kpad, vpad, bias2,
                                                    w=w_band, nck=4)
            mix_p = (ha, hb, ap[5], woa, wob)
            outs_p.setdefault('a_C', []).append(c1)
            outs_p.setdefault('a_n', []).append(n1)
            outs_p.setdefault('a_m', []).append(m1[:, :, 0])
            outs_p.setdefault('b_k', []).append(klast.reshape(nbp, w_band, nhb, dhb))
            outs_p.setdefault('b_v', []).append(vlast.reshape(nbp, w_band, nhb, dhb))

            ua, gts, qn, kn, vn = _proj_sample_call(
                xs, as_[3], as_[4], gm, wab, gbias, qg, kg, e, nh=nh, dhb=dhb)
            ha, c1, n1, m1 = _mlstm_call(
                ua, gts, state_a_C[i],
                jnp.broadcast_to(state_a_n[i][..., None], (nbs, nh, dh, LANES)),
                jnp.broadcast_to(state_a_m[i][..., None], (nbs, nh, LANES)), gout, tq=tdec, seg=tdec)
            hb = _band_sample_call(qn, kn, vn, cache_b_k[i].reshape(nbs, w_band, bw),
                                   cache_b_v[i].reshape(nbs, w_band, bw),
                                   bias[:, :tdec, :w_band + tdec].reshape(nhb // 2, 2 * tdec, w_band + tdec))
            xs = _mixout_call(xs, ha, hb, as_[5], woa, wob, nbs, tdec)
            outs_s.setdefault('a_C', []).append(c1)
            outs_s.setdefault('a_n', []).append(n1)
            outs_s.setdefault('a_m', []).append(m1[:, :, 0])
            outs_s.setdefault('b_k', []).append(kn.reshape(nbs, tdec, nhb, dhb))
            outs_s.setdefault('b_v', []).append(vn.reshape(nbs, tdec, nhb, dhb))
        else:
            r_w = c_lambda.shape[1]
            consts = (gm, c_w_in[i].astype(BF16), c_conv_w[i], c_conv_b[i].reshape(1, r_w), c_gate_w[i].astype(BF16),
                      c_gate_b[i][0].reshape(1, r_w), c_gate_b[i][1].reshape(1, r_w), c_lambda[i].reshape(1, r_w),
                      c_w_out[i].astype(BF16))
            xp, conv_p, h_p = _rglru_prompt_call(
                xp, ap[3], ap[4], ap[5], *consts,
                jnp.zeros((nbp, SUBLANES, r_w), F32), jnp.zeros((nbp, 1, r_w), F32), tq=512)
            xs, conv_s, h_s = _rglru_sample_call(
                xs, as_[3], as_[4], as_[5], *consts,
                _pad_rows_front(state_c_conv[i], SUBLANES), state_c_h[i][:, None, :])
            outs_p.setdefault('c_conv', []).append(conv_p[:, SUBLANES - (ncw - 1):])
            outs_p.setdefault('c_h', []).append(h_p[:, 0])
            outs_s.setdefault('c_conv', []).append(conv_s[:, SUBLANES - (ncw - 1):])
            outs_s.setdefault('c_h', []).append(h_s[:, 0])
        xp = _ffn_call(xp, ap[6], ap[7], ap[8], g2, ffn2_w_in, ffn2_w_out, l, 1, tl_p, mix=mix_p)
        xs = _ffn_stream_call(xs, as_[6], as_[7], as_[8], g2, ffn2_w_in, ffn2_w_out, l)

    names = ('a_C', 'a_n', 'a_m', 'b_k', 'b_v', 'c_conv', 'c_h')
    ps = [jnp.stack(outs_p[n], axis=0) for n in names]
    ss = [jnp.stack(outs_s[n], axis=0) for n in names]
    return (xp, xs, *ps, *ss)
```

```python
import functools

import jax
import jax.numpy as jnp
from jax import lax
from jax.experimental import pallas as pl
from jax.experimental.pallas import tpu as pltpu

F32 = jnp.float32
BF16 = jnp.bfloat16

EPS = 1e-6
CHUNK = 64
LRU_C = 8.0
LANES = 128
SUBLANES = 8
MIB = 1024 * 1024


def _cparams(semantics, vmem_mib):
    return pltpu.CompilerParams(dimension_semantics=semantics, vmem_limit_bytes=vmem_mib * MIB)


def _const_spec(shape):
    nd = len(shape)
    return pl.BlockSpec(shape, lambda *_: (0,) * nd, pipeline_mode=pl.Buffered(1))


def _dot(a, b):
    return jnp.dot(a, b, preferred_element_type=F32)


def _dot_nt(a, b):
    return lax.dot_general(a, b, (((1,), (1,)), ((), ())), preferred_element_type=F32)


def _dot_tn(a, b):
    return lax.dot_general(a, b, (((0,), (0,)), ((), ())), preferred_element_type=F32)


def _rms_mod(x, g, shift, scale):
    ms = jnp.mean(x * x, axis=-1, keepdims=True)
    return (x * lax.rsqrt(ms + EPS)) * (g * (1.0 + scale)) + shift


def _softplus(x):
    return jnp.maximum(x, 0.0) + jnp.log1p(jnp.exp(-jnp.abs(x)))


def _gelu_tanh(x):
    c = 0.7978845608028654
    inner = x * ((x * x) * (c * 0.044715) + c)
    return x * (0.5 * jnp.tanh(inner) + 0.5)


def _ada_kernel(c_ref, w_ref, b_ref, o_ref):
    c = c_ref[...].astype(BF16)
    w = w_ref[0].astype(BF16)
    o_ref[0] = _dot(c, w) + b_ref[0]


def _ada_call(c_all, ada_w, ada_b):
    depth, d, n = ada_w.shape
    m = c_all.shape[0]
    tn = d
    return pl.pallas_call(
        _ada_kernel,
        grid=(depth, n // tn),
        in_specs=[pl.BlockSpec((m, d), lambda l, j: (0, 0)),
                  pl.BlockSpec((1, d, tn), lambda l, j: (l, 0, j)),
                  pl.BlockSpec((1, 1, tn), lambda l, j: (l, 0, j))],
        out_specs=pl.BlockSpec((1, m, tn), lambda l, j: (l, 0, j)),
        out_shape=jax.ShapeDtypeStruct((depth, m, n), F32),
        compiler_params=_cparams(("arbitrary", "arbitrary"), 32),
        name="ada_proj",
    )(c_all, ada_w, ada_b.reshape(depth, 1, n))


FFN_TF = 256


def _ffn_kernel(*refs, mixed):
    if mixed:
        ha_ref, hb_ref, gm_ref, woa_ref, wob_ref = refs[:5]
        refs = refs[5:]
    x_ref, sh_ref, sc_ref, gt_ref, g_ref, win_ref, wo_ref, o_ref, act_scr = refs
    x = x_ref[...]
    bb, tl, d = x.shape
    dff = wo_ref.shape[0]
    if mixed:
        ym = (_dot(ha_ref[...].reshape(bb * tl, -1), woa_ref[...])
              + _dot(hb_ref[...].reshape(bb * tl, -1), wob_ref[...]))
        x = x + gm_ref[...] * ym.reshape(bb, tl, d)
    h = _rms_mod(x, g_ref[...], sh_ref[...], sc_ref[...]).reshape(bb * tl, d).astype(BF16)
    for c0 in range(0, dff, FFN_TF):
        gate = _dot(h, win_ref[:, c0:c0 + FFN_TF].astype(BF16))
        up = _dot(h, win_ref[:, dff + c0:dff + c0 + FFN_TF].astype(BF16))
        act_scr[:, c0:c0 + FFN_TF] = ((gate * jax.nn.sigmoid(gate)) * up).astype(BF16)
    y = _dot(act_scr[...], wo_ref[...].astype(BF16))
    o_ref[...] = x + (0.5 * gt_ref[...]) * y.reshape(bb, tl, d)


def _ffn_call(x, sh, sc, gt, g, w_in, w_out, layer, bb, tl, mix=None):
    nb, length, d = x.shape
    dff = w_out.shape[1]
    assert dff % FFN_TF == 0
    tm = bb * tl
    tile = lambda n: pl.BlockSpec((bb, tl, n), lambda i, t: (i, t, 0))
    ada_spec = pl.BlockSpec((bb, 1, d), lambda i, t: (i, 0, 0))
    layer_spec = lambda shape: pl.BlockSpec((None,) + shape, lambda i, t: (layer, 0, 0),
                                            pipeline_mode=pl.Buffered(1))
    mix_args, mix_specs = [], []
    if mix is not None:
        ha, hb, gm, woa, wob = mix
        mix_args = [ha, hb, gm, woa, wob]
        mix_specs = [tile(ha.shape[-1]), tile(hb.shape[-1]), ada_spec, _const_spec(woa.shape), _const_spec(wob.shape)]
    return pl.pallas_call(
        functools.partial(_ffn_kernel, mixed=mix is not None),
        grid=(nb // bb, length // tl),
        in_specs=mix_specs + [tile(d), ada_spec, ada_spec, ada_spec, _const_spec((1, d)),
                              layer_spec(w_in.shape[1:]), layer_spec(w_out.shape[1:])],
        out_specs=tile(d),
        out_shape=jax.ShapeDtypeStruct(x.shape, F32),
        scratch_shapes=[pltpu.VMEM((tm, dff), BF16)],
        compiler_params=_cparams(("arbitrary", "arbitrary"), 58),
        name="ffn",
    )(*mix_args, x, sh, sc, gt, g, w_in, w_out)


def _ffn_stream_kernel(x_ref, sh_ref, sc_ref, gt_ref, g_ref, wg_ref, wu_ref, wo_ref, o_ref, h_scr, acc_scr):
    c = pl.program_id(0)
    bb, tl, d = x_ref.shape

    @pl.when(c == 0)
    def _():
        h = _rms_mod(x_ref[...], g_ref[...], sh_ref[...], sc_ref[...])
        h_scr[...] = h.reshape(bb * tl, d).astype(BF16)
        acc_scr[...] = jnp.zeros_like(acc_scr)

    h = h_scr[...]
    gate = _dot(h, wg_ref[...].astype(BF16))
    up = _dot(h, wu_ref[...].astype(BF16))
    acc_scr[...] += _dot(((gate * jax.nn.sigmoid(gate)) * up).astype(BF16), wo_ref[...].astype(BF16))

    @pl.when(c == pl.num_programs(0) - 1)
    def _():
        o_ref[...] = x_ref[...] + (0.5 * gt_ref[...]) * acc_scr[...].reshape(bb, tl, d)


def _ffn_stream_call(x, sh, sc, gt, g, w_in, w_out, layer):
    nb, tl, d = x.shape
    dff = w_out.shape[1]
    nchunk = dff // FFN_TF
    tm = nb * tl
    full = pl.BlockSpec((nb, tl, d), lambda c: (0, 0, 0))
    ada_spec = pl.BlockSpec((nb, 1, d), lambda c: (0, 0, 0))
    return pl.pallas_call(
        _ffn_stream_kernel,
        grid=(nchunk,),
        in_specs=[full, ada_spec, ada_spec, ada_spec, _const_spec((1, d)),
                  pl.BlockSpec((None, d, FFN_TF), lambda c: (layer, 0, c)),
                  pl.BlockSpec((None, d, FFN_TF), lambda c: (layer, 0, c + nchunk)),
                  pl.BlockSpec((None, FFN_TF, d), lambda c: (layer, c, 0))],
        out_specs=full,
        out_shape=jax.ShapeDtypeStruct(x.shape, F32),
        scratch_shapes=[pltpu.VMEM((tm, d), BF16), pltpu.VMEM((tm, d), F32)],
        compiler_params=_cparams(("arbitrary",), 32),
        name="ffn_stream",
    )(x, sh, sc, gt, g, w_in, w_in, w_out)


def _head_rmsnorm(q, e, g, dhb):
    ss = _dot((q * q).astype(BF16), e)
    return q * lax.rsqrt(ss * (1.0 / dhb) + EPS) * g


def _proj_body(x, sh, sc, g, w_ref, gb_ref, qg_ref, kg_ref, e_ref, *, nh, bw, dhb):
    bb, tl, d = x.shape
    na = w_ref.shape[1] - 3 * bw - LANES
    h = _rms_mod(x, g, sh, sc).reshape(bb * tl, d).astype(BF16)
    ua = _dot(h, w_ref[:, :na])
    gg = _dot(h, w_ref[:, na + 3 * bw:]) + gb_ref[...]
    lane = lax.broadcasted_iota(jnp.int32, gg.shape, 1)
    gates = jnp.where(lane < nh, gg, -_softplus(-gg))
    ub = _dot(h, w_ref[:, na:na + 3 * bw])
    e = e_ref[...]
    qn = _head_rmsnorm(ub[:, :bw], e, qg_ref[...], dhb) * (dhb ** -0.5)
    kn = _head_rmsnorm(ub[:, bw:2 * bw], e, kg_ref[...], dhb)
    vb = ub[:, 2 * bw:]
    return ua, gates, qn, kn, vb


def _proj_prompt_kernel(x_ref, sh_ref, sc_ref, g_ref, w_ref, gb_ref, qg_ref, kg_ref, e_ref,
                        ua_ref, gt_ref, qn_ref, kp_ref, vp_ref, kl_ref, vl_ref, *, nh, bw, dhb):
    t = pl.program_id(1)
    nt = pl.num_programs(1)

    @pl.when(t == 0)
    def _():
        kp_ref[...] = jnp.zeros_like(kp_ref)
        vp_ref[...] = jnp.zeros_like(vp_ref)

    @pl.when(t > 0)
    def _():
        ua, gates, qn, kn, vb = _proj_body(x_ref[...], sh_ref[...], sc_ref[...], g_ref[...], w_ref,
                                           gb_ref, qg_ref, kg_ref, e_ref, nh=nh, bw=bw, dhb=dhb)
        ua_ref[0] = ua
        gt_ref[0] = gates
        qn_ref[0] = qn.astype(BF16)
        kp_ref[0] = kn.astype(BF16)
        vp_ref[0] = vb.astype(BF16)

        @pl.when(t == nt - 1)
        def _():
            kl_ref[0] = kn
            vl_ref[0] = vb


def _proj_prompt_call(x, sh, sc, g, wab, gbias, qg, kg, e, *, nh, dhb, w):
    nb, length, d = x.shape
    tl = w
    nt = length // tl
    bw = e.shape[0]
    aw4 = wab.shape[1] - 3 * bw - LANES
    prev = lambda b, t: (b, jnp.maximum(t - 1, 0), 0)
    ada_spec = pl.BlockSpec((1, 1, d), lambda b, t: (b, 0, 0))
    outs = pl.pallas_call(
        functools.partial(_proj_prompt_kernel, nh=nh, bw=bw, dhb=dhb),
        grid=(nb, nt + 1),
        in_specs=[pl.BlockSpec((1, tl, d), prev), ada_spec, ada_spec, _const_spec((1, d)),
                  _const_spec(wab.shape), _const_spec(gbias.shape),
                  _const_spec(qg.shape), _const_spec(kg.shape), _const_spec(e.shape)],
        out_specs=[pl.BlockSpec((1, tl, aw4), prev),
                   pl.BlockSpec((1, tl, LANES), prev),
                   pl.BlockSpec((1, tl, bw), prev),
                   pl.BlockSpec((1, tl, bw), lambda b, t: (b, t, 0)),
                   pl.BlockSpec((1, tl, bw), lambda b, t: (b, t, 0)),
                   pl.BlockSpec((1, tl, bw), lambda b, t: (b, 0, 0)),
                   pl.BlockSpec((1, tl, bw), lambda b, t: (b, 0, 0))],
        out_shape=[jax.ShapeDtypeStruct((nb, length, aw4), F32),
                   jax.ShapeDtypeStruct((nb, length, LANES), F32),
                   jax.ShapeDtypeStruct((nb, length, bw), BF16),
                   jax.ShapeDtypeStruct((nb, length + w, bw), BF16),
                   jax.ShapeDtypeStruct((nb, length + w, bw), BF16),
                   jax.ShapeDtypeStruct((nb, w, bw), F32),
                   jax.ShapeDtypeStruct((nb, w, bw), F32)],
        compiler_params=_cparams(("arbitrary", "arbitrary"), 48),
        name="proj_prompt",
    )(x, sh, sc, g, wab, gbias, qg, kg, e)
    return outs


def _proj_sample_kernel(x_ref, sh_ref, sc_ref, g_ref, w_ref, gb_ref, qg_ref, kg_ref, e_ref,
                        ua_ref, gt_ref, qn_ref, kn_ref, vb_ref, *, nh, bw, dhb):
    bb, tl, _ = x_ref.shape
    ua, gates, qn, kn, vb = _proj_body(x_ref[...], sh_ref[...], sc_ref[...], g_ref[...], w_ref,
                                       gb_ref, qg_ref, kg_ref, e_ref, nh=nh, bw=bw, dhb=dhb)
    ua_ref[...] = ua.reshape(bb, tl, -1)
    gt_ref[...] = gates.reshape(bb, tl, -1)
    qn_ref[...] = qn.reshape(bb, tl, -1).astype(BF16)
    kn_ref[...] = kn.reshape(bb, tl, -1)
    vb_ref[...] = vb.reshape(bb, tl, -1)


def _proj_sample_call(x, sh, sc, g, wab, gbias, qg, kg, e, *, nh, dhb):
    nb, length, d = x.shape
    bw = e.shape[0]
    aw4 = wab.shape[1] - 3 * bw - LANES
    full = lambda n: pl.BlockSpec((nb, length, n), lambda i: (0, 0, 0))
    ada_spec = pl.BlockSpec((nb, 1, d), lambda i: (0, 0, 0))
    return pl.pallas_call(
        functools.partial(_proj_sample_kernel, nh=nh, bw=bw, dhb=dhb),
        grid=(1,),
        in_specs=[full(d), ada_spec, ada_spec, _const_spec((1, d)),
                  _const_spec(wab.shape), _const_spec(gbias.shape),
                  _const_spec(qg.shape), _const_spec(kg.shape), _const_spec(e.shape)],
        out_specs=[full(aw4), full(LANES), full(bw), full(bw), full(bw)],
        out_shape=[jax.ShapeDtypeStruct((nb, length, aw4), F32),
                   jax.ShapeDtypeStruct((nb, length, LANES), F32),
                   jax.ShapeDtypeStruct((nb, length, bw), BF16),
                   jax.ShapeDtypeStruct((nb, length, bw), F32),
                   jax.ShapeDtypeStruct((nb, length, bw), F32)],
        compiler_params=_cparams(("arbitrary",), 48),
        name="proj_sample",
    )(x, sh, sc, g, wab, gbias, qg, kg, e)


def _mlstm_init(c0_ref, n0_ref, m0_ref, c_ref, m_ref, nrep_scr):
    @pl.when(pl.program_id(1) == 0)
    def _():
        c_ref[...] = c0_ref[...]
        nrep_scr[...] = n0_ref[0]
        m_ref[...] = m0_ref[...]


def _mlstm_kernel(ua_ref, g_ref, c0_ref, n0_ref, m0_ref, go_ref, ha_ref, c_ref, n_ref, m_ref,
                  nrep_scr, rep_scr, s_scr, pv_scr, kv_scr, qc_scr, *, seg, nh, dh):
    _mlstm_init(c0_ref, n0_ref, m0_ref, c_ref, m_ref, nrep_scr)
    for steps in _mlstm_tile(ua_ref, g_ref, go_ref, ha_ref, c_ref, n_ref, m_ref,
                             nrep_scr, rep_scr, s_scr, pv_scr, kv_scr, qc_scr, seg=seg, nh=nh, dh=dh):
        _run_passes(steps)


def _mlstm_tile(ua_ref, g_ref, go_ref, ha_ref, c_ref, n_ref, m_ref,
                nrep_scr, rep_scr, s_scr, pv_scr, kv_scr, qc_scr, *, seg, nh, dh):
    tq = ua_ref.shape[1]
    nck = tq // seg
    aw = nh * dh
    gates = g_ref[0]
    pos = lax.broadcasted_iota(jnp.int32, gates.shape, 0) % seg
    bt = gates
    s = 1
    while s < seg:
        bt = bt + jnp.where(pos >= s, pltpu.roll(bt, s, 0), 0.0)
        s *= 2
    dmb = pltpu.roll(gates, nh, 1) - bt
    pm = dmb
    s = 1
    while s < seg:
        pm = jnp.maximum(pm, jnp.where(pos >= s, pltpu.roll(pm, s, 0), -jnp.inf))
        s *= 2
    if tq % LANES:
        dsq = jnp.concatenate([dmb, jnp.zeros((LANES - tq % LANES, LANES), F32)], axis=0)
    else:
        dsq = dmb
    dtr = dsq.T
    ri = lax.broadcasted_iota(jnp.int32, (seg, seg), 0)
    ci = lax.broadcasted_iota(jnp.int32, (seg, seg), 1)
    causal = ri >= ci
    ones = jnp.ones((seg, LANES), BF16)
    ones_dh = jnp.ones((dh, LANES), BF16)
    for h in range(nh):
        ln = slice(nh + h, nh + h + 1)
        for j, arr in enumerate((bt, dmb, pm)):
            rep_scr[3 * h + j] = jnp.broadcast_to(arr[:, ln], (tq, LANES))

    def cols(jc, h):
        rows = slice(jc * seg, (jc + 1) * seg)
        return rows, rep_scr[3 * h, rows, :], rep_scr[3 * h + 1, rows, :], rep_scr[3 * h + 2, rows, :]

    def last(jc, h, j):
        r = (jc + 1) * seg - 1
        return rep_scr[3 * h + j, r:r + 1, :]

    groups = [(jc, h) for jc in range(nck) for h in range(nh)]

    def qkv(jc, h, which):
        rows = slice(jc * seg, (jc + 1) * seg)
        return ua_ref[0, rows, which * aw + h * dh:which * aw + (h + 1) * dh]

    state = []
    before = []

    def score(g, jc, h):
        k = qkv(jc, h, 1) * (dh ** -0.5)
        s_scr[g] = _dot_nt(qkv(jc, h, 0).astype(BF16), k.astype(BF16))

    def local(g, jc, h):
        rows, _, _, p_col = cols(jc, h)
        d_row = dtr[nh + h:nh + h + 1, rows]
        dloc = jnp.exp(jnp.where(causal, d_row - p_col[:, :seg], -jnp.inf))
        sl = (s_scr[g] * dloc).astype(BF16)
        v = qkv(jc, h, 2).astype(BF16)
        pv_scr[g] = _dot(sl, jnp.concatenate([v, ones], axis=1))

    def contrib(g, jc, h):
        _, _, d_col, _ = cols(jc, h)
        kw = (qkv(jc, h, 1) * (dh ** -0.5)) * jnp.exp(d_col - last(jc, h, 2))
        vx = jnp.concatenate([qkv(jc, h, 2).astype(BF16), ones], axis=1)
        kv_scr[g] = _dot_tn(kw.astype(BF16), vx)

    def carry(g, jc, h):
        if not state:
            state.extend((c_ref[0, hh], nrep_scr[hh], m_ref[0, hh:hh + 1, :]) for hh in range(nh))
        c_mem, n_rep, m = state[h]
        cn = jnp.concatenate([c_mem.astype(BF16), n_rep.astype(BF16)], axis=1)
        qc_scr[g] = _dot(qkv(jc, h, 0).astype(BF16), cn)
        before.append(m)
        p_last = last(jc, h, 2)
        mml = jnp.maximum(m, p_last)
        w_prev = jnp.exp(m - mml)
        f_new = jnp.exp(p_last - mml)
        kvx = kv_scr[g]
        state[h] = (w_prev * c_mem + f_new * kvx[:, :dh],
                    w_prev * n_rep + f_new * kvx[:, dh:],
                    last(jc, h, 0) + mml)
        if g == len(groups) - 1:
            for hh in range(nh):
                c_ref[0, hh], nrep_scr[hh], m_ref[0, hh:hh + 1, :] = state[hh]
                n_ref[0, hh:hh + 1, :] = state[hh][1].T[0:1, :]

    def combine(g, jc, h):
        rows, b_col, _, p_col = cols(jc, h)
        m = before[g]
        mm = jnp.maximum(m, p_col)
        iw = jnp.exp(m - mm)
        fl = jnp.exp(p_col - mm)
        pv = pv_scr[g]
        qc = qc_scr[g]
        num = iw * qc[:, :dh] + fl * pv[:, :dh]
        den = iw * qc[:, dh:] + fl * pv[:, dh:]
        hh = num / jnp.maximum(jnp.abs(den), jnp.exp(-(b_col + mm)))
        h2 = hh * hh
        hi = h2.astype(BF16)
        lo = (h2 - hi.astype(F32)).astype(BF16)
        ms = (_dot(hi, ones_dh) + _dot(lo, ones_dh)) * (1.0 / dh)
        hn = (hh * lax.rsqrt(ms + EPS) * go_ref[h:h + 1, :]) * jax.nn.sigmoid(qkv(jc, h, 3))
        ha_ref[0, rows, h * dh:(h + 1) * dh] = hn.astype(BF16)

    return [[functools.partial(fn, g, jc, h) for g, (jc, h) in enumerate(groups)]
            for fn in (score, local, contrib, carry, combine)]


def _run_passes(*pass_lists):
    for steps in zip(*[p + [None] * (max(map(len, pass_lists)) - len(p)) for p in pass_lists]):
        for step in steps:
            if step is not None:
                step()


def _mlstm_call(ua, gates, c0, n0rep, m0rep, gout, *, tq, seg):
    nb, length, aw4 = ua.shape
    _, nh, dh, _ = c0.shape
    assert dh == LANES
    groups = (tq // seg) * nh
    st = lambda shape: pl.BlockSpec((1,) + shape, lambda b, t: (b,) + (0,) * len(shape))
    tile = lambda n: pl.BlockSpec((1, tq, n), lambda b, t: (b, t, 0))
    return pl.pallas_call(
        functools.partial(_mlstm_kernel, seg=seg, nh=nh, dh=dh),
        grid=(nb, length // tq),
        in_specs=[tile(aw4), tile(LANES), st((nh, dh, dh)), st((nh, dh, LANES)), st((nh, LANES)),
                  _const_spec(gout.shape)],
        out_specs=[tile(nh * dh), st((nh, dh, dh)), st((nh, dh)), st((nh, LANES))],
        out_shape=[jax.ShapeDtypeStruct((nb, length, nh * dh), BF16),
                   jax.ShapeDtypeStruct((nb, nh, dh, dh), F32),
                   jax.ShapeDtypeStruct((nb, nh, dh), F32),
                   jax.ShapeDtypeStruct((nb, nh, LANES), F32)],
        scratch_shapes=[pltpu.VMEM((nh, dh, LANES), F32), pltpu.VMEM((3 * nh, tq, LANES), F32),
                        pltpu.VMEM((groups, seg, seg), F32), pltpu.VMEM((groups, seg, dh + LANES), F32),
                        pltpu.VMEM((groups, dh, dh + LANES), F32), pltpu.VMEM((groups, seg, dh + LANES), F32)],
        compiler_params=_cparams(("arbitrary", "arbitrary"), 32),
        name="mlstm",
    )(ua, gates, c0, n0rep, m0rep, gout)


def _relbias_kernel(b0_ref, o_ref):
    nhb, nq, nk = o_ref.shape
    for h in range(nhb):
        x = jnp.broadcast_to(b0_ref[h:h + 1, :], (nq, b0_ref.shape[1]))
        o_ref[h] = pltpu.roll(x, 0, 1, stride=1, stride_axis=0)[:, :nk]


def _relbias_call(table, w):
    nhb = table.shape[0]
    max_rel = (table.shape[1] - 1) // 2
    assert CHUNK - 1 <= max_rel <= w
    first = jnp.broadcast_to(table[:, :1], (nhb, w - max_rel))
    wrap = jnp.broadcast_to(table[:, :1], (nhb, CHUNK))
    b0 = jnp.concatenate([first, table[:, :max_rel + CHUNK], wrap], axis=1).astype(F32)
    return pl.pallas_call(
        _relbias_kernel,
        out_shape=jax.ShapeDtypeStruct((nhb, CHUNK, w + CHUNK), F32),
        name="rel_bias",
    )(b0)


def _band_tile(q_ref, k_ref, v_ref, bias_ref, o_ref, s_scr, m_scr, e_scr, *, masked, npair, w, nck):
    c4 = pl.program_id(1)
    nk = w + CHUNK
    lane = lax.broadcasted_iota(jnp.int32, (CHUNK, LANES), 1)
    low = lane < LANES // 2
    zero = jnp.zeros((CHUNK, LANES), BF16)
    ones = jnp.ones((nk, LANES), BF16)

    starts = [pl.multiple_of((c4 * nck + jc) * CHUNK, CHUNK) for jc in range(nck)]
    groups = [(jc, p) for jc in range(nck) for p in range(npair)]

    def score(g, jc, p):
        sl = slice(p * LANES, (p + 1) * LANES)
        qp = q_ref[0, jc * CHUNK:(jc + 1) * CHUNK, sl]
        q2 = jnp.concatenate([jnp.where(low, qp, zero), jnp.where(low, zero, qp)], axis=0)
        s = _dot_nt(q2, k_ref[0, pl.ds(starts[jc], nk), sl]) + bias_ref[p]
        if masked:
            col = lax.broadcasted_iota(jnp.int32, s.shape, 1)
            s = jnp.where(col + starts[jc] >= w, s, -jnp.inf)
        s_scr[g] = s
        m_scr[g] = jnp.max(s, axis=-1, keepdims=True)

    def expo(g, jc, p):
        e_scr[g] = jnp.exp(s_scr[g] - m_scr[g]).astype(BF16)

    def value(g, jc, p):
        sl = slice(p * LANES, (p + 1) * LANES)
        vx = jnp.concatenate([v_ref[0, pl.ds(starts[jc], nk), sl], ones], axis=1)
        r = _dot(e_scr[g], vx)
        o_lo = r[:CHUNK, :LANES] / r[:CHUNK, LANES:]
        o_hi = r[CHUNK:, :LANES] / r[CHUNK:, LANES:]
        o_ref[0, jc * CHUNK:(jc + 1) * CHUNK, sl] = jnp.where(low, o_lo, o_hi).astype(BF16)

    return [[functools.partial(fn, g, jc, p) for g, (jc, p) in enumerate(groups)] for fn in (score, expo, value)]


def _mixer_prompt_kernel(ua_ref, g_ref, c0_ref, n0_ref, m0_ref, go_ref, q_ref, k_ref, v_ref, bias_ref,
                         ha_ref, c_ref, n_ref, m_ref, hb_ref,
                         nrep_scr, rep_scr, sa_scr, pv_scr, kv_scr, qc_scr, sb_scr, mb_scr, eb_scr,
                         *, seg, nh, dh, npair, w, nck):
    _mlstm_init(c0_ref, n0_ref, m0_ref, c_ref, m_ref, nrep_scr)
    first_full = w // (CHUNK * nck)

    def tile(masked):
        b_score, b_exp, b_value = _band_tile(q_ref, k_ref, v_ref, bias_ref, hb_ref, sb_scr, mb_scr, eb_scr,
                                             masked=masked, npair=npair, w=w, nck=nck)
        a_score, a_local, a_contrib, a_carry, a_combine = _mlstm_tile(
            ua_ref, g_ref, go_ref, ha_ref, c_ref, n_ref, m_ref,
            nrep_scr, rep_scr, sa_scr, pv_scr, kv_scr, qc_scr, seg=seg, nh=nh, dh=dh)
        _run_passes(a_score, b_score)
        _run_passes(a_local, b_exp)
        _run_passes(a_contrib)
        _run_passes(a_carry)
        _run_passes(a_combine, b_value)

    @pl.when(pl.program_id(1) < first_full)
    def _():
        tile(True)

    @pl.when(pl.program_id(1) >= first_full)
    def _():
        tile(False)


def _mixer_prompt_call(ua, gates, c0, n0rep, m0rep, gout, qs, kpad, vpad, bias2, *, w, nck):
    nb, length, aw4 = ua.shape
    _, nh, dh, _ = c0.shape
    bw = qs.shape[2]
    npair = bias2.shape[0]
    lp = kpad.shape[1]
    tq = nck * CHUNK
    nk = w + CHUNK
    ga = nck * nh
    gb = nck * npair
    assert w % tq == 0 and dh == LANES
    st = lambda shape: pl.BlockSpec((1,) + shape, lambda b, t: (b,) + (0,) * len(shape))
    tile = lambda n: pl.BlockSpec((1, tq, n), lambda b, t: (b, t, 0))
    whole = pl.BlockSpec((1, lp, bw), lambda b, t: (b, 0, 0))
    return pl.pallas_call(
        functools.partial(_mixer_prompt_kernel, seg=CHUNK, nh=nh, dh=dh, npair=npair, w=w, nck=nck),
        grid=(nb, length // tq),
        in_specs=[tile(aw4), tile(LANES), st((nh, dh, dh)), st((nh, dh, LANES)), st((nh, LANES)),
                  _const_spec(gout.shape), tile(bw), whole, whole, _const_spec(bias2.shape)],
        out_specs=[tile(nh * dh), st((nh, dh, dh)), st((nh, dh)), st((nh, LANES)), tile(bw)],
        out_shape=[jax.ShapeDtypeStruct((nb, length, nh * dh), BF16),
                   jax.ShapeDtypeStruct((nb, nh, dh, dh), F32),
                   jax.ShapeDtypeStruct((nb, nh, dh), F32),
                   jax.ShapeDtypeStruct((nb, nh, LANES), F32),
                   jax.ShapeDtypeStruct((nb, length, bw), BF16)],
        scratch_shapes=[pltpu.VMEM((nh, dh, LANES), F32), pltpu.VMEM((3 * nh, tq, LANES), F32),
                        pltpu.VMEM((ga, CHUNK, CHUNK), F32), pltpu.VMEM((ga, CHUNK, dh + LANES), F32),
                        pltpu.VMEM((ga, dh, dh + LANES), F32), pltpu.VMEM((ga, CHUNK, dh + LANES), F32),
                        pltpu.VMEM((gb, 2 * CHUNK, nk), F32), pltpu.VMEM((gb, 2 * CHUNK, 1), F32),
                        pltpu.VMEM((gb, 2 * CHUNK, nk), BF16)],
        compiler_params=_cparams(("arbitrary", "arbitrary"), 48),
        name="mixer_prompt",
    )(ua, gates, c0, n0rep, m0rep, gout, qs, kpad, vpad, bias2)


def _band_sample_kernel(q_ref, kn_ref, vn_ref, ck_ref, cv_ref, bias_ref, o_ref, *, npair):
    tq = q_ref.shape[1]
    nk = ck_ref.shape[1] + tq
    lane = lax.broadcasted_iota(jnp.int32, (tq, LANES), 1)
    low = lane < LANES // 2
    zero = jnp.zeros((tq, LANES), BF16)
    ones = jnp.ones((nk, LANES), BF16)
    scores = []
    for p in range(npair):
        sl = slice(p * LANES, (p + 1) * LANES)
        qp = q_ref[0, :, sl]
        q2 = jnp.concatenate([jnp.where(low, qp, zero), jnp.where(low, zero, qp)], axis=0)
        kx = jnp.concatenate([ck_ref[0, :, sl].astype(BF16), kn_ref[0, :, sl].astype(BF16)], axis=0)
        scores.append(_dot_nt(q2, kx) + bias_ref[p])
    probs = [jnp.exp(s - jnp.max(s, axis=-1, keepdims=True)).astype(BF16) for s in scores]
    for p in range(npair):
        sl = slice(p * LANES, (p + 1) * LANES)
        vx = jnp.concatenate([cv_ref[0, :, sl].astype(BF16), vn_ref[0, :, sl].astype(BF16)], axis=0)
        r = _dot(probs[p], jnp.concatenate([vx, ones], axis=1))
        o_lo = r[:tq, :LANES] / r[:tq, LANES:]
        o_hi = r[tq:, :LANES] / r[tq:, LANES:]
        o_ref[0, :, sl] = jnp.where(low, o_lo, o_hi).astype(BF16)


def _band_sample_call(qn, kn, vn, ck, cv, bias2):
    nb, tq, bw = qn.shape
    w = ck.shape[1]
    npair = bias2.shape[0]
    new = pl.BlockSpec((1, tq, bw), lambda b: (b, 0, 0))
    cache = pl.BlockSpec((1, w, bw), lambda b: (b, 0, 0))
    return pl.pallas_call(
        functools.partial(_band_sample_kernel, npair=npair),
        grid=(nb,),
        in_specs=[new, new, new, cache, cache, _const_spec(bias2.shape)],
        out_specs=new,
        out_shape=jax.ShapeDtypeStruct((nb, tq, bw), BF16),
        compiler_params=_cparams(("arbitrary",), 32),
        name="band_sample",
    )(qn, kn, vn, ck, cv, bias2)


def _mixout_kernel(x_ref, ha_ref, hb_ref, gt_ref, woa_ref, wob_ref, o_ref):
    bb, tl, d = x_ref.shape
    ha = ha_ref[...].reshape(bb * tl, -1)
    hb = hb_ref[...].reshape(bb * tl, -1)
    y = _dot(ha, woa_ref[...]) + _dot(hb, wob_ref[...])
    o_ref[...] = x_ref[...] + gt_ref[...] * y.reshape(bb, tl, d)


def _mixout_call(x, ha, hb, gt, woa, wob, bb, tl):
    nb, length, d = x.shape
    tile = lambda n: pl.BlockSpec((bb, tl, n), lambda i, t: (i, t, 0))
    return pl.pallas_call(
        _mixout_kernel,
        grid=(nb // bb, length // tl),
        in_specs=[tile(d), tile(ha.shape[-1]), tile(hb.shape[-1]),
                  pl.BlockSpec((bb, 1, d), lambda i, t: (i, 0, 0)),
                  _const_spec(woa.shape), _const_spec(wob.shape)],
        out_specs=tile(d),
        out_shape=jax.ShapeDtypeStruct(x.shape, F32),
        compiler_params=_cparams(("arbitrary", "arbitrary"), 32),
        name="mix_out",
    )(x, ha, hb, gt, woa, wob)


def _rglru_gates(xc, gw_ref, rb, ib, lam, nblk):
    bwc = xc.shape[1] // nblk
    r_parts, i_parts = [], []
    for n in range(nblk):
        gn = _dot(xc[:, n * bwc:(n + 1) * bwc].astype(BF16), gw_ref[n])
        r_parts.append(gn[:, :bwc])
        i_parts.append(gn[:, bwc:])
    r = jax.nn.sigmoid(jnp.concatenate(r_parts, axis=1) + rb)
    ii = jax.nn.sigmoid(jnp.concatenate(i_parts, axis=1) + ib)
    log_a = r * (-LRU_C * _softplus(-lam))
    a = jnp.exp(log_a)
    th = jnp.tanh(log_a)
    v = -2.0 * th / (1.0 - th)
    root = jnp.where(v > 0.0, v * lax.rsqrt(v), 0.0)
    return a, root * (ii * xc)


def _rglru_prompt_kernel(x_ref, sh_ref, sc_ref, gt_ref, g_ref, win_ref, cw_ref, cb_ref, gw_ref, rb_ref, ib_ref,
                         lam_ref, wout_ref, conv0_ref, h0_ref, o_ref, conv_ref, hl_ref, xp_scr, a_scr, b_scr, *, nblk):
    t = pl.program_id(1)
    tq, d = x_ref.shape[1], x_ref.shape[2]
    r_w = lam_ref.shape[1]
    ncw = cw_ref.shape[0]

    @pl.when(t == 0)
    def _():
        xp_scr[0:SUBLANES, :] = conv0_ref[0]
        hl_ref[...] = h0_ref[...]

    x = x_ref[0]
    hm = _rms_mod(x, g_ref[...], sh_ref[0], sc_ref[0]).astype(BF16)
    u = _dot(hm, win_ref[...])
    gb = u[:, :r_w]
    xp_scr[SUBLANES:SUBLANES + tq, :] = u[:, r_w:]
    xc = cb_ref[...]
    for j in range(ncw):
        off = SUBLANES - (ncw - 1 - j)
        xc = xc + xp_scr[off:off + tq, :] * cw_ref[j:j + 1, :]
    conv_ref[0] = xp_scr[tq:tq + SUBLANES, :]
    xp_scr[0:SUBLANES, :] = xp_scr[tq:tq + SUBLANES, :]
    a, upd = _rglru_gates(xc, gw_ref, rb_ref[...], ib_ref[...], lam_ref[...], nblk)
    a_scr[...] = a
    b_scr[...] = upd
    row8 = lax.broadcasted_iota(jnp.int32, (SUBLANES, r_w), 0)

    def scan_body(i, h):
        rows = pl.ds(pl.multiple_of(i * SUBLANES, SUBLANES), SUBLANES)
        ai = a_scr[rows, :]
        bi = b_scr[rows, :]
        s = 1
        while s < SUBLANES:
            m = row8 >= s
            bi = jnp.where(m, ai * pltpu.roll(bi, s, 0) + bi, bi)
            ai = jnp.where(m, ai * pltpu.roll(ai, s, 0), ai)
            s *= 2
        hs = ai * h + bi
        a_scr[rows, :] = hs
        return hs[SUBLANES - 1:SUBLANES, :]

    h_fin = lax.fori_loop(0, tq // SUBLANES, scan_body, hl_ref[0])
    hl_ref[0] = h_fin
    y = _dot((_gelu_tanh(gb) * a_scr[...]).astype(BF16), wout_ref[...])
    o_ref[0] = x + gt_ref[0] * y


def _rglru_prompt_call(x, sh, sc, gt, g, win, cw, cb, gw, rb, ib, lam, wout, conv0, h0, *, tq):
    nb, length, d = x.shape
    r_w = lam.shape[1]
    nblk = gw.shape[0]
    ada_spec = pl.BlockSpec((1, 1, d), lambda b, t: (b, 0, 0))
    tile = pl.BlockSpec((1, tq, d), lambda b, t: (b, t, 0))
    conv_spec = pl.BlockSpec((1, SUBLANES, r_w), lambda b, t: (b, 0, 0))
    h_spec = pl.BlockSpec((1, 1, r_w), lambda b, t: (b, 0, 0))
    consts = [g, win, cw, cb, gw, rb, ib, lam, wout]
    return pl.pallas_call(
        functools.partial(_rglru_prompt_kernel, nblk=nblk),
        grid=(nb, length // tq),
        in_specs=[tile, ada_spec, ada_spec, ada_spec] + [_const_spec(a.shape) for a in consts] + [conv_spec, h_spec],
        out_specs=[tile, conv_spec, h_spec],
        out_shape=[jax.ShapeDtypeStruct(x.shape, F32),
                   jax.ShapeDtypeStruct((nb, SUBLANES, r_w), F32),
                   jax.ShapeDtypeStruct((nb, 1, r_w), F32)],
        scratch_shapes=[pltpu.VMEM((tq + SUBLANES, r_w), F32), pltpu.VMEM((tq, r_w), F32), pltpu.VMEM((tq, r_w), F32)],
        compiler_params=_cparams(("arbitrary", "arbitrary"), 48),
        name="rglru_prompt",
    )(x, sh, sc, gt, *consts, conv0, h0)


def _rglru_sample_kernel(x_ref, sh_ref, sc_ref, gt_ref, g_ref, win_ref, cw_ref, cb_ref, gw_ref, rb_ref, ib_ref,
                         lam_ref, wout_ref, conv0_ref, h0_ref, o_ref, conv_ref, hl_ref, xp_scr, *, nblk):
    bb, tl, d = x_ref.shape
    tm = bb * tl
    r_w = lam_ref.shape[1]
    ncw = cw_ref.shape[0]
    x = x_ref[...]
    hm = _rms_mod(x, g_ref[...], sh_ref[...], sc_ref[...]).reshape(tm, d).astype(BF16)
    u = _dot(hm, win_ref[...])
    gb = u[:, :r_w]
    xp_scr[:, 0:SUBLANES, :] = conv0_ref[...]
    xp_scr[:, SUBLANES:SUBLANES + tl, :] = u[:, r_w:].reshape(bb, tl, r_w)
    xc = jnp.broadcast_to(cb_ref[...], (bb, tl, r_w))
    for j in range(ncw):
        off = SUBLANES - (ncw - 1 - j)
        xc = xc + xp_scr[:, off:off + tl, :] * cw_ref[j:j + 1, :]
    conv_ref[...] = xp_scr[:, tl:tl + SUBLANES, :]
    a, b = _rglru_gates(xc.reshape(tm, r_w), gw_ref, rb_ref[...], ib_ref[...], lam_ref[...], nblk)
    pos = lax.broadcasted_iota(jnp.int32, (tm, r_w), 0) % tl
    s = 1
    while s < tl:
        m = pos >= s
        b = jnp.where(m, a * pltpu.roll(b, s, 0) + b, b)
        a = jnp.where(m, a * pltpu.roll(a, s, 0), a)
        s *= 2
    hs = a.reshape(bb, tl, r_w) * h0_ref[...] + b.reshape(bb, tl, r_w)
    hl_ref[...] = hs[:, tl - 1:tl, :]
    y = _dot((_gelu_tanh(gb) * hs.reshape(tm, r_w)).astype(BF16), wout_ref[...])
    o_ref[...] = x + gt_ref[...] * y.reshape(bb, tl, d)


def _rglru_sample_call(x, sh, sc, gt, g, win, cw, cb, gw, rb, ib, lam, wout, conv0, h0):
    nb, tl, d = x.shape
    r_w = lam.shape[1]
    nblk = gw.shape[0]
    full = lambda a, b: pl.BlockSpec((nb, a, b), lambda i: (0, 0, 0))
    consts = [g, win, cw, cb, gw, rb, ib, lam, wout]
    return pl.pallas_call(
        functools.partial(_rglru_sample_kernel, nblk=nblk),
        grid=(1,),
        in_specs=[full(tl, d), full(1, d), full(1, d), full(1, d)] + [_const_spec(a.shape) for a in consts]
                 + [full(SUBLANES, r_w), full(1, r_w)],
        out_specs=[full(tl, d), full(SUBLANES, r_w), full(1, r_w)],
        out_shape=[jax.ShapeDtypeStruct(x.shape, F32),
                   jax.ShapeDtypeStruct((nb, SUBLANES, r_w), F32),
                   jax.ShapeDtypeStruct((nb, 1, r_w), F32)],
        scratch_shapes=[pltpu.VMEM((nb, tl + SUBLANES, r_w), F32)],
        compiler_params=_cparams(("arbitrary",), 48),
        name="rglru_sample",
    )(x, sh, sc, gt, *consts, conv0, h0)


def _pad_rows_front(a, rows):
    return jnp.pad(a, ((0, 0), (rows - a.shape[1], 0), (0, 0)))


def kernel(x_prompt, x_sample, state_a_C, state_a_n, state_a_m, cache_b_k, cache_b_v, state_c_conv, state_c_h,
           c_prompt, c_sample, ffn1_norm, ffn1_w_in, ffn1_w_out, mix_norm, ffn2_norm, ffn2_w_in, ffn2_w_out,
           ada_w, ada_b, ab_w_in, ab_gate_bias, a_out_norm, b_q_norm, b_k_norm, b_rel_bias, ab_w_out,
           c_w_in, c_conv_w, c_conv_b, c_gate_w, c_gate_b, c_lambda, c_w_out):
    nbp, seq, d = x_prompt.shape
    nbs, tdec, _ = x_sample.shape
    depth = ada_w.shape[0]
    n_ada = ada_w.shape[2] // d
    _, _, nh, dh, _ = state_a_C.shape
    _, _, w_band, nhb, dhb = cache_b_k.shape
    aw, bw = nh * dh, nhb * dhb
    ncw = c_conv_w.shape[1]
    assert 2 * dhb == LANES and dh == LANES and w_band % CHUNK == 0 and seq % w_band == 0

    ada = _ada_call(jnp.concatenate([c_prompt, c_sample], axis=0), ada_w, ada_b)
    ada = ada.reshape(depth, nbp + nbs, n_ada, 1, d)
    ada_p = [[ada[l, :nbp, k] for k in range(n_ada)] for l in range(depth)]
    ada_s = [[ada[l, nbp:, k] for k in range(n_ada)] for l in range(depth)]

    tl_p = 512
    xp, xs = x_prompt, x_sample
    outs_p, outs_s = {}, {}
    for l in range(depth):
        ap, as_ = ada_p[l], ada_s[l]
        i = l // 2
        g1 = ffn1_norm[l].reshape(1, d)
        gm = mix_norm[l].reshape(1, d)
        g2 = ffn2_norm[l].reshape(1, d)
        mix_p = None
        xp = _ffn_call(xp, ap[0], ap[1], ap[2], g1, ffn1_w_in, ffn1_w_out, l, 1, tl_p)
        xs = _ffn_stream_call(xs, as_[0], as_[1], as_[2], g1, ffn1_w_in, ffn1_w_out, l)
        if l % 2 == 0:
            w_in = ab_w_in[i]
            wab = jnp.concatenate(
                [w_in[:, :4 * aw], w_in[:, 4 * aw + 2 * nh:], w_in[:, 4 * aw:4 * aw + 2 * nh],
                 jnp.zeros((d, LANES - 2 * nh), F32)], axis=1).astype(BF16)
            gbias = jnp.pad(ab_gate_bias[i], (0, LANES - 2 * nh)).reshape(1, LANES)
            qg = jnp.tile(b_q_norm[i], nhb).reshape(1, bw)
            kg = jnp.tile(b_k_norm[i], nhb).reshape(1, bw)
            head = jnp.arange(bw) // dhb
            e = (head[:, None] == head[None, :]).astype(BF16)
            woa = ab_w_out[i][:aw].astype(BF16)
            wob = ab_w_out[i][aw:].astype(BF16)
            gout = a_out_norm[i]
            bias = _relbias_call(b_rel_bias[i], w_band)
            bias2 = bias.reshape(nhb // 2, 2 * CHUNK, w_band + CHUNK)

            ua, gts, qn, kpad, vpad, klast, vlast = _proj_prompt_call(
                xp, ap[3], ap[4], gm, wab, gbias, qg, kg, e, nh=nh, dhb=dhb, w=w_band)
            zc = jnp.zeros((nbp, nh, dh, dh), F32)
            ha, c1, n1, m1, hb = _mixer_prompt_call(ua, gts, zc, zc, zc[:, :, 0], gout, qn, kpad, vpad, bias2,
                                                    w=w_band, nck=4)
            mix_p = (ha, hb, ap[5], woa, wob)
            outs_p.setdefault('a_C', []).append(c1)
            outs_p.setdefault('a_n', []).append(n1)
            outs_p.setdefault('a_m', []).append(m1[:, :, 0])
            outs_p.setdefault('b_k', []).append(klast.reshape(nbp, w_band, nhb, dhb))
            outs_p.setdefault('b_v', []).append(vlast.reshape(nbp, w_band, nhb, dhb))

            ua, gts, qn, kn, vn = _proj_sample_call(
                xs, as_[3], as_[4], gm, wab, gbias, qg, kg, e, nh=nh, dhb=dhb)
            ha, c1, n1, m1 = _mlstm_call(
                ua, gts, state_a_C[i],
                jnp.broadcast_to(state_a_n[i][..., None], (nbs, nh, dh, LANES)),
                jnp.broadcast_to(state_a_m[i][..., None], (nbs, nh, LANES)), gout, tq=tdec, seg=tdec)
            hb = _band_sample_call(qn, kn, vn, cache_b_k[i].reshape(nbs, w_band, bw),
                                   cache_b_v[i].reshape(nbs, w_band, bw),
                                   bias[:, :tdec, :w_band + tdec].reshape(nhb // 2, 2 * tdec, w_band + tdec))
            xs = _mixout_call(xs, ha, hb, as_[5], woa, wob, nbs, tdec)
            outs_s.setdefault('a_C', []).append(c1)
            outs_s.setdefault('a_n', []).append(n1)
            outs_s.setdefault('a_m', []).append(m1[:, :, 0])
            outs_s.setdefault('b_k', []).append(kn.reshape(nbs, tdec, nhb, dhb))
            outs_s.setdefault('b_v', []).append(vn.reshape(nbs, tdec, nhb, dhb))
        else:
            r_w = c_lambda.shape[1]
            consts = (gm, c_w_in[i].astype(BF16), c_conv_w[i], c_conv_b[i].reshape(1, r_w), c_gate_w[i].astype(BF16),
                      c_gate_b[i][0].reshape(1, r_w), c_gate_b[i][1].reshape(1, r_w), c_lambda[i].reshape(1, r_w),
                      c_w_out[i].astype(BF16))
            xp, conv_p, h_p = _rglru_prompt_call(
                xp, ap[3], ap[4], ap[5], *consts,
                jnp.zeros((nbp, SUBLANES, r_w), F32), jnp.zeros((nbp, 1, r_w), F32), tq=512)
            xs, conv_s, h_s = _rglru_sample_call(
                xs, as_[3], as_[4], as_[5], *consts,
                _pad_rows_front(state_c_conv[i], SUBLANES), state_c_h[i][:, None, :])
            outs_p.setdefault('c_conv', []).append(conv_p[:, SUBLANES - (ncw - 1):])
            outs_p.setdefault('c_h', []).append(h_p[:, 0])
            outs_s.setdefault('c_conv', []).append(conv_s[:, SUBLANES - (ncw - 1):])
            outs_s.setdefault('c_h', []).append(h_s[:, 0])
        xp = _ffn_call(xp, ap[6], ap[7], ap[8], g2, ffn2_w_in, ffn2_w_out, l, 1, tl_p, mix=mix_p)
        xs = _ffn_stream_call(xs, as_[6], as_[7], as_[8], g2, ffn2_w_in, ffn2_w_out, l)

    names = ('a_C', 'a_n', 'a_m', 'b_k', 'b_v', 'c_conv', 'c_h')
    ps = [jnp.stack(outs_p[n], axis=0) for n in names]
    ss = [jnp.stack(outs_s[n], axis=0) for n in names]
    return (xp, xs, *ps, *ss)
```

```python
import functools

import jax
import jax.numpy as jnp
from jax import lax
from jax.experimental import pallas as pl
from jax.experimental.pallas import tpu as pltpu

F32 = jnp.float32
BF16 = jnp.bfloat16

EPS = 1e-6
CHUNK = 64
LRU_C = 8.0
LANES = 128
SUBLANES = 8
MIB = 1024 * 1024


def _cparams(semantics, vmem_mib):
    return pltpu.CompilerParams(dimension_semantics=semantics, vmem_limit_bytes=vmem_mib * MIB)


def _const_spec(shape):
    nd = len(shape)
    return pl.BlockSpec(shape, lambda *_: (0,) * nd, pipeline_mode=pl.Buffered(1))


def _dot(a, b):
    return jnp.dot(a, b, preferred_element_type=F32)


def _dot_nt(a, b):
    return lax.dot_general(a, b, (((1,), (1,)), ((), ())), preferred_element_type=F32)


def _dot_tn(a, b):
    return lax.dot_general(a, b, (((0,), (0,)), ((), ())), preferred_element_type=F32)


def _rms_mod(x, g, shift, scale):
    ms = jnp.mean(x * x, axis=-1, keepdims=True)
    return (x * lax.rsqrt(ms + EPS)) * (g * (1.0 + scale)) + shift


def _softplus(x):
    return jnp.maximum(x, 0.0) + jnp.log1p(jnp.exp(-jnp.abs(x)))


def _gelu_tanh(x):
    c = 0.7978845608028654
    inner = x * ((x * x) * (c * 0.044715) + c)
    return x * (0.5 * jnp.tanh(inner) + 0.5)


def _ada_kernel(c_ref, w_ref, b_ref, o_ref):
    c = c_ref[...].astype(BF16)
    w = w_ref[0].astype(BF16)
    o_ref[0] = _dot(c, w) + b_ref[0]


def _ada_call(c_all, ada_w, ada_b):
    depth, d, n = ada_w.shape
    m = c_all.shape[0]
    tn = d
    return pl.pallas_call(
        _ada_kernel,
        grid=(depth, n // tn),
        in_specs=[pl.BlockSpec((m, d), lambda l, j: (0, 0)),
                  pl.BlockSpec((1, d, tn), lambda l, j: (l, 0, j)),
                  pl.BlockSpec((1, 1, tn), lambda l, j: (l, 0, j))],
        out_specs=pl.BlockSpec((1, m, tn), lambda l, j: (l, 0, j)),
        out_shape=jax.ShapeDtypeStruct((depth, m, n), F32),
        compiler_params=_cparams(("arbitrary", "arbitrary"), 32),
        name="ada_proj",
    )(c_all, ada_w, ada_b.reshape(depth, 1, n))


FFN_TF = 256


def _ffn_kernel(*refs, mixed, n_prompt):
    if mixed:
        ha_ref, hb_ref, gm_ref, woa_ref, wob_ref = refs[:5]
        refs = refs[5:]
    x_ref, sh_ref, sc_ref, gt_ref, xs_ref, adas_ref, g_ref, win_ref, wo_ref, o_ref, os_ref, act_scr = refs
    is_sample = pl.program_id(0) == n_prompt
    _, tl, d = x_ref.shape
    nbs = adas_ref.shape[1]
    dff = wo_ref.shape[0]

    def half_step(x, shift, scale, gate_vec):
        h = _rms_mod(x, g_ref[...], shift, scale).astype(BF16)
        for c0 in range(0, dff, FFN_TF):
            gate = _dot(h, win_ref[:, c0:c0 + FFN_TF].astype(BF16))
            up = _dot(h, win_ref[:, dff + c0:dff + c0 + FFN_TF].astype(BF16))
            act_scr[:, c0:c0 + FFN_TF] = ((gate * jax.nn.sigmoid(gate)) * up).astype(BF16)
        y = _dot(act_scr[...], wo_ref[...].astype(BF16))
        return x + (0.5 * gate_vec) * y

    @pl.when(jnp.logical_not(is_sample))
    def _():
        x = x_ref[0]
        if mixed:
            x = x + gm_ref[0] * (_dot(ha_ref[0], woa_ref[...]) + _dot(hb_ref[0], wob_ref[...]))
        o_ref[0] = half_step(x, sh_ref[0], sc_ref[0], gt_ref[0])

    @pl.when(is_sample)
    def _():
        def rows(k):
            return jnp.broadcast_to(adas_ref[k][:, None, :], (nbs, tl // nbs, d)).reshape(tl, d)
        os_ref[...] = half_step(xs_ref[...], rows(0), rows(1), rows(2))


def _ffn_call(xp, ada_p, xs, ada_s, g, w_in, w_out, layer, tl, mix=None):
    nb, length, d = xp.shape
    nbs, tdec, _ = xs.shape
    dff = w_out.shape[1]
    assert dff % FFN_TF == 0 and nbs * tdec == tl
    nt = length // tl
    n_prompt = nb * nt
    cur = lambda i: jnp.minimum(i, n_prompt - 1)
    tile = lambda n: pl.BlockSpec((1, tl, n), lambda i: (cur(i) // nt, cur(i) % nt, 0))
    ada_spec = pl.BlockSpec((1, 1, d), lambda i: (cur(i) // nt, 0, 0))
    once = lambda shape: pl.BlockSpec(shape, lambda i: (0,) * len(shape), pipeline_mode=pl.Buffered(1))
    layer_spec = lambda shape: pl.BlockSpec((None,) + shape, lambda i: (layer, 0, 0), pipeline_mode=pl.Buffered(1))
    mix_args, mix_specs = [], []
    if mix is not None:
        ha, hb, gm, woa, wob = mix
        mix_args = [ha, hb, gm, woa, wob]
        mix_specs = [tile(ha.shape[-1]), tile(hb.shape[-1]), ada_spec, once(woa.shape), once(wob.shape)]
    adas = jnp.stack([a[:, 0] for a in ada_s])
    op, os = pl.pallas_call(
        functools.partial(_ffn_kernel, mixed=mix is not None, n_prompt=n_prompt),
        grid=(n_prompt + 1,),
        in_specs=mix_specs + [tile(d), ada_spec, ada_spec, ada_spec, once((tl, d)), once(adas.shape), once((1, d)),
                              layer_spec(w_in.shape[1:]), layer_spec(w_out.shape[1:])],
        out_specs=[tile(d), pl.BlockSpec((tl, d), lambda i: (0, 0))],
        out_shape=[jax.ShapeDtypeStruct(xp.shape, F32), jax.ShapeDtypeStruct((tl, d), F32)],
        scratch_shapes=[pltpu.VMEM((tl, dff), BF16)],
        compiler_params=_cparams(("arbitrary",), 60),
        name="ffn",
    )(*mix_args, xp, *ada_p, xs.reshape(tl, d), adas, g, w_in, w_out)
    return op, os.reshape(nbs, tdec, d)


def _head_rmsnorm(q, e, g, dhb):
    ss = _dot((q * q).astype(BF16), e)
    return q * lax.rsqrt(ss * (1.0 / dhb) + EPS) * g


def _proj_body(x, sh, sc, g, w_ref, gb_ref, qg_ref, kg_ref, e_ref, *, nh, bw, dhb):
    bb, tl, d = x.shape
    na = w_ref.shape[1] - 3 * bw - LANES
    h = _rms_mod(x, g, sh, sc).reshape(bb * tl, d).astype(BF16)
    ua = _dot(h, w_ref[:, :na])
    gg = _dot(h, w_ref[:, na + 3 * bw:]) + gb_ref[...]
    lane = lax.broadcasted_iota(jnp.int32, gg.shape, 1)
    gates = jnp.where(lane < nh, gg, -_softplus(-gg))
    ub = _dot(h, w_ref[:, na:na + 3 * bw])
    e = e_ref[...]
    qn = _head_rmsnorm(ub[:, :bw], e, qg_ref[...], dhb) * (dhb ** -0.5)
    kn = _head_rmsnorm(ub[:, bw:2 * bw], e, kg_ref[...], dhb)
    vb = ub[:, 2 * bw:]
    return ua, gates, qn, kn, vb


def _proj_prompt_kernel(x_ref, sh_ref, sc_ref, g_ref, w_ref, gb_ref, qg_ref, kg_ref, e_ref,
                        ua_ref, gt_ref, qn_ref, kp_ref, vp_ref, kl_ref, vl_ref, *, nh, bw, dhb):
    t = pl.program_id(1)
    nt = pl.num_programs(1)

    @pl.when(t == 0)
    def _():
        kp_ref[...] = jnp.zeros_like(kp_ref)
        vp_ref[...] = jnp.zeros_like(vp_ref)

    @pl.when(t > 0)
    def _():
        ua, gates, qn, kn, vb = _proj_body(x_ref[...], sh_ref[...], sc_ref[...], g_ref[...], w_ref,
                                           gb_ref, qg_ref, kg_ref, e_ref, nh=nh, bw=bw, dhb=dhb)
        ua_ref[0] = ua
        gt_ref[0] = gates
        qn_ref[0] = qn.astype(BF16)
        kp_ref[0] = kn.astype(BF16)
        vp_ref[0] = vb.astype(BF16)

        @pl.when(t == nt - 1)
        def _():
            kl_ref[0] = kn
            vl_ref[0] = vb


def _proj_prompt_call(x, sh, sc, g, wab, gbias, qg, kg, e, *, nh, dhb, w):
    nb, length, d = x.shape
    tl = w
    nt = length // tl
    bw = e.shape[0]
    aw4 = wab.shape[1] - 3 * bw - LANES
    prev = lambda b, t: (b, jnp.maximum(t - 1, 0), 0)
    ada_spec = pl.BlockSpec((1, 1, d), lambda b, t: (b, 0, 0))
    outs = pl.pallas_call(
        functools.partial(_proj_prompt_kernel, nh=nh, bw=bw, dhb=dhb),
        grid=(nb, nt + 1),
        in_specs=[pl.BlockSpec((1, tl, d), prev), ada_spec, ada_spec, _const_spec((1, d)),
                  _const_spec(wab.shape), _const_spec(gbias.shape),
                  _const_spec(qg.shape), _const_spec(kg.shape), _const_spec(e.shape)],
        out_specs=[pl.BlockSpec((1, tl, aw4), prev),
                   pl.BlockSpec((1, tl, LANES), prev),
                   pl.BlockSpec((1, tl, bw), prev),
                   pl.BlockSpec((1, tl, bw), lambda b, t: (b, t, 0)),
                   pl.BlockSpec((1, tl, bw), lambda b, t: (b, t, 0)),
                   pl.BlockSpec((1, tl, bw), lambda b, t: (b, 0, 0)),
                   pl.BlockSpec((1, tl, bw), lambda b, t: (b, 0, 0))],
        out_shape=[jax.ShapeDtypeStruct((nb, length, aw4), F32),
                   jax.ShapeDtypeStruct((nb, length, LANES), F32),
                   jax.ShapeDtypeStruct((nb, length, bw), BF16),
                   jax.ShapeDtypeStruct((nb, length + w, bw), BF16),
                   jax.ShapeDtypeStruct((nb, length + w, bw), BF16),
                   jax.ShapeDtypeStruct((nb, w, bw), F32),
                   jax.ShapeDtypeStruct((nb, w, bw), F32)],
        compiler_params=_cparams(("arbitrary", "arbitrary"), 48),
        name="proj_prompt",
    )(x, sh, sc, g, wab, gbias, qg, kg, e)
    return outs


def _proj_sample_kernel(x_ref, sh_ref, sc_ref, g_ref, w_ref, gb_ref, qg_ref, kg_ref, e_ref,
                        ua_ref, gt_ref, qn_ref, kn_ref, vb_ref, *, nh, bw, dhb):
    bb, tl, _ = x_ref.shape
    ua, gates, qn, kn, vb = _proj_body(x_ref[...], sh_ref[...], sc_ref[...], g_ref[...], w_ref,
                                       gb_ref, qg_ref, kg_ref, e_ref, nh=nh, bw=bw, dhb=dhb)
    ua_ref[...] = ua.reshape(bb, tl, -1)
    gt_ref[...] = gates.reshape(bb, tl, -1)
    qn_ref[...] = qn.reshape(bb, tl, -1).astype(BF16)
    kn_ref[...] = kn.reshape(bb, tl, -1)
    vb_ref[...] = vb.reshape(bb, tl, -1)


def _proj_sample_call(x, sh, sc, g, wab, gbias, qg, kg, e, *, nh, dhb):
    nb, length, d = x.shape
    bw = e.shape[0]
    aw4 = wab.shape[1] - 3 * bw - LANES
    full = lambda n: pl.BlockSpec((nb, length, n), lambda i: (0, 0, 0))
    ada_spec = pl.BlockSpec((nb, 1, d), lambda i: (0, 0, 0))
    return pl.pallas_call(
        functools.partial(_proj_sample_kernel, nh=nh, bw=bw, dhb=dhb),
        grid=(1,),
        in_specs=[full(d), ada_spec, ada_spec, _const_spec((1, d)),
                  _const_spec(wab.shape), _const_spec(gbias.shape),
                  _const_spec(qg.shape), _const_spec(kg.shape), _const_spec(e.shape)],
        out_specs=[full(aw4), full(LANES), full(bw), full(bw), full(bw)],
        out_shape=[jax.ShapeDtypeStruct((nb, length, aw4), F32),
                   jax.ShapeDtypeStruct((nb, length, LANES), F32),
                   jax.ShapeDtypeStruct((nb, length, bw), BF16),
                   jax.ShapeDtypeStruct((nb, length, bw), F32),
                   jax.ShapeDtypeStruct((nb, length, bw), F32)],
        compiler_params=_cparams(("arbitrary",), 48),
        name="proj_sample",
    )(x, sh, sc, g, wab, gbias, qg, kg, e)


def _mlstm_init(c0_ref, n0_ref, m0_ref, c_ref, m_ref, nrep_scr):
    @pl.when(pl.program_id(1) == 0)
    def _():
        c_ref[...] = c0_ref[...]
        nrep_scr[...] = n0_ref[0]
        m_ref[...] = m0_ref[...]


def _mlstm_kernel(ua_ref, g_ref, c0_ref, n0_ref, m0_ref, go_ref, ha_ref, c_ref, n_ref, m_ref,
                  nrep_scr, rep_scr, s_scr, pv_scr, kv_scr, qc_scr, *, seg, nh, dh):
    _mlstm_init(c0_ref, n0_ref, m0_ref, c_ref, m_ref, nrep_scr)
    for steps in _mlstm_tile(ua_ref, g_ref, go_ref, ha_ref, c_ref, n_ref, m_ref,
                             nrep_scr, rep_scr, s_scr, pv_scr, kv_scr, qc_scr, seg=seg, nh=nh, dh=dh):
        _run_passes(steps)


def _mlstm_tile(ua_ref, g_ref, go_ref, ha_ref, c_ref, n_ref, m_ref,
                nrep_scr, rep_scr, s_scr, pv_scr, kv_scr, qc_scr, *, seg, nh, dh):
    tq = ua_ref.shape[1]
    nck = tq // seg
    aw = nh * dh
    gates = g_ref[0]
    pos = lax.broadcasted_iota(jnp.int32, gates.shape, 0) % seg
    bt = gates
    s = 1
    while s < seg:
        bt = bt + jnp.where(pos >= s, pltpu.roll(bt, s, 0), 0.0)
        s *= 2
    dmb = pltpu.roll(gates, nh, 1) - bt
    pm = dmb
    s = 1
    while s < seg:
        pm = jnp.maximum(pm, jnp.where(pos >= s, pltpu.roll(pm, s, 0), -jnp.inf))
        s *= 2
    if tq % LANES:
        dsq = jnp.concatenate([dmb, jnp.zeros((LANES - tq % LANES, LANES), F32)], axis=0)
    else:
        dsq = dmb
    dtr = dsq.T
    ri = lax.broadcasted_iota(jnp.int32, (seg, seg), 0)
    ci = lax.broadcasted_iota(jnp.int32, (seg, seg), 1)
    causal = ri >= ci
    ones = jnp.ones((seg, LANES), BF16)
    ones_dh = jnp.ones((dh, LANES), BF16)
    for h in range(nh):
        ln = slice(nh + h, nh + h + 1)
        for j, arr in enumerate((bt, dmb, pm)):
            rep_scr[3 * h + j] = jnp.broadcast_to(arr[:, ln], (tq, LANES))

    def cols(jc, h):
        rows = slice(jc * seg, (jc + 1) * seg)
        return rows, rep_scr[3 * h, rows, :], rep_scr[3 * h + 1, rows, :], rep_scr[3 * h + 2, rows, :]

    def last(jc, h, j):
        r = (jc + 1) * seg - 1
        return rep_scr[3 * h + j, r:r + 1, :]

    groups = [(jc, h) for jc in range(nck) for h in range(nh)]

    def qkv(jc, h, which):
        rows = slice(jc * seg, (jc + 1) * seg)
        return ua_ref[0, rows, which * aw + h * dh:which * aw + (h + 1) * dh]

    state = []
    before = []

    def score(g, jc, h):
        k = qkv(jc, h, 1) * (dh ** -0.5)
        s_scr[g] = _dot_nt(qkv(jc, h, 0).astype(BF16), k.astype(BF16))

    def local(g, jc, h):
        rows, _, _, p_col = cols(jc, h)
        d_row = dtr[nh + h:nh + h + 1, rows]
        dloc = jnp.exp(jnp.where(causal, d_row - p_col[:, :seg], -jnp.inf))
        sl = (s_scr[g] * dloc).astype(BF16)
        v = qkv(jc, h, 2).astype(BF16)
        pv_scr[g] = _dot(sl, jnp.concatenate([v, ones], axis=1))

    def contrib(g, jc, h):
        _, _, d_col, _ = cols(jc, h)
        kw = (qkv(jc, h, 1) * (dh ** -0.5)) * jnp.exp(d_col - last(jc, h, 2))
        vx = jnp.concatenate([qkv(jc, h, 2).astype(BF16), ones], axis=1)
        kv_scr[g] = _dot_tn(kw.astype(BF16), vx)

    def carry(g, jc, h):
        if not state:
            state.extend((c_ref[0, hh], nrep_scr[hh], m_ref[0, hh:hh + 1, :]) for hh in range(nh))
        c_mem, n_rep, m = state[h]
        cn = jnp.concatenate([c_mem.astype(BF16), n_rep.astype(BF16)], axis=1)
        qc_scr[g] = _dot(qkv(jc, h, 0).astype(BF16), cn)
        before.append(m)
        p_last = last(jc, h, 2)
        mml = jnp.maximum(m, p_last)
        w_prev = jnp.exp(m - mml)
        f_new = jnp.exp(p_last - mml)
        kvx = kv_scr[g]
        state[h] = (w_prev * c_mem + f_new * kvx[:, :dh],
                    w_prev * n_rep + f_new * kvx[:, dh:],
                    last(jc, h, 0) + mml)
        if g == len(groups) - 1:
            for hh in range(nh):
                c_ref[0, hh], nrep_scr[hh], m_ref[0, hh:hh + 1, :] = state[hh]
                n_ref[0, hh:hh + 1, :] = state[hh][1].T[0:1, :]

    def combine(g, jc, h):
        rows, b_col, _, p_col = cols(jc, h)
        m = before[g]
        mm = jnp.maximum(m, p_col)
        iw = jnp.exp(m - mm)
        fl = jnp.exp(p_col - mm)
        pv = pv_scr[g]
        qc = qc_scr[g]
        num = iw * qc[:, :dh] + fl * pv[:, :dh]
        den = iw * qc[:, dh:] + fl * pv[:, dh:]
        hh = num / jnp.maximum(jnp.abs(den), jnp.exp(-(b_col + mm)))
        h2 = hh * hh
        hi = h2.astype(BF16)
        lo = (h2 - hi.astype(F32)).astype(BF16)
        ms = (_dot(hi, ones_dh) + _dot(lo, ones_dh)) * (1.0 / dh)
        hn = (hh * lax.rsqrt(ms + EPS) * go_ref[h:h + 1, :]) * jax.nn.sigmoid(qkv(jc, h, 3))
        ha_ref[0, rows, h * dh:(h + 1) * dh] = hn.astype(BF16)

    return [[functools.partial(fn, g, jc, h) for g, (jc, h) in enumerate(groups)]
            for fn in (score, local, contrib, carry, combine)]


def _run_passes(*pass_lists):
    for steps in zip(*[p + [None] * (max(map(len, pass_lists)) - len(p)) for p in pass_lists]):
        for step in steps:
            if step is not None:
                step()


def _mlstm_call(ua, gates, c0, n0rep, m0rep, gout, *, tq, seg):
    nb, length, aw4 = ua.shape
    _, nh, dh, _ = c0.shape
    assert dh == LANES
    groups = (tq // seg) * nh
    st = lambda shape: pl.BlockSpec((1,) + shape, lambda b, t: (b,) + (0,) * len(shape))
    tile = lambda n: pl.BlockSpec((1, tq, n), lambda b, t: (b, t, 0))
    return pl.pallas_call(
        functools.partial(_mlstm_kernel, seg=seg, nh=nh, dh=dh),
        grid=(nb, length // tq),
        in_specs=[tile(aw4), tile(LANES), st((nh, dh, dh)), st((nh, dh, LANES)), st((nh, LANES)),
                  _const_spec(gout.shape)],
        out_specs=[tile(nh * dh), st((nh, dh, dh)), st((nh, dh)), st((nh, LANES))],
        out_shape=[jax.ShapeDtypeStruct((nb, length, nh * dh), BF16),
                   jax.ShapeDtypeStruct((nb, nh, dh, dh), F32),
                   jax.ShapeDtypeStruct((nb, nh, dh), F32),
                   jax.ShapeDtypeStruct((nb, nh, LANES), F32)],
        scratch_shapes=[pltpu.VMEM((nh, dh, LANES), F32), pltpu.VMEM((3 * nh, tq, LANES), F32),
                        pltpu.VMEM((groups, seg, seg), F32), pltpu.VMEM((groups, seg, dh + LANES), F32),
                        pltpu.VMEM((groups, dh, dh + LANES), F32), pltpu.VMEM((groups, seg, dh + LANES), F32)],
        compiler_params=_cparams(("arbitrary", "arbitrary"), 32),
        name="mlstm",
    )(ua, gates, c0, n0rep, m0rep, gout)


def _relbias_kernel(b0_ref, o_ref):
    nhb, nq, nk = o_ref.shape
    for h in range(nhb):
        x = jnp.broadcast_to(b0_ref[h:h + 1, :], (nq, b0_ref.shape[1]))
        o_ref[h] = pltpu.roll(x, 0, 1, stride=1, stride_axis=0)[:, :nk]


def _relbias_call(table, w):
    nhb = table.shape[0]
    max_rel = (table.shape[1] - 1) // 2
    assert CHUNK - 1 <= max_rel <= w
    first = jnp.broadcast_to(table[:, :1], (nhb, w - max_rel))
    wrap = jnp.broadcast_to(table[:, :1], (nhb, CHUNK))
    b0 = jnp.concatenate([first, table[:, :max_rel + CHUNK], wrap], axis=1).astype(F32)
    return pl.pallas_call(
        _relbias_kernel,
        out_shape=jax.ShapeDtypeStruct((nhb, CHUNK, w + CHUNK), F32),
        name="rel_bias",
    )(b0)


def _band_tile(q_ref, k_ref, v_ref, bias_ref, o_ref, s_scr, m_scr, e_scr, *, masked, npair, w, nck):
    c4 = pl.program_id(1)
    nk = w + CHUNK
    lane = lax.broadcasted_iota(jnp.int32, (CHUNK, LANES), 1)
    low = lane < LANES // 2
    zero = jnp.zeros((CHUNK, LANES), BF16)
    ones = jnp.ones((nk, LANES), BF16)

    starts = [pl.multiple_of((c4 * nck + jc) * CHUNK, CHUNK) for jc in range(nck)]
    groups = [(jc, p) for jc in range(nck) for p in range(npair)]

    def score(g, jc, p):
        sl = slice(p * LANES, (p + 1) * LANES)
        qp = q_ref[0, jc * CHUNK:(jc + 1) * CHUNK, sl]
        q2 = jnp.concatenate([jnp.where(low, qp, zero), jnp.where(low, zero, qp)], axis=0)
        s = _dot_nt(q2, k_ref[0, pl.ds(starts[jc], nk), sl]) + bias_ref[p]
        if masked:
            col = lax.broadcasted_iota(jnp.int32, s.shape, 1)
            s = jnp.where(col + starts[jc] >= w, s, -jnp.inf)
        s_scr[g] = s
        m_scr[g] = jnp.max(s, axis=-1, keepdims=True)

    def expo(g, jc, p):
        e_scr[g] = jnp.exp(s_scr[g] - m_scr[g]).astype(BF16)

    def value(g, jc, p):
        sl = slice(p * LANES, (p + 1) * LANES)
        vx = jnp.concatenate([v_ref[0, pl.ds(starts[jc], nk), sl], ones], axis=1)
        r = _dot(e_scr[g], vx)
        o_lo = r[:CHUNK, :LANES] / r[:CHUNK, LANES:]
        o_hi = r[CHUNK:, :LANES] / r[CHUNK:, LANES:]
        o_ref[0, jc * CHUNK:(jc + 1) * CHUNK, sl] = jnp.where(low, o_lo, o_hi).astype(BF16)

    return [[functools.partial(fn, g, jc, p) for g, (jc, p) in enumerate(groups)] for fn in (score, expo, value)]


def _mixer_prompt_kernel(ua_ref, g_ref, c0_ref, n0_ref, m0_ref, go_ref, q_ref, k_ref, v_ref, bias_ref,
                         ha_ref, c_ref, n_ref, m_ref, hb_ref,
                         nrep_scr, rep_scr, sa_scr, pv_scr, kv_scr, qc_scr, sb_scr, mb_scr, eb_scr,
                         *, seg, nh, dh, npair, w, nck):
    _mlstm_init(c0_ref, n0_ref, m0_ref, c_ref, m_ref, nrep_scr)
    first_full = w // (CHUNK * nck)

    def tile(masked):
        b_score, b_exp, b_value = _band_tile(q_ref, k_ref, v_ref, bias_ref, hb_ref, sb_scr, mb_scr, eb_scr,
                                             masked=masked, npair=npair, w=w, nck=nck)
        a_score, a_local, a_contrib, a_carry, a_combine = _mlstm_tile(
            ua_ref, g_ref, go_ref, ha_ref, c_ref, n_ref, m_ref,
            nrep_scr, rep_scr, sa_scr, pv_scr, kv_scr, qc_scr, seg=seg, nh=nh, dh=dh)
        _run_passes(a_score, b_score)
        _run_passes(a_local, b_exp)
        _run_passes(a_contrib)
        _run_passes(a_carry)
        _run_passes(a_combine, b_value)

    @pl.when(pl.program_id(1) < first_full)
    def _():
        tile(True)

    @pl.when(pl.program_id(1) >= first_full)
    def _():
        tile(False)


def _mixer_prompt_call(ua, gates, c0, n0rep, m0rep, gout, qs, kpad, vpad, bias2, *, w, nck):
    nb, length, aw4 = ua.shape
    _, nh, dh, _ = c0.shape
    bw = qs.shape[2]
    npair = bias2.shape[0]
    lp = kpad.shape[1]
    tq = nck * CHUNK
    nk = w + CHUNK
    ga = nck * nh
    gb = nck * npair
    assert w % tq == 0 and dh == LANES
    st = lambda shape: pl.BlockSpec((1,) + shape, lambda b, t: (b,) + (0,) * len(shape))
    tile = lambda n: pl.BlockSpec((1, tq, n), lambda b, t: (b, t, 0))
    whole = pl.BlockSpec((1, lp, bw), lambda b, t: (b, 0, 0))
    return pl.pallas_call(
        functools.partial(_mixer_prompt_kernel, seg=CHUNK, nh=nh, dh=dh, npair=npair, w=w, nck=nck),
        grid=(nb, length // tq),
        in_specs=[tile(aw4), tile(LANES), st((nh, dh, dh)), st((nh, dh, LANES)), st((nh, LANES)),
                  _const_spec(gout.shape), tile(bw), whole, whole, _const_spec(bias2.shape)],
        out_specs=[tile(nh * dh), st((nh, dh, dh)), st((nh, dh)), st((nh, LANES)), tile(bw)],
        out_shape=[jax.ShapeDtypeStruct((nb, length, nh * dh), BF16),
                   jax.ShapeDtypeStruct((nb, nh, dh, dh), F32),
                   jax.ShapeDtypeStruct((nb, nh, dh), F32),
                   jax.ShapeDtypeStruct((nb, nh, LANES), F32),
                   jax.ShapeDtypeStruct((nb, length, bw), BF16)],
        scratch_shapes=[pltpu.VMEM((nh, dh, LANES), F32), pltpu.VMEM((3 * nh, tq, LANES), F32),
                        pltpu.VMEM((ga, CHUNK, CHUNK), F32), pltpu.VMEM((ga, CHUNK, dh + LANES), F32),
                        pltpu.VMEM((ga, dh, dh + LANES), F32), pltpu.VMEM((ga, CHUNK, dh + LANES), F32),
                        pltpu.VMEM((gb, 2 * CHUNK, nk), F32), pltpu.VMEM((gb, 2 * CHUNK, 1), F32),
                        pltpu.VMEM((gb, 2 * CHUNK, nk), BF16)],
        compiler_params=_cparams(("arbitrary", "arbitrary"), 48),
        name="mixer_prompt",
    )(ua, gates, c0, n0rep, m0rep, gout, qs, kpad, vpad, bias2)


def _band_sample_kernel(q_ref, kn_ref, vn_ref, ck_ref, cv_ref, bias_ref, o_ref, *, npair):
    tq = q_ref.shape[1]
    nk = ck_ref.shape[1] + tq
    lane = lax.broadcasted_iota(jnp.int32, (tq, LANES), 1)
    low = lane < LANES // 2
    zero = jnp.zeros((tq, LANES), BF16)
    ones = jnp.ones((nk, LANES), BF16)
    scores = []
    for p in range(npair):
        sl = slice(p * LANES, (p + 1) * LANES)
        qp = q_ref[0, :, sl]
        q2 = jnp.concatenate([jnp.where(low, qp, zero), jnp.where(low, zero, qp)], axis=0)
        kx = jnp.concatenate([ck_ref[0, :, sl].astype(BF16), kn_ref[0, :, sl].astype(BF16)], axis=0)
        scores.append(_dot_nt(q2, kx) + bias_ref[p])
    probs = [jnp.exp(s - jnp.max(s, axis=-1, keepdims=True)).astype(BF16) for s in scores]
    for p in range(npair):
        sl = slice(p * LANES, (p + 1) * LANES)
        vx = jnp.concatenate([cv_ref[0, :, sl].astype(BF16), vn_ref[0, :, sl].astype(BF16)], axis=0)
        r = _dot(probs[p], jnp.concatenate([vx, ones], axis=1))
        o_lo = r[:tq, :LANES] / r[:tq, LANES:]
        o_hi = r[tq:, :LANES] / r[tq:, LANES:]
        o_ref[0, :, sl] = jnp.where(low, o_lo, o_hi).astype(BF16)


def _band_sample_call(qn, kn, vn, ck, cv, bias2):
    nb, tq, bw = qn.shape
    w = ck.shape[1]
    npair = bias2.shape[0]
    new = pl.BlockSpec((1, tq, bw), lambda b: (b, 0, 0))
    cache = pl.BlockSpec((1, w, bw), lambda b: (b, 0, 0))
    return pl.pallas_call(
        functools.partial(_band_sample_kernel, npair=npair),
        grid=(nb,),
        in_specs=[new, new, new, cache, cache, _const_spec(bias2.shape)],
        out_specs=new,
        out_shape=jax.ShapeDtypeStruct((nb, tq, bw), BF16),
        compiler_params=_cparams(("arbitrary",), 32),
        name="band_sample",
    )(qn, kn, vn, ck, cv, bias2)


def _mixout_kernel(x_ref, ha_ref, hb_ref, gt_ref, woa_ref, wob_ref, o_ref):
    bb, tl, d = x_ref.shape
    ha = ha_ref[...].reshape(bb * tl, -1)
    hb = hb_ref[...].reshape(bb * tl, -1)
    y = _dot(ha, woa_ref[...]) + _dot(hb, wob_ref[...])
    o_ref[...] = x_ref[...] + gt_ref[...] * y.reshape(bb, tl, d)


def _mixout_call(x, ha, hb, gt, woa, wob, bb, tl):
    nb, length, d = x.shape
    tile = lambda n: pl.BlockSpec((bb, tl, n), lambda i, t: (i, t, 0))
    return pl.pallas_call(
        _mixout_kernel,
        grid=(nb // bb, length // tl),
        in_specs=[tile(d), tile(ha.shape[-1]), tile(hb.shape[-1]),
                  pl.BlockSpec((bb, 1, d), lambda i, t: (i, 0, 0)),
                  _const_spec(woa.shape), _const_spec(wob.shape)],
        out_specs=tile(d),
        out_shape=jax.ShapeDtypeStruct(x.shape, F32),
        compiler_params=_cparams(("arbitrary", "arbitrary"), 32),
        name="mix_out",
    )(x, ha, hb, gt, woa, wob)


def _rglru_gates(xc, gw_ref, rb, ib, lam, nblk):
    bwc = xc.shape[1] // nblk
    r_parts, i_parts = [], []
    for n in range(nblk):
        gn = _dot(xc[:, n * bwc:(n + 1) * bwc].astype(BF16), gw_ref[n])
        r_parts.append(gn[:, :bwc])
        i_parts.append(gn[:, bwc:])
    r = jax.nn.sigmoid(jnp.concatenate(r_parts, axis=1) + rb)
    ii = jax.nn.sigmoid(jnp.concatenate(i_parts, axis=1) + ib)
    log_a = r * (-LRU_C * _softplus(-lam))
    a = jnp.exp(log_a)
    th = jnp.tanh(log_a)
    v = -2.0 * th / (1.0 - th)
    root = jnp.where(v > 0.0, v * lax.rsqrt(v), 0.0)
    return a, root * (ii * xc)


def _rglru_prompt_kernel(x_ref, sh_ref, sc_ref, gt_ref, g_ref, win_ref, cw_ref, cb_ref, gw_ref, rb_ref, ib_ref,
                         lam_ref, wout_ref, conv0_ref, h0_ref, o_ref, conv_ref, hl_ref, xp_scr, a_scr, b_scr, *, nblk):
    t = pl.program_id(1)
    tq, d = x_ref.shape[1], x_ref.shape[2]
    r_w = lam_ref.shape[1]
    ncw = cw_ref.shape[0]

    @pl.when(t == 0)
    def _():
        xp_scr[0:SUBLANES, :] = conv0_ref[0]
        hl_ref[...] = h0_ref[...]

    x = x_ref[0]
    hm = _rms_mod(x, g_ref[...], sh_ref[0], sc_ref[0]).astype(BF16)
    u = _dot(hm, win_ref[...])
    gb = u[:, :r_w]
    xp_scr[SUBLANES:SUBLANES + tq, :] = u[:, r_w:]
    xc = cb_ref[...]
    for j in range(ncw):
        off = SUBLANES - (ncw - 1 - j)
        xc = xc + xp_scr[off:off + tq, :] * cw_ref[j:j + 1, :]
    conv_ref[0] = xp_scr[tq:tq + SUBLANES, :]
    xp_scr[0:SUBLANES, :] = xp_scr[tq:tq + SUBLANES, :]
    a, upd = _rglru_gates(xc, gw_ref, rb_ref[...], ib_ref[...], lam_ref[...], nblk)
    a_scr[...] = a
    b_scr[...] = upd
    row8 = lax.broadcasted_iota(jnp.int32, (SUBLANES, r_w), 0)

    def scan_body(i, h):
        rows = pl.ds(pl.multiple_of(i * SUBLANES, SUBLANES), SUBLANES)
        ai = a_scr[rows, :]
        bi = b_scr[rows, :]
        s = 1
        while s < SUBLANES:
            m = row8 >= s
            bi = jnp.where(m, ai * pltpu.roll(bi, s, 0) + bi, bi)
            ai = jnp.where(m, ai * pltpu.roll(ai, s, 0), ai)
            s *= 2
        hs = ai * h + bi
        a_scr[rows, :] = hs
        return hs[SUBLANES - 1:SUBLANES, :]

    h_fin = lax.fori_loop(0, tq // SUBLANES, scan_body, hl_ref[0])
    hl_ref[0] = h_fin
    y = _dot((_gelu_tanh(gb) * a_scr[...]).astype(BF16), wout_ref[...])
    o_ref[0] = x + gt_ref[0] * y


def _rglru_prompt_call(x, sh, sc, gt, g, win, cw, cb, gw, rb, ib, lam, wout, conv0, h0, *, tq):
    nb, length, d = x.shape
    r_w = lam.shape[1]
    nblk = gw.shape[0]
    ada_spec = pl.BlockSpec((1, 1, d), lambda b, t: (b, 0, 0))
    tile = pl.BlockSpec((1, tq, d), lambda b, t: (b, t, 0))
    conv_spec = pl.BlockSpec((1, SUBLANES, r_w), lambda b, t: (b, 0, 0))
    h_spec = pl.BlockSpec((1, 1, r_w), lambda b, t: (b, 0, 0))
    consts = [g, win, cw, cb, gw, rb, ib, lam, wout]
    return pl.pallas_call(
        functools.partial(_rglru_prompt_kernel, nblk=nblk),
        grid=(nb, length // tq),
        in_specs=[tile, ada_spec, ada_spec, ada_spec] + [_const_spec(a.shape) for a in consts] + [conv_spec, h_spec],
        out_specs=[tile, conv_spec, h_spec],
        out_shape=[jax.ShapeDtypeStruct(x.shape, F32),
                   jax.ShapeDtypeStruct((nb, SUBLANES, r_w), F32),
                   jax.ShapeDtypeStruct((nb, 1, r_w), F32)],
        scratch_shapes=[pltpu.VMEM((tq + SUBLANES, r_w), F32), pltpu.VMEM((tq, r_w), F32), pltpu.VMEM((tq, r_w), F32)],
        compiler_params=_cparams(("arbitrary", "arbitrary"), 48),
        name="rglru_prompt",
    )(x, sh, sc, gt, *consts, conv0, h0)


def _rglru_sample_kernel(x_ref, sh_ref, sc_ref, gt_ref, g_ref, win_ref, cw_ref, cb_ref, gw_ref, rb_ref, ib_ref,
                         lam_ref, wout_ref, conv0_ref, h0_ref, o_ref, conv_ref, hl_ref, xp_scr, *, nblk):
    bb, tl, d = x_ref.shape
    tm = bb * tl
    r_w = lam_ref.shape[1]
    ncw = cw_ref.shape[0]
    x = x_ref[...]
    hm = _rms_mod(x, g_ref[...], sh_ref[...], sc_ref[...]).reshape(tm, d).astype(BF16)
    u = _dot(hm, win_ref[...])
    gb = u[:, :r_w]
    xp_scr[:, 0:SUBLANES, :] = conv0_ref[...]
    xp_scr[:, SUBLANES:SUBLANES + tl, :] = u[:, r_w:].reshape(bb, tl, r_w)
    xc = jnp.broadcast_to(cb_ref[...], (bb, tl, r_w))
    for j in range(ncw):
        off = SUBLANES - (ncw - 1 - j)
        xc = xc + xp_scr[:, off:off + tl, :] * cw_ref[j:j + 1, :]
    conv_ref[...] = xp_scr[:, tl:tl + SUBLANES, :]
    a, b = _rglru_gates(xc.reshape(tm, r_w), gw_ref, rb_ref[...], ib_ref[...], lam_ref[...], nblk)
    pos = lax.broadcasted_iota(jnp.int32, (tm, r_w), 0) % tl
    s = 1
    while s < tl:
        m = pos >= s
        b = jnp.where(m, a * pltpu.roll(b, s, 0) + b, b)
        a = jnp.where(m, a * pltpu.roll(a, s, 0), a)
        s *= 2
    hs = a.reshape(bb, tl, r_w) * h0_ref[...] + b.reshape(bb, tl, r_w)
    hl_ref[...] = hs[:, tl - 1:tl, :]
    y = _dot((_gelu_tanh(gb) * hs.reshape(tm, r_w)).astype(BF16), wout_ref[...])
    o_ref[...] = x + gt_ref[...] * y.reshape(bb, tl, d)


def _rglru_sample_call(x, sh, sc, gt, g, win, cw, cb, gw, rb, ib, lam, wout, conv0, h0):
    nb, tl, d = x.shape
    r_w = lam.shape[1]
    nblk = gw.shape[0]
    full = lambda a, b: pl.BlockSpec((nb, a, b), lambda i: (0, 0, 0))
    consts = [g, win, cw, cb, gw, rb, ib, lam, wout]
    return pl.pallas_call(
        functools.partial(_rglru_sample_kernel, nblk=nblk),
        grid=(1,),
        in_specs=[full(tl, d), full(1, d), full(1, d), full(1, d)] + [_const_spec(a.shape) for a in consts]
                 + [full(SUBLANES, r_w), full(1, r_w)],
        out_specs=[full(tl, d), full(SUBLANES, r_w), full(1, r_w)],
        out_shape=[jax.ShapeDtypeStruct(x.shape, F32),
                   jax.ShapeDtypeStruct((nb, SUBLANES, r_w), F32),
                   jax.ShapeDtypeStruct((nb, 1, r_w), F32)],
        scratch_shapes=[pltpu.VMEM((nb, tl + SUBLANES, r_w), F32)],
        compiler_params=_cparams(("arbitrary",), 48),
        name="rglru_sample",
    )(x, sh, sc, gt, *consts, conv0, h0)


def _pad_rows_front(a, rows):
    return jnp.pad(a, ((0, 0), (rows - a.shape[1], 0), (0, 0)))


def kernel(x_prompt, x_sample, state_a_C, state_a_n, state_a_m, cache_b_k, cache_b_v, state_c_conv, state_c_h,
           c_prompt, c_sample, ffn1_norm, ffn1_w_in, ffn1_w_out, mix_norm, ffn2_norm, ffn2_w_in, ffn2_w_out,
           ada_w, ada_b, ab_w_in, ab_gate_bias, a_out_norm, b_q_norm, b_k_norm, b_rel_bias, ab_w_out,
           c_w_in, c_conv_w, c_conv_b, c_gate_w, c_gate_b, c_lambda, c_w_out):
    nbp, seq, d = x_prompt.shape
    nbs, tdec, _ = x_sample.shape
    depth = ada_w.shape[0]
    n_ada = ada_w.shape[2] // d
    _, _, nh, dh, _ = state_a_C.shape
    _, _, w_band, nhb, dhb = cache_b_k.shape
    aw, bw = nh * dh, nhb * dhb
    ncw = c_conv_w.shape[1]
    assert 2 * dhb == LANES and dh == LANES and w_band % CHUNK == 0 and seq % w_band == 0

    ada = _ada_call(jnp.concatenate([c_prompt, c_sample], axis=0), ada_w, ada_b)
    ada = ada.reshape(depth, nbp + nbs, n_ada, 1, d)
    ada_p = [[ada[l, :nbp, k] for k in range(n_ada)] for l in range(depth)]
    ada_s = [[ada[l, nbp:, k] for k in range(n_ada)] for l in range(depth)]

    tl_p = 512
    xp, xs = x_prompt, x_sample
    outs_p, outs_s = {}, {}
    for l in range(depth):
        ap, as_ = ada_p[l], ada_s[l]
        i = l // 2
        g1 = ffn1_norm[l].reshape(1, d)
        gm = mix_norm[l].reshape(1, d)
        g2 = ffn2_norm[l].reshape(1, d)
        mix_p = None
        xp, xs = _ffn_call(xp, ap[0:3], xs, as_[0:3], g1, ffn1_w_in, ffn1_w_out, l, tl_p)
        if l % 2 == 0:
            w_in = ab_w_in[i]
            wab = jnp.concatenate(
                [w_in[:, :4 * aw], w_in[:, 4 * aw + 2 * nh:], w_in[:, 4 * aw:4 * aw + 2 * nh],
                 jnp.zeros((d, LANES - 2 * nh), F32)], axis=1).astype(BF16)
            gbias = jnp.pad(ab_gate_bias[i], (0, LANES - 2 * nh)).reshape(1, LANES)
            qg = jnp.tile(b_q_norm[i], nhb).reshape(1, bw)
            kg = jnp.tile(b_k_norm[i], nhb).reshape(1, bw)
            head = jnp.arange(bw) // dhb
            e = (head[:, None] == head[None, :]).astype(BF16)
            woa = ab_w_out[i][:aw].astype(BF16)
            wob = ab_w_out[i][aw:].astype(BF16)
            gout = a_out_norm[i]
            bias = _relbias_call(b_rel_bias[i], w_band)
            bias2 = bias.reshape(nhb // 2, 2 * CHUNK, w_band + CHUNK)

            ua, gts, qn, kpad, vpad, klast, vlast = _proj_prompt_call(
                xp, ap[3], ap[4], gm, wab, gbias, qg, kg, e, nh=nh, dhb=dhb, w=w_band)
            zc = jnp.zeros((nbp, nh, dh, dh), F32)
            ha, c1, n1, m1, hb = _mixer_prompt_call(ua, gts, zc, zc, zc[:, :, 0], gout, qn, kpad, vpad, bias2,
                                                    w=w_band, nck=4)
            mix_p = (ha, hb, ap[5], woa, wob)
            outs_p.setdefault('a_C', []).append(c1)
            outs_p.setdefault('a_n', []).append(n1)
            outs_p.setdefault('a_m', []).append(m1[:, :, 0])
            outs_p.setdefault('b_k', []).append(klast.reshape(nbp, w_band, nhb, dhb))
            outs_p.setdefault('b_v', []).append(vlast.reshape(nbp, w_band, nhb, dhb))

            ua, gts, qn, kn, vn = _proj_sample_call(
                xs, as_[3], as_[4], gm, wab, gbias, qg, kg, e, nh=nh, dhb=dhb)
            ha, c1, n1, m1 = _mlstm_call(
                ua, gts, state_a_C[i],
                jnp.broadcast_to(state_a_n[i][..., None], (nbs, nh, dh, LANES)),
                jnp.broadcast_to(state_a_m[i][..., None], (nbs, nh, LANES)), gout, tq=tdec, seg=tdec)
            hb = _band_sample_call(qn, kn, vn, cache_b_k[i].reshape(nbs, w_band, bw),
                                   cache_b_v[i].reshape(nbs, w_band, bw),
                                   bias[:, :tdec, :w_band + tdec].reshape(nhb // 2, 2 * tdec, w_band + tdec))
            xs = _mixout_call(xs, ha, hb, as_[5], woa, wob, nbs, tdec)
            outs_s.setdefault('a_C', []).append(c1)
            outs_s.setdefault('a_n', []).append(n1)
            outs_s.setdefault('a_m', []).append(m1[:, :, 0])
            outs_s.setdefault('b_k', []).append(kn.reshape(nbs, tdec, nhb, dhb))
            outs_s.setdefault('b_v', []).append(vn.reshape(nbs, tdec, nhb, dhb))
        else:
            r_w = c_lambda.shape[1]
            consts = (gm, c_w_in[i].astype(BF16), c_conv_w[i], c_conv_b[i].reshape(1, r_w), c_gate_w[i].astype(BF16),
                      c_gate_b[i][0].reshape(1, r_w), c_gate_b[i][1].reshape(1, r_w), c_lambda[i].reshape(1, r_w),
                      c_w_out[i].astype(BF16))
            xp, conv_p, h_p = _rglru_prompt_call(
                xp, ap[3], ap[4], ap[5], *consts,
                jnp.zeros((nbp, SUBLANES, r_w), F32), jnp.zeros((nbp, 1, r_w), F32), tq=512)
            xs, conv_s, h_s = _rglru_sample_call(
                xs, as_[3], as_[4], as_[5], *consts,
                _pad_rows_front(state_c_conv[i], SUBLANES), state_c_h[i][:, None, :])
            outs_p.setdefault('c_conv', []).append(conv_p[:, SUBLANES - (ncw - 1):])
            outs_p.setdefault('c_h', []).append(h_p[:, 0])
            outs_s.setdefault('c_conv', []).append(conv_s[:, SUBLANES - (ncw - 1):])
            outs_s.setdefault('c_h', []).append(h_s[:, 0])
        xp, xs = _ffn_call(xp, ap[6:9], xs, as_[6:9], g2, ffn2_w_in, ffn2_w_out, l, tl_p, mix=mix_p)

    names = ('a_C', 'a_n', 'a_m', 'b_k', 'b_v', 'c_conv', 'c_h')
    ps = [jnp.stack(outs_p[n], axis=0) for n in names]
    ss = [jnp.stack(outs_s[n], axis=0) for n in names]
    return (xp, xs, *ps, *ss)
```

```python
import functools

import jax
import jax.numpy as jnp
from jax import lax
from jax.experimental import pallas as pl
from jax.experimental.pallas import tpu as pltpu

F32 = jnp.float32
BF16 = jnp.bfloat16

EPS = 1e-6
CHUNK = 64
LRU_C = 8.0
LANES = 128
SUBLANES = 8
MIB = 1024 * 1024


def _cparams(semantics, vmem_mib):
    return pltpu.CompilerParams(dimension_semantics=semantics, vmem_limit_bytes=vmem_mib * MIB)


def _const_spec(shape):
    nd = len(shape)
    return pl.BlockSpec(shape, lambda *_: (0,) * nd, pipeline_mode=pl.Buffered(1))


def _dot(a, b):
    return jnp.dot(a, b, preferred_element_type=F32)


def _dot_nt(a, b):
    return lax.dot_general(a, b, (((1,), (1,)), ((), ())), preferred_element_type=F32)


def _dot_tn(a, b):
    return lax.dot_general(a, b, (((0,), (0,)), ((), ())), preferred_element_type=F32)


def _rms_mod(x, g, shift, scale):
    ms = jnp.mean(x * x, axis=-1, keepdims=True)
    return (x * lax.rsqrt(ms + EPS)) * (g * (1.0 + scale)) + shift


def _softplus(x):
    return jnp.maximum(x, 0.0) + jnp.log1p(jnp.exp(-jnp.abs(x)))


def _gelu_tanh(x):
    c = 0.7978845608028654
    inner = x * ((x * x) * (c * 0.044715) + c)
    return x * (0.5 * jnp.tanh(inner) + 0.5)


def _ada_kernel(c_ref, w_ref, b_ref, o_ref):
    c = c_ref[...].astype(BF16)
    w = w_ref[0].astype(BF16)
    o_ref[0] = _dot(c, w) + b_ref[0]


def _ada_call(c_all, ada_w, ada_b):
    depth, d, n = ada_w.shape
    m = c_all.shape[0]
    tn = d
    return pl.pallas_call(
        _ada_kernel,
        grid=(depth, n // tn),
        in_specs=[pl.BlockSpec((m, d), lambda l, j: (0, 0)),
                  pl.BlockSpec((1, d, tn), lambda l, j: (l, 0, j)),
                  pl.BlockSpec((1, 1, tn), lambda l, j: (l, 0, j))],
        out_specs=pl.BlockSpec((1, m, tn), lambda l, j: (l, 0, j)),
        out_shape=jax.ShapeDtypeStruct((depth, m, n), F32),
        compiler_params=_cparams(("arbitrary", "arbitrary"), 32),
        name="ada_proj",
    )(c_all, ada_w, ada_b.reshape(depth, 1, n))


FFN_TF = 256


def _ffn_kernel(*refs, mixed, n_prompt):
    if mixed:
        ha_ref, hb_ref, gm_ref, woa_ref, wob_ref = refs[:5]
        refs = refs[5:]
    x_ref, sh_ref, sc_ref, gt_ref, xs_ref, adas_ref, g_ref, win_ref, wo_ref, o_ref, os_ref, act_scr = refs
    is_sample = pl.program_id(0) == n_prompt
    _, tl, d = x_ref.shape
    nbs = adas_ref.shape[1]
    dff = wo_ref.shape[0]

    def half_step(x, shift, scale, gate_vec):
        h = _rms_mod(x, g_ref[...], shift, scale).astype(BF16)
        for c0 in range(0, dff, FFN_TF):
            gate = _dot(h, win_ref[:, c0:c0 + FFN_TF].astype(BF16))
            up = _dot(h, win_ref[:, dff + c0:dff + c0 + FFN_TF].astype(BF16))
            act_scr[:, c0:c0 + FFN_TF] = ((gate * jax.nn.sigmoid(gate)) * up).astype(BF16)
        y = _dot(act_scr[...], wo_ref[...].astype(BF16))
        return x + (0.5 * gate_vec) * y

    @pl.when(jnp.logical_not(is_sample))
    def _():
        x = x_ref[0]
        if mixed:
            x = x + gm_ref[0] * (_dot(ha_ref[0], woa_ref[...]) + _dot(hb_ref[0], wob_ref[...]))
        o_ref[0] = half_step(x, sh_ref[0], sc_ref[0], gt_ref[0])

    @pl.when(is_sample)
    def _():
        def rows(k):
            return jnp.broadcast_to(adas_ref[k][:, None, :], (nbs, tl // nbs, d)).reshape(tl, d)
        os_ref[...] = half_step(xs_ref[...], rows(0), rows(1), rows(2))


def _ffn_call(xp, ada_p, xs, ada_s, g, w_in, w_out, layer, tl, mix=None):
    nb, length, d = xp.shape
    nbs, tdec, _ = xs.shape
    dff = w_out.shape[1]
    assert dff % FFN_TF == 0 and nbs * tdec == tl
    nt = length // tl
    n_prompt = nb * nt
    cur = lambda i: jnp.minimum(i, n_prompt - 1)
    tile = lambda n: pl.BlockSpec((1, tl, n), lambda i: (cur(i) // nt, cur(i) % nt, 0))
    ada_spec = pl.BlockSpec((1, 1, d), lambda i: (cur(i) // nt, 0, 0))
    once = lambda shape: pl.BlockSpec(shape, lambda i: (0,) * len(shape), pipeline_mode=pl.Buffered(1))
    layer_spec = lambda shape: pl.BlockSpec((None,) + shape, lambda i: (layer, 0, 0), pipeline_mode=pl.Buffered(1))
    mix_args, mix_specs = [], []
    if mix is not None:
        ha, hb, gm, woa, wob = mix
        mix_args = [ha, hb, gm, woa, wob]
        mix_specs = [tile(ha.shape[-1]), tile(hb.shape[-1]), ada_spec, once(woa.shape), once(wob.shape)]
    adas = jnp.stack([a[:, 0] for a in ada_s])
    op, os = pl.pallas_call(
        functools.partial(_ffn_kernel, mixed=mix is not None, n_prompt=n_prompt),
        grid=(n_prompt + 1,),
        in_specs=mix_specs + [tile(d), ada_spec, ada_spec, ada_spec, once((tl, d)), once(adas.shape), once((1, d)),
                              layer_spec(w_in.shape[1:]), layer_spec(w_out.shape[1:])],
        out_specs=[tile(d), pl.BlockSpec((tl, d), lambda i: (0, 0))],
        out_shape=[jax.ShapeDtypeStruct(xp.shape, F32), jax.ShapeDtypeStruct((tl, d), F32)],
        scratch_shapes=[pltpu.VMEM((tl, dff), BF16)],
        compiler_params=_cparams(("arbitrary",), 60),
        name="ffn",
    )(*mix_args, xp, *ada_p, xs.reshape(tl, d), adas, g, w_in, w_out)
    return op, os.reshape(nbs, tdec, d)


def _head_rmsnorm(q, e, g, dhb):
    ss = _dot((q * q).astype(BF16), e)
    return q * lax.rsqrt(ss * (1.0 / dhb) + EPS) * g


def _proj_body(x, sh, sc, g, w_ref, gb_ref, qg_ref, kg_ref, e_ref, *, nh, bw, dhb):
    bb, tl, d = x.shape
    na = w_ref.shape[1] - 3 * bw - LANES
    h = _rms_mod(x, g, sh, sc).reshape(bb * tl, d).astype(BF16)
    ua = _dot(h, w_ref[:, :na])
    gg = _dot(h, w_ref[:, na + 3 * bw:]) + gb_ref[...]
    lane = lax.broadcasted_iota(jnp.int32, gg.shape, 1)
    gates = jnp.where(lane < nh, gg, -_softplus(-gg))
    ub = _dot(h, w_ref[:, na:na + 3 * bw])
    e = e_ref[...]
    qn = _head_rmsnorm(ub[:, :bw], e, qg_ref[...], dhb) * (dhb ** -0.5)
    kn = _head_rmsnorm(ub[:, bw:2 * bw], e, kg_ref[...], dhb)
    vb = ub[:, 2 * bw:]
    return ua, gates, qn, kn, vb


N_PROJ_IN, N_PROJ_OUT, N_RIDER_IN, N_RIDER_OUT = 9, 7, 12, 5


def _proj_prompt_kernel(*refs, nh, bw, dhb, rider):
    x_ref, sh_ref, sc_ref, g_ref, w_ref, gb_ref, qg_ref, kg_ref, e_ref = refs[:N_PROJ_IN]
    refs = refs[N_PROJ_IN:]
    if rider:
        (uas_ref, gs_ref, c0_ref, n0_ref, m0_ref, go_ref, qs_ref, kns_ref, vns_ref, ck_ref, cv_ref,
         bs_ref) = refs[:N_RIDER_IN]
        refs = refs[N_RIDER_IN:]
    ua_ref, gt_ref, qn_ref, kp_ref, vp_ref, kl_ref, vl_ref = refs[:N_PROJ_OUT]
    refs = refs[N_PROJ_OUT:]
    if rider:
        has_ref, cs_ref, ns_ref, ms_ref, hbs_ref = refs[:N_RIDER_OUT]
        nrep_scr, rep_scr, s_scr, pv_scr, kv_scr, qc_scr = refs[N_RIDER_OUT:]
    t = pl.program_id(1)
    nt = pl.num_programs(1)

    @pl.when(t == 0)
    def _():
        kp_ref[...] = jnp.zeros_like(kp_ref)
        vp_ref[...] = jnp.zeros_like(vp_ref)

    @pl.when(t > 0)
    def _():
        ua, gates, qn, kn, vb = _proj_body(x_ref[...], sh_ref[...], sc_ref[...], g_ref[...], w_ref,
                                           gb_ref, qg_ref, kg_ref, e_ref, nh=nh, bw=bw, dhb=dhb)
        ua_ref[0] = ua
        gt_ref[0] = gates
        qn_ref[0] = qn.astype(BF16)
        kp_ref[0] = kn.astype(BF16)
        vp_ref[0] = vb.astype(BF16)
        if rider:
            cs_ref[...] = c0_ref[...]
            nrep_scr[...] = n0_ref[0]
            ms_ref[...] = m0_ref[...]
            for steps in _mlstm_tile(uas_ref, gs_ref, go_ref, has_ref, cs_ref, ns_ref, ms_ref, nrep_scr, rep_scr,
                                     s_scr, pv_scr, kv_scr, qc_scr, seg=uas_ref.shape[1], nh=nh, dh=go_ref.shape[1]):
                _run_passes(steps)
            _band_sample_kernel(qs_ref, kns_ref, vns_ref, ck_ref, cv_ref, bs_ref, hbs_ref, npair=bs_ref.shape[0])

        @pl.when(t == nt - 1)
        def _():
            kl_ref[0] = kn
            vl_ref[0] = vb


def _proj_prompt_call(x, sh, sc, g, wab, gbias, qg, kg, e, *, nh, dhb, w, rider=None):
    nb, length, d = x.shape
    tl = w
    nt = length // tl
    bw = e.shape[0]
    aw4 = wab.shape[1] - 3 * bw - LANES
    prev = lambda b, t: (b, jnp.maximum(t - 1, 0), 0)
    ada_spec = pl.BlockSpec((1, 1, d), lambda b, t: (b, 0, 0))
    in_specs = [pl.BlockSpec((1, tl, d), prev), ada_spec, ada_spec, _const_spec((1, d)),
                _const_spec(wab.shape), _const_spec(gbias.shape),
                _const_spec(qg.shape), _const_spec(kg.shape), _const_spec(e.shape)]
    out_specs = [pl.BlockSpec((1, tl, aw4), prev),
                 pl.BlockSpec((1, tl, LANES), prev),
                 pl.BlockSpec((1, tl, bw), prev),
                 pl.BlockSpec((1, tl, bw), lambda b, t: (b, t, 0)),
                 pl.BlockSpec((1, tl, bw), lambda b, t: (b, t, 0)),
                 pl.BlockSpec((1, tl, bw), lambda b, t: (b, 0, 0)),
                 pl.BlockSpec((1, tl, bw), lambda b, t: (b, 0, 0))]
    out_shape = [jax.ShapeDtypeStruct((nb, length, aw4), F32),
                 jax.ShapeDtypeStruct((nb, length, LANES), F32),
                 jax.ShapeDtypeStruct((nb, length, bw), BF16),
                 jax.ShapeDtypeStruct((nb, length + w, bw), BF16),
                 jax.ShapeDtypeStruct((nb, length + w, bw), BF16),
                 jax.ShapeDtypeStruct((nb, w, bw), F32),
                 jax.ShapeDtypeStruct((nb, w, bw), F32)]
    args = [x, sh, sc, g, wab, gbias, qg, kg, e]
    scratch = []
    if rider is not None:
        ua_s, g_s, c0, n0rep, m0rep, gout, q_s, kn_s, vn_s, ck, cv, bias2_s = rider
        nbs, tdec, _ = ua_s.shape
        _, nha, dh, _ = c0.shape
        assert nbs == nb * nt and nha == nh
        per = lambda a: pl.BlockSpec((1,) + a.shape[1:],
                                     lambda b, t: (b * nt + jnp.maximum(t - 1, 0),) + (0,) * (a.ndim - 1))
        streams = [ua_s, g_s, c0, n0rep, m0rep]
        in_specs += [per(a) for a in streams] + [_const_spec(gout.shape)]
        in_specs += [per(a) for a in (q_s, kn_s, vn_s, ck, cv)] + [_const_spec(bias2_s.shape)]
        args += streams + [gout, q_s, kn_s, vn_s, ck, cv, bias2_s]
        r_shapes = [jax.ShapeDtypeStruct((nbs, tdec, nh * dh), BF16), jax.ShapeDtypeStruct((nbs, nh, dh, dh), F32),
                    jax.ShapeDtypeStruct((nbs, nh, dh), F32), jax.ShapeDtypeStruct((nbs, nh, LANES), F32),
                    jax.ShapeDtypeStruct((nbs, tdec, bw), BF16)]
        out_shape += r_shapes
        out_specs += [per(a) for a in r_shapes]
        scratch = [pltpu.VMEM((nh, dh, LANES), F32), pltpu.VMEM((3 * nh, tdec, LANES), F32),
                   pltpu.VMEM((nh, tdec, tdec), F32), pltpu.VMEM((nh, tdec, dh + LANES), F32),
                   pltpu.VMEM((nh, dh, dh + LANES), F32), pltpu.VMEM((nh, tdec, dh + LANES), F32)]
    outs = pl.pallas_call(
        functools.partial(_proj_prompt_kernel, nh=nh, bw=bw, dhb=dhb, rider=rider is not None),
        grid=(nb, nt + 1),
        in_specs=in_specs,
        out_specs=out_specs,
        out_shape=out_shape,
        scratch_shapes=scratch,
        compiler_params=_cparams(("arbitrary", "arbitrary"), 48),
        name="proj_prompt",
    )(*args)
    return outs[:N_PROJ_OUT], outs[N_PROJ_OUT:]


def _proj_sample_kernel(x_ref, sh_ref, sc_ref, g_ref, w_ref, gb_ref, qg_ref, kg_ref, e_ref,
                        ua_ref, gt_ref, qn_ref, kn_ref, vb_ref, *, nh, bw, dhb):
    bb, tl, _ = x_ref.shape
    ua, gates, qn, kn, vb = _proj_body(x_ref[...], sh_ref[...], sc_ref[...], g_ref[...], w_ref,
                                       gb_ref, qg_ref, kg_ref, e_ref, nh=nh, bw=bw, dhb=dhb)
    ua_ref[...] = ua.reshape(bb, tl, -1)
    gt_ref[...] = gates.reshape(bb, tl, -1)
    qn_ref[...] = qn.reshape(bb, tl, -1).astype(BF16)
    kn_ref[...] = kn.reshape(bb, tl, -1)
    vb_ref[...] = vb.reshape(bb, tl, -1)


def _proj_sample_call(x, sh, sc, g, wab, gbias, qg, kg, e, *, nh, dhb):
    nb, length, d = x.shape
    bw = e.shape[0]
    aw4 = wab.shape[1] - 3 * bw - LANES
    full = lambda n: pl.BlockSpec((nb, length, n), lambda i: (0, 0, 0))
    ada_spec = pl.BlockSpec((nb, 1, d), lambda i: (0, 0, 0))
    return pl.pallas_call(
        functools.partial(_proj_sample_kernel, nh=nh, bw=bw, dhb=dhb),
        grid=(1,),
        in_specs=[full(d), ada_spec, ada_spec, _const_spec((1, d)),
                  _const_spec(wab.shape), _const_spec(gbias.shape),
                  _const_spec(qg.shape), _const_spec(kg.shape), _const_spec(e.shape)],
        out_specs=[full(aw4), full(LANES), full(bw), full(bw), full(bw)],
        out_shape=[jax.ShapeDtypeStruct((nb, length, aw4), F32),
                   jax.ShapeDtypeStruct((nb, length, LANES), F32),
                   jax.ShapeDtypeStruct((nb, length, bw), BF16),
                   jax.ShapeDtypeStruct((nb, length, bw), F32),
                   jax.ShapeDtypeStruct((nb, length, bw), F32)],
        compiler_params=_cparams(("arbitrary",), 48),
        name="proj_sample",
    )(x, sh, sc, g, wab, gbias, qg, kg, e)


def _mlstm_init(c0_ref, n0_ref, m0_ref, c_ref, m_ref, nrep_scr):
    @pl.when(pl.program_id(1) == 0)
    def _():
        c_ref[...] = c0_ref[...]
        nrep_scr[...] = n0_ref[0]
        m_ref[...] = m0_ref[...]


def _mlstm_kernel(ua_ref, g_ref, c0_ref, n0_ref, m0_ref, go_ref, ha_ref, c_ref, n_ref, m_ref,
                  nrep_scr, rep_scr, s_scr, pv_scr, kv_scr, qc_scr, *, seg, nh, dh):
    _mlstm_init(c0_ref, n0_ref, m0_ref, c_ref, m_ref, nrep_scr)
    for steps in _mlstm_tile(ua_ref, g_ref, go_ref, ha_ref, c_ref, n_ref, m_ref,
                             nrep_scr, rep_scr, s_scr, pv_scr, kv_scr, qc_scr, seg=seg, nh=nh, dh=dh):
        _run_passes(steps)


def _mlstm_tile(ua_ref, g_ref, go_ref, ha_ref, c_ref, n_ref, m_ref,
                nrep_scr, rep_scr, s_scr, pv_scr, kv_scr, qc_scr, *, seg, nh, dh):
    tq = ua_ref.shape[1]
    nck = tq // seg
    aw = nh * dh
    gates = g_ref[0]
    pos = lax.broadcasted_iota(jnp.int32, gates.shape, 0) % seg
    bt = gates
    s = 1
    while s < seg:
        bt = bt + jnp.where(pos >= s, pltpu.roll(bt, s, 0), 0.0)
        s *= 2
    dmb = pltpu.roll(gates, nh, 1) - bt
    pm = dmb
    s = 1
    while s < seg:
        pm = jnp.maximum(pm, jnp.where(pos >= s, pltpu.roll(pm, s, 0), -jnp.inf))
        s *= 2
    if tq % LANES:
        dsq = jnp.concatenate([dmb, jnp.zeros((LANES - tq % LANES, LANES), F32)], axis=0)
    else:
        dsq = dmb
    dtr = dsq.T
    ri = lax.broadcasted_iota(jnp.int32, (seg, seg), 0)
    ci = lax.broadcasted_iota(jnp.int32, (seg, seg), 1)
    causal = ri >= ci
    ones = jnp.ones((seg, LANES), BF16)
    ones_dh = jnp.ones((dh, LANES), BF16)
    for h in range(nh):
        ln = slice(nh + h, nh + h + 1)
        for j, arr in enumerate((bt, dmb, pm)):
            rep_scr[3 * h + j] = jnp.broadcast_to(arr[:, ln], (tq, LANES))

    def cols(jc, h):
        rows = slice(jc * seg, (jc + 1) * seg)
        return rows, rep_scr[3 * h, rows, :], rep_scr[3 * h + 1, rows, :], rep_scr[3 * h + 2, rows, :]

    def last(jc, h, j):
        r = (jc + 1) * seg - 1
        return rep_scr[3 * h + j, r:r + 1, :]

    groups = [(jc, h) for jc in range(nck) for h in range(nh)]

    def qkv(jc, h, which):
        rows = slice(jc * seg, (jc + 1) * seg)
        return ua_ref[0, rows, which * aw + h * dh:which * aw + (h + 1) * dh]

    state = []
    before = []

    def score(g, jc, h):
        k = qkv(jc, h, 1) * (dh ** -0.5)
        s_scr[g] = _dot_nt(qkv(jc, h, 0).astype(BF16), k.astype(BF16))

    def local(g, jc, h):
        rows, _, _, p_col = cols(jc, h)
        d_row = dtr[nh + h:nh + h + 1, rows]
        dloc = jnp.exp(jnp.where(causal, d_row - p_col[:, :seg], -jnp.inf))
        sl = (s_scr[g] * dloc).astype(BF16)
        v = qkv(jc, h, 2).astype(BF16)
        pv_scr[g] = _dot(sl, jnp.concatenate([v, ones], axis=1))

    def contrib(g, jc, h):
        _, _, d_col, _ = cols(jc, h)
        kw = (qkv(jc, h, 1) * (dh ** -0.5)) * jnp.exp(d_col - last(jc, h, 2))
        vx = jnp.concatenate([qkv(jc, h, 2).astype(BF16), ones], axis=1)
        kv_scr[g] = _dot_tn(kw.astype(BF16), vx)

    def carry(g, jc, h):
        if not state:
            state.extend((c_ref[0, hh], nrep_scr[hh], m_ref[0, hh:hh + 1, :]) for hh in range(nh))
        c_mem, n_rep, m = state[h]
        cn = jnp.concatenate([c_mem.astype(BF16), n_rep.astype(BF16)], axis=1)
        qc_scr[g] = _dot(qkv(jc, h, 0).astype(BF16), cn)
        before.append(m)
        p_last = last(jc, h, 2)
        mml = jnp.maximum(m, p_last)
        w_prev = jnp.exp(m - mml)
        f_new = jnp.exp(p_last - mml)
        kvx = kv_scr[g]
        state[h] = (w_prev * c_mem + f_new * kvx[:, :dh],
                    w_prev * n_rep + f_new * kvx[:, dh:],
                    last(jc, h, 0) + mml)
        if g == len(groups) - 1:
            for hh in range(nh):
                c_ref[0, hh], nrep_scr[hh], m_ref[0, hh:hh + 1, :] = state[hh]
                n_ref[0, hh:hh + 1, :] = state[hh][1].T[0:1, :]

    def combine(g, jc, h):
        rows, b_col, _, p_col = cols(jc, h)
        m = before[g]
        mm = jnp.maximum(m, p_col)
        iw = jnp.exp(m - mm)
        fl = jnp.exp(p_col - mm)
        pv = pv_scr[g]
        qc = qc_scr[g]
        num = iw * qc[:, :dh] + fl * pv[:, :dh]
        den = iw * qc[:, dh:] + fl * pv[:, dh:]
        hh = num / jnp.maximum(jnp.abs(den), jnp.exp(-(b_col + mm)))
        h2 = hh * hh
        hi = h2.astype(BF16)
        lo = (h2 - hi.astype(F32)).astype(BF16)
        ms = (_dot(hi, ones_dh) + _dot(lo, ones_dh)) * (1.0 / dh)
        hn = (hh * lax.rsqrt(ms + EPS) * go_ref[h:h + 1, :]) * jax.nn.sigmoid(qkv(jc, h, 3))
        ha_ref[0, rows, h * dh:(h + 1) * dh] = hn.astype(BF16)

    return [[functools.partial(fn, g, jc, h) for g, (jc, h) in enumerate(groups)]
            for fn in (score, local, contrib, carry, combine)]


def _run_passes(*pass_lists):
    for steps in zip(*[p + [None] * (max(map(len, pass_lists)) - len(p)) for p in pass_lists]):
        for step in steps:
            if step is not None:
                step()


def _mlstm_call(ua, gates, c0, n0rep, m0rep, gout, *, tq, seg):
    nb, length, aw4 = ua.shape
    _, nh, dh, _ = c0.shape
    assert dh == LANES
    groups = (tq // seg) * nh
    st = lambda shape: pl.BlockSpec((1,) + shape, lambda b, t: (b,) + (0,) * len(shape))
    tile = lambda n: pl.BlockSpec((1, tq, n), lambda b, t: (b, t, 0))
    return pl.pallas_call(
        functools.partial(_mlstm_kernel, seg=seg, nh=nh, dh=dh),
        grid=(nb, length // tq),
        in_specs=[tile(aw4), tile(LANES), st((nh, dh, dh)), st((nh, dh, LANES)), st((nh, LANES)),
                  _const_spec(gout.shape)],
        out_specs=[tile(nh * dh), st((nh, dh, dh)), st((nh, dh)), st((nh, LANES))],
        out_shape=[jax.ShapeDtypeStruct((nb, length, nh * dh), BF16),
                   jax.ShapeDtypeStruct((nb, nh, dh, dh), F32),
                   jax.ShapeDtypeStruct((nb, nh, dh), F32),
                   jax.ShapeDtypeStruct((nb, nh, LANES), F32)],
        scratch_shapes=[pltpu.VMEM((nh, dh, LANES), F32), pltpu.VMEM((3 * nh, tq, LANES), F32),
                        pltpu.VMEM((groups, seg, seg), F32), pltpu.VMEM((groups, seg, dh + LANES), F32),
                        pltpu.VMEM((groups, dh, dh + LANES), F32), pltpu.VMEM((groups, seg, dh + LANES), F32)],
        compiler_params=_cparams(("arbitrary", "arbitrary"), 32),
        name="mlstm",
    )(ua, gates, c0, n0rep, m0rep, gout)


def _relbias_kernel(b0_ref, o_ref):
    nhb, nq, nk = o_ref.shape
    for h in range(nhb):
        x = jnp.broadcast_to(b0_ref[h:h + 1, :], (nq, b0_ref.shape[1]))
        o_ref[h] = pltpu.roll(x, 0, 1, stride=1, stride_axis=0)[:, :nk]


def _relbias_call(table, w):
    nhb = table.shape[0]
    max_rel = (table.shape[1] - 1) // 2
    assert CHUNK - 1 <= max_rel <= w
    first = jnp.broadcast_to(table[:, :1], (nhb, w - max_rel))
    wrap = jnp.broadcast_to(table[:, :1], (nhb, CHUNK))
    b0 = jnp.concatenate([first, table[:, :max_rel + CHUNK], wrap], axis=1).astype(F32)
    return pl.pallas_call(
        _relbias_kernel,
        out_shape=jax.ShapeDtypeStruct((nhb, CHUNK, w + CHUNK), F32),
        name="rel_bias",
    )(b0)


def _band_tile(q_ref, k_ref, v_ref, bias_ref, o_ref, s_scr, m_scr, e_scr, *, masked, npair, w, nck):
    c4 = pl.program_id(1)
    nk = w + CHUNK
    lane = lax.broadcasted_iota(jnp.int32, (CHUNK, LANES), 1)
    low = lane < LANES // 2
    zero = jnp.zeros((CHUNK, LANES), BF16)
    ones = jnp.ones((nk, LANES), BF16)

    starts = [pl.multiple_of((c4 * nck + jc) * CHUNK, CHUNK) for jc in range(nck)]
    groups = [(jc, p) for jc in range(nck) for p in range(npair)]

    def score(g, jc, p):
        sl = slice(p * LANES, (p + 1) * LANES)
        qp = q_ref[0, jc * CHUNK:(jc + 1) * CHUNK, sl]
        q2 = jnp.concatenate([jnp.where(low, qp, zero), jnp.where(low, zero, qp)], axis=0)
        s = _dot_nt(q2, k_ref[0, pl.ds(starts[jc], nk), sl]) + bias_ref[p]
        if masked:
            col = lax.broadcasted_iota(jnp.int32, s.shape, 1)
            s = jnp.where(col + starts[jc] >= w, s, -jnp.inf)
        s_scr[g] = s
        m_scr[g] = jnp.max(s, axis=-1, keepdims=True)

    def expo(g, jc, p):
        e_scr[g] = jnp.exp(s_scr[g] - m_scr[g]).astype(BF16)

    def value(g, jc, p):
        sl = slice(p * LANES, (p + 1) * LANES)
        vx = jnp.concatenate([v_ref[0, pl.ds(starts[jc], nk), sl], ones], axis=1)
        r = _dot(e_scr[g], vx)
        o_lo = r[:CHUNK, :LANES] / r[:CHUNK, LANES:]
        o_hi = r[CHUNK:, :LANES] / r[CHUNK:, LANES:]
        o_ref[0, jc * CHUNK:(jc + 1) * CHUNK, sl] = jnp.where(low, o_lo, o_hi).astype(BF16)

    return [[functools.partial(fn, g, jc, p) for g, (jc, p) in enumerate(groups)] for fn in (score, expo, value)]


def _mixer_prompt_kernel(ua_ref, g_ref, c0_ref, n0_ref, m0_ref, go_ref, q_ref, k_ref, v_ref, bias_ref,
                         ha_ref, c_ref, n_ref, m_ref, hb_ref,
                         nrep_scr, rep_scr, sa_scr, pv_scr, kv_scr, qc_scr, sb_scr, mb_scr, eb_scr,
                         *, seg, nh, dh, npair, w, nck):
    _mlstm_init(c0_ref, n0_ref, m0_ref, c_ref, m_ref, nrep_scr)
    first_full = w // (CHUNK * nck)

    def tile(masked):
        b_score, b_exp, b_value = _band_tile(q_ref, k_ref, v_ref, bias_ref, hb_ref, sb_scr, mb_scr, eb_scr,
                                             masked=masked, npair=npair, w=w, nck=nck)
        a_score, a_local, a_contrib, a_carry, a_combine = _mlstm_tile(
            ua_ref, g_ref, go_ref, ha_ref, c_ref, n_ref, m_ref,
            nrep_scr, rep_scr, sa_scr, pv_scr, kv_scr, qc_scr, seg=seg, nh=nh, dh=dh)
        _run_passes(a_score, b_score)
        _run_passes(a_local, b_exp)
        _run_passes(a_contrib)
        _run_passes(a_carry)
        _run_passes(a_combine, b_value)

    @pl.when(pl.program_id(1) < first_full)
    def _():
        tile(True)

    @pl.when(pl.program_id(1) >= first_full)
    def _():
        tile(False)


def _mixer_prompt_call(ua, gates, c0, n0rep, m0rep, gout, qs, kpad, vpad, bias2, *, w, nck):
    nb, length, aw4 = ua.shape
    _, nh, dh, _ = c0.shape
    bw = qs.shape[2]
    npair = bias2.shape[0]
    lp = kpad.shape[1]
    tq = nck * CHUNK
    nk = w + CHUNK
    ga = nck * nh
    gb = nck * npair
    assert w % tq == 0 and dh == LANES
    st = lambda shape: pl.BlockSpec((1,) + shape, lambda b, t: (b,) + (0,) * len(shape))
    tile = lambda n: pl.BlockSpec((1, tq, n), lambda b, t: (b, t, 0))
    whole = pl.BlockSpec((1, lp, bw), lambda b, t: (b, 0, 0))
    return pl.pallas_call(
        functools.partial(_mixer_prompt_kernel, seg=CHUNK, nh=nh, dh=dh, npair=npair, w=w, nck=nck),
        grid=(nb, length // tq),
        in_specs=[tile(aw4), tile(LANES), st((nh, dh, dh)), st((nh, dh, LANES)), st((nh, LANES)),
                  _const_spec(gout.shape), tile(bw), whole, whole, _const_spec(bias2.shape)],
        out_specs=[tile(nh * dh), st((nh, dh, dh)), st((nh, dh)), st((nh, LANES)), tile(bw)],
        out_shape=[jax.ShapeDtypeStruct((nb, length, nh * dh), BF16),
                   jax.ShapeDtypeStruct((nb, nh, dh, dh), F32),
                   jax.ShapeDtypeStruct((nb, nh, dh), F32),
                   jax.ShapeDtypeStruct((nb, nh, LANES), F32),
                   jax.ShapeDtypeStruct((nb, length, bw), BF16)],
        scratch_shapes=[pltpu.VMEM((nh, dh, LANES), F32), pltpu.VMEM((3 * nh, tq, LANES), F32),
                        pltpu.VMEM((ga, CHUNK, CHUNK), F32), pltpu.VMEM((ga, CHUNK, dh + LANES), F32),
                        pltpu.VMEM((ga, dh, dh + LANES), F32), pltpu.VMEM((ga, CHUNK, dh + LANES), F32),
                        pltpu.VMEM((gb, 2 * CHUNK, nk), F32), pltpu.VMEM((gb, 2 * CHUNK, 1), F32),
                        pltpu.VMEM((gb, 2 * CHUNK, nk), BF16)],
        compiler_params=_cparams(("arbitrary", "arbitrary"), 48),
        name="mixer_prompt",
    )(ua, gates, c0, n0rep, m0rep, gout, qs, kpad, vpad, bias2)


def _band_sample_kernel(q_ref, kn_ref, vn_ref, ck_ref, cv_ref, bias_ref, o_ref, *, npair):
    tq = q_ref.shape[1]
    nk = ck_ref.shape[1] + tq
    lane = lax.broadcasted_iota(jnp.int32, (tq, LANES), 1)
    low = lane < LANES // 2
    zero = jnp.zeros((tq, LANES), BF16)
    ones = jnp.ones((nk, LANES), BF16)
    scores = []
    for p in range(npair):
        sl = slice(p * LANES, (p + 1) * LANES)
        qp = q_ref[0, :, sl]
        q2 = jnp.concatenate([jnp.where(low, qp, zero), jnp.where(low, zero, qp)], axis=0)
        kx = jnp.concatenate([ck_ref[0, :, sl].astype(BF16), kn_ref[0, :, sl].astype(BF16)], axis=0)
        scores.append(_dot_nt(q2, kx) + bias_ref[p])
    probs = [jnp.exp(s - jnp.max(s, axis=-1, keepdims=True)).astype(BF16) for s in scores]
    for p in range(npair):
        sl = slice(p * LANES, (p + 1) * LANES)
        vx = jnp.concatenate([cv_ref[0, :, sl].astype(BF16), vn_ref[0, :, sl].astype(BF16)], axis=0)
        r = _dot(probs[p], jnp.concatenate([vx, ones], axis=1))
        o_lo = r[:tq, :LANES] / r[:tq, LANES:]
        o_hi = r[tq:, :LANES] / r[tq:, LANES:]
        o_ref[0, :, sl] = jnp.where(low, o_lo, o_hi).astype(BF16)


def _band_sample_call(qn, kn, vn, ck, cv, bias2):
    nb, tq, bw = qn.shape
    w = ck.shape[1]
    npair = bias2.shape[0]
    new = pl.BlockSpec((1, tq, bw), lambda b: (b, 0, 0))
    cache = pl.BlockSpec((1, w, bw), lambda b: (b, 0, 0))
    return pl.pallas_call(
        functools.partial(_band_sample_kernel, npair=npair),
        grid=(nb,),
        in_specs=[new, new, new, cache, cache, _const_spec(bias2.shape)],
        out_specs=new,
        out_shape=jax.ShapeDtypeStruct((nb, tq, bw), BF16),
        compiler_params=_cparams(("arbitrary",), 32),
        name="band_sample",
    )(qn, kn, vn, ck, cv, bias2)


def _mixout_kernel(x_ref, ha_ref, hb_ref, gt_ref, woa_ref, wob_ref, o_ref):
    bb, tl, d = x_ref.shape
    ha = ha_ref[...].reshape(bb * tl, -1)
    hb = hb_ref[...].reshape(bb * tl, -1)
    y = _dot(ha, woa_ref[...]) + _dot(hb, wob_ref[...])
    o_ref[...] = x_ref[...] + gt_ref[...] * y.reshape(bb, tl, d)


def _mixout_call(x, ha, hb, gt, woa, wob, bb, tl):
    nb, length, d = x.shape
    tile = lambda n: pl.BlockSpec((bb, tl, n), lambda i, t: (i, t, 0))
    return pl.pallas_call(
        _mixout_kernel,
        grid=(nb // bb, length // tl),
        in_specs=[tile(d), tile(ha.shape[-1]), tile(hb.shape[-1]),
                  pl.BlockSpec((bb, 1, d), lambda i, t: (i, 0, 0)),
                  _const_spec(woa.shape), _const_spec(wob.shape)],
        out_specs=tile(d),
        out_shape=jax.ShapeDtypeStruct(x.shape, F32),
        compiler_params=_cparams(("arbitrary", "arbitrary"), 32),
        name="mix_out",
    )(x, ha, hb, gt, woa, wob)


def _rglru_gates(xc, gw_ref, rb, ib, lam, nblk):
    bwc = xc.shape[1] // nblk
    r_parts, i_parts = [], []
    for n in range(nblk):
        gn = _dot(xc[:, n * bwc:(n + 1) * bwc].astype(BF16), gw_ref[n])
        r_parts.append(gn[:, :bwc])
        i_parts.append(gn[:, bwc:])
    r = jax.nn.sigmoid(jnp.concatenate(r_parts, axis=1) + rb)
    ii = jax.nn.sigmoid(jnp.concatenate(i_parts, axis=1) + ib)
    log_a = r * (-LRU_C * _softplus(-lam))
    a = jnp.exp(log_a)
    th = jnp.tanh(log_a)
    v = -2.0 * th / (1.0 - th)
    root = jnp.where(v > 0.0, v * lax.rsqrt(v), 0.0)
    return a, root * (ii * xc)


def _rglru_prompt_kernel(x_ref, sh_ref, sc_ref, gt_ref, g_ref, win_ref, cw_ref, cb_ref, gw_ref, rb_ref, ib_ref,
                         lam_ref, wout_ref, conv0_ref, h0_ref, o_ref, conv_ref, hl_ref, xp_scr, a_scr, b_scr, *, nblk):
    t = pl.program_id(1)
    tq, d = x_ref.shape[1], x_ref.shape[2]
    r_w = lam_ref.shape[1]
    ncw = cw_ref.shape[0]

    @pl.when(t == 0)
    def _():
        xp_scr[0:SUBLANES, :] = conv0_ref[0]
        hl_ref[...] = h0_ref[...]

    x = x_ref[0]
    hm = _rms_mod(x, g_ref[...], sh_ref[0], sc_ref[0]).astype(BF16)
    u = _dot(hm, win_ref[...])
    gb = u[:, :r_w]
    xp_scr[SUBLANES:SUBLANES + tq, :] = u[:, r_w:]
    xc = cb_ref[...]
    for j in range(ncw):
        off = SUBLANES - (ncw - 1 - j)
        xc = xc + xp_scr[off:off + tq, :] * cw_ref[j:j + 1, :]
    conv_ref[0] = xp_scr[tq:tq + SUBLANES, :]
    xp_scr[0:SUBLANES, :] = xp_scr[tq:tq + SUBLANES, :]
    a, upd = _rglru_gates(xc, gw_ref, rb_ref[...], ib_ref[...], lam_ref[...], nblk)
    a_scr[...] = a
    b_scr[...] = upd
    row8 = lax.broadcasted_iota(jnp.int32, (SUBLANES, r_w), 0)

    def scan_body(i, h):
        rows = pl.ds(pl.multiple_of(i * SUBLANES, SUBLANES), SUBLANES)
        ai = a_scr[rows, :]
        bi = b_scr[rows, :]
        s = 1
        while s < SUBLANES:
            m = row8 >= s
            bi = jnp.where(m, ai * pltpu.roll(bi, s, 0) + bi, bi)
            ai = jnp.where(m, ai * pltpu.roll(ai, s, 0), ai)
            s *= 2
        hs = ai * h + bi
        a_scr[rows, :] = hs
        return hs[SUBLANES - 1:SUBLANES, :]

    h_fin = lax.fori_loop(0, tq // SUBLANES, scan_body, hl_ref[0])
    hl_ref[0] = h_fin
    y = _dot((_gelu_tanh(gb) * a_scr[...]).astype(BF16), wout_ref[...])
    o_ref[0] = x + gt_ref[0] * y


def _rglru_prompt_call(x, sh, sc, gt, g, win, cw, cb, gw, rb, ib, lam, wout, conv0, h0, *, tq):
    nb, length, d = x.shape
    r_w = lam.shape[1]
    nblk = gw.shape[0]
    ada_spec = pl.BlockSpec((1, 1, d), lambda b, t: (b, 0, 0))
    tile = pl.BlockSpec((1, tq, d), lambda b, t: (b, t, 0))
    conv_spec = pl.BlockSpec((1, SUBLANES, r_w), lambda b, t: (b, 0, 0))
    h_spec = pl.BlockSpec((1, 1, r_w), lambda b, t: (b, 0, 0))
    consts = [g, win, cw, cb, gw, rb, ib, lam, wout]
    return pl.pallas_call(
        functools.partial(_rglru_prompt_kernel, nblk=nblk),
        grid=(nb, length // tq),
        in_specs=[tile, ada_spec, ada_spec, ada_spec] + [_const_spec(a.shape) for a in consts] + [conv_spec, h_spec],
        out_specs=[tile, conv_spec, h_spec],
        out_shape=[jax.ShapeDtypeStruct(x.shape, F32),
                   jax.ShapeDtypeStruct((nb, SUBLANES, r_w), F32),
                   jax.ShapeDtypeStruct((nb, 1, r_w), F32)],
        scratch_shapes=[pltpu.VMEM((tq + SUBLANES, r_w), F32), pltpu.VMEM((tq, r_w), F32), pltpu.VMEM((tq, r_w), F32)],
        compiler_params=_cparams(("arbitrary", "arbitrary"), 48),
        name="rglru_prompt",
    )(x, sh, sc, gt, *consts, conv0, h0)


def _rglru_sample_kernel(x_ref, sh_ref, sc_ref, gt_ref, g_ref, win_ref, cw_ref, cb_ref, gw_ref, rb_ref, ib_ref,
                         lam_ref, wout_ref, conv0_ref, h0_ref, o_ref, conv_ref, hl_ref, xp_scr, *, nblk):
    bb, tl, d = x_ref.shape
    tm = bb * tl
    r_w = lam_ref.shape[1]
    ncw = cw_ref.shape[0]
    x = x_ref[...]
    hm = _rms_mod(x, g_ref[...], sh_ref[...], sc_ref[...]).reshape(tm, d).astype(BF16)
    u = _dot(hm, win_ref[...])
    gb = u[:, :r_w]
    xp_scr[:, 0:SUBLANES, :] = conv0_ref[...]
    xp_scr[:, SUBLANES:SUBLANES + tl, :] = u[:, r_w:].reshape(bb, tl, r_w)
    xc = jnp.broadcast_to(cb_ref[...], (bb, tl, r_w))
    for j in range(ncw):
        off = SUBLANES - (ncw - 1 - j)
        xc = xc + xp_scr[:, off:off + tl, :] * cw_ref[j:j + 1, :]
    conv_ref[...] = xp_scr[:, tl:tl + SUBLANES, :]
    a, b = _rglru_gates(xc.reshape(tm, r_w), gw_ref, rb_ref[...], ib_ref[...], lam_ref[...], nblk)
    pos = lax.broadcasted_iota(jnp.int32, (tm, r_w), 0) % tl
    s = 1
    while s < tl:
        m = pos >= s
        b = jnp.where(m, a * pltpu.roll(b, s, 0) + b, b)
        a = jnp.where(m, a * pltpu.roll(a, s, 0), a)
        s *= 2
    hs = a.reshape(bb, tl, r_w) * h0_ref[...] + b.reshape(bb, tl, r_w)
    hl_ref[...] = hs[:, tl - 1:tl, :]
    y = _dot((_gelu_tanh(gb) * hs.reshape(tm, r_w)).astype(BF16), wout_ref[...])
    o_ref[...] = x + gt_ref[...] * y.reshape(bb, tl, d)


def _rglru_sample_call(x, sh, sc, gt, g, win, cw, cb, gw, rb, ib, lam, wout, conv0, h0):
    nb, tl, d = x.shape
    r_w = lam.shape[1]
    nblk = gw.shape[0]
    full = lambda a, b: pl.BlockSpec((nb, a, b), lambda i: (0, 0, 0))
    consts = [g, win, cw, cb, gw, rb, ib, lam, wout]
    return pl.pallas_call(
        functools.partial(_rglru_sample_kernel, nblk=nblk),
        grid=(1,),
        in_specs=[full(tl, d), full(1, d), full(1, d), full(1, d)] + [_const_spec(a.shape) for a in consts]
                 + [full(SUBLANES, r_w), full(1, r_w)],
        out_specs=[full(tl, d), full(SUBLANES, r_w), full(1, r_w)],
        out_shape=[jax.ShapeDtypeStruct(x.shape, F32),
                   jax.ShapeDtypeStruct((nb, SUBLANES, r_w), F32),
                   jax.ShapeDtypeStruct((nb, 1, r_w), F32)],
        scratch_shapes=[pltpu.VMEM((nb, tl + SUBLANES, r_w), F32)],
        compiler_params=_cparams(("arbitrary",), 48),
        name="rglru_sample",
    )(x, sh, sc, gt, *consts, conv0, h0)


def _pad_rows_front(a, rows):
    return jnp.pad(a, ((0, 0), (rows - a.shape[1], 0), (0, 0)))


def kernel(x_prompt, x_sample, state_a_C, state_a_n, state_a_m, cache_b_k, cache_b_v, state_c_conv, state_c_h,
           c_prompt, c_sample, ffn1_norm, ffn1_w_in, ffn1_w_out, mix_norm, ffn2_norm, ffn2_w_in, ffn2_w_out,
           ada_w, ada_b, ab_w_in, ab_gate_bias, a_out_norm, b_q_norm, b_k_norm, b_rel_bias, ab_w_out,
           c_w_in, c_conv_w, c_conv_b, c_gate_w, c_gate_b, c_lambda, c_w_out):
    nbp, seq, d = x_prompt.shape
    nbs, tdec, _ = x_sample.shape
    depth = ada_w.shape[0]
    n_ada = ada_w.shape[2] // d
    _, _, nh, dh, _ = state_a_C.shape
    _, _, w_band, nhb, dhb = cache_b_k.shape
    aw, bw = nh * dh, nhb * dhb
    ncw = c_conv_w.shape[1]
    assert 2 * dhb == LANES and dh == LANES and w_band % CHUNK == 0 and seq % w_band == 0

    ada = _ada_call(jnp.concatenate([c_prompt, c_sample], axis=0), ada_w, ada_b)
    ada = ada.reshape(depth, nbp + nbs, n_ada, 1, d)
    ada_p = [[ada[l, :nbp, k] for k in range(n_ada)] for l in range(depth)]
    ada_s = [[ada[l, nbp:, k] for k in range(n_ada)] for l in range(depth)]

    tl_p = 512
    xp, xs = x_prompt, x_sample
    outs_p, outs_s = {}, {}
    for l in range(depth):
        ap, as_ = ada_p[l], ada_s[l]
        i = l // 2
        g1 = ffn1_norm[l].reshape(1, d)
        gm = mix_norm[l].reshape(1, d)
        g2 = ffn2_norm[l].reshape(1, d)
        mix_p = None
        xp, xs = _ffn_call(xp, ap[0:3], xs, as_[0:3], g1, ffn1_w_in, ffn1_w_out, l, tl_p)
        if l % 2 == 0:
            w_in = ab_w_in[i]
            wab = jnp.concatenate(
                [w_in[:, :4 * aw], w_in[:, 4 * aw + 2 * nh:], w_in[:, 4 * aw:4 * aw + 2 * nh],
                 jnp.zeros((d, LANES - 2 * nh), F32)], axis=1).astype(BF16)
            gbias = jnp.pad(ab_gate_bias[i], (0, LANES - 2 * nh)).reshape(1, LANES)
            qg = jnp.tile(b_q_norm[i], nhb).reshape(1, bw)
            kg = jnp.tile(b_k_norm[i], nhb).reshape(1, bw)
            head = jnp.arange(bw) // dhb
            e = (head[:, None] == head[None, :]).astype(BF16)
            woa = ab_w_out[i][:aw].astype(BF16)
            wob = ab_w_out[i][aw:].astype(BF16)
            gout = a_out_norm[i]
            bias = _relbias_call(b_rel_bias[i], w_band)
            bias2 = bias.reshape(nhb // 2, 2 * CHUNK, w_band + CHUNK)

            ua_s, gts_s, qn_s, kn, vn = _proj_sample_call(
                xs, as_[3], as_[4], gm, wab, gbias, qg, kg, e, nh=nh, dhb=dhb)
            sample_mix = (ua_s, gts_s, state_a_C[i],
                          jnp.broadcast_to(state_a_n[i][..., None], (nbs, nh, dh, LANES)),
                          jnp.broadcast_to(state_a_m[i][..., None], (nbs, nh, LANES)), gout,
                          qn_s, kn, vn, cache_b_k[i].reshape(nbs, w_band, bw), cache_b_v[i].reshape(nbs, w_band, bw),
                          bias[:, :tdec, :w_band + tdec].reshape(nhb // 2, 2 * tdec, w_band + tdec))
            ride = nbs == nbp * (seq // w_band)

            (ua, gts, qn, kpad, vpad, klast, vlast), rode = _proj_prompt_call(
                xp, ap[3], ap[4], gm, wab, gbias, qg, kg, e, nh=nh, dhb=dhb, w=w_band,
                rider=sample_mix if ride else None)
            zc = jnp.zeros((nbp, nh, dh, dh), F32)
            ha, c1, n1, m1, hb = _mixer_prompt_call(ua, gts, zc, zc, zc[:, :, 0], gout, qn, kpad, vpad, bias2,
                                                    w=w_band, nck=4)
            mix_p = (ha, hb, ap[5], woa, wob)
            outs_p.setdefault('a_C', []).append(c1)
            outs_p.setdefault('a_n', []).append(n1)
            outs_p.setdefault('a_m', []).append(m1[:, :, 0])
            outs_p.setdefault('b_k', []).append(klast.reshape(nbp, w_band, nhb, dhb))
            outs_p.setdefault('b_v', []).append(vlast.reshape(nbp, w_band, nhb, dhb))

            if ride:
                ha, c1, n1, m1, hb = rode
            else:
                ha, c1, n1, m1 = _mlstm_call(*sample_mix[:6], tq=tdec, seg=tdec)
                hb = _band_sample_call(*sample_mix[6:])
            xs = _mixout_call(xs, ha, hb, as_[5], woa, wob, nbs, tdec)
            outs_s.setdefault('a_C', []).append(c1)
            outs_s.setdefault('a_n', []).append(n1)
            outs_s.setdefault('a_m', []).append(m1[:, :, 0])
            outs_s.setdefault('b_k', []).append(kn.reshape(nbs, tdec, nhb, dhb))
            outs_s.setdefault('b_v', []).append(vn.reshape(nbs, tdec, nhb, dhb))
        else:
            r_w = c_lambda.shape[1]
            consts = (gm, c_w_in[i].astype(BF16), c_conv_w[i], c_conv_b[i].reshape(1, r_w), c_gate_w[i].astype(BF16),
                      c_gate_b[i][0].reshape(1, r_w), c_gate_b[i][1].reshape(1, r_w), c_lambda[i].reshape(1, r_w),
                      c_w_out[i].astype(BF16))
            xp, conv_p, h_p = _rglru_prompt_call(
                xp, ap[3], ap[4], ap[5], *consts,
                jnp.zeros((nbp, SUBLANES, r_w), F32), jnp.zeros((nbp, 1, r_w), F32), tq=512)
            xs, conv_s, h_s = _rglru_sample_call(
                xs, as_[3], as_[4], as_[5], *consts,
                _pad_rows_front(state_c_conv[i], SUBLANES), state_c_h[i][:, None, :])
            outs_p.setdefault('c_conv', []).append(conv_p[:, SUBLANES - (ncw - 1):])
            outs_p.setdefault('c_h', []).append(h_p[:, 0])
            outs_s.setdefault('c_conv', []).append(conv_s[:, SUBLANES - (ncw - 1):])
            outs_s.setdefault('c_h', []).append(h_s[:, 0])
        xp, xs = _ffn_call(xp, ap[6:9], xs, as_[6:9], g2, ffn2_w_in, ffn2_w_out, l, tl_p, mix=mix_p)

    names = ('a_C', 'a_n', 'a_m', 'b_k', 'b_v', 'c_conv', 'c_h')
    ps = [jnp.stack(outs_p[n], axis=0) for n in names]
    ss = [jnp.stack(outs_s[n], axis=0) for n in names]
    return (xp, xs, *ps, *ss)
```

```python
import functools

import jax
import jax.numpy as jnp
from jax import lax
from jax.experimental import pallas as pl
from jax.experimental.pallas import tpu as pltpu

F32 = jnp.float32
BF16 = jnp.bfloat16

EPS = 1e-6
CHUNK = 64
LRU_C = 8.0
LANES = 128
SUBLANES = 8
MIB = 1024 * 1024


def _cparams(semantics, vmem_mib):
    return pltpu.CompilerParams(dimension_semantics=semantics, vmem_limit_bytes=vmem_mib * MIB)


def _const_spec(shape):
    nd = len(shape)
    return pl.BlockSpec(shape, lambda *_: (0,) * nd, pipeline_mode=pl.Buffered(1))


def _dot(a, b):
    return jnp.dot(a, b, preferred_element_type=F32)


def _dot_nt(a, b):
    return lax.dot_general(a, b, (((1,), (1,)), ((), ())), preferred_element_type=F32)


def _dot_tn(a, b):
    return lax.dot_general(a, b, (((0,), (0,)), ((), ())), preferred_element_type=F32)


def _rms_mod(x, g, shift, scale):
    ms = jnp.mean(x * x, axis=-1, keepdims=True)
    return (x * lax.rsqrt(ms + EPS)) * (g * (1.0 + scale)) + shift


def _softplus(x):
    return jnp.maximum(x, 0.0) + jnp.log1p(jnp.exp(-jnp.abs(x)))


def _gelu_tanh(x):
    c = 0.7978845608028654
    inner = x * ((x * x) * (c * 0.044715) + c)
    return x * (0.5 * jnp.tanh(inner) + 0.5)


def _ada_kernel(c_ref, w_ref, b_ref, o_ref):
    c = c_ref[...].astype(BF16)
    w = w_ref[0].astype(BF16)
    o_ref[0] = _dot(c, w) + b_ref[0]


def _ada_call(c_all, ada_w, ada_b):
    depth, d, n = ada_w.shape
    m = c_all.shape[0]
    tn = d
    return pl.pallas_call(
        _ada_kernel,
        grid=(depth, n // tn),
        in_specs=[pl.BlockSpec((m, d), lambda l, j: (0, 0)),
                  pl.BlockSpec((1, d, tn), lambda l, j: (l, 0, j)),
                  pl.BlockSpec((1, 1, tn), lambda l, j: (l, 0, j))],
        out_specs=pl.BlockSpec((1, m, tn), lambda l, j: (l, 0, j)),
        out_shape=jax.ShapeDtypeStruct((depth, m, n), F32),
        compiler_params=_cparams(("arbitrary", "arbitrary"), 32),
        name="ada_proj",
    )(c_all, ada_w, ada_b.reshape(depth, 1, n))


FFN_TF = 256


def _ffn_kernel(*refs, mixed, n_prompt):
    if mixed:
        ha_ref, hb_ref, gm_ref, woa_ref, wob_ref = refs[:5]
        refs = refs[5:]
    x_ref, sh_ref, sc_ref, gt_ref, xs_ref, adas_ref, g_ref, win_ref, wo_ref, o_ref, os_ref, act_scr = refs
    is_sample = pl.program_id(0) == n_prompt
    _, tl, d = x_ref.shape
    nbs = adas_ref.shape[1]
    dff = wo_ref.shape[0]

    def half_step(x, shift, scale, gate_vec):
        h = _rms_mod(x, g_ref[...], shift, scale).astype(BF16)
        for c0 in range(0, dff, FFN_TF):
            gate = _dot(h, win_ref[:, c0:c0 + FFN_TF].astype(BF16))
            up = _dot(h, win_ref[:, dff + c0:dff + c0 + FFN_TF].astype(BF16))
            act_scr[:, c0:c0 + FFN_TF] = ((gate * jax.nn.sigmoid(gate)) * up).astype(BF16)
        y = _dot(act_scr[...], wo_ref[...].astype(BF16))
        return x + (0.5 * gate_vec) * y

    @pl.when(jnp.logical_not(is_sample))
    def _():
        x = x_ref[0]
        if mixed:
            x = x + gm_ref[0] * (_dot(ha_ref[0], woa_ref[...]) + _dot(hb_ref[0], wob_ref[...]))
        o_ref[0] = half_step(x, sh_ref[0], sc_ref[0], gt_ref[0])

    @pl.when(is_sample)
    def _():
        def rows(k):
            return jnp.broadcast_to(adas_ref[k][:, None, :], (nbs, tl // nbs, d)).reshape(tl, d)
        os_ref[...] = half_step(xs_ref[...], rows(0), rows(1), rows(2))


def _ffn_call(xp, ada_p, xs, ada_s, g, w_in, w_out, layer, tl, mix=None):
    nb, length, d = xp.shape
    nbs, tdec, _ = xs.shape
    dff = w_out.shape[1]
    assert dff % FFN_TF == 0 and nbs * tdec == tl
    nt = length // tl
    n_prompt = nb * nt
    cur = lambda i: jnp.minimum(i, n_prompt - 1)
    tile = lambda n: pl.BlockSpec((1, tl, n), lambda i: (cur(i) // nt, cur(i) % nt, 0))
    ada_spec = pl.BlockSpec((1, 1, d), lambda i: (cur(i) // nt, 0, 0))
    once = lambda shape: pl.BlockSpec(shape, lambda i: (0,) * len(shape), pipeline_mode=pl.Buffered(1))
    layer_spec = lambda shape: pl.BlockSpec((None,) + shape, lambda i: (layer, 0, 0), pipeline_mode=pl.Buffered(1))
    mix_args, mix_specs = [], []
    if mix is not None:
        ha, hb, gm, woa, wob = mix
        mix_args = [ha, hb, gm, woa, wob]
        mix_specs = [tile(ha.shape[-1]), tile(hb.shape[-1]), ada_spec, once(woa.shape), once(wob.shape)]
    adas = jnp.stack([a[:, 0] for a in ada_s])
    op, os = pl.pallas_call(
        functools.partial(_ffn_kernel, mixed=mix is not None, n_prompt=n_prompt),
        grid=(n_prompt + 1,),
        in_specs=mix_specs + [tile(d), ada_spec, ada_spec, ada_spec, once((tl, d)), once(adas.shape), once((1, d)),
                              layer_spec(w_in.shape[1:]), layer_spec(w_out.shape[1:])],
        out_specs=[tile(d), pl.BlockSpec((tl, d), lambda i: (0, 0))],
        out_shape=[jax.ShapeDtypeStruct(xp.shape, F32), jax.ShapeDtypeStruct((tl, d), F32)],
        scratch_shapes=[pltpu.VMEM((tl, dff), BF16)],
        compiler_params=_cparams(("arbitrary",), 60),
        name="ffn",
    )(*mix_args, xp, *ada_p, xs.reshape(tl, d), adas, g, w_in, w_out)
    return op, os.reshape(nbs, tdec, d)


def _head_rmsnorm(q, e, g, dhb):
    ss = _dot((q * q).astype(BF16), e)
    return q * lax.rsqrt(ss * (1.0 / dhb) + EPS) * g


def _proj_body(x, sh, sc, g, w_ref, gb_ref, qg_ref, kg_ref, e_ref, *, nh, bw, dhb):
    bb, tl, d = x.shape
    na = w_ref.shape[1] - 3 * bw - LANES
    h = _rms_mod(x, g, sh, sc).reshape(bb * tl, d).astype(BF16)
    ua = _dot(h, w_ref[:, :na])
    gg = _dot(h, w_ref[:, na + 3 * bw:]) + gb_ref[...]
    lane = lax.broadcasted_iota(jnp.int32, gg.shape, 1)
    gates = jnp.where(lane < nh, gg, -_softplus(-gg))
    ub = _dot(h, w_ref[:, na:na + 3 * bw])
    e = e_ref[...]
    qn = _head_rmsnorm(ub[:, :bw], e, qg_ref[...], dhb) * (dhb ** -0.5)
    kn = _head_rmsnorm(ub[:, bw:2 * bw], e, kg_ref[...], dhb)
    vb = ub[:, 2 * bw:]
    return ua, gates, qn, kn, vb


N_PROJ_IN, N_PROJ_OUT, N_RIDER_IN, N_RIDER_OUT = 9, 7, 12, 5


def _proj_prompt_kernel(*refs, nh, bw, dhb, rider):
    x_ref, sh_ref, sc_ref, g_ref, w_ref, gb_ref, qg_ref, kg_ref, e_ref = refs[:N_PROJ_IN]
    refs = refs[N_PROJ_IN:]
    if rider:
        (uas_ref, gs_ref, c0_ref, n0_ref, m0_ref, go_ref, qs_ref, kns_ref, vns_ref, ck_ref, cv_ref,
         bs_ref) = refs[:N_RIDER_IN]
        refs = refs[N_RIDER_IN:]
    ua_ref, gt_ref, qn_ref, kp_ref, vp_ref, kl_ref, vl_ref = refs[:N_PROJ_OUT]
    refs = refs[N_PROJ_OUT:]
    if rider:
        has_ref, cs_ref, ns_ref, ms_ref, hbs_ref = refs[:N_RIDER_OUT]
        nrep_scr, rep_scr, s_scr, pv_scr, kv_scr, qc_scr = refs[N_RIDER_OUT:]
    t = pl.program_id(1)
    nt = pl.num_programs(1)

    @pl.when(t == 0)
    def _():
        kp_ref[...] = jnp.zeros_like(kp_ref)
        vp_ref[...] = jnp.zeros_like(vp_ref)

    @pl.when(t > 0)
    def _():
        ua, gates, qn, kn, vb = _proj_body(x_ref[...], sh_ref[...], sc_ref[...], g_ref[...], w_ref,
                                           gb_ref, qg_ref, kg_ref, e_ref, nh=nh, bw=bw, dhb=dhb)
        ua_ref[0] = ua
        gt_ref[0] = gates
        qn_ref[0] = qn.astype(BF16)
        kp_ref[0] = kn.astype(BF16)
        vp_ref[0] = vb.astype(BF16)
        if rider:
            cs_ref[...] = c0_ref[...]
            nrep_scr[...] = n0_ref[0]
            ms_ref[...] = m0_ref[...]
            for steps in _mlstm_tile(uas_ref, gs_ref, go_ref, has_ref, cs_ref, ns_ref, ms_ref, nrep_scr, rep_scr,
                                     s_scr, pv_scr, kv_scr, qc_scr, seg=uas_ref.shape[1], nh=nh, dh=go_ref.shape[1]):
                _run_passes(steps)
            _band_sample_kernel(qs_ref, kns_ref, vns_ref, ck_ref, cv_ref, bs_ref, hbs_ref, npair=bs_ref.shape[0])

        @pl.when(t == nt - 1)
        def _():
            kl_ref[0] = kn
            vl_ref[0] = vb


def _proj_prompt_call(x, sh, sc, g, wab, gbias, qg, kg, e, *, nh, dhb, w, rider=None):
    nb, length, d = x.shape
    tl = w
    nt = length // tl
    bw = e.shape[0]
    aw4 = wab.shape[1] - 3 * bw - LANES
    prev = lambda b, t: (b, jnp.maximum(t - 1, 0), 0)
    ada_spec = pl.BlockSpec((1, 1, d), lambda b, t: (b, 0, 0))
    in_specs = [pl.BlockSpec((1, tl, d), prev), ada_spec, ada_spec, _const_spec((1, d)),
                _const_spec(wab.shape), _const_spec(gbias.shape),
                _const_spec(qg.shape), _const_spec(kg.shape), _const_spec(e.shape)]
    out_specs = [pl.BlockSpec((1, tl, aw4), prev),
                 pl.BlockSpec((1, tl, LANES), prev),
                 pl.BlockSpec((1, tl, bw), prev),
                 pl.BlockSpec((1, tl, bw), lambda b, t: (b, t, 0)),
                 pl.BlockSpec((1, tl, bw), lambda b, t: (b, t, 0)),
                 pl.BlockSpec((1, tl, bw), lambda b, t: (b, 0, 0)),
                 pl.BlockSpec((1, tl, bw), lambda b, t: (b, 0, 0))]
    out_shape = [jax.ShapeDtypeStruct((nb, length, aw4), F32),
                 jax.ShapeDtypeStruct((nb, length, LANES), F32),
                 jax.ShapeDtypeStruct((nb, length, bw), BF16),
                 jax.ShapeDtypeStruct((nb, length + w, bw), BF16),
                 jax.ShapeDtypeStruct((nb, length + w, bw), BF16),
                 jax.ShapeDtypeStruct((nb, w, bw), F32),
                 jax.ShapeDtypeStruct((nb, w, bw), F32)]
    args = [x, sh, sc, g, wab, gbias, qg, kg, e]
    scratch = []
    if rider is not None:
        ua_s, g_s, c0, n0rep, m0rep, gout, q_s, kn_s, vn_s, ck, cv, bias2_s = rider
        nbs, tdec, _ = ua_s.shape
        _, nha, dh, _ = c0.shape
        assert nbs == nb * nt and nha == nh
        per = lambda a: pl.BlockSpec((1,) + a.shape[1:],
                                     lambda b, t: (b * nt + jnp.maximum(t - 1, 0),) + (0,) * (a.ndim - 1))
        streams = [ua_s, g_s, c0, n0rep, m0rep]
        in_specs += [per(a) for a in streams] + [_const_spec(gout.shape)]
        in_specs += [per(a) for a in (q_s, kn_s, vn_s, ck, cv)] + [_const_spec(bias2_s.shape)]
        args += streams + [gout, q_s, kn_s, vn_s, ck, cv, bias2_s]
        r_shapes = [jax.ShapeDtypeStruct((nbs, tdec, nh * dh), BF16), jax.ShapeDtypeStruct((nbs, nh, dh, dh), F32),
                    jax.ShapeDtypeStruct((nbs, nh, dh), F32), jax.ShapeDtypeStruct((nbs, nh, LANES), F32),
                    jax.ShapeDtypeStruct((nbs, tdec, bw), BF16)]
        out_shape += r_shapes
        out_specs += [per(a) for a in r_shapes]
        scratch = [pltpu.VMEM((nh, dh, LANES), F32), pltpu.VMEM((3 * nh, tdec, LANES), F32),
                   pltpu.VMEM((nh, tdec, tdec), F32), pltpu.VMEM((nh, tdec, dh + LANES), F32),
                   pltpu.VMEM((nh, dh, dh + LANES), F32), pltpu.VMEM((nh, tdec, dh + LANES), F32)]
    outs = pl.pallas_call(
        functools.partial(_proj_prompt_kernel, nh=nh, bw=bw, dhb=dhb, rider=rider is not None),
        grid=(nb, nt + 1),
        in_specs=in_specs,
        out_specs=out_specs,
        out_shape=out_shape,
        scratch_shapes=scratch,
        compiler_params=_cparams(("arbitrary", "arbitrary"), 48),
        name="proj_prompt",
    )(*args)
    return outs[:N_PROJ_OUT], outs[N_PROJ_OUT:]


def _proj_sample_kernel(x_ref, sh_ref, sc_ref, g_ref, w_ref, gb_ref, qg_ref, kg_ref, e_ref,
                        ua_ref, gt_ref, qn_ref, kn_ref, vb_ref, *, nh, bw, dhb):
    bb, tl, _ = x_ref.shape
    ua, gates, qn, kn, vb = _proj_body(x_ref[...], sh_ref[...], sc_ref[...], g_ref[...], w_ref,
                                       gb_ref, qg_ref, kg_ref, e_ref, nh=nh, bw=bw, dhb=dhb)
    ua_ref[...] = ua.reshape(bb, tl, -1)
    gt_ref[...] = gates.reshape(bb, tl, -1)
    qn_ref[...] = qn.reshape(bb, tl, -1).astype(BF16)
    kn_ref[...] = kn.reshape(bb, tl, -1)
    vb_ref[...] = vb.reshape(bb, tl, -1)


def _proj_sample_call(x, sh, sc, g, wab, gbias, qg, kg, e, *, nh, dhb):
    nb, length, d = x.shape
    bw = e.shape[0]
    aw4 = wab.shape[1] - 3 * bw - LANES
    full = lambda n: pl.BlockSpec((nb, length, n), lambda i: (0, 0, 0))
    ada_spec = pl.BlockSpec((nb, 1, d), lambda i: (0, 0, 0))
    return pl.pallas_call(
        functools.partial(_proj_sample_kernel, nh=nh, bw=bw, dhb=dhb),
        grid=(1,),
        in_specs=[full(d), ada_spec, ada_spec, _const_spec((1, d)),
                  _const_spec(wab.shape), _const_spec(gbias.shape),
                  _const_spec(qg.shape), _const_spec(kg.shape), _const_spec(e.shape)],
        out_specs=[full(aw4), full(LANES), full(bw), full(bw), full(bw)],
        out_shape=[jax.ShapeDtypeStruct((nb, length, aw4), F32),
                   jax.ShapeDtypeStruct((nb, length, LANES), F32),
                   jax.ShapeDtypeStruct((nb, length, bw), BF16),
                   jax.ShapeDtypeStruct((nb, length, bw), F32),
                   jax.ShapeDtypeStruct((nb, length, bw), F32)],
        compiler_params=_cparams(("arbitrary",), 48),
        name="proj_sample",
    )(x, sh, sc, g, wab, gbias, qg, kg, e)


def _mlstm_init(c0_ref, n0_ref, m0_ref, c_ref, m_ref, nrep_scr):
    @pl.when(pl.program_id(1) == 0)
    def _():
        c_ref[...] = c0_ref[...]
        nrep_scr[...] = n0_ref[0]
        m_ref[...] = m0_ref[...]


def _mlstm_kernel(ua_ref, g_ref, c0_ref, n0_ref, m0_ref, go_ref, ha_ref, c_ref, n_ref, m_ref,
                  nrep_scr, rep_scr, s_scr, pv_scr, kv_scr, qc_scr, *, seg, nh, dh):
    _mlstm_init(c0_ref, n0_ref, m0_ref, c_ref, m_ref, nrep_scr)
    for steps in _mlstm_tile(ua_ref, g_ref, go_ref, ha_ref, c_ref, n_ref, m_ref,
                             nrep_scr, rep_scr, s_scr, pv_scr, kv_scr, qc_scr, seg=seg, nh=nh, dh=dh):
        _run_passes(steps)


def _mlstm_tile(ua_ref, g_ref, go_ref, ha_ref, c_ref, n_ref, m_ref,
                nrep_scr, rep_scr, s_scr, pv_scr, kv_scr, qc_scr, *, seg, nh, dh):
    tq = ua_ref.shape[1]
    nck = tq // seg
    aw = nh * dh
    gates = g_ref[0]
    pos = lax.broadcasted_iota(jnp.int32, gates.shape, 0) % seg
    bt = gates
    s = 1
    while s < seg:
        bt = bt + jnp.where(pos >= s, pltpu.roll(bt, s, 0), 0.0)
        s *= 2
    dmb = pltpu.roll(gates, nh, 1) - bt
    pm = dmb
    s = 1
    while s < seg:
        pm = jnp.maximum(pm, jnp.where(pos >= s, pltpu.roll(pm, s, 0), -jnp.inf))
        s *= 2
    if tq % LANES:
        dsq = jnp.concatenate([dmb, jnp.zeros((LANES - tq % LANES, LANES), F32)], axis=0)
    else:
        dsq = dmb
    dtr = dsq.T
    ri = lax.broadcasted_iota(jnp.int32, (seg, seg), 0)
    ci = lax.broadcasted_iota(jnp.int32, (seg, seg), 1)
    causal = ri >= ci
    ones = jnp.ones((seg, LANES), BF16)
    ones_dh = jnp.ones((dh, LANES), BF16)
    for h in range(nh):
        ln = slice(nh + h, nh + h + 1)
        for j, arr in enumerate((bt, dmb, pm)):
            rep_scr[3 * h + j] = jnp.broadcast_to(arr[:, ln], (tq, LANES))

    def cols(jc, h):
        rows = slice(jc * seg, (jc + 1) * seg)
        return rows, rep_scr[3 * h, rows, :], rep_scr[3 * h + 1, rows, :], rep_scr[3 * h + 2, rows, :]

    def last(jc, h, j):
        r = (jc + 1) * seg - 1
        return rep_scr[3 * h + j, r:r + 1, :]

    groups = [(jc, h) for jc in range(nck) for h in range(nh)]

    def qkv(jc, h, which):
        rows = slice(jc * seg, (jc + 1) * seg)
        return ua_ref[0, rows, which * aw + h * dh:which * aw + (h + 1) * dh]

    state = []
    before = []

    def score(g, jc, h):
        k = qkv(jc, h, 1) * (dh ** -0.5)
        s_scr[g] = _dot_nt(qkv(jc, h, 0).astype(BF16), k.astype(BF16))

    def local(g, jc, h):
        rows, _, _, p_col = cols(jc, h)
        d_row = dtr[nh + h:nh + h + 1, rows]
        dloc = jnp.exp(jnp.where(causal, d_row - p_col[:, :seg], -jnp.inf))
        sl = (s_scr[g] * dloc).astype(BF16)
        v = qkv(jc, h, 2).astype(BF16)
        pv_scr[g] = _dot(sl, jnp.concatenate([v, ones], axis=1))

    def contrib(g, jc, h):
        _, _, d_col, _ = cols(jc, h)
        kw = (qkv(jc, h, 1) * (dh ** -0.5)) * jnp.exp(d_col - last(jc, h, 2))
        vx = jnp.concatenate([qkv(jc, h, 2).astype(BF16), ones], axis=1)
        kv_scr[g] = _dot_tn(kw.astype(BF16), vx)

    def carry(g, jc, h):
        if not state:
            state.extend((c_ref[0, hh], nrep_scr[hh], m_ref[0, hh:hh + 1, :]) for hh in range(nh))
        c_mem, n_rep, m = state[h]
        cn = jnp.concatenate([c_mem.astype(BF16), n_rep.astype(BF16)], axis=1)
        qc_scr[g] = _dot(qkv(jc, h, 0).astype(BF16), cn)
        before.append(m)
        p_last = last(jc, h, 2)
        mml = jnp.maximum(m, p_last)
        w_prev = jnp.exp(m - mml)
        f_new = jnp.exp(p_last - mml)
        kvx = kv_scr[g]
        state[h] = (w_prev * c_mem + f_new * kvx[:, :dh],
                    w_prev * n_rep + f_new * kvx[:, dh:],
                    last(jc, h, 0) + mml)
        if g == len(groups) - 1:
            for hh in range(nh):
                c_ref[0, hh], nrep_scr[hh], m_ref[0, hh:hh + 1, :] = state[hh]
                n_ref[0, hh:hh + 1, :] = state[hh][1].T[0:1, :]

    def combine(g, jc, h):
        rows, b_col, _, p_col = cols(jc, h)
        m = before[g]
        mm = jnp.maximum(m, p_col)
        iw = jnp.exp(m - mm)
        fl = jnp.exp(p_col - mm)
        pv = pv_scr[g]
        qc = qc_scr[g]
        num = iw * qc[:, :dh] + fl * pv[:, :dh]
        den = iw * qc[:, dh:] + fl * pv[:, dh:]
        hh = num / jnp.maximum(jnp.abs(den), jnp.exp(-(b_col + mm)))
        h2 = hh * hh
        hi = h2.astype(BF16)
        lo = (h2 - hi.astype(F32)).astype(BF16)
        ms = (_dot(hi, ones_dh) + _dot(lo, ones_dh)) * (1.0 / dh)
        hn = (hh * lax.rsqrt(ms + EPS) * go_ref[h:h + 1, :]) * jax.nn.sigmoid(qkv(jc, h, 3))
        ha_ref[0, rows, h * dh:(h + 1) * dh] = hn.astype(BF16)

    return [[functools.partial(fn, g, jc, h) for g, (jc, h) in enumerate(groups)]
            for fn in (score, local, contrib, carry, combine)]


def _run_passes(*pass_lists):
    for steps in zip(*[p + [None] * (max(map(len, pass_lists)) - len(p)) for p in pass_lists]):
        for step in steps:
            if step is not None:
                step()


def _mlstm_call(ua, gates, c0, n0rep, m0rep, gout, *, tq, seg):
    nb, length, aw4 = ua.shape
    _, nh, dh, _ = c0.shape
    assert dh == LANES
    groups = (tq // seg) * nh
    st = lambda shape: pl.BlockSpec((1,) + shape, lambda b, t: (b,) + (0,) * len(shape))
    tile = lambda n: pl.BlockSpec((1, tq, n), lambda b, t: (b, t, 0))
    return pl.pallas_call(
        functools.partial(_mlstm_kernel, seg=seg, nh=nh, dh=dh),
        grid=(nb, length // tq),
        in_specs=[tile(aw4), tile(LANES), st((nh, dh, dh)), st((nh, dh, LANES)), st((nh, LANES)),
                  _const_spec(gout.shape)],
        out_specs=[tile(nh * dh), st((nh, dh, dh)), st((nh, dh)), st((nh, LANES))],
        out_shape=[jax.ShapeDtypeStruct((nb, length, nh * dh), BF16),
                   jax.ShapeDtypeStruct((nb, nh, dh, dh), F32),
                   jax.ShapeDtypeStruct((nb, nh, dh), F32),
                   jax.ShapeDtypeStruct((nb, nh, LANES), F32)],
        scratch_shapes=[pltpu.VMEM((nh, dh, LANES), F32), pltpu.VMEM((3 * nh, tq, LANES), F32),
                        pltpu.VMEM((groups, seg, seg), F32), pltpu.VMEM((groups, seg, dh + LANES), F32),
                        pltpu.VMEM((groups, dh, dh + LANES), F32), pltpu.VMEM((groups, seg, dh + LANES), F32)],
        compiler_params=_cparams(("arbitrary", "arbitrary"), 32),
        name="mlstm",
    )(ua, gates, c0, n0rep, m0rep, gout)


def _relbias_kernel(b0_ref, o_ref):
    nhb, nq, nk = o_ref.shape
    for h in range(nhb):
        x = jnp.broadcast_to(b0_ref[h:h + 1, :], (nq, b0_ref.shape[1]))
        o_ref[h] = pltpu.roll(x, 0, 1, stride=1, stride_axis=0)[:, :nk]


def _relbias_call(table, w):
    nhb = table.shape[0]
    max_rel = (table.shape[1] - 1) // 2
    assert CHUNK - 1 <= max_rel <= w
    first = jnp.broadcast_to(table[:, :1], (nhb, w - max_rel))
    wrap = jnp.broadcast_to(table[:, :1], (nhb, CHUNK))
    b0 = jnp.concatenate([first, table[:, :max_rel + CHUNK], wrap], axis=1).astype(F32)
    return pl.pallas_call(
        _relbias_kernel,
        out_shape=jax.ShapeDtypeStruct((nhb, CHUNK, w + CHUNK), F32),
        name="rel_bias",
    )(b0)


def _band_tile(q_ref, k_ref, v_ref, bias_ref, o_ref, s_scr, m_scr, e_scr, *, masked, npair, w, nck):
    c4 = pl.program_id(1)
    nk = w + CHUNK
    lane = lax.broadcasted_iota(jnp.int32, (CHUNK, LANES), 1)
    low = lane < LANES // 2
    zero = jnp.zeros((CHUNK, LANES), BF16)
    ones = jnp.ones((nk, LANES), BF16)

    starts = [pl.multiple_of((c4 * nck + jc) * CHUNK, CHUNK) for jc in range(nck)]
    groups = [(jc, p) for jc in range(nck) for p in range(npair)]

    def score(g, jc, p):
        sl = slice(p * LANES, (p + 1) * LANES)
        qp = q_ref[0, jc * CHUNK:(jc + 1) * CHUNK, sl]
        q2 = jnp.concatenate([jnp.where(low, qp, zero), jnp.where(low, zero, qp)], axis=0)
        s = _dot_nt(q2, k_ref[0, pl.ds(starts[jc], nk), sl]) + bias_ref[p]
        if masked:
            col = lax.broadcasted_iota(jnp.int32, s.shape, 1)
            s = jnp.where(col + starts[jc] >= w, s, -jnp.inf)
        s_scr[g] = s
        m_scr[g] = jnp.max(s, axis=-1, keepdims=True)

    def expo(g, jc, p):
        e_scr[g] = jnp.exp(s_scr[g] - m_scr[g]).astype(BF16)

    def value(g, jc, p):
        sl = slice(p * LANES, (p + 1) * LANES)
        vx = jnp.concatenate([v_ref[0, pl.ds(starts[jc], nk), sl], ones], axis=1)
        r = _dot(e_scr[g], vx)
        o_lo = r[:CHUNK, :LANES] / r[:CHUNK, LANES:]
        o_hi = r[CHUNK:, :LANES] / r[CHUNK:, LANES:]
        o_ref[0, jc * CHUNK:(jc + 1) * CHUNK, sl] = jnp.where(low, o_lo, o_hi).astype(BF16)

    return [[functools.partial(fn, g, jc, p) for g, (jc, p) in enumerate(groups)] for fn in (score, expo, value)]


def _mixer_prompt_kernel(ua_ref, g_ref, c0_ref, n0_ref, m0_ref, go_ref, q_ref, k_ref, v_ref, bias_ref,
                         ha_ref, c_ref, n_ref, m_ref, hb_ref,
                         nrep_scr, rep_scr, sa_scr, pv_scr, kv_scr, qc_scr, sb_scr, mb_scr, eb_scr,
                         *, seg, nh, dh, npair, w, nck):
    _mlstm_init(c0_ref, n0_ref, m0_ref, c_ref, m_ref, nrep_scr)
    first_full = w // (CHUNK * nck)

    def tile(masked):
        b_score, b_exp, b_value = _band_tile(q_ref, k_ref, v_ref, bias_ref, hb_ref, sb_scr, mb_scr, eb_scr,
                                             masked=masked, npair=npair, w=w, nck=nck)
        a_score, a_local, a_contrib, a_carry, a_combine = _mlstm_tile(
            ua_ref, g_ref, go_ref, ha_ref, c_ref, n_ref, m_ref,
            nrep_scr, rep_scr, sa_scr, pv_scr, kv_scr, qc_scr, seg=seg, nh=nh, dh=dh)
        _run_passes(a_score, b_score)
        _run_passes(a_local, b_exp)
        _run_passes(a_contrib)
        _run_passes(a_carry)
        _run_passes(a_combine, b_value)

    @pl.when(pl.program_id(1) < first_full)
    def _():
        tile(True)

    @pl.when(pl.program_id(1) >= first_full)
    def _():
        tile(False)


def _mixer_prompt_call(ua, gates, c0, n0rep, m0rep, gout, qs, kpad, vpad, bias2, *, w, nck):
    nb, length, aw4 = ua.shape
    _, nh, dh, _ = c0.shape
    bw = qs.shape[2]
    npair = bias2.shape[0]
    lp = kpad.shape[1]
    tq = nck * CHUNK
    nk = w + CHUNK
    ga = nck * nh
    gb = nck * npair
    assert w % tq == 0 and dh == LANES
    st = lambda shape: pl.BlockSpec((1,) + shape, lambda b, t: (b,) + (0,) * len(shape))
    tile = lambda n: pl.BlockSpec((1, tq, n), lambda b, t: (b, t, 0))
    whole = pl.BlockSpec((1, lp, bw), lambda b, t: (b, 0, 0))
    return pl.pallas_call(
        functools.partial(_mixer_prompt_kernel, seg=CHUNK, nh=nh, dh=dh, npair=npair, w=w, nck=nck),
        grid=(nb, length // tq),
        in_specs=[tile(aw4), tile(LANES), st((nh, dh, dh)), st((nh, dh, LANES)), st((nh, LANES)),
                  _const_spec(gout.shape), tile(bw), whole, whole, _const_spec(bias2.shape)],
        out_specs=[tile(nh * dh), st((nh, dh, dh)), st((nh, dh)), st((nh, LANES)), tile(bw)],
        out_shape=[jax.ShapeDtypeStruct((nb, length, nh * dh), BF16),
                   jax.ShapeDtypeStruct((nb, nh, dh, dh), F32),
                   jax.ShapeDtypeStruct((nb, nh, dh), F32),
                   jax.ShapeDtypeStruct((nb, nh, LANES), F32),
                   jax.ShapeDtypeStruct((nb, length, bw), BF16)],
        scratch_shapes=[pltpu.VMEM((nh, dh, LANES), F32), pltpu.VMEM((3 * nh, tq, LANES), F32),
                        pltpu.VMEM((ga, CHUNK, CHUNK), F32), pltpu.VMEM((ga, CHUNK, dh + LANES), F32),
                        pltpu.VMEM((ga, dh, dh + LANES), F32), pltpu.VMEM((ga, CHUNK, dh + LANES), F32),
                        pltpu.VMEM((gb, 2 * CHUNK, nk), F32), pltpu.VMEM((gb, 2 * CHUNK, 1), F32),
                        pltpu.VMEM((gb, 2 * CHUNK, nk), BF16)],
        compiler_params=_cparams(("arbitrary", "arbitrary"), 48),
        name="mixer_prompt",
    )(ua, gates, c0, n0rep, m0rep, gout, qs, kpad, vpad, bias2)


def _band_sample_kernel(q_ref, kn_ref, vn_ref, ck_ref, cv_ref, bias_ref, o_ref, *, npair):
    tq = q_ref.shape[1]
    nk = ck_ref.shape[1] + tq
    lane = lax.broadcasted_iota(jnp.int32, (tq, LANES), 1)
    low = lane < LANES // 2
    zero = jnp.zeros((tq, LANES), BF16)
    ones = jnp.ones((nk, LANES), BF16)
    scores = []
    for p in range(npair):
        sl = slice(p * LANES, (p + 1) * LANES)
        qp = q_ref[0, :, sl]
        q2 = jnp.concatenate([jnp.where(low, qp, zero), jnp.where(low, zero, qp)], axis=0)
        kx = jnp.concatenate([ck_ref[0, :, sl].astype(BF16), kn_ref[0, :, sl].astype(BF16)], axis=0)
        scores.append(_dot_nt(q2, kx) + bias_ref[p])
    probs = [jnp.exp(s - jnp.max(s, axis=-1, keepdims=True)).astype(BF16) for s in scores]
    for p in range(npair):
        sl = slice(p * LANES, (p + 1) * LANES)
        vx = jnp.concatenate([cv_ref[0, :, sl].astype(BF16), vn_ref[0, :, sl].astype(BF16)], axis=0)
        r = _dot(probs[p], jnp.concatenate([vx, ones], axis=1))
        o_lo = r[:tq, :LANES] / r[:tq, LANES:]
        o_hi = r[tq:, :LANES] / r[tq:, LANES:]
        o_ref[0, :, sl] = jnp.where(low, o_lo, o_hi).astype(BF16)


def _band_sample_call(qn, kn, vn, ck, cv, bias2):
    nb, tq, bw = qn.shape
    w = ck.shape[1]
    npair = bias2.shape[0]
    new = pl.BlockSpec((1, tq, bw), lambda b: (b, 0, 0))
    cache = pl.BlockSpec((1, w, bw), lambda b: (b, 0, 0))
    return pl.pallas_call(
        functools.partial(_band_sample_kernel, npair=npair),
        grid=(nb,),
        in_specs=[new, new, new, cache, cache, _const_spec(bias2.shape)],
        out_specs=new,
        out_shape=jax.ShapeDtypeStruct((nb, tq, bw), BF16),
        compiler_params=_cparams(("arbitrary",), 32),
        name="band_sample",
    )(qn, kn, vn, ck, cv, bias2)


def _mixout_kernel(x_ref, ha_ref, hb_ref, gt_ref, woa_ref, wob_ref, o_ref):
    bb, tl, d = x_ref.shape
    ha = ha_ref[...].reshape(bb * tl, -1)
    hb = hb_ref[...].reshape(bb * tl, -1)
    y = _dot(ha, woa_ref[...]) + _dot(hb, wob_ref[...])
    o_ref[...] = x_ref[...] + gt_ref[...] * y.reshape(bb, tl, d)


def _mixout_call(x, ha, hb, gt, woa, wob, bb, tl):
    nb, length, d = x.shape
    tile = lambda n: pl.BlockSpec((bb, tl, n), lambda i, t: (i, t, 0))
    return pl.pallas_call(
        _mixout_kernel,
        grid=(nb // bb, length // tl),
        in_specs=[tile(d), tile(ha.shape[-1]), tile(hb.shape[-1]),
                  pl.BlockSpec((bb, 1, d), lambda i, t: (i, 0, 0)),
                  _const_spec(woa.shape), _const_spec(wob.shape)],
        out_specs=tile(d),
        out_shape=jax.ShapeDtypeStruct(x.shape, F32),
        compiler_params=_cparams(("arbitrary", "arbitrary"), 32),
        name="mix_out",
    )(x, ha, hb, gt, woa, wob)


def _rglru_gates(xc, gw_ref, rb, ib, lam, nblk):
    bwc = xc.shape[1] // nblk
    r_parts, i_parts = [], []
    for n in range(nblk):
        gn = _dot(xc[:, n * bwc:(n + 1) * bwc].astype(BF16), gw_ref[n])
        r_parts.append(gn[:, :bwc])
        i_parts.append(gn[:, bwc:])
    r = jax.nn.sigmoid(jnp.concatenate(r_parts, axis=1) + rb)
    ii = jax.nn.sigmoid(jnp.concatenate(i_parts, axis=1) + ib)
    log_a = r * (-LRU_C * _softplus(-lam))
    a = jnp.exp(log_a)
    th = jnp.tanh(log_a)
    v = -2.0 * th / (1.0 - th)
    root = jnp.where(v > 0.0, v * lax.rsqrt(v), 0.0)
    return a, root * (ii * xc)


def _rglru_prompt_kernel(x_ref, sh_ref, sc_ref, gt_ref, g_ref, win_ref, cw_ref, cb_ref, gw_ref, rb_ref, ib_ref,
                         lam_ref, wout_ref, conv0_ref, h0_ref, o_ref, conv_ref, hl_ref,
                         u_scr, x_scr, xp_scr, a_scr, h_scr, *, nblk, tiles_per_seq, proj_chunks):
    i = pl.program_id(0)
    tq, d = x_ref.shape[1], x_ref.shape[2]
    r_w = lam_ref.shape[1]
    ncw = cw_ref.shape[0]

    @pl.when(i == 0)
    def _():
        u_scr[1] = jnp.zeros(u_scr.shape[1:], F32)
        x_scr[1] = jnp.zeros(x_scr.shape[1:], F32)
        xp_scr[0:SUBLANES, :] = jnp.zeros((SUBLANES, r_w), F32)
        h_scr[...] = jnp.zeros_like(h_scr)

    @pl.when((i >= 1) & ((i - 1) % tiles_per_seq == 0))
    def _():
        xp_scr[0:SUBLANES, :] = conv0_ref[0]
        h_scr[...] = h0_ref[0]

    row8 = lax.broadcasted_iota(jnp.int32, (SUBLANES, r_w), 0)

    def body(slot):
        prev = 1 - slot
        x = x_ref[0]
        x_scr[slot] = x
        hm = _rms_mod(x, g_ref[...], sh_ref[0], sc_ref[0]).astype(BF16)
        wcols = 2 * r_w // proj_chunks

        def project(c):
            cols = slice(c * wcols, (c + 1) * wcols)
            uc = _dot(hm, win_ref[:, cols])
            u_scr[slot, :, cols] = uc
            bits = pltpu.bitcast(uc[0:SUBLANES, 0:LANES], jnp.uint32)
            zero = pltpu.bitcast(lax.shift_right_logical(bits, jnp.uint32(32)), F32)
            return zero[0:1, :]

        def wide(z, n):
            return jnp.tile(z, (1, n // LANES))

        pending = list(range(proj_chunks))

        def issue(n):
            z = None
            for _ in range(n):
                if pending:
                    zc = project(pending.pop(0))
                    z = zc if z is None else z + zc
            return z

        xp_scr[SUBLANES:SUBLANES + tq, :] = u_scr[prev, :, r_w:]
        xc = cb_ref[...]
        for j in range(ncw):
            off = SUBLANES - (ncw - 1 - j)
            xc = xc + xp_scr[off:off + tq, :] * cw_ref[j:j + 1, :]
        conv_ref[0] = xp_scr[tq:tq + SUBLANES, :]
        xp_scr[0:SUBLANES, :] = xp_scr[tq:tq + SUBLANES, :]
        a, upd = _rglru_gates(xc, gw_ref, rb_ref[...] + wide(issue(2), r_w), ib_ref[...], lam_ref[...], nblk)
        h = h_scr[...] + wide(issue(2), r_w)
        groups = tq // SUBLANES
        for gi in range(groups):
            r0 = gi * SUBLANES
            ai = a[r0:r0 + SUBLANES, :]
            bi = upd[r0:r0 + SUBLANES, :]
            s = 1
            while s < SUBLANES:
                m = row8 >= s
                bi = jnp.where(m, ai * pltpu.roll(bi, s, 0) + bi, bi)
                ai = jnp.where(m, ai * pltpu.roll(ai, s, 0), ai)
                s *= 2
            hs = ai * h + bi
            a_scr[r0:r0 + SUBLANES, :] = hs
            h = hs[SUBLANES - 1:SUBLANES, :]
            if gi % (groups // 4) == groups // 4 - 1 and gi != groups - 1:
                h = h + wide(issue(1), r_w)
        h_scr[...] = h
        hl_ref[0] = h
        gate_vec = gt_ref[0] + wide(issue(proj_chunks), d)
        y = _dot((_gelu_tanh(u_scr[prev, :, :r_w]) * a_scr[...]).astype(BF16), wout_ref[...])
        o_ref[0] = x_scr[prev] + gate_vec * y

    @pl.when(i % 2 == 0)
    def _():
        body(0)

    @pl.when(i % 2 == 1)
    def _():
        body(1)


def _rglru_prompt_call(x, sh, sc, gt, g, win, cw, cb, gw, rb, ib, lam, wout, conv0, h0, *, tq):
    nb, length, d = x.shape
    r_w = lam.shape[1]
    nblk = gw.shape[0]
    nt = length // tq
    ntiles = nb * nt
    cur = lambda i: jnp.minimum(i, ntiles - 1)
    prv = lambda i: jnp.maximum(i - 1, 0)
    consts = [g, win, cw, cb, gw, rb, ib, lam, wout]
    ada_cur = pl.BlockSpec((1, 1, d), lambda i: (cur(i) // nt, 0, 0))
    ada_prv = pl.BlockSpec((1, 1, d), lambda i: (prv(i) // nt, 0, 0))
    conv_spec = pl.BlockSpec((1, SUBLANES, r_w), lambda i: (prv(i) // nt, 0, 0))
    h_spec = pl.BlockSpec((1, 1, r_w), lambda i: (prv(i) // nt, 0, 0))
    return pl.pallas_call(
        functools.partial(_rglru_prompt_kernel, nblk=nblk, tiles_per_seq=nt, proj_chunks=8),
        grid=(ntiles + 1,),
        in_specs=[pl.BlockSpec((1, tq, d), lambda i: (cur(i) // nt, cur(i) % nt, 0)), ada_cur, ada_cur, ada_prv]
                 + [_const_spec(a.shape) for a in consts] + [conv_spec, h_spec],
        out_specs=[pl.BlockSpec((1, tq, d), lambda i: (prv(i) // nt, prv(i) % nt, 0)), conv_spec, h_spec],
        out_shape=[jax.ShapeDtypeStruct(x.shape, F32),
                   jax.ShapeDtypeStruct((nb, SUBLANES, r_w), F32),
                   jax.ShapeDtypeStruct((nb, 1, r_w), F32)],
        scratch_shapes=[pltpu.VMEM((2, tq, 2 * r_w), F32), pltpu.VMEM((2, tq, d), F32),
                        pltpu.VMEM((tq + SUBLANES, r_w), F32), pltpu.VMEM((tq, r_w), F32), pltpu.VMEM((1, r_w), F32)],
        compiler_params=_cparams(("arbitrary",), 56),
        name="rglru_prompt",
    )(x, sh, sc, gt, *consts, conv0, h0)


def _rglru_sample_kernel(x_ref, sh_ref, sc_ref, gt_ref, g_ref, win_ref, cw_ref, cb_ref, gw_ref, rb_ref, ib_ref,
                         lam_ref, wout_ref, conv0_ref, h0_ref, o_ref, conv_ref, hl_ref, xp_scr, *, nblk):
    bb, tl, d = x_ref.shape
    tm = bb * tl
    r_w = lam_ref.shape[1]
    ncw = cw_ref.shape[0]
    x = x_ref[...]
    hm = _rms_mod(x, g_ref[...], sh_ref[...], sc_ref[...]).reshape(tm, d).astype(BF16)
    u = _dot(hm, win_ref[...])
    gb = u[:, :r_w]
    xp_scr[:, 0:SUBLANES, :] = conv0_ref[...]
    xp_scr[:, SUBLANES:SUBLANES + tl, :] = u[:, r_w:].reshape(bb, tl, r_w)
    xc = jnp.broadcast_to(cb_ref[...], (bb, tl, r_w))
    for j in range(ncw):
        off = SUBLANES - (ncw - 1 - j)
        xc = xc + xp_scr[:, off:off + tl, :] * cw_ref[j:j + 1, :]
    conv_ref[...] = xp_scr[:, tl:tl + SUBLANES, :]
    a, b = _rglru_gates(xc.reshape(tm, r_w), gw_ref, rb_ref[...], ib_ref[...], lam_ref[...], nblk)
    pos = lax.broadcasted_iota(jnp.int32, (tm, r_w), 0) % tl
    s = 1
    while s < tl:
        m = pos >= s
        b = jnp.where(m, a * pltpu.roll(b, s, 0) + b, b)
        a = jnp.where(m, a * pltpu.roll(a, s, 0), a)
        s *= 2
    hs = a.reshape(bb, tl, r_w) * h0_ref[...] + b.reshape(bb, tl, r_w)
    hl_ref[...] = hs[:, tl - 1:tl, :]
    y = _dot((_gelu_tanh(gb) * hs.reshape(tm, r_w)).astype(BF16), wout_ref[...])
    o_ref[...] = x + gt_ref[...] * y.reshape(bb, tl, d)


def _rglru_sample_call(x, sh, sc, gt, g, win, cw, cb, gw, rb, ib, lam, wout, conv0, h0):
    nb, tl, d = x.shape
    r_w = lam.shape[1]
    nblk = gw.shape[0]
    full = lambda a, b: pl.BlockSpec((nb, a, b), lambda i: (0, 0, 0))
    consts = [g, win, cw, cb, gw, rb, ib, lam, wout]
    return pl.pallas_call(
        functools.partial(_rglru_sample_kernel, nblk=nblk),
        grid=(1,),
        in_specs=[full(tl, d), full(1, d), full(1, d), full(1, d)] + [_const_spec(a.shape) for a in consts]
                 + [full(SUBLANES, r_w), full(1, r_w)],
        out_specs=[full(tl, d), full(SUBLANES, r_w), full(1, r_w)],
        out_shape=[jax.ShapeDtypeStruct(x.shape, F32),
                   jax.ShapeDtypeStruct((nb, SUBLANES, r_w), F32),
                   jax.ShapeDtypeStruct((nb, 1, r_w), F32)],
        scratch_shapes=[pltpu.VMEM((nb, tl + SUBLANES, r_w), F32)],
        compiler_params=_cparams(("arbitrary",), 48),
        name="rglru_sample",
    )(x, sh, sc, gt, *consts, conv0, h0)


def _pad_rows_front(a, rows):
    return jnp.pad(a, ((0, 0), (rows - a.shape[1], 0), (0, 0)))


def kernel(x_prompt, x_sample, state_a_C, state_a_n, state_a_m, cache_b_k, cache_b_v, state_c_conv, state_c_h,
           c_prompt, c_sample, ffn1_norm, ffn1_w_in, ffn1_w_out, mix_norm, ffn2_norm, ffn2_w_in, ffn2_w_out,
           ada_w, ada_b, ab_w_in, ab_gate_bias, a_out_norm, b_q_norm, b_k_norm, b_rel_bias, ab_w_out,
           c_w_in, c_conv_w, c_conv_b, c_gate_w, c_gate_b, c_lambda, c_w_out):
    nbp, seq, d = x_prompt.shape
    nbs, tdec, _ = x_sample.shape
    depth = ada_w.shape[0]
    n_ada = ada_w.shape[2] // d
    _, _, nh, dh, _ = state_a_C.shape
    _, _, w_band, nhb, dhb = cache_b_k.shape
    aw, bw = nh * dh, nhb * dhb
    ncw = c_conv_w.shape[1]
    assert 2 * dhb == LANES and dh == LANES and w_band % CHUNK == 0 and seq % w_band == 0

    ada = _ada_call(jnp.concatenate([c_prompt, c_sample], axis=0), ada_w, ada_b)
    ada = ada.reshape(depth, nbp + nbs, n_ada, 1, d)
    ada_p = [[ada[l, :nbp, k] for k in range(n_ada)] for l in range(depth)]
    ada_s = [[ada[l, nbp:, k] for k in range(n_ada)] for l in range(depth)]

    tl_p = 512
    xp, xs = x_prompt, x_sample
    outs_p, outs_s = {}, {}
    for l in range(depth):
        ap, as_ = ada_p[l], ada_s[l]
        i = l // 2
        g1 = ffn1_norm[l].reshape(1, d)
        gm = mix_norm[l].reshape(1, d)
        g2 = ffn2_norm[l].reshape(1, d)
        mix_p = None
        xp, xs = _ffn_call(xp, ap[0:3], xs, as_[0:3], g1, ffn1_w_in, ffn1_w_out, l, tl_p)
        if l % 2 == 0:
            w_in = ab_w_in[i]
            wab = jnp.concatenate(
                [w_in[:, :4 * aw], w_in[:, 4 * aw + 2 * nh:], w_in[:, 4 * aw:4 * aw + 2 * nh],
                 jnp.zeros((d, LANES - 2 * nh), F32)], axis=1).astype(BF16)
            gbias = jnp.pad(ab_gate_bias[i], (0, LANES - 2 * nh)).reshape(1, LANES)
            qg = jnp.tile(b_q_norm[i], nhb).reshape(1, bw)
            kg = jnp.tile(b_k_norm[i], nhb).reshape(1, bw)
            head = jnp.arange(bw) // dhb
            e = (head[:, None] == head[None, :]).astype(BF16)
            woa = ab_w_out[i][:aw].astype(BF16)
            wob = ab_w_out[i][aw:].astype(BF16)
            gout = a_out_norm[i]
            bias = _relbias_call(b_rel_bias[i], w_band)
            bias2 = bias.reshape(nhb // 2, 2 * CHUNK, w_band + CHUNK)

            ua_s, gts_s, qn_s, kn, vn = _proj_sample_call(
                xs, as_[3], as_[4], gm, wab, gbias, qg, kg, e, nh=nh, dhb=dhb)
            sample_mix = (ua_s, gts_s, state_a_C[i],
                          jnp.broadcast_to(state_a_n[i][..., None], (nbs, nh, dh, LANES)),
                          jnp.broadcast_to(state_a_m[i][..., None], (nbs, nh, LANES)), gout,
                          qn_s, kn, vn, cache_b_k[i].reshape(nbs, w_band, bw), cache_b_v[i].reshape(nbs, w_band, bw),
                          bias[:, :tdec, :w_band + tdec].reshape(nhb // 2, 2 * tdec, w_band + tdec))
            ride = nbs == nbp * (seq // w_band)

            (ua, gts, qn, kpad, vpad, klast, vlast), rode = _proj_prompt_call(
                xp, ap[3], ap[4], gm, wab, gbias, qg, kg, e, nh=nh, dhb=dhb, w=w_band,
                rider=sample_mix if ride else None)
            zc = jnp.zeros((nbp, nh, dh, dh), F32)
            ha, c1, n1, m1, hb = _mixer_prompt_call(ua, gts, zc, zc, zc[:, :, 0], gout, qn, kpad, vpad, bias2,
                                                    w=w_band, nck=4)
            mix_p = (ha, hb, ap[5], woa, wob)
            outs_p.setdefault('a_C', []).append(c1)
            outs_p.setdefault('a_n', []).append(n1)
            outs_p.setdefault('a_m', []).append(m1[:, :, 0])
            outs_p.setdefault('b_k', []).append(klast.reshape(nbp, w_band, nhb, dhb))
            outs_p.setdefault('b_v', []).append(vlast.reshape(nbp, w_band, nhb, dhb))

            if ride:
                ha, c1, n1, m1, hb = rode
            else:
                ha, c1, n1, m1 = _mlstm_call(*sample_mix[:6], tq=tdec, seg=tdec)
                hb = _band_sample_call(*sample_mix[6:])
            xs = _mixout_call(xs, ha, hb, as_[5], woa, wob, nbs, tdec)
            outs_s.setdefault('a_C', []).append(c1)
            outs_s.setdefault('a_n', []).append(n1)
            outs_s.setdefault('a_m', []).append(m1[:, :, 0])
            outs_s.setdefault('b_k', []).append(kn.reshape(nbs, tdec, nhb, dhb))
            outs_s.setdefault('b_v', []).append(vn.reshape(nbs, tdec, nhb, dhb))
        else:
            r_w = c_lambda.shape[1]
            consts = (gm, c_w_in[i].astype(BF16), c_conv_w[i], c_conv_b[i].reshape(1, r_w), c_gate_w[i].astype(BF16),
                      c_gate_b[i][0].reshape(1, r_w), c_gate_b[i][1].reshape(1, r_w), c_lambda[i].reshape(1, r_w),
                      c_w_out[i].astype(BF16))
            xp, conv_p, h_p = _rglru_prompt_call(
                xp, ap[3], ap[4], ap[5], *consts,
                jnp.zeros((nbp, SUBLANES, r_w), F32), jnp.zeros((nbp, 1, r_w), F32), tq=512)
            xs, conv_s, h_s = _rglru_sample_call(
                xs, as_[3], as_[4], as_[5], *consts,
                _pad_rows_front(state_c_conv[i], SUBLANES), state_c_h[i][:, None, :])
            outs_p.setdefault('c_conv', []).append(conv_p[:, SUBLANES - (ncw - 1):])
            outs_p.setdefault('c_h', []).append(h_p[:, 0])
            outs_s.setdefault('c_conv', []).append(conv_s[:, SUBLANES - (ncw - 1):])
            outs_s.setdefault('c_h', []).append(h_s[:, 0])
        xp, xs = _ffn_call(xp, ap[6:9], xs, as_[6:9], g2, ffn2_w_in, ffn2_w_out, l, tl_p, mix=mix_p)

    names = ('a_C', 'a_n', 'a_m', 'b_k', 'b_v', 'c_conv', 'c_h')
    ps = [jnp.stack(outs_p[n], axis=0) for n in names]
    ss = [jnp.stack(outs_s[n], axis=0) for n in names]
    return (xp, xs, *ps, *ss)
```

```python
import functools

import jax
import jax.numpy as jnp
from jax import lax
from jax.experimental import pallas as pl
from jax.experimental.pallas import tpu as pltpu

F32 = jnp.float32
BF16 = jnp.bfloat16

EPS = 1e-6
CHUNK = 64
LRU_C = 8.0
LANES = 128
SUBLANES = 8
MIB = 1024 * 1024
FFN_ROWS = 512
MIXER_CHUNKS = 4
RGLRU_ROWS = 512


def _cparams(semantics, vmem_mib):
    return pltpu.CompilerParams(dimension_semantics=semantics, vmem_limit_bytes=vmem_mib * MIB)


def _const_spec(shape):
    nd = len(shape)
    return pl.BlockSpec(shape, lambda *_: (0,) * nd, pipeline_mode=pl.Buffered(1))


def _dot(a, b):
    return jnp.dot(a, b, preferred_element_type=F32)


def _dot_nt(a, b):
    return lax.dot_general(a, b, (((1,), (1,)), ((), ())), preferred_element_type=F32)


def _dot_tn(a, b):
    return lax.dot_general(a, b, (((0,), (0,)), ((), ())), preferred_element_type=F32)


def _rms_mod(x, g, shift, scale):
    ms = jnp.mean(x * x, axis=-1, keepdims=True)
    return (x * lax.rsqrt(ms + EPS)) * (g * (1.0 + scale)) + shift


def _softplus(x):
    return jnp.maximum(x, 0.0) + jnp.log1p(jnp.exp(-jnp.abs(x)))


def _gelu_tanh(x):
    c = 0.7978845608028654
    inner = x * ((x * x) * (c * 0.044715) + c)
    return x * (0.5 * jnp.tanh(inner) + 0.5)


def _ada_kernel(c_ref, w_ref, b_ref, o_ref):
    c = c_ref[...].astype(BF16)
    w = w_ref[0].astype(BF16)
    o_ref[0] = _dot(c, w) + b_ref[0]


def _ada_call(c_all, ada_w, ada_b):
    depth, d, n = ada_w.shape
    m = c_all.shape[0]
    tn = d
    return pl.pallas_call(
        _ada_kernel,
        grid=(depth, n // tn),
        in_specs=[pl.BlockSpec((m, d), lambda l, j: (0, 0)),
                  pl.BlockSpec((1, d, tn), lambda l, j: (l, 0, j)),
                  pl.BlockSpec((1, 1, tn), lambda l, j: (l, 0, j))],
        out_specs=pl.BlockSpec((1, m, tn), lambda l, j: (l, 0, j)),
        out_shape=jax.ShapeDtypeStruct((depth, m, n), F32),
        compiler_params=_cparams(("arbitrary", "arbitrary"), 32),
        name="ada_proj",
    )(c_all, ada_w, ada_b.reshape(depth, 1, n))


FFN_TF = 256


def _ffn_kernel(*refs, mixed, n_prompt):
    if mixed:
        ha_ref, hb_ref, gm_ref, woa_ref, wob_ref = refs[:5]
        refs = refs[5:]
    x_ref, sh_ref, sc_ref, gt_ref, xs_ref, adas_ref, g_ref, win_ref, wo_ref, o_ref, os_ref, act_scr = refs
    is_sample = pl.program_id(0) == n_prompt
    _, tl, d = x_ref.shape
    nbs = adas_ref.shape[1]
    dff = wo_ref.shape[0]

    def half_step(x, shift, scale, gate_vec):
        h = _rms_mod(x, g_ref[...], shift, scale).astype(BF16)
        for c0 in range(0, dff, FFN_TF):
            gate = _dot(h, win_ref[:, c0:c0 + FFN_TF].astype(BF16))
            up = _dot(h, win_ref[:, dff + c0:dff + c0 + FFN_TF].astype(BF16))
            act_scr[:, c0:c0 + FFN_TF] = ((gate * jax.nn.sigmoid(gate)) * up).astype(BF16)
        y = _dot(act_scr[...], wo_ref[...].astype(BF16))
        return x + (0.5 * gate_vec) * y

    @pl.when(jnp.logical_not(is_sample))
    def _():
        x = x_ref[0]
        if mixed:
            x = x + gm_ref[0] * (_dot(ha_ref[0], woa_ref[...]) + _dot(hb_ref[0], wob_ref[...]))
        o_ref[0] = half_step(x, sh_ref[0], sc_ref[0], gt_ref[0])

    @pl.when(is_sample)
    def _():
        def rows(k):
            return jnp.broadcast_to(adas_ref[k][:, None, :], (nbs, tl // nbs, d)).reshape(tl, d)
        os_ref[...] = half_step(xs_ref[...], rows(0), rows(1), rows(2))


def _ffn_call(xp, ada_p, xs, ada_s, g, w_in, w_out, layer, tl, mix=None):
    nb, length, d = xp.shape
    nbs, tdec, _ = xs.shape
    dff = w_out.shape[1]
    assert dff % FFN_TF == 0 and nbs * tdec == tl
    nt = length // tl
    n_prompt = nb * nt
    cur = lambda i: jnp.minimum(i, n_prompt - 1)
    tile = lambda n: pl.BlockSpec((1, tl, n), lambda i: (cur(i) // nt, cur(i) % nt, 0))
    ada_spec = pl.BlockSpec((1, 1, d), lambda i: (cur(i) // nt, 0, 0))
    once = lambda shape: pl.BlockSpec(shape, lambda i: (0,) * len(shape), pipeline_mode=pl.Buffered(1))
    layer_spec = lambda shape: pl.BlockSpec((None,) + shape, lambda i: (layer, 0, 0), pipeline_mode=pl.Buffered(1))
    mix_args, mix_specs = [], []
    if mix is not None:
        ha, hb, gm, woa, wob = mix
        mix_args = [ha, hb, gm, woa, wob]
        mix_specs = [tile(ha.shape[-1]), tile(hb.shape[-1]), ada_spec, once(woa.shape), once(wob.shape)]
    adas = jnp.stack([a[:, 0] for a in ada_s])
    op, os = pl.pallas_call(
        functools.partial(_ffn_kernel, mixed=mix is not None, n_prompt=n_prompt),
        grid=(n_prompt + 1,),
        in_specs=mix_specs + [tile(d), ada_spec, ada_spec, ada_spec, once((tl, d)), once(adas.shape), once((1, d)),
                              layer_spec(w_in.shape[1:]), layer_spec(w_out.shape[1:])],
        out_specs=[tile(d), pl.BlockSpec((tl, d), lambda i: (0, 0))],
        out_shape=[jax.ShapeDtypeStruct(xp.shape, F32), jax.ShapeDtypeStruct((tl, d), F32)],
        scratch_shapes=[pltpu.VMEM((tl, dff), BF16)],
        compiler_params=_cparams(("arbitrary",), 60),
        name="ffn",
    )(*mix_args, xp, *ada_p, xs.reshape(tl, d), adas, g, w_in, w_out)
    return op, os.reshape(nbs, tdec, d)


def _head_rmsnorm(q, e, g, dhb):
    ss = _dot((q * q).astype(BF16), e)
    return q * lax.rsqrt(ss * (1.0 / dhb) + EPS) * g


def _proj_body(x, sh, sc, g, w_ref, gb_ref, qg_ref, kg_ref, e_ref, *, nh, bw, dhb):
    bb, tl, d = x.shape
    na = w_ref.shape[1] - 3 * bw - LANES
    h = _rms_mod(x, g, sh, sc).reshape(bb * tl, d).astype(BF16)
    ua = _dot(h, w_ref[:, :na])
    gg = _dot(h, w_ref[:, na + 3 * bw:]) + gb_ref[...]
    lane = lax.broadcasted_iota(jnp.int32, gg.shape, 1)
    gates = jnp.where(lane < nh, gg, -_softplus(-gg))
    ub = _dot(h, w_ref[:, na:na + 3 * bw])
    e = e_ref[...]
    qn = _head_rmsnorm(ub[:, :bw], e, qg_ref[...], dhb) * (dhb ** -0.5)
    kn = _head_rmsnorm(ub[:, bw:2 * bw], e, kg_ref[...], dhb)
    vb = ub[:, 2 * bw:]
    return ua, gates, qn, kn, vb


N_PROJ_IN, N_PROJ_OUT, N_RIDER_IN, N_RIDER_OUT = 9, 7, 12, 5


def _proj_prompt_kernel(*refs, nh, bw, dhb, rider):
    x_ref, sh_ref, sc_ref, g_ref, w_ref, gb_ref, qg_ref, kg_ref, e_ref = refs[:N_PROJ_IN]
    refs = refs[N_PROJ_IN:]
    if rider:
        (uas_ref, gs_ref, c0_ref, n0_ref, m0_ref, go_ref, qs_ref, kns_ref, vns_ref, ck_ref, cv_ref,
         bs_ref) = refs[:N_RIDER_IN]
        refs = refs[N_RIDER_IN:]
    ua_ref, gt_ref, qn_ref, kp_ref, vp_ref, kl_ref, vl_ref = refs[:N_PROJ_OUT]
    refs = refs[N_PROJ_OUT:]
    if rider:
        has_ref, cs_ref, ns_ref, ms_ref, hbs_ref = refs[:N_RIDER_OUT]
        nrep_scr, rep_scr, s_scr, pv_scr, kv_scr, qc_scr = refs[N_RIDER_OUT:]
    t = pl.program_id(1)
    nt = pl.num_programs(1)

    @pl.when(t == 0)
    def _():
        kp_ref[...] = jnp.zeros_like(kp_ref)
        vp_ref[...] = jnp.zeros_like(vp_ref)

    @pl.when(t > 0)
    def _():
        ua, gates, qn, kn, vb = _proj_body(x_ref[...], sh_ref[...], sc_ref[...], g_ref[...], w_ref,
                                           gb_ref, qg_ref, kg_ref, e_ref, nh=nh, bw=bw, dhb=dhb)
        ua_ref[0] = ua
        gt_ref[0] = gates
        qn_ref[0] = qn.astype(BF16)
        kp_ref[0] = kn.astype(BF16)
        vp_ref[0] = vb.astype(BF16)
        if rider:
            cs_ref[...] = c0_ref[...]
            nrep_scr[...] = n0_ref[0]
            ms_ref[...] = m0_ref[...]
            for steps in _mlstm_tile(uas_ref, gs_ref, go_ref, has_ref, cs_ref, ns_ref, ms_ref, nrep_scr, rep_scr,
                                     s_scr, pv_scr, kv_scr, qc_scr, seg=uas_ref.shape[1], nh=nh, dh=go_ref.shape[1]):
                _run_passes(steps)
            _band_sample_kernel(qs_ref, kns_ref, vns_ref, ck_ref, cv_ref, bs_ref, hbs_ref, npair=bs_ref.shape[0])

        @pl.when(t == nt - 1)
        def _():
            kl_ref[0] = kn
            vl_ref[0] = vb


def _proj_prompt_call(x, sh, sc, g, wab, gbias, qg, kg, e, *, nh, dhb, w, rider=None):
    nb, length, d = x.shape
    tl = w
    nt = length // tl
    bw = e.shape[0]
    aw4 = wab.shape[1] - 3 * bw - LANES
    prev = lambda b, t: (b, jnp.maximum(t - 1, 0), 0)
    ada_spec = pl.BlockSpec((1, 1, d), lambda b, t: (b, 0, 0))
    in_specs = [pl.BlockSpec((1, tl, d), prev), ada_spec, ada_spec, _const_spec((1, d)),
                _const_spec(wab.shape), _const_spec(gbias.shape),
                _const_spec(qg.shape), _const_spec(kg.shape), _const_spec(e.shape)]
    out_specs = [pl.BlockSpec((1, tl, aw4), prev),
                 pl.BlockSpec((1, tl, LANES), prev),
                 pl.BlockSpec((1, tl, bw), prev),
                 pl.BlockSpec((1, tl, bw), lambda b, t: (b, t, 0)),
                 pl.BlockSpec((1, tl, bw), lambda b, t: (b, t, 0)),
                 pl.BlockSpec((1, tl, bw), lambda b, t: (b, 0, 0)),
                 pl.BlockSpec((1, tl, bw), lambda b, t: (b, 0, 0))]
    out_shape = [jax.ShapeDtypeStruct((nb, length, aw4), F32),
                 jax.ShapeDtypeStruct((nb, length, LANES), F32),
                 jax.ShapeDtypeStruct((nb, length, bw), BF16),
                 jax.ShapeDtypeStruct((nb, length + w, bw), BF16),
                 jax.ShapeDtypeStruct((nb, length + w, bw), BF16),
                 jax.ShapeDtypeStruct((nb, w, bw), F32),
                 jax.ShapeDtypeStruct((nb, w, bw), F32)]
    args = [x, sh, sc, g, wab, gbias, qg, kg, e]
    scratch = []
    if rider is not None:
        ua_s, g_s, c0, n0rep, m0rep, gout, q_s, kn_s, vn_s, ck, cv, bias2_s = rider
        nbs, tdec, _ = ua_s.shape
        _, nha, dh, _ = c0.shape
        assert nbs == nb * nt and nha == nh
        per = lambda a: pl.BlockSpec((1,) + a.shape[1:],
                                     lambda b, t: (b * nt + jnp.maximum(t - 1, 0),) + (0,) * (a.ndim - 1))
        streams = [ua_s, g_s, c0, n0rep, m0rep]
        in_specs += [per(a) for a in streams] + [_const_spec(gout.shape)]
        in_specs += [per(a) for a in (q_s, kn_s, vn_s, ck, cv)] + [_const_spec(bias2_s.shape)]
        args += streams + [gout, q_s, kn_s, vn_s, ck, cv, bias2_s]
        r_shapes = [jax.ShapeDtypeStruct((nbs, tdec, nh * dh), BF16), jax.ShapeDtypeStruct((nbs, nh, dh, dh), F32),
                    jax.ShapeDtypeStruct((nbs, nh, dh), F32), jax.ShapeDtypeStruct((nbs, nh, LANES), F32),
                    jax.ShapeDtypeStruct((nbs, tdec, bw), BF16)]
        out_shape += r_shapes
        out_specs += [per(a) for a in r_shapes]
        scratch = [pltpu.VMEM((nh, dh, LANES), F32), pltpu.VMEM((3 * nh, tdec, LANES), F32),
                   pltpu.VMEM((nh, tdec, tdec), F32), pltpu.VMEM((nh, tdec, dh + LANES), F32),
                   pltpu.VMEM((nh, dh, dh + LANES), F32), pltpu.VMEM((nh, tdec, dh + LANES), F32)]
    outs = pl.pallas_call(
        functools.partial(_proj_prompt_kernel, nh=nh, bw=bw, dhb=dhb, rider=rider is not None),
        grid=(nb, nt + 1),
        in_specs=in_specs,
        out_specs=out_specs,
        out_shape=out_shape,
        scratch_shapes=scratch,
        compiler_params=_cparams(("arbitrary", "arbitrary"), 48),
        name="proj_prompt",
    )(*args)
    return outs[:N_PROJ_OUT], outs[N_PROJ_OUT:]


def _proj_sample_kernel(x_ref, sh_ref, sc_ref, g_ref, w_ref, gb_ref, qg_ref, kg_ref, e_ref,
                        ua_ref, gt_ref, qn_ref, kn_ref, vb_ref, *, nh, bw, dhb):
    bb, tl, _ = x_ref.shape
    ua, gates, qn, kn, vb = _proj_body(x_ref[...], sh_ref[...], sc_ref[...], g_ref[...], w_ref,
                                       gb_ref, qg_ref, kg_ref, e_ref, nh=nh, bw=bw, dhb=dhb)
    ua_ref[...] = ua.reshape(bb, tl, -1)
    gt_ref[...] = gates.reshape(bb, tl, -1)
    qn_ref[...] = qn.reshape(bb, tl, -1).astype(BF16)
    kn_ref[...] = kn.reshape(bb, tl, -1)
    vb_ref[...] = vb.reshape(bb, tl, -1)


def _proj_sample_call(x, sh, sc, g, wab, gbias, qg, kg, e, *, nh, dhb):
    nb, length, d = x.shape
    bw = e.shape[0]
    aw4 = wab.shape[1] - 3 * bw - LANES
    full = lambda n: pl.BlockSpec((nb, length, n), lambda i: (0, 0, 0))
    ada_spec = pl.BlockSpec((nb, 1, d), lambda i: (0, 0, 0))
    return pl.pallas_call(
        functools.partial(_proj_sample_kernel, nh=nh, bw=bw, dhb=dhb),
        grid=(1,),
        in_specs=[full(d), ada_spec, ada_spec, _const_spec((1, d)),
                  _const_spec(wab.shape), _const_spec(gbias.shape),
                  _const_spec(qg.shape), _const_spec(kg.shape), _const_spec(e.shape)],
        out_specs=[full(aw4), full(LANES), full(bw), full(bw), full(bw)],
        out_shape=[jax.ShapeDtypeStruct((nb, length, aw4), F32),
                   jax.ShapeDtypeStruct((nb, length, LANES), F32),
                   jax.ShapeDtypeStruct((nb, length, bw), BF16),
                   jax.ShapeDtypeStruct((nb, length, bw), F32),
                   jax.ShapeDtypeStruct((nb, length, bw), F32)],
        compiler_params=_cparams(("arbitrary",), 48),
        name="proj_sample",
    )(x, sh, sc, g, wab, gbias, qg, kg, e)


def _mlstm_init(c0_ref, n0_ref, m0_ref, c_ref, m_ref, nrep_scr):
    @pl.when(pl.program_id(1) == 0)
    def _():
        c_ref[...] = c0_ref[...]
        nrep_scr[...] = n0_ref[0]
        m_ref[...] = m0_ref[...]


def _mlstm_kernel(ua_ref, g_ref, c0_ref, n0_ref, m0_ref, go_ref, ha_ref, c_ref, n_ref, m_ref,
                  nrep_scr, rep_scr, s_scr, pv_scr, kv_scr, qc_scr, *, seg, nh, dh):
    _mlstm_init(c0_ref, n0_ref, m0_ref, c_ref, m_ref, nrep_scr)
    for steps in _mlstm_tile(ua_ref, g_ref, go_ref, ha_ref, c_ref, n_ref, m_ref,
                             nrep_scr, rep_scr, s_scr, pv_scr, kv_scr, qc_scr, seg=seg, nh=nh, dh=dh):
        _run_passes(steps)


def _mlstm_tile(ua_ref, g_ref, go_ref, ha_ref, c_ref, n_ref, m_ref,
                nrep_scr, rep_scr, s_scr, pv_scr, kv_scr, qc_scr, *, seg, nh, dh):
    tq = ua_ref.shape[1]
    nck = tq // seg
    aw = nh * dh
    gates = g_ref[0]
    pos = lax.broadcasted_iota(jnp.int32, gates.shape, 0) % seg
    bt = gates
    s = 1
    while s < seg:
        bt = bt + jnp.where(pos >= s, pltpu.roll(bt, s, 0), 0.0)
        s *= 2
    dmb = pltpu.roll(gates, nh, 1) - bt
    pm = dmb
    s = 1
    while s < seg:
        pm = jnp.maximum(pm, jnp.where(pos >= s, pltpu.roll(pm, s, 0), -jnp.inf))
        s *= 2
    if tq % LANES:
        dsq = jnp.concatenate([dmb, jnp.zeros((LANES - tq % LANES, LANES), F32)], axis=0)
    else:
        dsq = dmb
    dtr = dsq.T
    ri = lax.broadcasted_iota(jnp.int32, (seg, seg), 0)
    ci = lax.broadcasted_iota(jnp.int32, (seg, seg), 1)
    causal = ri >= ci
    ones = jnp.ones((seg, LANES), BF16)
    ones_dh = jnp.ones((dh, LANES), BF16)
    for h in range(nh):
        ln = slice(nh + h, nh + h + 1)
        for j, arr in enumerate((bt, dmb, pm)):
            rep_scr[3 * h + j] = jnp.broadcast_to(arr[:, ln], (tq, LANES))

    def cols(jc, h):
        rows = slice(jc * seg, (jc + 1) * seg)
        return rows, rep_scr[3 * h, rows, :], rep_scr[3 * h + 1, rows, :], rep_scr[3 * h + 2, rows, :]

    def last(jc, h, j):
        r = (jc + 1) * seg - 1
        return rep_scr[3 * h + j, r:r + 1, :]

    groups = [(jc, h) for jc in range(nck) for h in range(nh)]

    def qkv(jc, h, which):
        rows = slice(jc * seg, (jc + 1) * seg)
        return ua_ref[0, rows, which * aw + h * dh:which * aw + (h + 1) * dh]

    state = []
    before = []

    def score(g, jc, h):
        k = qkv(jc, h, 1) * (dh ** -0.5)
        s_scr[g] = _dot_nt(qkv(jc, h, 0).astype(BF16), k.astype(BF16))

    def local(g, jc, h):
        rows, _, _, p_col = cols(jc, h)
        d_row = dtr[nh + h:nh + h + 1, rows]
        dloc = jnp.exp(jnp.where(causal, d_row - p_col[:, :seg], -jnp.inf))
        sl = (s_scr[g] * dloc).astype(BF16)
        v = qkv(jc, h, 2).astype(BF16)
        pv_scr[g] = _dot(sl, jnp.concatenate([v, ones], axis=1))

    def contrib(g, jc, h):
        _, _, d_col, _ = cols(jc, h)
        kw = (qkv(jc, h, 1) * (dh ** -0.5)) * jnp.exp(d_col - last(jc, h, 2))
        vx = jnp.concatenate([qkv(jc, h, 2).astype(BF16), ones], axis=1)
        kv_scr[g] = _dot_tn(kw.astype(BF16), vx)

    def carry(g, jc, h):
        if not state:
            state.extend((c_ref[0, hh], nrep_scr[hh], m_ref[0, hh:hh + 1, :]) for hh in range(nh))
        c_mem, n_rep, m = state[h]
        cn = jnp.concatenate([c_mem.astype(BF16), n_rep.astype(BF16)], axis=1)
        qc_scr[g] = _dot(qkv(jc, h, 0).astype(BF16), cn)
        before.append(m)
        p_last = last(jc, h, 2)
        mml = jnp.maximum(m, p_last)
        w_prev = jnp.exp(m - mml)
        f_new = jnp.exp(p_last - mml)
        kvx = kv_scr[g]
        state[h] = (w_prev * c_mem + f_new * kvx[:, :dh],
                    w_prev * n_rep + f_new * kvx[:, dh:],
                    last(jc, h, 0) + mml)
        if g == len(groups) - 1:
            for hh in range(nh):
                c_ref[0, hh], nrep_scr[hh], m_ref[0, hh:hh + 1, :] = state[hh]
                n_ref[0, hh:hh + 1, :] = state[hh][1].T[0:1, :]

    def combine(g, jc, h):
        rows, b_col, _, p_col = cols(jc, h)
        m = before[g]
        mm = jnp.maximum(m, p_col)
        iw = jnp.exp(m - mm)
        fl = jnp.exp(p_col - mm)
        pv = pv_scr[g]
        qc = qc_scr[g]
        num = iw * qc[:, :dh] + fl * pv[:, :dh]
        den = iw * qc[:, dh:] + fl * pv[:, dh:]
        hh = num / jnp.maximum(jnp.abs(den), jnp.exp(-(b_col + mm)))
        h2 = hh * hh
        hi = h2.astype(BF16)
        lo = (h2 - hi.astype(F32)).astype(BF16)
        ms = (_dot(hi, ones_dh) + _dot(lo, ones_dh)) * (1.0 / dh)
        hn = (hh * lax.rsqrt(ms + EPS) * go_ref[h:h + 1, :]) * jax.nn.sigmoid(qkv(jc, h, 3))
        ha_ref[0, rows, h * dh:(h + 1) * dh] = hn.astype(BF16)

    return [[functools.partial(fn, g, jc, h) for g, (jc, h) in enumerate(groups)]
            for fn in (score, local, contrib, carry, combine)]


def _run_passes(*pass_lists):
    for steps in zip(*[p + [None] * (max(map(len, pass_lists)) - len(p)) for p in pass_lists]):
        for step in steps:
            if step is not None:
                step()


def _mlstm_call(ua, gates, c0, n0rep, m0rep, gout, *, tq, seg):
    nb, length, aw4 = ua.shape
    _, nh, dh, _ = c0.shape
    assert dh == LANES
    groups = (tq // seg) * nh
    st = lambda shape: pl.BlockSpec((1,) + shape, lambda b, t: (b,) + (0,) * len(shape))
    tile = lambda n: pl.BlockSpec((1, tq, n), lambda b, t: (b, t, 0))
    return pl.pallas_call(
        functools.partial(_mlstm_kernel, seg=seg, nh=nh, dh=dh),
        grid=(nb, length // tq),
        in_specs=[tile(aw4), tile(LANES), st((nh, dh, dh)), st((nh, dh, LANES)), st((nh, LANES)),
                  _const_spec(gout.shape)],
        out_specs=[tile(nh * dh), st((nh, dh, dh)), st((nh, dh)), st((nh, LANES))],
        out_shape=[jax.ShapeDtypeStruct((nb, length, nh * dh), BF16),
                   jax.ShapeDtypeStruct((nb, nh, dh, dh), F32),
                   jax.ShapeDtypeStruct((nb, nh, dh), F32),
                   jax.ShapeDtypeStruct((nb, nh, LANES), F32)],
        scratch_shapes=[pltpu.VMEM((nh, dh, LANES), F32), pltpu.VMEM((3 * nh, tq, LANES), F32),
                        pltpu.VMEM((groups, seg, seg), F32), pltpu.VMEM((groups, seg, dh + LANES), F32),
                        pltpu.VMEM((groups, dh, dh + LANES), F32), pltpu.VMEM((groups, seg, dh + LANES), F32)],
        compiler_params=_cparams(("arbitrary", "arbitrary"), 32),
        name="mlstm",
    )(ua, gates, c0, n0rep, m0rep, gout)


def _relbias_kernel(b0_ref, o_ref):
    nhb, nq, nk = o_ref.shape
    for h in range(nhb):
        x = jnp.broadcast_to(b0_ref[h:h + 1, :], (nq, b0_ref.shape[1]))
        o_ref[h] = pltpu.roll(x, 0, 1, stride=1, stride_axis=0)[:, :nk]


def _relbias_call(table, w):
    nhb = table.shape[0]
    max_rel = (table.shape[1] - 1) // 2
    assert CHUNK - 1 <= max_rel <= w
    first = jnp.broadcast_to(table[:, :1], (nhb, w - max_rel))
    wrap = jnp.broadcast_to(table[:, :1], (nhb, CHUNK))
    b0 = jnp.concatenate([first, table[:, :max_rel + CHUNK], wrap], axis=1).astype(F32)
    return pl.pallas_call(
        _relbias_kernel,
        out_shape=jax.ShapeDtypeStruct((nhb, CHUNK, w + CHUNK), F32),
        name="rel_bias",
    )(b0)


def _band_tile(q_ref, k_ref, v_ref, bias_ref, o_ref, s_scr, m_scr, e_scr, *, masked, npair, w, nck):
    c4 = pl.program_id(1)
    nk = w + CHUNK
    lane = lax.broadcasted_iota(jnp.int32, (CHUNK, LANES), 1)
    low = lane < LANES // 2
    zero = jnp.zeros((CHUNK, LANES), BF16)
    ones = jnp.ones((nk, LANES), BF16)

    starts = [pl.multiple_of((c4 * nck + jc) * CHUNK, CHUNK) for jc in range(nck)]
    groups = [(jc, p) for jc in range(nck) for p in range(npair)]

    def score(g, jc, p):
        sl = slice(p * LANES, (p + 1) * LANES)
        qp = q_ref[0, jc * CHUNK:(jc + 1) * CHUNK, sl]
        q2 = jnp.concatenate([jnp.where(low, qp, zero), jnp.where(low, zero, qp)], axis=0)
        s = _dot_nt(q2, k_ref[0, pl.ds(starts[jc], nk), sl]) + bias_ref[p]
        if masked:
            col = lax.broadcasted_iota(jnp.int32, s.shape, 1)
            s = jnp.where(col + starts[jc] >= w, s, -jnp.inf)
        s_scr[g] = s
        m_scr[g] = jnp.max(s, axis=-1, keepdims=True)

    def expo(g, jc, p):
        e_scr[g] = jnp.exp(s_scr[g] - m_scr[g]).astype(BF16)

    def value(g, jc, p):
        sl = slice(p * LANES, (p + 1) * LANES)
        vx = jnp.concatenate([v_ref[0, pl.ds(starts[jc], nk), sl], ones], axis=1)
        r = _dot(e_scr[g], vx)
        o_lo = r[:CHUNK, :LANES] / r[:CHUNK, LANES:]
        o_hi = r[CHUNK:, :LANES] / r[CHUNK:, LANES:]
        o_ref[0, jc * CHUNK:(jc + 1) * CHUNK, sl] = jnp.where(low, o_lo, o_hi).astype(BF16)

    return [[functools.partial(fn, g, jc, p) for g, (jc, p) in enumerate(groups)] for fn in (score, expo, value)]


def _mixer_prompt_kernel(ua_ref, g_ref, c0_ref, n0_ref, m0_ref, go_ref, q_ref, k_ref, v_ref, bias_ref,
                         ha_ref, c_ref, n_ref, m_ref, hb_ref,
                         nrep_scr, rep_scr, sa_scr, pv_scr, kv_scr, qc_scr, sb_scr, mb_scr, eb_scr,
                         *, seg, nh, dh, npair, w, nck):
    _mlstm_init(c0_ref, n0_ref, m0_ref, c_ref, m_ref, nrep_scr)
    first_full = w // (CHUNK * nck)

    def tile(masked):
        b_score, b_exp, b_value = _band_tile(q_ref, k_ref, v_ref, bias_ref, hb_ref, sb_scr, mb_scr, eb_scr,
                                             masked=masked, npair=npair, w=w, nck=nck)
        a_score, a_local, a_contrib, a_carry, a_combine = _mlstm_tile(
            ua_ref, g_ref, go_ref, ha_ref, c_ref, n_ref, m_ref,
            nrep_scr, rep_scr, sa_scr, pv_scr, kv_scr, qc_scr, seg=seg, nh=nh, dh=dh)
        _run_passes(a_score, b_score)
        _run_passes(a_local, b_exp)
        _run_passes(a_contrib)
        _run_passes(a_carry)
        _run_passes(a_combine, b_value)

    @pl.when(pl.program_id(1) < first_full)
    def _():
        tile(True)

    @pl.when(pl.program_id(1) >= first_full)
    def _():
        tile(False)


def _mixer_prompt_call(ua, gates, c0, n0rep, m0rep, gout, qs, kpad, vpad, bias2, *, w, nck):
    nb, length, aw4 = ua.shape
    _, nh, dh, _ = c0.shape
    bw = qs.shape[2]
    npair = bias2.shape[0]
    lp = kpad.shape[1]
    tq = nck * CHUNK
    nk = w + CHUNK
    ga = nck * nh
    gb = nck * npair
    assert w % tq == 0 and dh == LANES
    st = lambda shape: pl.BlockSpec((1,) + shape, lambda b, t: (b,) + (0,) * len(shape))
    tile = lambda n: pl.BlockSpec((1, tq, n), lambda b, t: (b, t, 0))
    whole = pl.BlockSpec((1, lp, bw), lambda b, t: (b, 0, 0))
    return pl.pallas_call(
        functools.partial(_mixer_prompt_kernel, seg=CHUNK, nh=nh, dh=dh, npair=npair, w=w, nck=nck),
        grid=(nb, length // tq),
        in_specs=[tile(aw4), tile(LANES), st((nh, dh, dh)), st((nh, dh, LANES)), st((nh, LANES)),
                  _const_spec(gout.shape), tile(bw), whole, whole, _const_spec(bias2.shape)],
        out_specs=[tile(nh * dh), st((nh, dh, dh)), st((nh, dh)), st((nh, LANES)), tile(bw)],
        out_shape=[jax.ShapeDtypeStruct((nb, length, nh * dh), BF16),
                   jax.ShapeDtypeStruct((nb, nh, dh, dh), F32),
                   jax.ShapeDtypeStruct((nb, nh, dh), F32),
                   jax.ShapeDtypeStruct((nb, nh, LANES), F32),
                   jax.ShapeDtypeStruct((nb, length, bw), BF16)],
        scratch_shapes=[pltpu.VMEM((nh, dh, LANES), F32), pltpu.VMEM((3 * nh, tq, LANES), F32),
                        pltpu.VMEM((ga, CHUNK, CHUNK), F32), pltpu.VMEM((ga, CHUNK, dh + LANES), F32),
                        pltpu.VMEM((ga, dh, dh + LANES), F32), pltpu.VMEM((ga, CHUNK, dh + LANES), F32),
                        pltpu.VMEM((gb, 2 * CHUNK, nk), F32), pltpu.VMEM((gb, 2 * CHUNK, 1), F32),
                        pltpu.VMEM((gb, 2 * CHUNK, nk), BF16)],
        compiler_params=_cparams(("arbitrary", "arbitrary"), 48),
        name="mixer_prompt",
    )(ua, gates, c0, n0rep, m0rep, gout, qs, kpad, vpad, bias2)


def _band_sample_kernel(q_ref, kn_ref, vn_ref, ck_ref, cv_ref, bias_ref, o_ref, *, npair):
    tq = q_ref.shape[1]
    nk = ck_ref.shape[1] + tq
    lane = lax.broadcasted_iota(jnp.int32, (tq, LANES), 1)
    low = lane < LANES // 2
    zero = jnp.zeros((tq, LANES), BF16)
    ones = jnp.ones((nk, LANES), BF16)
    scores = []
    for p in range(npair):
        sl = slice(p * LANES, (p + 1) * LANES)
        qp = q_ref[0, :, sl]
        q2 = jnp.concatenate([jnp.where(low, qp, zero), jnp.where(low, zero, qp)], axis=0)
        kx = jnp.concatenate([ck_ref[0, :, sl].astype(BF16), kn_ref[0, :, sl].astype(BF16)], axis=0)
        scores.append(_dot_nt(q2, kx) + bias_ref[p])
    probs = [jnp.exp(s - jnp.max(s, axis=-1, keepdims=True)).astype(BF16) for s in scores]
    for p in range(npair):
        sl = slice(p * LANES, (p + 1) * LANES)
        vx = jnp.concatenate([cv_ref[0, :, sl].astype(BF16), vn_ref[0, :, sl].astype(BF16)], axis=0)
        r = _dot(probs[p], jnp.concatenate([vx, ones], axis=1))
        o_lo = r[:tq, :LANES] / r[:tq, LANES:]
        o_hi = r[tq:, :LANES] / r[tq:, LANES:]
        o_ref[0, :, sl] = jnp.where(low, o_lo, o_hi).astype(BF16)


def _band_sample_call(qn, kn, vn, ck, cv, bias2):
    nb, tq, bw = qn.shape
    w = ck.shape[1]
    npair = bias2.shape[0]
    new = pl.BlockSpec((1, tq, bw), lambda b: (b, 0, 0))
    cache = pl.BlockSpec((1, w, bw), lambda b: (b, 0, 0))
    return pl.pallas_call(
        functools.partial(_band_sample_kernel, npair=npair),
        grid=(nb,),
        in_specs=[new, new, new, cache, cache, _const_spec(bias2.shape)],
        out_specs=new,
        out_shape=jax.ShapeDtypeStruct((nb, tq, bw), BF16),
        compiler_params=_cparams(("arbitrary",), 32),
        name="band_sample",
    )(qn, kn, vn, ck, cv, bias2)


def _mixout_kernel(x_ref, ha_ref, hb_ref, gt_ref, woa_ref, wob_ref, o_ref):
    bb, tl, d = x_ref.shape
    ha = ha_ref[...].reshape(bb * tl, -1)
    hb = hb_ref[...].reshape(bb * tl, -1)
    y = _dot(ha, woa_ref[...]) + _dot(hb, wob_ref[...])
    o_ref[...] = x_ref[...] + gt_ref[...] * y.reshape(bb, tl, d)


def _mixout_call(x, ha, hb, gt, woa, wob, bb, tl):
    nb, length, d = x.shape
    tile = lambda n: pl.BlockSpec((bb, tl, n), lambda i, t: (i, t, 0))
    return pl.pallas_call(
        _mixout_kernel,
        grid=(nb // bb, length // tl),
        in_specs=[tile(d), tile(ha.shape[-1]), tile(hb.shape[-1]),
                  pl.BlockSpec((bb, 1, d), lambda i, t: (i, 0, 0)),
                  _const_spec(woa.shape), _const_spec(wob.shape)],
        out_specs=tile(d),
        out_shape=jax.ShapeDtypeStruct(x.shape, F32),
        compiler_params=_cparams(("arbitrary", "arbitrary"), 32),
        name="mix_out",
    )(x, ha, hb, gt, woa, wob)


def _rglru_gates(xc, gw_ref, rb, ib, lam, nblk):
    bwc = xc.shape[1] // nblk
    r_parts, i_parts = [], []
    for n in range(nblk):
        gn = _dot(xc[:, n * bwc:(n + 1) * bwc].astype(BF16), gw_ref[n])
        r_parts.append(gn[:, :bwc])
        i_parts.append(gn[:, bwc:])
    r = jax.nn.sigmoid(jnp.concatenate(r_parts, axis=1) + rb)
    ii = jax.nn.sigmoid(jnp.concatenate(i_parts, axis=1) + ib)
    log_a = r * (-LRU_C * _softplus(-lam))
    a = jnp.exp(log_a)
    th = jnp.tanh(log_a)
    v = -2.0 * th / (1.0 - th)
    root = jnp.where(v > 0.0, v * lax.rsqrt(v), 0.0)
    return a, root * (ii * xc)


def _rglru_prompt_kernel(x_ref, sh_ref, sc_ref, gt_ref, g_ref, win_ref, cw_ref, cb_ref, gw_ref, rb_ref, ib_ref,
                         lam_ref, wout_ref, conv0_ref, h0_ref, o_ref, conv_ref, hl_ref,
                         u_scr, x_scr, xp_scr, a_scr, h_scr, *, nblk, tiles_per_seq, proj_chunks):
    i = pl.program_id(0)
    tq, d = x_ref.shape[1], x_ref.shape[2]
    r_w = lam_ref.shape[1]
    ncw = cw_ref.shape[0]

    @pl.when(i == 0)
    def _():
        u_scr[1] = jnp.zeros(u_scr.shape[1:], F32)
        x_scr[1] = jnp.zeros(x_scr.shape[1:], F32)
        xp_scr[0:SUBLANES, :] = jnp.zeros((SUBLANES, r_w), F32)
        h_scr[...] = jnp.zeros_like(h_scr)

    @pl.when((i >= 1) & ((i - 1) % tiles_per_seq == 0))
    def _():
        xp_scr[0:SUBLANES, :] = conv0_ref[0]
        h_scr[...] = h0_ref[0]

    row8 = lax.broadcasted_iota(jnp.int32, (SUBLANES, r_w), 0)

    def body(slot):
        prev = 1 - slot
        x = x_ref[0]
        x_scr[slot] = x
        hm = _rms_mod(x, g_ref[...], sh_ref[0], sc_ref[0]).astype(BF16)
        wcols = 2 * r_w // proj_chunks

        def project(c):
            cols = slice(c * wcols, (c + 1) * wcols)
            uc = _dot(hm, win_ref[:, cols])
            u_scr[slot, :, cols] = uc
            bits = pltpu.bitcast(uc[0:SUBLANES, 0:LANES], jnp.uint32)
            zero = pltpu.bitcast(lax.shift_right_logical(bits, jnp.uint32(32)), F32)
            return zero[0:1, :]

        def wide(z, n):
            return jnp.tile(z, (1, n // LANES))

        pending = list(range(proj_chunks))

        def issue(n):
            z = None
            for _ in range(n):
                if pending:
                    zc = project(pending.pop(0))
                    z = zc if z is None else z + zc
            return z

        xp_scr[SUBLANES:SUBLANES + tq, :] = u_scr[prev, :, r_w:]
        xc = cb_ref[...]
        for j in range(ncw):
            off = SUBLANES - (ncw - 1 - j)
            xc = xc + xp_scr[off:off + tq, :] * cw_ref[j:j + 1, :]
        conv_ref[0] = xp_scr[tq:tq + SUBLANES, :]
        xp_scr[0:SUBLANES, :] = xp_scr[tq:tq + SUBLANES, :]
        a, upd = _rglru_gates(xc, gw_ref, rb_ref[...] + wide(issue(2), r_w), ib_ref[...], lam_ref[...], nblk)
        h = h_scr[...] + wide(issue(2), r_w)
        groups = tq // SUBLANES
        for gi in range(groups):
            r0 = gi * SUBLANES
            ai = a[r0:r0 + SUBLANES, :]
            bi = upd[r0:r0 + SUBLANES, :]
            s = 1
            while s < SUBLANES:
                m = row8 >= s
                bi = jnp.where(m, ai * pltpu.roll(bi, s, 0) + bi, bi)
                ai = jnp.where(m, ai * pltpu.roll(ai, s, 0), ai)
                s *= 2
            hs = ai * h + bi
            a_scr[r0:r0 + SUBLANES, :] = hs
            h = hs[SUBLANES - 1:SUBLANES, :]
            if gi % (groups // 4) == groups // 4 - 1 and gi != groups - 1:
                h = h + wide(issue(1), r_w)
        h_scr[...] = h
        hl_ref[0] = h
        gate_vec = gt_ref[0] + wide(issue(proj_chunks), d)
        y = _dot((_gelu_tanh(u_scr[prev, :, :r_w]) * a_scr[...]).astype(BF16), wout_ref[...])
        o_ref[0] = x_scr[prev] + gate_vec * y

    @pl.when(i % 2 == 0)
    def _():
        body(0)

    @pl.when(i % 2 == 1)
    def _():
        body(1)


def _rglru_prompt_call(x, sh, sc, gt, g, win, cw, cb, gw, rb, ib, lam, wout, conv0, h0, *, tq):
    nb, length, d = x.shape
    r_w = lam.shape[1]
    nblk = gw.shape[0]
    nt = length // tq
    ntiles = nb * nt
    cur = lambda i: jnp.minimum(i, ntiles - 1)
    prv = lambda i: jnp.maximum(i - 1, 0)
    consts = [g, win, cw, cb, gw, rb, ib, lam, wout]
    ada_cur = pl.BlockSpec((1, 1, d), lambda i: (cur(i) // nt, 0, 0))
    ada_prv = pl.BlockSpec((1, 1, d), lambda i: (prv(i) // nt, 0, 0))
    conv_spec = pl.BlockSpec((1, SUBLANES, r_w), lambda i: (prv(i) // nt, 0, 0))
    h_spec = pl.BlockSpec((1, 1, r_w), lambda i: (prv(i) // nt, 0, 0))
    return pl.pallas_call(
        functools.partial(_rglru_prompt_kernel, nblk=nblk, tiles_per_seq=nt, proj_chunks=8),
        grid=(ntiles + 1,),
        in_specs=[pl.BlockSpec((1, tq, d), lambda i: (cur(i) // nt, cur(i) % nt, 0)), ada_cur, ada_cur, ada_prv]
                 + [_const_spec(a.shape) for a in consts] + [conv_spec, h_spec],
        out_specs=[pl.BlockSpec((1, tq, d), lambda i: (prv(i) // nt, prv(i) % nt, 0)), conv_spec, h_spec],
        out_shape=[jax.ShapeDtypeStruct(x.shape, F32),
                   jax.ShapeDtypeStruct((nb, SUBLANES, r_w), F32),
                   jax.ShapeDtypeStruct((nb, 1, r_w), F32)],
        scratch_shapes=[pltpu.VMEM((2, tq, 2 * r_w), F32), pltpu.VMEM((2, tq, d), F32),
                        pltpu.VMEM((tq + SUBLANES, r_w), F32), pltpu.VMEM((tq, r_w), F32), pltpu.VMEM((1, r_w), F32)],
        compiler_params=_cparams(("arbitrary",), 56),
        name="rglru_prompt",
    )(x, sh, sc, gt, *consts, conv0, h0)


def _rglru_sample_kernel(x_ref, sh_ref, sc_ref, gt_ref, g_ref, win_ref, cw_ref, cb_ref, gw_ref, rb_ref, ib_ref,
                         lam_ref, wout_ref, conv0_ref, h0_ref, o_ref, conv_ref, hl_ref, xp_scr, *, nblk):
    bb, tl, d = x_ref.shape
    tm = bb * tl
    r_w = lam_ref.shape[1]
    ncw = cw_ref.shape[0]
    x = x_ref[...]
    hm = _rms_mod(x, g_ref[...], sh_ref[...], sc_ref[...]).reshape(tm, d).astype(BF16)
    u = _dot(hm, win_ref[...])
    gb = u[:, :r_w]
    xp_scr[:, 0:SUBLANES, :] = conv0_ref[...]
    xp_scr[:, SUBLANES:SUBLANES + tl, :] = u[:, r_w:].reshape(bb, tl, r_w)
    xc = jnp.broadcast_to(cb_ref[...], (bb, tl, r_w))
    for j in range(ncw):
        off = SUBLANES - (ncw - 1 - j)
        xc = xc + xp_scr[:, off:off + tl, :] * cw_ref[j:j + 1, :]
    conv_ref[...] = xp_scr[:, tl:tl + SUBLANES, :]
    a, b = _rglru_gates(xc.reshape(tm, r_w), gw_ref, rb_ref[...], ib_ref[...], lam_ref[...], nblk)
    pos = lax.broadcasted_iota(jnp.int32, (tm, r_w), 0) % tl
    s = 1
    while s < tl:
        m = pos >= s
        b = jnp.where(m, a * pltpu.roll(b, s, 0) + b, b)
        a = jnp.where(m, a * pltpu.roll(a, s, 0), a)
        s *= 2
    hs = a.reshape(bb, tl, r_w) * h0_ref[...] + b.reshape(bb, tl, r_w)
    hl_ref[...] = hs[:, tl - 1:tl, :]
    y = _dot((_gelu_tanh(gb) * hs.reshape(tm, r_w)).astype(BF16), wout_ref[...])
    o_ref[...] = x + gt_ref[...] * y.reshape(bb, tl, d)


def _rglru_sample_call(x, sh, sc, gt, g, win, cw, cb, gw, rb, ib, lam, wout, conv0, h0):
    nb, tl, d = x.shape
    r_w = lam.shape[1]
    nblk = gw.shape[0]
    full = lambda a, b: pl.BlockSpec((nb, a, b), lambda i: (0, 0, 0))
    consts = [g, win, cw, cb, gw, rb, ib, lam, wout]
    return pl.pallas_call(
        functools.partial(_rglru_sample_kernel, nblk=nblk),
        grid=(1,),
        in_specs=[full(tl, d), full(1, d), full(1, d), full(1, d)] + [_const_spec(a.shape) for a in consts]
                 + [full(SUBLANES, r_w), full(1, r_w)],
        out_specs=[full(tl, d), full(SUBLANES, r_w), full(1, r_w)],
        out_shape=[jax.ShapeDtypeStruct(x.shape, F32),
                   jax.ShapeDtypeStruct((nb, SUBLANES, r_w), F32),
                   jax.ShapeDtypeStruct((nb, 1, r_w), F32)],
        scratch_shapes=[pltpu.VMEM((nb, tl + SUBLANES, r_w), F32)],
        compiler_params=_cparams(("arbitrary",), 48),
        name="rglru_sample",
    )(x, sh, sc, gt, *consts, conv0, h0)


def _pad_rows_front(a, rows):
    return jnp.pad(a, ((0, 0), (rows - a.shape[1], 0), (0, 0)))


def kernel(x_prompt, x_sample, state_a_C, state_a_n, state_a_m, cache_b_k, cache_b_v, state_c_conv, state_c_h,
           c_prompt, c_sample, ffn1_norm, ffn1_w_in, ffn1_w_out, mix_norm, ffn2_norm, ffn2_w_in, ffn2_w_out,
           ada_w, ada_b, ab_w_in, ab_gate_bias, a_out_norm, b_q_norm, b_k_norm, b_rel_bias, ab_w_out,
           c_w_in, c_conv_w, c_conv_b, c_gate_w, c_gate_b, c_lambda, c_w_out):
    nbp, seq, d = x_prompt.shape
    nbs, tdec, _ = x_sample.shape
    depth = ada_w.shape[0]
    n_ada = ada_w.shape[2] // d
    _, _, nh, dh, _ = state_a_C.shape
    _, _, w_band, nhb, dhb = cache_b_k.shape
    aw, bw = nh * dh, nhb * dhb
    ncw = c_conv_w.shape[1]
    assert 2 * dhb == LANES and dh == LANES and w_band % CHUNK == 0 and seq % w_band == 0

    ada = _ada_call(jnp.concatenate([c_prompt, c_sample], axis=0), ada_w, ada_b)
    ada = ada.reshape(depth, nbp + nbs, n_ada, 1, d)
    ada_p = [[ada[l, :nbp, k] for k in range(n_ada)] for l in range(depth)]
    ada_s = [[ada[l, nbp:, k] for k in range(n_ada)] for l in range(depth)]

    tl_p = FFN_ROWS
    xp, xs = x_prompt, x_sample
    outs_p, outs_s = {}, {}
    for l in range(depth):
        ap, as_ = ada_p[l], ada_s[l]
        i = l // 2
        g1 = ffn1_norm[l].reshape(1, d)
        gm = mix_norm[l].reshape(1, d)
        g2 = ffn2_norm[l].reshape(1, d)
        mix_p = None
        xp, xs = _ffn_call(xp, ap[0:3], xs, as_[0:3], g1, ffn1_w_in, ffn1_w_out, l, tl_p)
        if l % 2 == 0:
            w_in = ab_w_in[i]
            wab = jnp.concatenate(
                [w_in[:, :4 * aw], w_in[:, 4 * aw + 2 * nh:], w_in[:, 4 * aw:4 * aw + 2 * nh],
                 jnp.zeros((d, LANES - 2 * nh), F32)], axis=1).astype(BF16)
            gbias = jnp.pad(ab_gate_bias[i], (0, LANES - 2 * nh)).reshape(1, LANES)
            qg = jnp.tile(b_q_norm[i], nhb).reshape(1, bw)
            kg = jnp.tile(b_k_norm[i], nhb).reshape(1, bw)
            head = jnp.arange(bw) // dhb
            e = (head[:, None] == head[None, :]).astype(BF16)
            woa = ab_w_out[i][:aw].astype(BF16)
            wob = ab_w_out[i][aw:].astype(BF16)
            gout = a_out_norm[i]
            bias = _relbias_call(b_rel_bias[i], w_band)
            bias2 = bias.reshape(nhb // 2, 2 * CHUNK, w_band + CHUNK)

            ua_s, gts_s, qn_s, kn, vn = _proj_sample_call(
                xs, as_[3], as_[4], gm, wab, gbias, qg, kg, e, nh=nh, dhb=dhb)
            sample_mix = (ua_s, gts_s, state_a_C[i],
                          jnp.broadcast_to(state_a_n[i][..., None], (nbs, nh, dh, LANES)),
                          jnp.broadcast_to(state_a_m[i][..., None], (nbs, nh, LANES)), gout,
                          qn_s, kn, vn, cache_b_k[i].reshape(nbs, w_band, bw), cache_b_v[i].reshape(nbs, w_band, bw),
                          bias[:, :tdec, :w_band + tdec].reshape(nhb // 2, 2 * tdec, w_band + tdec))
            ride = nbs == nbp * (seq // w_band)

            (ua, gts, qn, kpad, vpad, klast, vlast), rode = _proj_prompt_call(
                xp, ap[3], ap[4], gm, wab, gbias, qg, kg, e, nh=nh, dhb=dhb, w=w_band,
                rider=sample_mix if ride else None)
            zc = jnp.zeros((nbp, nh, dh, dh), F32)
            ha, c1, n1, m1, hb = _mixer_prompt_call(ua, gts, zc, zc, zc[:, :, 0], gout, qn, kpad, vpad, bias2,
                                                    w=w_band, nck=MIXER_CHUNKS)
            mix_p = (ha, hb, ap[5], woa, wob)
            outs_p.setdefault('a_C', []).append(c1)
            outs_p.setdefault('a_n', []).append(n1)
            outs_p.setdefault('a_m', []).append(m1[:, :, 0])
            outs_p.setdefault('b_k', []).append(klast.reshape(nbp, w_band, nhb, dhb))
            outs_p.setdefault('b_v', []).append(vlast.reshape(nbp, w_band, nhb, dhb))

            if ride:
                ha, c1, n1, m1, hb = rode
            else:
                ha, c1, n1, m1 = _mlstm_call(*sample_mix[:6], tq=tdec, seg=tdec)
                hb = _band_sample_call(*sample_mix[6:])
            xs = _mixout_call(xs, ha, hb, as_[5], woa, wob, nbs, tdec)
            outs_s.setdefault('a_C', []).append(c1)
            outs_s.setdefault('a_n', []).append(n1)
            outs_s.setdefault('a_m', []).append(m1[:, :, 0])
            outs_s.setdefault('b_k', []).append(kn.reshape(nbs, tdec, nhb, dhb))
            outs_s.setdefault('b_v', []).append(vn.reshape(nbs, tdec, nhb, dhb))
        else:
            r_w = c_lambda.shape[1]
            consts = (gm, c_w_in[i].astype(BF16), c_conv_w[i], c_conv_b[i].reshape(1, r_w), c_gate_w[i].astype(BF16),
                      c_gate_b[i][0].reshape(1, r_w), c_gate_b[i][1].reshape(1, r_w), c_lambda[i].reshape(1, r_w),
                      c_w_out[i].astype(BF16))
            xp, conv_p, h_p = _rglru_prompt_call(
                xp, ap[3], ap[4], ap[5], *consts,
                jnp.zeros((nbp, SUBLANES, r_w), F32), jnp.zeros((nbp, 1, r_w), F32), tq=RGLRU_ROWS)
            xs, conv_s, h_s = _rglru_sample_call(
                xs, as_[3], as_[4], as_[5], *consts,
                _pad_rows_front(state_c_conv[i], SUBLANES), state_c_h[i][:, None, :])
            outs_p.setdefault('c_conv', []).append(conv_p[:, SUBLANES - (ncw - 1):])
            outs_p.setdefault('c_h', []).append(h_p[:, 0])
            outs_s.setdefault('c_conv', []).append(conv_s[:, SUBLANES - (ncw - 1):])
            outs_s.setdefault('c_h', []).append(h_s[:, 0])
        xp, xs = _ffn_call(xp, ap[6:9], xs, as_[6:9], g2, ffn2_w_in, ffn2_w_out, l, tl_p, mix=mix_p)

    names = ('a_C', 'a_n', 'a_m', 'b_k', 'b_v', 'c_conv', 'c_h')
    ps = [jnp.stack(outs_p[n], axis=0) for n in names]
    ss = [jnp.stack(outs_s[n], axis=0) for n in names]
    return (xp, xs, *ps, *ss)
```

```python
import functools

import jax
import jax.numpy as jnp
from jax import lax
from jax.experimental import pallas as pl
from jax.experimental.pallas import tpu as pltpu

F32 = jnp.float32
BF16 = jnp.bfloat16

EPS = 1e-6
CHUNK = 64
LRU_C = 8.0
LANES = 128
SUBLANES = 8
MIB = 1024 * 1024
FFN_ROWS = 512
MIXER_CHUNKS = 4
RGLRU_ROWS = 512


def _cparams(semantics, vmem_mib):
    return pltpu.CompilerParams(dimension_semantics=semantics, vmem_limit_bytes=vmem_mib * MIB)


def _const_spec(shape):
    nd = len(shape)
    return pl.BlockSpec(shape, lambda *_: (0,) * nd, pipeline_mode=pl.Buffered(1))


def _dot(a, b):
    return jnp.dot(a, b, preferred_element_type=F32)


def _dot_nt(a, b):
    return lax.dot_general(a, b, (((1,), (1,)), ((), ())), preferred_element_type=F32)


def _dot_tn(a, b):
    return lax.dot_general(a, b, (((0,), (0,)), ((), ())), preferred_element_type=F32)


def _rms_mod(x, g, shift, scale):
    ms = jnp.mean(x * x, axis=-1, keepdims=True)
    return (x * lax.rsqrt(ms + EPS)) * (g * (1.0 + scale)) + shift


def _softplus(x):
    return jnp.maximum(x, 0.0) + jnp.log1p(jnp.exp(-jnp.abs(x)))


def _gelu_tanh(x):
    c = 0.7978845608028654
    inner = x * ((x * x) * (c * 0.044715) + c)
    return x * (0.5 * jnp.tanh(inner) + 0.5)


def _ada_kernel(c_ref, w_ref, b_ref, o_ref):
    c = c_ref[...].astype(BF16)
    w = w_ref[0].astype(BF16)
    o_ref[0] = _dot(c, w) + b_ref[0]


def _ada_call(c_all, ada_w, ada_b):
    depth, d, n = ada_w.shape
    m = c_all.shape[0]
    tn = d
    return pl.pallas_call(
        _ada_kernel,
        grid=(depth, n // tn),
        in_specs=[pl.BlockSpec((m, d), lambda l, j: (0, 0)),
                  pl.BlockSpec((1, d, tn), lambda l, j: (l, 0, j)),
                  pl.BlockSpec((1, 1, tn), lambda l, j: (l, 0, j))],
        out_specs=pl.BlockSpec((1, m, tn), lambda l, j: (l, 0, j)),
        out_shape=jax.ShapeDtypeStruct((depth, m, n), F32),
        compiler_params=_cparams(("arbitrary", "arbitrary"), 32),
        name="ada_proj",
    )(c_all, ada_w, ada_b.reshape(depth, 1, n))


FFN_TF = 256


def _ffn_kernel(*refs, mixed, n_prompt):
    if mixed:
        ha_ref, hb_ref, gm_ref, woa_ref, wob_ref = refs[:5]
        refs = refs[5:]
    x_ref, sh_ref, sc_ref, gt_ref, xs_ref, adas_ref, g_ref, win_ref, wo_ref, o_ref, os_ref, act_scr = refs
    is_sample = pl.program_id(0) == n_prompt
    _, tl, d = x_ref.shape
    nbs = adas_ref.shape[1]
    dff = wo_ref.shape[0]

    def half_step(x, shift, scale, gate_vec):
        h = _rms_mod(x, g_ref[...], shift, scale).astype(BF16)
        for c0 in range(0, dff, FFN_TF):
            gate = _dot(h, win_ref[:, c0:c0 + FFN_TF].astype(BF16))
            up = _dot(h, win_ref[:, dff + c0:dff + c0 + FFN_TF].astype(BF16))
            act_scr[:, c0:c0 + FFN_TF] = ((gate * jax.nn.sigmoid(gate)) * up).astype(BF16)
        y = _dot(act_scr[...], wo_ref[...].astype(BF16))
        return x + (0.5 * gate_vec) * y

    @pl.when(jnp.logical_not(is_sample))
    def _():
        x = x_ref[0]
        if mixed:
            x = x + gm_ref[0] * (_dot(ha_ref[0], woa_ref[...]) + _dot(hb_ref[0], wob_ref[...]))
        o_ref[0] = half_step(x, sh_ref[0], sc_ref[0], gt_ref[0])

    @pl.when(is_sample)
    def _():
        def rows(k):
            return jnp.broadcast_to(adas_ref[k][:, None, :], (nbs, tl // nbs, d)).reshape(tl, d)
        os_ref[...] = half_step(xs_ref[...], rows(0), rows(1), rows(2))


def _ffn_call(xp, ada_p, xs, ada_s, g, w_in, w_out, layer, tl, mix=None):
    nb, length, d = xp.shape
    nbs, tdec, _ = xs.shape
    dff = w_out.shape[1]
    assert dff % FFN_TF == 0 and nbs * tdec == tl
    nt = length // tl
    n_prompt = nb * nt
    cur = lambda i: jnp.minimum(i, n_prompt - 1)
    tile = lambda n: pl.BlockSpec((1, tl, n), lambda i: (cur(i) // nt, cur(i) % nt, 0))
    ada_spec = pl.BlockSpec((1, 1, d), lambda i: (cur(i) // nt, 0, 0))
    once = lambda shape: pl.BlockSpec(shape, lambda i: (0,) * len(shape), pipeline_mode=pl.Buffered(1))
    layer_spec = lambda shape: pl.BlockSpec((None,) + shape, lambda i: (layer, 0, 0), pipeline_mode=pl.Buffered(1))
    mix_args, mix_specs = [], []
    if mix is not None:
        ha, hb, gm, woa, wob = mix
        mix_args = [ha, hb, gm, woa, wob]
        mix_specs = [tile(ha.shape[-1]), tile(hb.shape[-1]), ada_spec, once(woa.shape), once(wob.shape)]
    adas = jnp.stack([a[:, 0] for a in ada_s])
    op, os = pl.pallas_call(
        functools.partial(_ffn_kernel, mixed=mix is not None, n_prompt=n_prompt),
        grid=(n_prompt + 1,),
        in_specs=mix_specs + [tile(d), ada_spec, ada_spec, ada_spec, once((tl, d)), once(adas.shape), once((1, d)),
                              layer_spec(w_in.shape[1:]), layer_spec(w_out.shape[1:])],
        out_specs=[tile(d), pl.BlockSpec((tl, d), lambda i: (0, 0))],
        out_shape=[jax.ShapeDtypeStruct(xp.shape, F32), jax.ShapeDtypeStruct((tl, d), F32)],
        scratch_shapes=[pltpu.VMEM((tl, dff), BF16)],
        compiler_params=_cparams(("arbitrary",), 60),
        name="ffn",
    )(*mix_args, xp, *ada_p, xs.reshape(tl, d), adas, g, w_in, w_out)
    return op, os.reshape(nbs, tdec, d)


def _head_rmsnorm(q, e, g, dhb):
    ss = _dot((q * q).astype(BF16), e)
    return q * lax.rsqrt(ss * (1.0 / dhb) + EPS) * g


def _proj_body(x, sh, sc, g, w_ref, gb_ref, qg_ref, kg_ref, e_ref, *, nh, bw, dhb):
    bb, tl, d = x.shape
    na = w_ref.shape[1] - 3 * bw - LANES
    h = _rms_mod(x, g, sh, sc).reshape(bb * tl, d).astype(BF16)
    ua = _dot(h, w_ref[:, :na])
    gg = _dot(h, w_ref[:, na + 3 * bw:]) + gb_ref[...]
    lane = lax.broadcasted_iota(jnp.int32, gg.shape, 1)
    gates = jnp.where(lane < nh, gg, -_softplus(-gg))
    ub = _dot(h, w_ref[:, na:na + 3 * bw])
    e = e_ref[...]
    qn = _head_rmsnorm(ub[:, :bw], e, qg_ref[...], dhb) * (dhb ** -0.5)
    kn = _head_rmsnorm(ub[:, bw:2 * bw], e, kg_ref[...], dhb)
    vb = ub[:, 2 * bw:]
    return ua, gates, qn, kn, vb


N_PROJ_IN, N_PROJ_OUT, N_RIDER_IN, N_RIDER_OUT = 9, 7, 12, 5


def _proj_prompt_kernel(*refs, nh, bw, dhb, rider):
    x_ref, sh_ref, sc_ref, g_ref, w_ref, gb_ref, qg_ref, kg_ref, e_ref = refs[:N_PROJ_IN]
    refs = refs[N_PROJ_IN:]
    if rider:
        (uas_ref, gs_ref, c0_ref, n0_ref, m0_ref, go_ref, qs_ref, kns_ref, vns_ref, ck_ref, cv_ref,
         bs_ref) = refs[:N_RIDER_IN]
        refs = refs[N_RIDER_IN:]
    ua_ref, gt_ref, qn_ref, kp_ref, vp_ref, kl_ref, vl_ref = refs[:N_PROJ_OUT]
    refs = refs[N_PROJ_OUT:]
    if rider:
        has_ref, cs_ref, ns_ref, ms_ref, hbs_ref = refs[:N_RIDER_OUT]
        nrep_scr, rep_scr, s_scr, pv_scr, kv_scr, qc_scr = refs[N_RIDER_OUT:]
    t = pl.program_id(1)
    nt = pl.num_programs(1)

    @pl.when(t == 0)
    def _():
        kp_ref[...] = jnp.zeros_like(kp_ref)
        vp_ref[...] = jnp.zeros_like(vp_ref)

    @pl.when(t > 0)
    def _():
        ua, gates, qn, kn, vb = _proj_body(x_ref[...], sh_ref[...], sc_ref[...], g_ref[...], w_ref,
                                           gb_ref, qg_ref, kg_ref, e_ref, nh=nh, bw=bw, dhb=dhb)
        ua_ref[0] = ua
        gt_ref[0] = gates
        qn_ref[0] = qn.astype(BF16)
        kp_ref[0] = kn.astype(BF16)
        vp_ref[0] = vb.astype(BF16)
        if rider:
            cs_ref[...] = c0_ref[...]
            nrep_scr[...] = n0_ref[0]
            ms_ref[...] = m0_ref[...]
            for steps in _mlstm_tile(uas_ref, gs_ref, go_ref, has_ref, cs_ref, ns_ref, ms_ref, nrep_scr, rep_scr,
                                     s_scr, pv_scr, kv_scr, qc_scr, seg=uas_ref.shape[1], nh=nh, dh=go_ref.shape[1]):
                _run_passes(steps)
            _band_sample_kernel(qs_ref, kns_ref, vns_ref, ck_ref, cv_ref, bs_ref, hbs_ref, npair=bs_ref.shape[0])

        @pl.when(t == nt - 1)
        def _():
            kl_ref[0] = kn
            vl_ref[0] = vb


def _proj_prompt_call(x, sh, sc, g, wab, gbias, qg, kg, e, *, nh, dhb, w, rider=None):
    nb, length, d = x.shape
    tl = w
    nt = length // tl
    bw = e.shape[0]
    aw4 = wab.shape[1] - 3 * bw - LANES
    prev = lambda b, t: (b, jnp.maximum(t - 1, 0), 0)
    ada_spec = pl.BlockSpec((1, 1, d), lambda b, t: (b, 0, 0))
    in_specs = [pl.BlockSpec((1, tl, d), prev), ada_spec, ada_spec, _const_spec((1, d)),
                _const_spec(wab.shape), _const_spec(gbias.shape),
                _const_spec(qg.shape), _const_spec(kg.shape), _const_spec(e.shape)]
    out_specs = [pl.BlockSpec((1, tl, aw4), prev),
                 pl.BlockSpec((1, tl, LANES), prev),
                 pl.BlockSpec((1, tl, bw), prev),
                 pl.BlockSpec((1, tl, bw), lambda b, t: (b, t, 0)),
                 pl.BlockSpec((1, tl, bw), lambda b, t: (b, t, 0)),
                 pl.BlockSpec((1, tl, bw), lambda b, t: (b, 0, 0)),
                 pl.BlockSpec((1, tl, bw), lambda b, t: (b, 0, 0))]
    out_shape = [jax.ShapeDtypeStruct((nb, length, aw4), F32),
                 jax.ShapeDtypeStruct((nb, length, LANES), F32),
                 jax.ShapeDtypeStruct((nb, length, bw), BF16),
                 jax.ShapeDtypeStruct((nb, length + w, bw), BF16),
                 jax.ShapeDtypeStruct((nb, length + w, bw), BF16),
                 jax.ShapeDtypeStruct((nb, w, bw), F32),
                 jax.ShapeDtypeStruct((nb, w, bw), F32)]
    args = [x, sh, sc, g, wab, gbias, qg, kg, e]
    scratch = []
    if rider is not None:
        ua_s, g_s, c0, n0rep, m0rep, gout, q_s, kn_s, vn_s, ck, cv, bias2_s = rider
        nbs, tdec, _ = ua_s.shape
        _, nha, dh, _ = c0.shape
        assert nbs == nb * nt and nha == nh
        per = lambda a: pl.BlockSpec((1,) + a.shape[1:],
                                     lambda b, t: (b * nt + jnp.maximum(t - 1, 0),) + (0,) * (a.ndim - 1))
        streams = [ua_s, g_s, c0, n0rep, m0rep]
        in_specs += [per(a) for a in streams] + [_const_spec(gout.shape)]
        in_specs += [per(a) for a in (q_s, kn_s, vn_s, ck, cv)] + [_const_spec(bias2_s.shape)]
        args += streams + [gout, q_s, kn_s, vn_s, ck, cv, bias2_s]
        r_shapes = [jax.ShapeDtypeStruct((nbs, tdec, nh * dh), BF16), jax.ShapeDtypeStruct((nbs, nh, dh, dh), F32),
                    jax.ShapeDtypeStruct((nbs, nh, dh), F32), jax.ShapeDtypeStruct((nbs, nh, LANES), F32),
                    jax.ShapeDtypeStruct((nbs, tdec, bw), BF16)]
        out_shape += r_shapes
        out_specs += [per(a) for a in r_shapes]
        scratch = [pltpu.VMEM((nh, dh, LANES), F32), pltpu.VMEM((3 * nh, tdec, LANES), F32),
                   pltpu.VMEM((nh, tdec, tdec), F32), pltpu.VMEM((nh, tdec, dh + LANES), F32),
                   pltpu.VMEM((nh, dh, dh + LANES), F32), pltpu.VMEM((nh, tdec, dh + LANES), F32)]
    outs = pl.pallas_call(
        functools.partial(_proj_prompt_kernel, nh=nh, bw=bw, dhb=dhb, rider=rider is not None),
        grid=(nb, nt + 1),
        in_specs=in_specs,
        out_specs=out_specs,
        out_shape=out_shape,
        scratch_shapes=scratch,
        compiler_params=_cparams(("arbitrary", "arbitrary"), 48),
        name="proj_prompt",
    )(*args)
    return outs[:N_PROJ_OUT], outs[N_PROJ_OUT:]


def _proj_sample_kernel(x_ref, sh_ref, sc_ref, g_ref, w_ref, gb_ref, qg_ref, kg_ref, e_ref,
                        ua_ref, gt_ref, qn_ref, kn_ref, vb_ref, *, nh, bw, dhb):
    bb, tl, _ = x_ref.shape
    ua, gates, qn, kn, vb = _proj_body(x_ref[...], sh_ref[...], sc_ref[...], g_ref[...], w_ref,
                                       gb_ref, qg_ref, kg_ref, e_ref, nh=nh, bw=bw, dhb=dhb)
    ua_ref[...] = ua.reshape(bb, tl, -1)
    gt_ref[...] = gates.reshape(bb, tl, -1)
    qn_ref[...] = qn.reshape(bb, tl, -1).astype(BF16)
    kn_ref[...] = kn.reshape(bb, tl, -1)
    vb_ref[...] = vb.reshape(bb, tl, -1)


def _proj_sample_call(x, sh, sc, g, wab, gbias, qg, kg, e, *, nh, dhb):
    nb, length, d = x.shape
    bw = e.shape[0]
    aw4 = wab.shape[1] - 3 * bw - LANES
    full = lambda n: pl.BlockSpec((nb, length, n), lambda i: (0, 0, 0))
    ada_spec = pl.BlockSpec((nb, 1, d), lambda i: (0, 0, 0))
    return pl.pallas_call(
        functools.partial(_proj_sample_kernel, nh=nh, bw=bw, dhb=dhb),
        grid=(1,),
        in_specs=[full(d), ada_spec, ada_spec, _const_spec((1, d)),
                  _const_spec(wab.shape), _const_spec(gbias.shape),
                  _const_spec(qg.shape), _const_spec(kg.shape), _const_spec(e.shape)],
        out_specs=[full(aw4), full(LANES), full(bw), full(bw), full(bw)],
        out_shape=[jax.ShapeDtypeStruct((nb, length, aw4), F32),
                   jax.ShapeDtypeStruct((nb, length, LANES), F32),
                   jax.ShapeDtypeStruct((nb, length, bw), BF16),
                   jax.ShapeDtypeStruct((nb, length, bw), F32),
                   jax.ShapeDtypeStruct((nb, length, bw), F32)],
        compiler_params=_cparams(("arbitrary",), 48),
        name="proj_sample",
    )(x, sh, sc, g, wab, gbias, qg, kg, e)


def _mlstm_init(c0_ref, n0_ref, m0_ref, c_ref, m_ref, nrep_scr):
    @pl.when(pl.program_id(1) == 0)
    def _():
        c_ref[...] = c0_ref[...]
        nrep_scr[...] = n0_ref[0]
        m_ref[...] = m0_ref[...]


def _mlstm_kernel(ua_ref, g_ref, c0_ref, n0_ref, m0_ref, go_ref, ha_ref, c_ref, n_ref, m_ref,
                  nrep_scr, rep_scr, s_scr, pv_scr, kv_scr, qc_scr, *, seg, nh, dh):
    _mlstm_init(c0_ref, n0_ref, m0_ref, c_ref, m_ref, nrep_scr)
    for steps in _mlstm_tile(ua_ref, g_ref, go_ref, ha_ref, c_ref, n_ref, m_ref,
                             nrep_scr, rep_scr, s_scr, pv_scr, kv_scr, qc_scr, seg=seg, nh=nh, dh=dh):
        _run_passes(steps)


def _mlstm_tile(ua_ref, g_ref, go_ref, ha_ref, c_ref, n_ref, m_ref,
                nrep_scr, rep_scr, s_scr, pv_scr, kv_scr, qc_scr, *, seg, nh, dh):
    tq = ua_ref.shape[1]
    nck = tq // seg
    aw = nh * dh
    gates = g_ref[0]
    pos = lax.broadcasted_iota(jnp.int32, gates.shape, 0) % seg
    bt = gates
    s = 1
    while s < seg:
        bt = bt + jnp.where(pos >= s, pltpu.roll(bt, s, 0), 0.0)
        s *= 2
    dmb = pltpu.roll(gates, nh, 1) - bt
    pm = dmb
    s = 1
    while s < seg:
        pm = jnp.maximum(pm, jnp.where(pos >= s, pltpu.roll(pm, s, 0), -jnp.inf))
        s *= 2
    if tq % LANES:
        dsq = jnp.concatenate([dmb, jnp.zeros((LANES - tq % LANES, LANES), F32)], axis=0)
    else:
        dsq = dmb
    dtr = dsq.T
    ri = lax.broadcasted_iota(jnp.int32, (seg, seg), 0)
    ci = lax.broadcasted_iota(jnp.int32, (seg, seg), 1)
    causal = ri >= ci
    ones = jnp.ones((seg, LANES), BF16)
    ones_dh = jnp.ones((dh, LANES), BF16)
    for h in range(nh):
        ln = slice(nh + h, nh + h + 1)
        for j, arr in enumerate((bt, dmb, pm)):
            rep_scr[3 * h + j] = jnp.broadcast_to(arr[:, ln], (tq, LANES))

    def cols(jc, h):
        rows = slice(jc * seg, (jc + 1) * seg)
        return rows, rep_scr[3 * h, rows, :], rep_scr[3 * h + 1, rows, :], rep_scr[3 * h + 2, rows, :]

    def last(jc, h, j):
        r = (jc + 1) * seg - 1
        return rep_scr[3 * h + j, r:r + 1, :]

    groups = [(jc, h) for jc in range(nck) for h in range(nh)]

    def qkv(jc, h, which):
        rows = slice(jc * seg, (jc + 1) * seg)
        return ua_ref[0, rows, which * aw + h * dh:which * aw + (h + 1) * dh]

    state = []
    before = []

    def score(g, jc, h):
        k = qkv(jc, h, 1) * (dh ** -0.5)
        s_scr[g] = _dot_nt(qkv(jc, h, 0).astype(BF16), k.astype(BF16))

    def local(g, jc, h):
        rows, _, _, p_col = cols(jc, h)
        d_row = dtr[nh + h:nh + h + 1, rows]
        dloc = jnp.exp(jnp.where(causal, d_row - p_col[:, :seg], -jnp.inf))
        sl = (s_scr[g] * dloc).astype(BF16)
        v = qkv(jc, h, 2).astype(BF16)
        pv_scr[g] = _dot(sl, jnp.concatenate([v, ones], axis=1))

    def contrib(g, jc, h):
        _, _, d_col, _ = cols(jc, h)
        kw = (qkv(jc, h, 1) * (dh ** -0.5)) * jnp.exp(d_col - last(jc, h, 2))
        vx = jnp.concatenate([qkv(jc, h, 2).astype(BF16), ones], axis=1)
        kv_scr[g] = _dot_tn(kw.astype(BF16), vx)

    def carry(g, jc, h):
        if not state:
            state.extend((c_ref[0, hh], nrep_scr[hh], m_ref[0, hh:hh + 1, :]) for hh in range(nh))
        c_mem, n_rep, m = state[h]
        cn = jnp.concatenate([c_mem.astype(BF16), n_rep.astype(BF16)], axis=1)
        qc_scr[g] = _dot(qkv(jc, h, 0).astype(BF16), cn)
        before.append(m)
        p_last = last(jc, h, 2)
        mml = jnp.maximum(m, p_last)
        w_prev = jnp.exp(m - mml)
        f_new = jnp.exp(p_last - mml)
        kvx = kv_scr[g]
        state[h] = (w_prev * c_mem + f_new * kvx[:, :dh],
                    w_prev * n_rep + f_new * kvx[:, dh:],
                    last(jc, h, 0) + mml)
        if g == len(groups) - 1:
            for hh in range(nh):
                c_ref[0, hh], nrep_scr[hh], m_ref[0, hh:hh + 1, :] = state[hh]
                n_ref[0, hh:hh + 1, :] = state[hh][1].T[0:1, :]

    def combine(g, jc, h):
        rows, b_col, _, p_col = cols(jc, h)
        m = before[g]
        mm = jnp.maximum(m, p_col)
        iw = jnp.exp(m - mm)
        fl = jnp.exp(p_col - mm)
        pv = pv_scr[g]
        qc = qc_scr[g]
        num = iw * qc[:, :dh] + fl * pv[:, :dh]
        den = iw * qc[:, dh:] + fl * pv[:, dh:]
        hh = num / jnp.maximum(jnp.abs(den), jnp.exp(-(b_col + mm)))
        h2 = hh * hh
        hi = h2.astype(BF16)
        lo = (h2 - hi.astype(F32)).astype(BF16)
        ms = (_dot(hi, ones_dh) + _dot(lo, ones_dh)) * (1.0 / dh)
        hn = (hh * lax.rsqrt(ms + EPS) * go_ref[h:h + 1, :]) * jax.nn.sigmoid(qkv(jc, h, 3))
        ha_ref[0, rows, h * dh:(h + 1) * dh] = hn.astype(BF16)

    return [[functools.partial(fn, g, jc, h) for g, (jc, h) in enumerate(groups)]
            for fn in (score, local, contrib, carry, combine)]


def _run_passes(*pass_lists):
    longest = max(map(len, pass_lists))
    for slot in range(longest):
        for steps in pass_lists:
            for k, step in enumerate(steps):
                if k * longest // len(steps) == slot:
                    step()


def _mlstm_call(ua, gates, c0, n0rep, m0rep, gout, *, tq, seg):
    nb, length, aw4 = ua.shape
    _, nh, dh, _ = c0.shape
    assert dh == LANES
    groups = (tq // seg) * nh
    st = lambda shape: pl.BlockSpec((1,) + shape, lambda b, t: (b,) + (0,) * len(shape))
    tile = lambda n: pl.BlockSpec((1, tq, n), lambda b, t: (b, t, 0))
    return pl.pallas_call(
        functools.partial(_mlstm_kernel, seg=seg, nh=nh, dh=dh),
        grid=(nb, length // tq),
        in_specs=[tile(aw4), tile(LANES), st((nh, dh, dh)), st((nh, dh, LANES)), st((nh, LANES)),
                  _const_spec(gout.shape)],
        out_specs=[tile(nh * dh), st((nh, dh, dh)), st((nh, dh)), st((nh, LANES))],
        out_shape=[jax.ShapeDtypeStruct((nb, length, nh * dh), BF16),
                   jax.ShapeDtypeStruct((nb, nh, dh, dh), F32),
                   jax.ShapeDtypeStruct((nb, nh, dh), F32),
                   jax.ShapeDtypeStruct((nb, nh, LANES), F32)],
        scratch_shapes=[pltpu.VMEM((nh, dh, LANES), F32), pltpu.VMEM((3 * nh, tq, LANES), F32),
                        pltpu.VMEM((groups, seg, seg), F32), pltpu.VMEM((groups, seg, dh + LANES), F32),
                        pltpu.VMEM((groups, dh, dh + LANES), F32), pltpu.VMEM((groups, seg, dh + LANES), F32)],
        compiler_params=_cparams(("arbitrary", "arbitrary"), 32),
        name="mlstm",
    )(ua, gates, c0, n0rep, m0rep, gout)


def _relbias_kernel(b0_ref, o_ref):
    nhb, nq, nk = o_ref.shape
    for h in range(nhb):
        x = jnp.broadcast_to(b0_ref[h:h + 1, :], (nq, b0_ref.shape[1]))
        o_ref[h] = pltpu.roll(x, 0, 1, stride=1, stride_axis=0)[:, :nk]


def _relbias_call(table, w):
    nhb = table.shape[0]
    max_rel = (table.shape[1] - 1) // 2
    assert CHUNK - 1 <= max_rel <= w
    first = jnp.broadcast_to(table[:, :1], (nhb, w - max_rel))
    wrap = jnp.broadcast_to(table[:, :1], (nhb, CHUNK))
    b0 = jnp.concatenate([first, table[:, :max_rel + CHUNK], wrap], axis=1).astype(F32)
    return pl.pallas_call(
        _relbias_kernel,
        out_shape=jax.ShapeDtypeStruct((nhb, CHUNK, w + CHUNK), F32),
        name="rel_bias",
    )(b0)


def _band_tile(q_ref, k_ref, v_ref, bias_ref, o_ref, s_scr, m_scr, e_scr, *, masked, npair, w, nck):
    c4 = pl.program_id(1)
    nk = w + CHUNK
    lane = lax.broadcasted_iota(jnp.int32, (CHUNK, LANES), 1)
    low = lane < LANES // 2
    zero = jnp.zeros((CHUNK, LANES), BF16)
    ones = jnp.ones((nk, LANES), BF16)

    starts = [pl.multiple_of((c4 * nck + jc) * CHUNK, CHUNK) for jc in range(nck)]
    groups = [(jc, p) for jc in range(nck) for p in range(npair)]

    def score(g, jc, p):
        sl = slice(p * LANES, (p + 1) * LANES)
        qp = q_ref[0, jc * CHUNK:(jc + 1) * CHUNK, sl]
        q2 = jnp.concatenate([jnp.where(low, qp, zero), jnp.where(low, zero, qp)], axis=0)
        s = _dot_nt(q2, k_ref[0, pl.ds(starts[jc], nk), sl]) + bias_ref[p]
        if masked:
            col = lax.broadcasted_iota(jnp.int32, s.shape, 1)
            s = jnp.where(col + starts[jc] >= w, s, -jnp.inf)
        s_scr[g] = s
        m_scr[g] = jnp.max(s, axis=-1, keepdims=True)

    def expo(g, jc, p):
        e_scr[g] = jnp.exp(s_scr[g] - m_scr[g]).astype(BF16)

    def value(g, jc, p):
        sl = slice(p * LANES, (p + 1) * LANES)
        vx = jnp.concatenate([v_ref[0, pl.ds(starts[jc], nk), sl], ones], axis=1)
        r = _dot(e_scr[g], vx)
        o_lo = r[:CHUNK, :LANES] / r[:CHUNK, LANES:]
        o_hi = r[CHUNK:, :LANES] / r[CHUNK:, LANES:]
        o_ref[0, jc * CHUNK:(jc + 1) * CHUNK, sl] = jnp.where(low, o_lo, o_hi).astype(BF16)

    return [[functools.partial(fn, g, jc, p) for g, (jc, p) in enumerate(groups)] for fn in (score, expo, value)]


def _mixer_prompt_kernel(ua_ref, g_ref, c0_ref, n0_ref, m0_ref, go_ref, q_ref, k_ref, v_ref, bias_ref,
                         ha_ref, c_ref, n_ref, m_ref, hb_ref,
                         nrep_scr, rep_scr, sa_scr, pv_scr, kv_scr, qc_scr, sb_scr, mb_scr, eb_scr,
                         *, seg, nh, dh, npair, w, nck):
    _mlstm_init(c0_ref, n0_ref, m0_ref, c_ref, m_ref, nrep_scr)
    first_full = w // (CHUNK * nck)

    def tile(masked):
        b_score, b_exp, b_value = _band_tile(q_ref, k_ref, v_ref, bias_ref, hb_ref, sb_scr, mb_scr, eb_scr,
                                             masked=masked, npair=npair, w=w, nck=nck)
        a_score, a_local, a_contrib, a_carry, a_combine = _mlstm_tile(
            ua_ref, g_ref, go_ref, ha_ref, c_ref, n_ref, m_ref,
            nrep_scr, rep_scr, sa_scr, pv_scr, kv_scr, qc_scr, seg=seg, nh=nh, dh=dh)
        _run_passes(b_score, a_score)
        _run_passes(b_exp, a_local)
        _run_passes(a_contrib)
        half = len(b_value) // 2
        _run_passes(b_value[:half], a_carry)
        _run_passes(b_value[half:], a_combine)

    @pl.when(pl.program_id(1) < first_full)
    def _():
        tile(True)

    @pl.when(pl.program_id(1) >= first_full)
    def _():
        tile(False)


def _mixer_prompt_call(ua, gates, c0, n0rep, m0rep, gout, qs, kpad, vpad, bias2, *, w, nck):
    nb, length, aw4 = ua.shape
    _, nh, dh, _ = c0.shape
    bw = qs.shape[2]
    npair = bias2.shape[0]
    lp = kpad.shape[1]
    tq = nck * CHUNK
    nk = w + CHUNK
    ga = nck * nh
    gb = nck * npair
    assert w % tq == 0 and dh == LANES
    st = lambda shape: pl.BlockSpec((1,) + shape, lambda b, t: (b,) + (0,) * len(shape))
    tile = lambda n: pl.BlockSpec((1, tq, n), lambda b, t: (b, t, 0))
    whole = pl.BlockSpec((1, lp, bw), lambda b, t: (b, 0, 0))
    return pl.pallas_call(
        functools.partial(_mixer_prompt_kernel, seg=CHUNK, nh=nh, dh=dh, npair=npair, w=w, nck=nck),
        grid=(nb, length // tq),
        in_specs=[tile(aw4), tile(LANES), st((nh, dh, dh)), st((nh, dh, LANES)), st((nh, LANES)),
                  _const_spec(gout.shape), tile(bw), whole, whole, _const_spec(bias2.shape)],
        out_specs=[tile(nh * dh), st((nh, dh, dh)), st((nh, dh)), st((nh, LANES)), tile(bw)],
        out_shape=[jax.ShapeDtypeStruct((nb, length, nh * dh), BF16),
                   jax.ShapeDtypeStruct((nb, nh, dh, dh), F32),
                   jax.ShapeDtypeStruct((nb, nh, dh), F32),
                   jax.ShapeDtypeStruct((nb, nh, LANES), F32),
                   jax.ShapeDtypeStruct((nb, length, bw), BF16)],
        scratch_shapes=[pltpu.VMEM((nh, dh, LANES), F32), pltpu.VMEM((3 * nh, tq, LANES), F32),
                        pltpu.VMEM((ga, CHUNK, CHUNK), F32), pltpu.VMEM((ga, CHUNK, dh + LANES), F32),
                        pltpu.VMEM((ga, dh, dh + LANES), F32), pltpu.VMEM((ga, CHUNK, dh + LANES), F32),
                        pltpu.VMEM((gb, 2 * CHUNK, nk), F32), pltpu.VMEM((gb, 2 * CHUNK, 1), F32),
                        pltpu.VMEM((gb, 2 * CHUNK, nk), BF16)],
        compiler_params=_cparams(("arbitrary", "arbitrary"), 48),
        name="mixer_prompt",
    )(ua, gates, c0, n0rep, m0rep, gout, qs, kpad, vpad, bias2)


def _band_sample_kernel(q_ref, kn_ref, vn_ref, ck_ref, cv_ref, bias_ref, o_ref, *, npair):
    tq = q_ref.shape[1]
    nk = ck_ref.shape[1] + tq
    lane = lax.broadcasted_iota(jnp.int32, (tq, LANES), 1)
    low = lane < LANES // 2
    zero = jnp.zeros((tq, LANES), BF16)
    ones = jnp.ones((nk, LANES), BF16)
    scores = []
    for p in range(npair):
        sl = slice(p * LANES, (p + 1) * LANES)
        qp = q_ref[0, :, sl]
        q2 = jnp.concatenate([jnp.where(low, qp, zero), jnp.where(low, zero, qp)], axis=0)
        kx = jnp.concatenate([ck_ref[0, :, sl].astype(BF16), kn_ref[0, :, sl].astype(BF16)], axis=0)
        scores.append(_dot_nt(q2, kx) + bias_ref[p])
    probs = [jnp.exp(s - jnp.max(s, axis=-1, keepdims=True)).astype(BF16) for s in scores]
    for p in range(npair):
        sl = slice(p * LANES, (p + 1) * LANES)
        vx = jnp.concatenate([cv_ref[0, :, sl].astype(BF16), vn_ref[0, :, sl].astype(BF16)], axis=0)
        r = _dot(probs[p], jnp.concatenate([vx, ones], axis=1))
        o_lo = r[:tq, :LANES] / r[:tq, LANES:]
        o_hi = r[tq:, :LANES] / r[tq:, LANES:]
        o_ref[0, :, sl] = jnp.where(low, o_lo, o_hi).astype(BF16)


def _band_sample_call(qn, kn, vn, ck, cv, bias2):
    nb, tq, bw = qn.shape
    w = ck.shape[1]
    npair = bias2.shape[0]
    new = pl.BlockSpec((1, tq, bw), lambda b: (b, 0, 0))
    cache = pl.BlockSpec((1, w, bw), lambda b: (b, 0, 0))
    return pl.pallas_call(
        functools.partial(_band_sample_kernel, npair=npair),
        grid=(nb,),
        in_specs=[new, new, new, cache, cache, _const_spec(bias2.shape)],
        out_specs=new,
        out_shape=jax.ShapeDtypeStruct((nb, tq, bw), BF16),
        compiler_params=_cparams(("arbitrary",), 32),
        name="band_sample",
    )(qn, kn, vn, ck, cv, bias2)


def _mixout_kernel(x_ref, ha_ref, hb_ref, gt_ref, woa_ref, wob_ref, o_ref):
    bb, tl, d = x_ref.shape
    ha = ha_ref[...].reshape(bb * tl, -1)
    hb = hb_ref[...].reshape(bb * tl, -1)
    y = _dot(ha, woa_ref[...]) + _dot(hb, wob_ref[...])
    o_ref[...] = x_ref[...] + gt_ref[...] * y.reshape(bb, tl, d)


def _mixout_call(x, ha, hb, gt, woa, wob, bb, tl):
    nb, length, d = x.shape
    tile = lambda n: pl.BlockSpec((bb, tl, n), lambda i, t: (i, t, 0))
    return pl.pallas_call(
        _mixout_kernel,
        grid=(nb // bb, length // tl),
        in_specs=[tile(d), tile(ha.shape[-1]), tile(hb.shape[-1]),
                  pl.BlockSpec((bb, 1, d), lambda i, t: (i, 0, 0)),
                  _const_spec(woa.shape), _const_spec(wob.shape)],
        out_specs=tile(d),
        out_shape=jax.ShapeDtypeStruct(x.shape, F32),
        compiler_params=_cparams(("arbitrary", "arbitrary"), 32),
        name="mix_out",
    )(x, ha, hb, gt, woa, wob)


def _rglru_gates(xc, gw_ref, rb, ib, lam, nblk):
    bwc = xc.shape[1] // nblk
    r_parts, i_parts = [], []
    for n in range(nblk):
        gn = _dot(xc[:, n * bwc:(n + 1) * bwc].astype(BF16), gw_ref[n])
        r_parts.append(gn[:, :bwc])
        i_parts.append(gn[:, bwc:])
    r = jax.nn.sigmoid(jnp.concatenate(r_parts, axis=1) + rb)
    ii = jax.nn.sigmoid(jnp.concatenate(i_parts, axis=1) + ib)
    log_a = r * (-LRU_C * _softplus(-lam))
    a = jnp.exp(log_a)
    th = jnp.tanh(log_a)
    v = -2.0 * th / (1.0 - th)
    root = jnp.where(v > 0.0, v * lax.rsqrt(v), 0.0)
    return a, root * (ii * xc)


def _rglru_prompt_kernel(x_ref, sh_ref, sc_ref, gt_ref, g_ref, win_ref, cw_ref, cb_ref, gw_ref, rb_ref, ib_ref,
                         lam_ref, wout_ref, conv0_ref, h0_ref, o_ref, conv_ref, hl_ref,
                         u_scr, x_scr, xp_scr, a_scr, h_scr, *, nblk, tiles_per_seq, proj_chunks):
    i = pl.program_id(0)
    tq, d = x_ref.shape[1], x_ref.shape[2]
    r_w = lam_ref.shape[1]
    ncw = cw_ref.shape[0]

    @pl.when(i == 0)
    def _():
        u_scr[1] = jnp.zeros(u_scr.shape[1:], F32)
        x_scr[1] = jnp.zeros(x_scr.shape[1:], F32)
        xp_scr[0:SUBLANES, :] = jnp.zeros((SUBLANES, r_w), F32)
        h_scr[...] = jnp.zeros_like(h_scr)

    @pl.when((i >= 1) & ((i - 1) % tiles_per_seq == 0))
    def _():
        xp_scr[0:SUBLANES, :] = conv0_ref[0]
        h_scr[...] = h0_ref[0]

    row8 = lax.broadcasted_iota(jnp.int32, (SUBLANES, r_w), 0)

    def body(slot):
        prev = 1 - slot
        x = x_ref[0]
        x_scr[slot] = x
        hm = _rms_mod(x, g_ref[...], sh_ref[0], sc_ref[0]).astype(BF16)
        wcols = 2 * r_w // proj_chunks

        def project(c):
            cols = slice(c * wcols, (c + 1) * wcols)
            uc = _dot(hm, win_ref[:, cols])
            u_scr[slot, :, cols] = uc
            bits = pltpu.bitcast(uc[0:SUBLANES, 0:LANES], jnp.uint32)
            zero = pltpu.bitcast(lax.shift_right_logical(bits, jnp.uint32(32)), F32)
            return zero[0:1, :]

        def wide(z, n):
            return 0.0 if z is None else jnp.tile(z, (1, n // LANES))

        pending = list(range(proj_chunks))

        def issue(n):
            z = None
            for _ in range(n):
                if pending:
                    zc = project(pending.pop(0))
                    z = zc if z is None else z + zc
            return z

        xp_scr[SUBLANES:SUBLANES + tq, :] = u_scr[prev, :, r_w:]
        xc = cb_ref[...]
        for j in range(ncw):
            off = SUBLANES - (ncw - 1 - j)
            xc = xc + xp_scr[off:off + tq, :] * cw_ref[j:j + 1, :]
        conv_ref[0] = xp_scr[tq:tq + SUBLANES, :]
        xp_scr[0:SUBLANES, :] = xp_scr[tq:tq + SUBLANES, :]
        a, upd = _rglru_gates(xc, gw_ref, rb_ref[...] + wide(issue(3), r_w), ib_ref[...], lam_ref[...], nblk)
        h = h_scr[...] + wide(issue(3), r_w)
        groups = tq // SUBLANES
        for gi in range(groups):
            r0 = gi * SUBLANES
            ai = a[r0:r0 + SUBLANES, :]
            bi = upd[r0:r0 + SUBLANES, :]
            s = 1
            while s < SUBLANES:
                m = row8 >= s
                bi = jnp.where(m, ai * pltpu.roll(bi, s, 0) + bi, bi)
                ai = jnp.where(m, ai * pltpu.roll(ai, s, 0), ai)
                s *= 2
            hs = ai * h + bi
            a_scr[r0:r0 + SUBLANES, :] = hs
            h = hs[SUBLANES - 1:SUBLANES, :]
            if gi == groups // 2 - 1:
                h = h + wide(issue(1), r_w)
        h_scr[...] = h
        hl_ref[0] = h
        gate_vec = gt_ref[0] + wide(issue(proj_chunks), d)
        y = _dot((_gelu_tanh(u_scr[prev, :, :r_w]) * a_scr[...]).astype(BF16), wout_ref[...])
        o_ref[0] = x_scr[prev] + gate_vec * y

    @pl.when(i % 2 == 0)
    def _():
        body(0)

    @pl.when(i % 2 == 1)
    def _():
        body(1)


def _rglru_prompt_call(x, sh, sc, gt, g, win, cw, cb, gw, rb, ib, lam, wout, conv0, h0, *, tq):
    nb, length, d = x.shape
    r_w = lam.shape[1]
    nblk = gw.shape[0]
    nt = length // tq
    ntiles = nb * nt
    cur = lambda i: jnp.minimum(i, ntiles - 1)
    prv = lambda i: jnp.maximum(i - 1, 0)
    consts = [g, win, cw, cb, gw, rb, ib, lam, wout]
    ada_cur = pl.BlockSpec((1, 1, d), lambda i: (cur(i) // nt, 0, 0))
    ada_prv = pl.BlockSpec((1, 1, d), lambda i: (prv(i) // nt, 0, 0))
    conv_spec = pl.BlockSpec((1, SUBLANES, r_w), lambda i: (prv(i) // nt, 0, 0))
    h_spec = pl.BlockSpec((1, 1, r_w), lambda i: (prv(i) // nt, 0, 0))
    return pl.pallas_call(
        functools.partial(_rglru_prompt_kernel, nblk=nblk, tiles_per_seq=nt, proj_chunks=8),
        grid=(ntiles + 1,),
        in_specs=[pl.BlockSpec((1, tq, d), lambda i: (cur(i) // nt, cur(i) % nt, 0)), ada_cur, ada_cur, ada_prv]
                 + [_const_spec(a.shape) for a in consts] + [conv_spec, h_spec],
        out_specs=[pl.BlockSpec((1, tq, d), lambda i: (prv(i) // nt, prv(i) % nt, 0)), conv_spec, h_spec],
        out_shape=[jax.ShapeDtypeStruct(x.shape, F32),
                   jax.ShapeDtypeStruct((nb, SUBLANES, r_w), F32),
                   jax.ShapeDtypeStruct((nb, 1, r_w), F32)],
        scratch_shapes=[pltpu.VMEM((2, tq, 2 * r_w), F32), pltpu.VMEM((2, tq, d), F32),
                        pltpu.VMEM((tq + SUBLANES, r_w), F32), pltpu.VMEM((tq, r_w), F32), pltpu.VMEM((1, r_w), F32)],
        compiler_params=_cparams(("arbitrary",), 56),
        name="rglru_prompt",
    )(x, sh, sc, gt, *consts, conv0, h0)


def _rglru_sample_kernel(x_ref, sh_ref, sc_ref, gt_ref, g_ref, win_ref, cw_ref, cb_ref, gw_ref, rb_ref, ib_ref,
                         lam_ref, wout_ref, conv0_ref, h0_ref, o_ref, conv_ref, hl_ref, xp_scr, *, nblk):
    bb, tl, d = x_ref.shape
    tm = bb * tl
    r_w = lam_ref.shape[1]
    ncw = cw_ref.shape[0]
    x = x_ref[...]
    hm = _rms_mod(x, g_ref[...], sh_ref[...], sc_ref[...]).reshape(tm, d).astype(BF16)
    u = _dot(hm, win_ref[...])
    gb = u[:, :r_w]
    xp_scr[:, 0:SUBLANES, :] = conv0_ref[...]
    xp_scr[:, SUBLANES:SUBLANES + tl, :] = u[:, r_w:].reshape(bb, tl, r_w)
    xc = jnp.broadcast_to(cb_ref[...], (bb, tl, r_w))
    for j in range(ncw):
        off = SUBLANES - (ncw - 1 - j)
        xc = xc + xp_scr[:, off:off + tl, :] * cw_ref[j:j + 1, :]
    conv_ref[...] = xp_scr[:, tl:tl + SUBLANES, :]
    a, b = _rglru_gates(xc.reshape(tm, r_w), gw_ref, rb_ref[...], ib_ref[...], lam_ref[...], nblk)
    pos = lax.broadcasted_iota(jnp.int32, (tm, r_w), 0) % tl
    s = 1
    while s < tl:
        m = pos >= s
        b = jnp.where(m, a * pltpu.roll(b, s, 0) + b, b)
        a = jnp.where(m, a * pltpu.roll(a, s, 0), a)
        s *= 2
    hs = a.reshape(bb, tl, r_w) * h0_ref[...] + b.reshape(bb, tl, r_w)
    hl_ref[...] = hs[:, tl - 1:tl, :]
    y = _dot((_gelu_tanh(gb) * hs.reshape(tm, r_w)).astype(BF16), wout_ref[...])
    o_ref[...] = x + gt_ref[...] * y.reshape(bb, tl, d)


def _rglru_sample_call(x, sh, sc, gt, g, win, cw, cb, gw, rb, ib, lam, wout, conv0, h0):
    nb, tl, d = x.shape
    r_w = lam.shape[1]
    nblk = gw.shape[0]
    full = lambda a, b: pl.BlockSpec((nb, a, b), lambda i: (0, 0, 0))
    consts = [g, win, cw, cb, gw, rb, ib, lam, wout]
    return pl.pallas_call(
        functools.partial(_rglru_sample_kernel, nblk=nblk),
        grid=(1,),
        in_specs=[full(tl, d), full(1, d), full(1, d), full(1, d)] + [_const_spec(a.shape) for a in consts]
                 + [full(SUBLANES, r_w), full(1, r_w)],
        out_specs=[full(tl, d), full(SUBLANES, r_w), full(1, r_w)],
        out_shape=[jax.ShapeDtypeStruct(x.shape, F32),
                   jax.ShapeDtypeStruct((nb, SUBLANES, r_w), F32),
                   jax.ShapeDtypeStruct((nb, 1, r_w), F32)],
        scratch_shapes=[pltpu.VMEM((nb, tl + SUBLANES, r_w), F32)],
        compiler_params=_cparams(("arbitrary",), 48),
        name="rglru_sample",
    )(x, sh, sc, gt, *consts, conv0, h0)


def _pad_rows_front(a, rows):
    return jnp.pad(a, ((0, 0), (rows - a.shape[1], 0), (0, 0)))


def kernel(x_prompt, x_sample, state_a_C, state_a_n, state_a_m, cache_b_k, cache_b_v, state_c_conv, state_c_h,
           c_prompt, c_sample, ffn1_norm, ffn1_w_in, ffn1_w_out, mix_norm, ffn2_norm, ffn2_w_in, ffn2_w_out,
           ada_w, ada_b, ab_w_in, ab_gate_bias, a_out_norm, b_q_norm, b_k_norm, b_rel_bias, ab_w_out,
           c_w_in, c_conv_w, c_conv_b, c_gate_w, c_gate_b, c_lambda, c_w_out):
    nbp, seq, d = x_prompt.shape
    nbs, tdec, _ = x_sample.shape
    depth = ada_w.shape[0]
    n_ada = ada_w.shape[2] // d
    _, _, nh, dh, _ = state_a_C.shape
    _, _, w_band, nhb, dhb = cache_b_k.shape
    aw, bw = nh * dh, nhb * dhb
    ncw = c_conv_w.shape[1]
    assert 2 * dhb == LANES and dh == LANES and w_band % CHUNK == 0 and seq % w_band == 0

    ada = _ada_call(jnp.concatenate([c_prompt, c_sample], axis=0), ada_w, ada_b)
    ada = ada.reshape(depth, nbp + nbs, n_ada, 1, d)
    ada_p = [[ada[l, :nbp, k] for k in range(n_ada)] for l in range(depth)]
    ada_s = [[ada[l, nbp:, k] for k in range(n_ada)] for l in range(depth)]

    tl_p = FFN_ROWS
    xp, xs = x_prompt, x_sample
    outs_p, outs_s = {}, {}
    for l in range(depth):
        ap, as_ = ada_p[l], ada_s[l]
        i = l // 2
        g1 = ffn1_norm[l].reshape(1, d)
        gm = mix_norm[l].reshape(1, d)
        g2 = ffn2_norm[l].reshape(1, d)
        mix_p = None
        xp, xs = _ffn_call(xp, ap[0:3], xs, as_[0:3], g1, ffn1_w_in, ffn1_w_out, l, tl_p)
        if l % 2 == 0:
            w_in = ab_w_in[i]
            wab = jnp.concatenate(
                [w_in[:, :4 * aw], w_in[:, 4 * aw + 2 * nh:], w_in[:, 4 * aw:4 * aw + 2 * nh],
                 jnp.zeros((d, LANES - 2 * nh), F32)], axis=1).astype(BF16)
            gbias = jnp.pad(ab_gate_bias[i], (0, LANES - 2 * nh)).reshape(1, LANES)
            qg = jnp.tile(b_q_norm[i], nhb).reshape(1, bw)
            kg = jnp.tile(b_k_norm[i], nhb).reshape(1, bw)
            head = jnp.arange(bw) // dhb
            e = (head[:, None] == head[None, :]).astype(BF16)
            woa = ab_w_out[i][:aw].astype(BF16)
            wob = ab_w_out[i][aw:].astype(BF16)
            gout = a_out_norm[i]
            bias = _relbias_call(b_rel_bias[i], w_band)
            bias2 = bias.reshape(nhb // 2, 2 * CHUNK, w_band + CHUNK)

            ua_s, gts_s, qn_s, kn, vn = _proj_sample_call(
                xs, as_[3], as_[4], gm, wab, gbias, qg, kg, e, nh=nh, dhb=dhb)
            sample_mix = (ua_s, gts_s, state_a_C[i],
                          jnp.broadcast_to(state_a_n[i][..., None], (nbs, nh, dh, LANES)),
                          jnp.broadcast_to(state_a_m[i][..., None], (nbs, nh, LANES)), gout,
                          qn_s, kn, vn, cache_b_k[i].reshape(nbs, w_band, bw), cache_b_v[i].reshape(nbs, w_band, bw),
                          bias[:, :tdec, :w_band + tdec].reshape(nhb // 2, 2 * tdec, w_band + tdec))
            ride = nbs == nbp * (seq // w_band)

            (ua, gts, qn, kpad, vpad, klast, vlast), rode = _proj_prompt_call(
                xp, ap[3], ap[4], gm, wab, gbias, qg, kg, e, nh=nh, dhb=dhb, w=w_band,
                rider=sample_mix if ride else None)
            zc = jnp.zeros((nbp, nh, dh, dh), F32)
            ha, c1, n1, m1, hb = _mixer_prompt_call(ua, gts, zc, zc, zc[:, :, 0], gout, qn, kpad, vpad, bias2,
                                                    w=w_band, nck=MIXER_CHUNKS)
            mix_p = (ha, hb, ap[5], woa, wob)
            outs_p.setdefault('a_C', []).append(c1)
            outs_p.setdefault('a_n', []).append(n1)
            outs_p.setdefault('a_m', []).append(m1[:, :, 0])
            outs_p.setdefault('b_k', []).append(klast.reshape(nbp, w_band, nhb, dhb))
            outs_p.setdefault('b_v', []).append(vlast.reshape(nbp, w_band, nhb, dhb))

            if ride:
                ha, c1, n1, m1, hb = rode
            else:
                ha, c1, n1, m1 = _mlstm_call(*sample_mix[:6], tq=tdec, seg=tdec)
                hb = _band_sample_call(*sample_mix[6:])
            xs = _mixout_call(xs, ha, hb, as_[5], woa, wob, nbs, tdec)
            outs_s.setdefault('a_C', []).append(c1)
            outs_s.setdefault('a_n', []).append(n1)
            outs_s.setdefault('a_m', []).append(m1[:, :, 0])
            outs_s.setdefault('b_k', []).append(kn.reshape(nbs, tdec, nhb, dhb))
            outs_s.setdefault('b_v', []).append(vn.reshape(nbs, tdec, nhb, dhb))
        else:
            r_w = c_lambda.shape[1]
            consts = (gm, c_w_in[i].astype(BF16), c_conv_w[i], c_conv_b[i].reshape(1, r_w), c_gate_w[i].astype(BF16),
                      c_gate_b[i][0].reshape(1, r_w), c_gate_b[i][1].reshape(1, r_w), c_lambda[i].reshape(1, r_w),
                      c_w_out[i].astype(BF16))
            xp, conv_p, h_p = _rglru_prompt_call(
                xp, ap[3], ap[4], ap[5], *consts,
                jnp.zeros((nbp, SUBLANES, r_w), F32), jnp.zeros((nbp, 1, r_w), F32), tq=RGLRU_ROWS)
            xs, conv_s, h_s = _rglru_sample_call(
                xs, as_[3], as_[4], as_[5], *consts,
                _pad_rows_front(state_c_conv[i], SUBLANES), state_c_h[i][:, None, :])
            outs_p.setdefault('c_conv', []).append(conv_p[:, SUBLANES - (ncw - 1):])
            outs_p.setdefault('c_h', []).append(h_p[:, 0])
            outs_s.setdefault('c_conv', []).append(conv_s[:, SUBLANES - (ncw - 1):])
            outs_s.setdefault('c_h', []).append(h_s[:, 0])
        xp, xs = _ffn_call(xp, ap[6:9], xs, as_[6:9], g2, ffn2_w_in, ffn2_w_out, l, tl_p, mix=mix_p)

    names = ('a_C', 'a_n', 'a_m', 'b_k', 'b_v', 'c_conv', 'c_h')
    ps = [jnp.stack(outs_p[n], axis=0) for n in names]
    ss = [jnp.stack(outs_s[n], axis=0) for n in names]
    return (xp, xs, *ps, *ss)
```

```python
import functools

import jax
import jax.numpy as jnp
from jax import lax
from jax.experimental import pallas as pl
from jax.experimental.pallas import tpu as pltpu

F32 = jnp.float32
BF16 = jnp.bfloat16

EPS = 1e-6
CHUNK = 64
LRU_C = 8.0
LANES = 128
SUBLANES = 8
MIB = 1024 * 1024
FFN_ROWS = 512
MIXER_CHUNKS = 4
RGLRU_ROWS = 512


def _cparams(semantics, vmem_mib):
    return pltpu.CompilerParams(dimension_semantics=semantics, vmem_limit_bytes=vmem_mib * MIB)


def _const_spec(shape):
    nd = len(shape)
    return pl.BlockSpec(shape, lambda *_: (0,) * nd, pipeline_mode=pl.Buffered(1))


def _dot(a, b):
    return jnp.dot(a, b, preferred_element_type=F32)


def _dot_nt(a, b):
    return lax.dot_general(a, b, (((1,), (1,)), ((), ())), preferred_element_type=F32)


def _dot_tn(a, b):
    return lax.dot_general(a, b, (((0,), (0,)), ((), ())), preferred_element_type=F32)


def _rms_mod(x, g, shift, scale):
    ms = jnp.mean(x * x, axis=-1, keepdims=True)
    return (x * lax.rsqrt(ms + EPS)) * (g * (1.0 + scale)) + shift


def _softplus(x):
    return jnp.maximum(x, 0.0) + jnp.log1p(jnp.exp(-jnp.abs(x)))


def _gelu_tanh(x):
    c = 0.7978845608028654
    inner = x * ((x * x) * (c * 0.044715) + c)
    return x * (0.5 * jnp.tanh(inner) + 0.5)


def _ada_kernel(c_ref, w_ref, b_ref, o_ref):
    c = c_ref[...].astype(BF16)
    w = w_ref[0].astype(BF16)
    o_ref[0] = _dot(c, w) + b_ref[0]


def _ada_call(c_all, ada_w, ada_b):
    depth, d, n = ada_w.shape
    m = c_all.shape[0]
    tn = d
    return pl.pallas_call(
        _ada_kernel,
        grid=(depth, n // tn),
        in_specs=[pl.BlockSpec((m, d), lambda l, j: (0, 0)),
                  pl.BlockSpec((1, d, tn), lambda l, j: (l, 0, j)),
                  pl.BlockSpec((1, 1, tn), lambda l, j: (l, 0, j))],
        out_specs=pl.BlockSpec((1, m, tn), lambda l, j: (l, 0, j)),
        out_shape=jax.ShapeDtypeStruct((depth, m, n), F32),
        compiler_params=_cparams(("arbitrary", "arbitrary"), 32),
        name="ada_proj",
    )(c_all, ada_w, ada_b.reshape(depth, 1, n))


FFN_TF = 256


def _ffn_kernel(*refs, mixed, n_prompt, layer):
    if mixed:
        ha_ref, hb_ref, gm_ref, woa_ref, wob_ref = refs[:5]
        refs = refs[5:]
    (x_ref, sh_ref, sc_ref, gt_ref, xs_ref, adas_ref, g_ref, win_hbm, wo_hbm, o_ref, os_ref,
     act_scr, win_ref, wo_ref, sems) = refs
    step = pl.program_id(0)
    is_sample = step == n_prompt
    _, tl, d = x_ref.shape
    nbs = adas_ref.shape[1]
    dff = wo_ref.shape[0]
    nchunk = dff // FFN_TF

    def weight_copy(kind, k):
        rows = pl.ds(k * FFN_TF, FFN_TF)
        if kind == 2:
            return pltpu.make_async_copy(wo_hbm.at[layer, rows, :], wo_ref.at[rows, :], sems.at[2 * nchunk + k])
        cols = pl.ds(kind * dff + k * FFN_TF, FFN_TF)
        return pltpu.make_async_copy(win_hbm.at[layer, :, cols], win_ref.at[:, cols], sems.at[kind * nchunk + k])

    def half_step(x, shift, scale, gate_vec, first):
        h = _rms_mod(x, g_ref[...], shift, scale).astype(BF16)
        for k in range(nchunk):
            c0 = k * FFN_TF
            if first:
                weight_copy(0, k).wait()
                weight_copy(1, k).wait()
            gate = _dot(h, win_ref[:, c0:c0 + FFN_TF].astype(BF16))
            up = _dot(h, win_ref[:, dff + c0:dff + c0 + FFN_TF].astype(BF16))
            act_scr[:, c0:c0 + FFN_TF] = ((gate * jax.nn.sigmoid(gate)) * up).astype(BF16)
        if first:
            for k in range(nchunk):
                weight_copy(2, k).wait()
        y = _dot(act_scr[...], wo_ref[...].astype(BF16))
        return x + (0.5 * gate_vec) * y

    def prompt_tile(first):
        x = x_ref[0]
        if mixed:
            x = x + gm_ref[0] * (_dot(ha_ref[0], woa_ref[...]) + _dot(hb_ref[0], wob_ref[...]))
        o_ref[0] = half_step(x, sh_ref[0], sc_ref[0], gt_ref[0], first)

    @pl.when(step == 0)
    def _():
        for k in range(nchunk):
            weight_copy(0, k).start()
            weight_copy(1, k).start()
        for k in range(nchunk):
            weight_copy(2, k).start()
        prompt_tile(True)

    @pl.when((step > 0) & jnp.logical_not(is_sample))
    def _():
        prompt_tile(False)

    @pl.when(is_sample)
    def _():
        def rows(k):
            return jnp.broadcast_to(adas_ref[k][:, None, :], (nbs, tl // nbs, d)).reshape(tl, d)
        os_ref[...] = half_step(xs_ref[...], rows(0), rows(1), rows(2), False)


def _ffn_call(xp, ada_p, xs, ada_s, g, w_in, w_out, layer, tl, mix=None):
    nb, length, d = xp.shape
    nbs, tdec, _ = xs.shape
    dff = w_out.shape[1]
    assert dff % FFN_TF == 0 and nbs * tdec == tl
    nt = length // tl
    n_prompt = nb * nt
    cur = lambda i: jnp.minimum(i, n_prompt - 1)
    tile = lambda n: pl.BlockSpec((1, tl, n), lambda i: (cur(i) // nt, cur(i) % nt, 0))
    ada_spec = pl.BlockSpec((1, 1, d), lambda i: (cur(i) // nt, 0, 0))
    once = lambda shape: pl.BlockSpec(shape, lambda i: (0,) * len(shape), pipeline_mode=pl.Buffered(1))
    in_hbm = pl.BlockSpec(memory_space=pl.ANY)
    mix_args, mix_specs = [], []
    if mix is not None:
        ha, hb, gm, woa, wob = mix
        mix_args = [ha, hb, gm, woa, wob]
        mix_specs = [tile(ha.shape[-1]), tile(hb.shape[-1]), ada_spec, once(woa.shape), once(wob.shape)]
    adas = jnp.stack([a[:, 0] for a in ada_s])
    op, os = pl.pallas_call(
        functools.partial(_ffn_kernel, mixed=mix is not None, n_prompt=n_prompt, layer=layer),
        grid=(n_prompt + 1,),
        in_specs=mix_specs + [tile(d), ada_spec, ada_spec, ada_spec, once((tl, d)), once(adas.shape), once((1, d)),
                              in_hbm, in_hbm],
        out_specs=[tile(d), pl.BlockSpec((tl, d), lambda i: (0, 0))],
        out_shape=[jax.ShapeDtypeStruct(xp.shape, F32), jax.ShapeDtypeStruct((tl, d), F32)],
        scratch_shapes=[pltpu.VMEM((tl, dff), BF16), pltpu.VMEM(w_in.shape[1:], F32), pltpu.VMEM(w_out.shape[1:], F32),
                        pltpu.SemaphoreType.DMA((3 * (dff // FFN_TF),))],
        compiler_params=_cparams(("arbitrary",), 60),
        name="ffn",
    )(*mix_args, xp, *ada_p, xs.reshape(tl, d), adas, g, w_in, w_out)
    return op, os.reshape(nbs, tdec, d)


def _head_rmsnorm(q, e, g, dhb):
    ss = _dot((q * q).astype(BF16), e)
    return q * lax.rsqrt(ss * (1.0 / dhb) + EPS) * g


def _proj_body(x, sh, sc, g, w_ref, gb_ref, qg_ref, kg_ref, e_ref, *, nh, bw, dhb):
    bb, tl, d = x.shape
    na = w_ref.shape[1] - 3 * bw - LANES
    h = _rms_mod(x, g, sh, sc).reshape(bb * tl, d).astype(BF16)
    ua = _dot(h, w_ref[:, :na])
    gg = _dot(h, w_ref[:, na + 3 * bw:]) + gb_ref[...]
    lane = lax.broadcasted_iota(jnp.int32, gg.shape, 1)
    gates = jnp.where(lane < nh, gg, -_softplus(-gg))
    ub = _dot(h, w_ref[:, na:na + 3 * bw])
    e = e_ref[...]
    qn = _head_rmsnorm(ub[:, :bw], e, qg_ref[...], dhb) * (dhb ** -0.5)
    kn = _head_rmsnorm(ub[:, bw:2 * bw], e, kg_ref[...], dhb)
    vb = ub[:, 2 * bw:]
    return ua, gates, qn, kn, vb


N_PROJ_IN, N_PROJ_OUT, N_RIDER_IN, N_RIDER_OUT = 9, 7, 12, 5


def _proj_prompt_kernel(*refs, nh, bw, dhb, rider):
    x_ref, sh_ref, sc_ref, g_ref, w_ref, gb_ref, qg_ref, kg_ref, e_ref = refs[:N_PROJ_IN]
    refs = refs[N_PROJ_IN:]
    if rider:
        (uas_ref, gs_ref, c0_ref, n0_ref, m0_ref, go_ref, qs_ref, kns_ref, vns_ref, ck_ref, cv_ref,
         bs_ref) = refs[:N_RIDER_IN]
        refs = refs[N_RIDER_IN:]
    ua_ref, gt_ref, qn_ref, kp_ref, vp_ref, kl_ref, vl_ref = refs[:N_PROJ_OUT]
    refs = refs[N_PROJ_OUT:]
    if rider:
        has_ref, cs_ref, ns_ref, ms_ref, hbs_ref = refs[:N_RIDER_OUT]
        nrep_scr, rep_scr, s_scr, pv_scr, kv_scr, qc_scr = refs[N_RIDER_OUT:]
    t = pl.program_id(1)
    nt = pl.num_programs(1)

    @pl.when(t == 0)
    def _():
        kp_ref[...] = jnp.zeros_like(kp_ref)
        vp_ref[...] = jnp.zeros_like(vp_ref)

    @pl.when(t > 0)
    def _():
        ua, gates, qn, kn, vb = _proj_body(x_ref[...], sh_ref[...], sc_ref[...], g_ref[...], w_ref,
                                           gb_ref, qg_ref, kg_ref, e_ref, nh=nh, bw=bw, dhb=dhb)
        ua_ref[0] = ua
        gt_ref[0] = gates
        qn_ref[0] = qn.astype(BF16)
        kp_ref[0] = kn.astype(BF16)
        vp_ref[0] = vb.astype(BF16)
        if rider:
            cs_ref[...] = c0_ref[...]
            nrep_scr[...] = n0_ref[0]
            ms_ref[...] = m0_ref[...]
            for steps in _mlstm_tile(uas_ref, gs_ref, go_ref, has_ref, cs_ref, ns_ref, ms_ref, nrep_scr, rep_scr,
                                     s_scr, pv_scr, kv_scr, qc_scr, seg=uas_ref.shape[1], nh=nh, dh=go_ref.shape[1]):
                _run_passes(steps)
            _band_sample_kernel(qs_ref, kns_ref, vns_ref, ck_ref, cv_ref, bs_ref, hbs_ref, npair=bs_ref.shape[0])

        @pl.when(t == nt - 1)
        def _():
            kl_ref[0] = kn
            vl_ref[0] = vb


def _proj_prompt_call(x, sh, sc, g, wab, gbias, qg, kg, e, *, nh, dhb, w, rider=None):
    nb, length, d = x.shape
    tl = w
    nt = length // tl
    bw = e.shape[0]
    aw4 = wab.shape[1] - 3 * bw - LANES
    prev = lambda b, t: (b, jnp.maximum(t - 1, 0), 0)
    ada_spec = pl.BlockSpec((1, 1, d), lambda b, t: (b, 0, 0))
    in_specs = [pl.BlockSpec((1, tl, d), prev), ada_spec, ada_spec, _const_spec((1, d)),
                _const_spec(wab.shape), _const_spec(gbias.shape),
                _const_spec(qg.shape), _const_spec(kg.shape), _const_spec(e.shape)]
    out_specs = [pl.BlockSpec((1, tl, aw4), prev),
                 pl.BlockSpec((1, tl, LANES), prev),
                 pl.BlockSpec((1, tl, bw), prev),
                 pl.BlockSpec((1, tl, bw), lambda b, t: (b, t, 0)),
                 pl.BlockSpec((1, tl, bw), lambda b, t: (b, t, 0)),
                 pl.BlockSpec((1, tl, bw), lambda b, t: (b, 0, 0)),
                 pl.BlockSpec((1, tl, bw), lambda b, t: (b, 0, 0))]
    out_shape = [jax.ShapeDtypeStruct((nb, length, aw4), F32),
                 jax.ShapeDtypeStruct((nb, length, LANES), F32),
                 jax.ShapeDtypeStruct((nb, length, bw), BF16),
                 jax.ShapeDtypeStruct((nb, length + w, bw), BF16),
                 jax.ShapeDtypeStruct((nb, length + w, bw), BF16),
                 jax.ShapeDtypeStruct((nb, w, bw), F32),
                 jax.ShapeDtypeStruct((nb, w, bw), F32)]
    args = [x, sh, sc, g, wab, gbias, qg, kg, e]
    scratch = []
    if rider is not None:
        ua_s, g_s, c0, n0rep, m0rep, gout, q_s, kn_s, vn_s, ck, cv, bias2_s = rider
        nbs, tdec, _ = ua_s.shape
        _, nha, dh, _ = c0.shape
        assert nbs == nb * nt and nha == nh
        per = lambda a: pl.BlockSpec((1,) + a.shape[1:],
                                     lambda b, t: (b * nt + jnp.maximum(t - 1, 0),) + (0,) * (a.ndim - 1))
        streams = [ua_s, g_s, c0, n0rep, m0rep]
        in_specs += [per(a) for a in streams] + [_const_spec(gout.shape)]
        in_specs += [per(a) for a in (q_s, kn_s, vn_s, ck, cv)] + [_const_spec(bias2_s.shape)]
        args += streams + [gout, q_s, kn_s, vn_s, ck, cv, bias2_s]
        r_shapes = [jax.ShapeDtypeStruct((nbs, tdec, nh * dh), BF16), jax.ShapeDtypeStruct((nbs, nh, dh, dh), F32),
                    jax.ShapeDtypeStruct((nbs, nh, dh), F32), jax.ShapeDtypeStruct((nbs, nh, LANES), F32),
                    jax.ShapeDtypeStruct((nbs, tdec, bw), BF16)]
        out_shape += r_shapes
        out_specs += [per(a) for a in r_shapes]
        scratch = [pltpu.VMEM((nh, dh, LANES), F32), pltpu.VMEM((3 * nh, tdec, LANES), F32),
                   pltpu.VMEM((nh, tdec, tdec), F32), pltpu.VMEM((nh, tdec, dh + LANES), F32),
                   pltpu.VMEM((nh, dh, dh + LANES), F32), pltpu.VMEM((nh, tdec, dh + LANES), F32)]
    outs = pl.pallas_call(
        functools.partial(_proj_prompt_kernel, nh=nh, bw=bw, dhb=dhb, rider=rider is not None),
        grid=(nb, nt + 1),
        in_specs=in_specs,
        out_specs=out_specs,
        out_shape=out_shape,
        scratch_shapes=scratch,
        compiler_params=_cparams(("arbitrary", "arbitrary"), 48),
        name="proj_prompt",
    )(*args)
    return outs[:N_PROJ_OUT], outs[N_PROJ_OUT:]


def _proj_sample_kernel(x_ref, sh_ref, sc_ref, g_ref, w_ref, gb_ref, qg_ref, kg_ref, e_ref,
                        ua_ref, gt_ref, qn_ref, kn_ref, vb_ref, *, nh, bw, dhb):
    bb, tl, _ = x_ref.shape
    ua, gates, qn, kn, vb = _proj_body(x_ref[...], sh_ref[...], sc_ref[...], g_ref[...], w_ref,
                                       gb_ref, qg_ref, kg_ref, e_ref, nh=nh, bw=bw, dhb=dhb)
    ua_ref[...] = ua.reshape(bb, tl, -1)
    gt_ref[...] = gates.reshape(bb, tl, -1)
    qn_ref[...] = qn.reshape(bb, tl, -1).astype(BF16)
    kn_ref[...] = kn.reshape(bb, tl, -1)
    vb_ref[...] = vb.reshape(bb, tl, -1)


def _proj_sample_call(x, sh, sc, g, wab, gbias, qg, kg, e, *, nh, dhb):
    nb, length, d = x.shape
    bw = e.shape[0]
    aw4 = wab.shape[1] - 3 * bw - LANES
    full = lambda n: pl.BlockSpec((nb, length, n), lambda i: (0, 0, 0))
    ada_spec = pl.BlockSpec((nb, 1, d), lambda i: (0, 0, 0))
    return pl.pallas_call(
        functools.partial(_proj_sample_kernel, nh=nh, bw=bw, dhb=dhb),
        grid=(1,),
        in_specs=[full(d), ada_spec, ada_spec, _const_spec((1, d)),
                  _const_spec(wab.shape), _const_spec(gbias.shape),
                  _const_spec(qg.shape), _const_spec(kg.shape), _const_spec(e.shape)],
        out_specs=[full(aw4), full(LANES), full(bw), full(bw), full(bw)],
        out_shape=[jax.ShapeDtypeStruct((nb, length, aw4), F32),
                   jax.ShapeDtypeStruct((nb, length, LANES), F32),
                   jax.ShapeDtypeStruct((nb, length, bw), BF16),
                   jax.ShapeDtypeStruct((nb, length, bw), F32),
                   jax.ShapeDtypeStruct((nb, length, bw), F32)],
        compiler_params=_cparams(("arbitrary",), 48),
        name="proj_sample",
    )(x, sh, sc, g, wab, gbias, qg, kg, e)


def _mlstm_init(c0_ref, n0_ref, m0_ref, c_ref, m_ref, nrep_scr):
    @pl.when(pl.program_id(1) == 0)
    def _():
        c_ref[...] = c0_ref[...]
        nrep_scr[...] = n0_ref[0]
        m_ref[...] = m0_ref[...]


def _mlstm_kernel(ua_ref, g_ref, c0_ref, n0_ref, m0_ref, go_ref, ha_ref, c_ref, n_ref, m_ref,
                  nrep_scr, rep_scr, s_scr, pv_scr, kv_scr, qc_scr, *, seg, nh, dh):
    _mlstm_init(c0_ref, n0_ref, m0_ref, c_ref, m_ref, nrep_scr)
    for steps in _mlstm_tile(ua_ref, g_ref, go_ref, ha_ref, c_ref, n_ref, m_ref,
                             nrep_scr, rep_scr, s_scr, pv_scr, kv_scr, qc_scr, seg=seg, nh=nh, dh=dh):
        _run_passes(steps)


def _mlstm_tile(ua_ref, g_ref, go_ref, ha_ref, c_ref, n_ref, m_ref,
                nrep_scr, rep_scr, s_scr, pv_scr, kv_scr, qc_scr, *, seg, nh, dh):
    tq = ua_ref.shape[1]
    nck = tq // seg
    aw = nh * dh
    gates = g_ref[0]
    pos = lax.broadcasted_iota(jnp.int32, gates.shape, 0) % seg
    bt = gates
    s = 1
    while s < seg:
        bt = bt + jnp.where(pos >= s, pltpu.roll(bt, s, 0), 0.0)
        s *= 2
    dmb = pltpu.roll(gates, nh, 1) - bt
    pm = dmb
    s = 1
    while s < seg:
        pm = jnp.maximum(pm, jnp.where(pos >= s, pltpu.roll(pm, s, 0), -jnp.inf))
        s *= 2
    if tq % LANES:
        dsq = jnp.concatenate([dmb, jnp.zeros((LANES - tq % LANES, LANES), F32)], axis=0)
    else:
        dsq = dmb
    dtr = dsq.T
    ri = lax.broadcasted_iota(jnp.int32, (seg, seg), 0)
    ci = lax.broadcasted_iota(jnp.int32, (seg, seg), 1)
    causal = ri >= ci
    ones = jnp.ones((seg, LANES), BF16)
    ones_dh = jnp.ones((dh, LANES), BF16)
    for h in range(nh):
        ln = slice(nh + h, nh + h + 1)
        for j, arr in enumerate((bt, dmb, pm)):
            rep_scr[3 * h + j] = jnp.broadcast_to(arr[:, ln], (tq, LANES))

    def cols(jc, h):
        rows = slice(jc * seg, (jc + 1) * seg)
        return rows, rep_scr[3 * h, rows, :], rep_scr[3 * h + 1, rows, :], rep_scr[3 * h + 2, rows, :]

    def last(jc, h, j):
        r = (jc + 1) * seg - 1
        return rep_scr[3 * h + j, r:r + 1, :]

    groups = [(jc, h) for jc in range(nck) for h in range(nh)]

    def qkv(jc, h, which):
        rows = slice(jc * seg, (jc + 1) * seg)
        return ua_ref[0, rows, which * aw + h * dh:which * aw + (h + 1) * dh]

    state = []
    before = []

    def score(g, jc, h):
        k = qkv(jc, h, 1) * (dh ** -0.5)
        s_scr[g] = _dot_nt(qkv(jc, h, 0).astype(BF16), k.astype(BF16))

    def local(g, jc, h):
        rows, _, _, p_col = cols(jc, h)
        d_row = dtr[nh + h:nh + h + 1, rows]
        dloc = jnp.exp(jnp.where(causal, d_row - p_col[:, :seg], -jnp.inf))
        sl = (s_scr[g] * dloc).astype(BF16)
        v = qkv(jc, h, 2).astype(BF16)
        pv_scr[g] = _dot(sl, jnp.concatenate([v, ones], axis=1))

    def contrib(g, jc, h):
        _, _, d_col, _ = cols(jc, h)
        kw = (qkv(jc, h, 1) * (dh ** -0.5)) * jnp.exp(d_col - last(jc, h, 2))
        vx = jnp.concatenate([qkv(jc, h, 2).astype(BF16), ones], axis=1)
        kv_scr[g] = _dot_tn(kw.astype(BF16), vx)

    def carry(g, jc, h):
        if not state:
            state.extend((c_ref[0, hh], nrep_scr[hh], m_ref[0, hh:hh + 1, :]) for hh in range(nh))
        c_mem, n_rep, m = state[h]
        cn = jnp.concatenate([c_mem.astype(BF16), n_rep.astype(BF16)], axis=1)
        qc_scr[g] = _dot(qkv(jc, h, 0).astype(BF16), cn)
        before.append(m)
        p_last = last(jc, h, 2)
        mml = jnp.maximum(m, p_last)
        w_prev = jnp.exp(m - mml)
        f_new = jnp.exp(p_last - mml)
        kvx = kv_scr[g]
        state[h] = (w_prev * c_mem + f_new * kvx[:, :dh],
                    w_prev * n_rep + f_new * kvx[:, dh:],
                    last(jc, h, 0) + mml)
        if g == len(groups) - 1:
            for hh in range(nh):
                c_ref[0, hh], nrep_scr[hh], m_ref[0, hh:hh + 1, :] = state[hh]
                n_ref[0, hh:hh + 1, :] = state[hh][1].T[0:1, :]

    def combine(g, jc, h):
        rows, b_col, _, p_col = cols(jc, h)
        m = before[g]
        mm = jnp.maximum(m, p_col)
        iw = jnp.exp(m - mm)
        fl = jnp.exp(p_col - mm)
        pv = pv_scr[g]
        qc = qc_scr[g]
        num = iw * qc[:, :dh] + fl * pv[:, :dh]
        den = iw * qc[:, dh:] + fl * pv[:, dh:]
        hh = num / jnp.maximum(jnp.abs(den), jnp.exp(-(b_col + mm)))
        h2 = hh * hh
        hi = h2.astype(BF16)
        lo = (h2 - hi.astype(F32)).astype(BF16)
        ms = (_dot(hi, ones_dh) + _dot(lo, ones_dh)) * (1.0 / dh)
        hn = (hh * lax.rsqrt(ms + EPS) * go_ref[h:h + 1, :]) * jax.nn.sigmoid(qkv(jc, h, 3))
        ha_ref[0, rows, h * dh:(h + 1) * dh] = hn.astype(BF16)

    return [[functools.partial(fn, g, jc, h) for g, (jc, h) in enumerate(groups)]
            for fn in (score, local, contrib, carry, combine)]


def _run_passes(*pass_lists):
    longest = max(map(len, pass_lists))
    for slot in range(longest):
        for steps in pass_lists:
            for k, step in enumerate(steps):
                if k * longest // len(steps) == slot:
                    step()


def _mlstm_call(ua, gates, c0, n0rep, m0rep, gout, *, tq, seg):
    nb, length, aw4 = ua.shape
    _, nh, dh, _ = c0.shape
    assert dh == LANES
    groups = (tq // seg) * nh
    st = lambda shape: pl.BlockSpec((1,) + shape, lambda b, t: (b,) + (0,) * len(shape))
    tile = lambda n: pl.BlockSpec((1, tq, n), lambda b, t: (b, t, 0))
    return pl.pallas_call(
        functools.partial(_mlstm_kernel, seg=seg, nh=nh, dh=dh),
        grid=(nb, length // tq),
        in_specs=[tile(aw4), tile(LANES), st((nh, dh, dh)), st((nh, dh, LANES)), st((nh, LANES)),
                  _const_spec(gout.shape)],
        out_specs=[tile(nh * dh), st((nh, dh, dh)), st((nh, dh)), st((nh, LANES))],
        out_shape=[jax.ShapeDtypeStruct((nb, length, nh * dh), BF16),
                   jax.ShapeDtypeStruct((nb, nh, dh, dh), F32),
                   jax.ShapeDtypeStruct((nb, nh, dh), F32),
                   jax.ShapeDtypeStruct((nb, nh, LANES), F32)],
        scratch_shapes=[pltpu.VMEM((nh, dh, LANES), F32), pltpu.VMEM((3 * nh, tq, LANES), F32),
                        pltpu.VMEM((groups, seg, seg), F32), pltpu.VMEM((groups, seg, dh + LANES), F32),
                        pltpu.VMEM((groups, dh, dh + LANES), F32), pltpu.VMEM((groups, seg, dh + LANES), F32)],
        compiler_params=_cparams(("arbitrary", "arbitrary"), 32),
        name="mlstm",
    )(ua, gates, c0, n0rep, m0rep, gout)


def _relbias_kernel(b0_ref, o_ref):
    nhb, nq, nk = o_ref.shape
    for h in range(nhb):
        x = jnp.broadcast_to(b0_ref[h:h + 1, :], (nq, b0_ref.shape[1]))
        o_ref[h] = pltpu.roll(x, 0, 1, stride=1, stride_axis=0)[:, :nk]


def _relbias_call(table, w):
    nhb = table.shape[0]
    max_rel = (table.shape[1] - 1) // 2
    assert CHUNK - 1 <= max_rel <= w
    first = jnp.broadcast_to(table[:, :1], (nhb, w - max_rel))
    wrap = jnp.broadcast_to(table[:, :1], (nhb, CHUNK))
    b0 = jnp.concatenate([first, table[:, :max_rel + CHUNK], wrap], axis=1).astype(F32)
    return pl.pallas_call(
        _relbias_kernel,
        out_shape=jax.ShapeDtypeStruct((nhb, CHUNK, w + CHUNK), F32),
        name="rel_bias",
    )(b0)


def _band_tile(q_ref, k_ref, v_ref, bias_ref, o_ref, s_scr, m_scr, e_scr, *, masked, npair, w, nck):
    c4 = pl.program_id(1)
    nk = w + CHUNK
    lane = lax.broadcasted_iota(jnp.int32, (CHUNK, LANES), 1)
    low = lane < LANES // 2
    zero = jnp.zeros((CHUNK, LANES), BF16)
    ones = jnp.ones((nk, LANES), BF16)

    starts = [pl.multiple_of((c4 * nck + jc) * CHUNK, CHUNK) for jc in range(nck)]
    groups = [(jc, p) for jc in range(nck) for p in range(npair)]

    def score(g, jc, p):
        sl = slice(p * LANES, (p + 1) * LANES)
        qp = q_ref[0, jc * CHUNK:(jc + 1) * CHUNK, sl]
        q2 = jnp.concatenate([jnp.where(low, qp, zero), jnp.where(low, zero, qp)], axis=0)
        s = _dot_nt(q2, k_ref[0, pl.ds(starts[jc], nk), sl]) + bias_ref[p]
        if masked:
            col = lax.broadcasted_iota(jnp.int32, s.shape, 1)
            s = jnp.where(col + starts[jc] >= w, s, -jnp.inf)
        s_scr[g] = s
        m_scr[g] = jnp.max(s, axis=-1, keepdims=True)

    def expo(g, jc, p):
        e_scr[g] = jnp.exp(s_scr[g] - m_scr[g]).astype(BF16)

    def value(g, jc, p):
        sl = slice(p * LANES, (p + 1) * LANES)
        vx = jnp.concatenate([v_ref[0, pl.ds(starts[jc], nk), sl], ones], axis=1)
        r = _dot(e_scr[g], vx)
        o_lo = r[:CHUNK, :LANES] / r[:CHUNK, LANES:]
        o_hi = r[CHUNK:, :LANES] / r[CHUNK:, LANES:]
        o_ref[0, jc * CHUNK:(jc + 1) * CHUNK, sl] = jnp.where(low, o_lo, o_hi).astype(BF16)

    return [[functools.partial(fn, g, jc, p) for g, (jc, p) in enumerate(groups)] for fn in (score, expo, value)]


def _mixer_prompt_kernel(ua_ref, g_ref, c0_ref, n0_ref, m0_ref, go_ref, q_ref, k_ref, v_ref, bias_ref,
                         ha_ref, c_ref, n_ref, m_ref, hb_ref,
                         nrep_scr, rep_scr, sa_scr, pv_scr, kv_scr, qc_scr, sb_scr, mb_scr, eb_scr,
                         *, seg, nh, dh, npair, w, nck):
    _mlstm_init(c0_ref, n0_ref, m0_ref, c_ref, m_ref, nrep_scr)
    first_full = w // (CHUNK * nck)

    def tile(masked):
        b_score, b_exp, b_value = _band_tile(q_ref, k_ref, v_ref, bias_ref, hb_ref, sb_scr, mb_scr, eb_scr,
                                             masked=masked, npair=npair, w=w, nck=nck)
        a_score, a_local, a_contrib, a_carry, a_combine = _mlstm_tile(
            ua_ref, g_ref, go_ref, ha_ref, c_ref, n_ref, m_ref,
            nrep_scr, rep_scr, sa_scr, pv_scr, kv_scr, qc_scr, seg=seg, nh=nh, dh=dh)
        _run_passes(b_score, a_score)
        _run_passes(b_exp, a_local)
        _run_passes(a_contrib)
        half = len(b_value) // 2
        _run_passes(b_value[:half], a_carry)
        _run_passes(b_value[half:], a_combine)

    @pl.when(pl.program_id(1) < first_full)
    def _():
        tile(True)

    @pl.when(pl.program_id(1) >= first_full)
    def _():
        tile(False)


def _mixer_prompt_call(ua, gates, c0, n0rep, m0rep, gout, qs, kpad, vpad, bias2, *, w, nck):
    nb, length, aw4 = ua.shape
    _, nh, dh, _ = c0.shape
    bw = qs.shape[2]
    npair = bias2.shape[0]
    lp = kpad.shape[1]
    tq = nck * CHUNK
    nk = w + CHUNK
    ga = nck * nh
    gb = nck * npair
    assert w % tq == 0 and dh == LANES
    st = lambda shape: pl.BlockSpec((1,) + shape, lambda b, t: (b,) + (0,) * len(shape))
    tile = lambda n: pl.BlockSpec((1, tq, n), lambda b, t: (b, t, 0))
    whole = pl.BlockSpec((1, lp, bw), lambda b, t: (b, 0, 0))
    return pl.pallas_call(
        functools.partial(_mixer_prompt_kernel, seg=CHUNK, nh=nh, dh=dh, npair=npair, w=w, nck=nck),
        grid=(nb, length // tq),
        in_specs=[tile(aw4), tile(LANES), st((nh, dh, dh)), st((nh, dh, LANES)), st((nh, LANES)),
                  _const_spec(gout.shape), tile(bw), whole, whole, _const_spec(bias2.shape)],
        out_specs=[tile(nh * dh), st((nh, dh, dh)), st((nh, dh)), st((nh, LANES)), tile(bw)],
        out_shape=[jax.ShapeDtypeStruct((nb, length, nh * dh), BF16),
                   jax.ShapeDtypeStruct((nb, nh, dh, dh), F32),
                   jax.ShapeDtypeStruct((nb, nh, dh), F32),
                   jax.ShapeDtypeStruct((nb, nh, LANES), F32),
                   jax.ShapeDtypeStruct((nb, length, bw), BF16)],
        scratch_shapes=[pltpu.VMEM((nh, dh, LANES), F32), pltpu.VMEM((3 * nh, tq, LANES), F32),
                        pltpu.VMEM((ga, CHUNK, CHUNK), F32), pltpu.VMEM((ga, CHUNK, dh + LANES), F32),
                        pltpu.VMEM((ga, dh, dh + LANES), F32), pltpu.VMEM((ga, CHUNK, dh + LANES), F32),
                        pltpu.VMEM((gb, 2 * CHUNK, nk), F32), pltpu.VMEM((gb, 2 * CHUNK, 1), F32),
                        pltpu.VMEM((gb, 2 * CHUNK, nk), BF16)],
        compiler_params=_cparams(("arbitrary", "arbitrary"), 48),
        name="mixer_prompt",
    )(ua, gates, c0, n0rep, m0rep, gout, qs, kpad, vpad, bias2)


def _band_sample_kernel(q_ref, kn_ref, vn_ref, ck_ref, cv_ref, bias_ref, o_ref, *, npair):
    tq = q_ref.shape[1]
    nk = ck_ref.shape[1] + tq
    lane = lax.broadcasted_iota(jnp.int32, (tq, LANES), 1)
    low = lane < LANES // 2
    zero = jnp.zeros((tq, LANES), BF16)
    ones = jnp.ones((nk, LANES), BF16)
    scores = []
    for p in range(npair):
        sl = slice(p * LANES, (p + 1) * LANES)
        qp = q_ref[0, :, sl]
        q2 = jnp.concatenate([jnp.where(low, qp, zero), jnp.where(low, zero, qp)], axis=0)
        kx = jnp.concatenate([ck_ref[0, :, sl].astype(BF16), kn_ref[0, :, sl].astype(BF16)], axis=0)
        scores.append(_dot_nt(q2, kx) + bias_ref[p])
    probs = [jnp.exp(s - jnp.max(s, axis=-1, keepdims=True)).astype(BF16) for s in scores]
    for p in range(npair):
        sl = slice(p * LANES, (p + 1) * LANES)
        vx = jnp.concatenate([cv_ref[0, :, sl].astype(BF16), vn_ref[0, :, sl].astype(BF16)], axis=0)
        r = _dot(probs[p], jnp.concatenate([vx, ones], axis=1))
        o_lo = r[:tq, :LANES] / r[:tq, LANES:]
        o_hi = r[tq:, :LANES] / r[tq:, LANES:]
        o_ref[0, :, sl] = jnp.where(low, o_lo, o_hi).astype(BF16)


def _band_sample_call(qn, kn, vn, ck, cv, bias2):
    nb, tq, bw = qn.shape
    w = ck.shape[1]
    npair = bias2.shape[0]
    new = pl.BlockSpec((1, tq, bw), lambda b: (b, 0, 0))
    cache = pl.BlockSpec((1, w, bw), lambda b: (b, 0, 0))
    return pl.pallas_call(
        functools.partial(_band_sample_kernel, npair=npair),
        grid=(nb,),
        in_specs=[new, new, new, cache, cache, _const_spec(bias2.shape)],
        out_specs=new,
        out_shape=jax.ShapeDtypeStruct((nb, tq, bw), BF16),
        compiler_params=_cparams(("arbitrary",), 32),
        name="band_sample",
    )(qn, kn, vn, ck, cv, bias2)


def _mixout_kernel(x_ref, ha_ref, hb_ref, gt_ref, woa_ref, wob_ref, o_ref):
    bb, tl, d = x_ref.shape
    ha = ha_ref[...].reshape(bb * tl, -1)
    hb = hb_ref[...].reshape(bb * tl, -1)
    y = _dot(ha, woa_ref[...]) + _dot(hb, wob_ref[...])
    o_ref[...] = x_ref[...] + gt_ref[...] * y.reshape(bb, tl, d)


def _mixout_call(x, ha, hb, gt, woa, wob, bb, tl):
    nb, length, d = x.shape
    tile = lambda n: pl.BlockSpec((bb, tl, n), lambda i, t: (i, t, 0))
    return pl.pallas_call(
        _mixout_kernel,
        grid=(nb // bb, length // tl),
        in_specs=[tile(d), tile(ha.shape[-1]), tile(hb.shape[-1]),
                  pl.BlockSpec((bb, 1, d), lambda i, t: (i, 0, 0)),
                  _const_spec(woa.shape), _const_spec(wob.shape)],
        out_specs=tile(d),
        out_shape=jax.ShapeDtypeStruct(x.shape, F32),
        compiler_params=_cparams(("arbitrary", "arbitrary"), 32),
        name="mix_out",
    )(x, ha, hb, gt, woa, wob)


def _rglru_gates(xc, gw_ref, rb, ib, lam, nblk):
    bwc = xc.shape[1] // nblk
    r_parts, i_parts = [], []
    for n in range(nblk):
        gn = _dot(xc[:, n * bwc:(n + 1) * bwc].astype(BF16), gw_ref[n])
        r_parts.append(gn[:, :bwc])
        i_parts.append(gn[:, bwc:])
    r = jax.nn.sigmoid(jnp.concatenate(r_parts, axis=1) + rb)
    ii = jax.nn.sigmoid(jnp.concatenate(i_parts, axis=1) + ib)
    log_a = r * (-LRU_C * _softplus(-lam))
    a = jnp.exp(log_a)
    th = jnp.tanh(log_a)
    v = -2.0 * th / (1.0 - th)
    root = jnp.where(v > 0.0, v * lax.rsqrt(v), 0.0)
    return a, root * (ii * xc)


def _rglru_prompt_kernel(x_ref, sh_ref, sc_ref, gt_ref, g_ref, win_ref, cw_ref, cb_ref, gw_ref, rb_ref, ib_ref,
                         lam_ref, wout_ref, conv0_ref, h0_ref, o_ref, conv_ref, hl_ref,
                         u_scr, x_scr, xp_scr, a_scr, h_scr, *, nblk, tiles_per_seq, proj_chunks):
    i = pl.program_id(0)
    tq, d = x_ref.shape[1], x_ref.shape[2]
    r_w = lam_ref.shape[1]
    ncw = cw_ref.shape[0]

    @pl.when(i == 0)
    def _():
        u_scr[1] = jnp.zeros(u_scr.shape[1:], F32)
        x_scr[1] = jnp.zeros(x_scr.shape[1:], F32)
        xp_scr[0:SUBLANES, :] = jnp.zeros((SUBLANES, r_w), F32)
        h_scr[...] = jnp.zeros_like(h_scr)

    @pl.when((i >= 1) & ((i - 1) % tiles_per_seq == 0))
    def _():
        xp_scr[0:SUBLANES, :] = conv0_ref[0]
        h_scr[...] = h0_ref[0]

    row8 = lax.broadcasted_iota(jnp.int32, (SUBLANES, r_w), 0)

    def body(slot):
        prev = 1 - slot
        x = x_ref[0]
        x_scr[slot] = x
        hm = _rms_mod(x, g_ref[...], sh_ref[0], sc_ref[0]).astype(BF16)
        wcols = 2 * r_w // proj_chunks

        def project(c):
            cols = slice(c * wcols, (c + 1) * wcols)
            uc = _dot(hm, win_ref[:, cols])
            u_scr[slot, :, cols] = uc
            bits = pltpu.bitcast(uc[0:SUBLANES, 0:LANES], jnp.uint32)
            zero = pltpu.bitcast(lax.shift_right_logical(bits, jnp.uint32(32)), F32)
            return zero[0:1, :]

        def wide(z, n):
            return 0.0 if z is None else jnp.tile(z, (1, n // LANES))

        pending = list(range(proj_chunks))

        def issue(n):
            z = None
            for _ in range(n):
                if pending:
                    zc = project(pending.pop(0))
                    z = zc if z is None else z + zc
            return z

        xp_scr[SUBLANES:SUBLANES + tq, :] = u_scr[prev, :, r_w:]
        xc = cb_ref[...]
        for j in range(ncw):
            off = SUBLANES - (ncw - 1 - j)
            xc = xc + xp_scr[off:off + tq, :] * cw_ref[j:j + 1, :]
        conv_ref[0] = xp_scr[tq:tq + SUBLANES, :]
        xp_scr[0:SUBLANES, :] = xp_scr[tq:tq + SUBLANES, :]
        a, upd = _rglru_gates(xc, gw_ref, rb_ref[...] + wide(issue(3), r_w), ib_ref[...], lam_ref[...], nblk)
        h = h_scr[...] + wide(issue(3), r_w)
        groups = tq // SUBLANES
        for gi in range(groups):
            r0 = gi * SUBLANES
            ai = a[r0:r0 + SUBLANES, :]
            bi = upd[r0:r0 + SUBLANES, :]
            s = 1
            while s < SUBLANES:
                m = row8 >= s
                bi = jnp.where(m, ai * pltpu.roll(bi, s, 0) + bi, bi)
                ai = jnp.where(m, ai * pltpu.roll(ai, s, 0), ai)
                s *= 2
            hs = ai * h + bi
            a_scr[r0:r0 + SUBLANES, :] = hs
            h = hs[SUBLANES - 1:SUBLANES, :]
            if gi == groups // 2 - 1:
                h = h + wide(issue(1), r_w)
        h_scr[...] = h
        hl_ref[0] = h
        gate_vec = gt_ref[0] + wide(issue(proj_chunks), d)
        y = _dot((_gelu_tanh(u_scr[prev, :, :r_w]) * a_scr[...]).astype(BF16), wout_ref[...])
        o_ref[0] = x_scr[prev] + gate_vec * y

    @pl.when(i % 2 == 0)
    def _():
        body(0)

    @pl.when(i % 2 == 1)
    def _():
        body(1)


def _rglru_prompt_call(x, sh, sc, gt, g, win, cw, cb, gw, rb, ib, lam, wout, conv0, h0, *, tq):
    nb, length, d = x.shape
    r_w = lam.shape[1]
    nblk = gw.shape[0]
    nt = length // tq
    ntiles = nb * nt
    cur = lambda i: jnp.minimum(i, ntiles - 1)
    prv = lambda i: jnp.maximum(i - 1, 0)
    consts = [g, win, cw, cb, gw, rb, ib, lam, wout]
    ada_cur = pl.BlockSpec((1, 1, d), lambda i: (cur(i) // nt, 0, 0))
    ada_prv = pl.BlockSpec((1, 1, d), lambda i: (prv(i) // nt, 0, 0))
    conv_spec = pl.BlockSpec((1, SUBLANES, r_w), lambda i: (prv(i) // nt, 0, 0))
    h_spec = pl.BlockSpec((1, 1, r_w), lambda i: (prv(i) // nt, 0, 0))
    return pl.pallas_call(
        functools.partial(_rglru_prompt_kernel, nblk=nblk, tiles_per_seq=nt, proj_chunks=8),
        grid=(ntiles + 1,),
        in_specs=[pl.BlockSpec((1, tq, d), lambda i: (cur(i) // nt, cur(i) % nt, 0)), ada_cur, ada_cur, ada_prv]
                 + [_const_spec(a.shape) for a in consts] + [conv_spec, h_spec],
        out_specs=[pl.BlockSpec((1, tq, d), lambda i: (prv(i) // nt, prv(i) % nt, 0)), conv_spec, h_spec],
        out_shape=[jax.ShapeDtypeStruct(x.shape, F32),
                   jax.ShapeDtypeStruct((nb, SUBLANES, r_w), F32),
                   jax.ShapeDtypeStruct((nb, 1, r_w), F32)],
        scratch_shapes=[pltpu.VMEM((2, tq, 2 * r_w), F32), pltpu.VMEM((2, tq, d), F32),
                        pltpu.VMEM((tq + SUBLANES, r_w), F32), pltpu.VMEM((tq, r_w), F32), pltpu.VMEM((1, r_w), F32)],
        compiler_params=_cparams(("arbitrary",), 56),
        name="rglru_prompt",
    )(x, sh, sc, gt, *consts, conv0, h0)


def _rglru_sample_kernel(x_ref, sh_ref, sc_ref, gt_ref, g_ref, win_ref, cw_ref, cb_ref, gw_ref, rb_ref, ib_ref,
                         lam_ref, wout_ref, conv0_ref, h0_ref, o_ref, conv_ref, hl_ref, xp_scr, *, nblk):
    bb, tl, d = x_ref.shape
    tm = bb * tl
    r_w = lam_ref.shape[1]
    ncw = cw_ref.shape[0]
    x = x_ref[...]
    hm = _rms_mod(x, g_ref[...], sh_ref[...], sc_ref[...]).reshape(tm, d).astype(BF16)
    u = _dot(hm, win_ref[...])
    gb = u[:, :r_w]
    xp_scr[:, 0:SUBLANES, :] = conv0_ref[...]
    xp_scr[:, SUBLANES:SUBLANES + tl, :] = u[:, r_w:].reshape(bb, tl, r_w)
    xc = jnp.broadcast_to(cb_ref[...], (bb, tl, r_w))
    for j in range(ncw):
        off = SUBLANES - (ncw - 1 - j)
        xc = xc + xp_scr[:, off:off + tl, :] * cw_ref[j:j + 1, :]
    conv_ref[...] = xp_scr[:, tl:tl + SUBLANES, :]
    a, b = _rglru_gates(xc.reshape(tm, r_w), gw_ref, rb_ref[...], ib_ref[...], lam_ref[...], nblk)
    pos = lax.broadcasted_iota(jnp.int32, (tm, r_w), 0) % tl
    s = 1
    while s < tl:
        m = pos >= s
        b = jnp.where(m, a * pltpu.roll(b, s, 0) + b, b)
        a = jnp.where(m, a * pltpu.roll(a, s, 0), a)
        s *= 2
    hs = a.reshape(bb, tl, r_w) * h0_ref[...] + b.reshape(bb, tl, r_w)
    hl_ref[...] = hs[:, tl - 1:tl, :]
    y = _dot((_gelu_tanh(gb) * hs.reshape(tm, r_w)).astype(BF16), wout_ref[...])
    o_ref[...] = x + gt_ref[...] * y.reshape(bb, tl, d)


def _rglru_sample_call(x, sh, sc, gt, g, win, cw, cb, gw, rb, ib, lam, wout, conv0, h0):
    nb, tl, d = x.shape
    r_w = lam.shape[1]
    nblk = gw.shape[0]
    full = lambda a, b: pl.BlockSpec((nb, a, b), lambda i: (0, 0, 0))
    consts = [g, win, cw, cb, gw, rb, ib, lam, wout]
    return pl.pallas_call(
        functools.partial(_rglru_sample_kernel, nblk=nblk),
        grid=(1,),
        in_specs=[full(tl, d), full(1, d), full(1, d), full(1, d)] + [_const_spec(a.shape) for a in consts]
                 + [full(SUBLANES, r_w), full(1, r_w)],
        out_specs=[full(tl, d), full(SUBLANES, r_w), full(1, r_w)],
        out_shape=[jax.ShapeDtypeStruct(x.shape, F32),
                   jax.ShapeDtypeStruct((nb, SUBLANES, r_w), F32),
                   jax.ShapeDtypeStruct((nb, 1, r_w), F32)],
        scratch_shapes=[pltpu.VMEM((nb, tl + SUBLANES, r_w), F32)],
        compiler_params=_cparams(("arbitrary",), 48),
        name="rglru_sample",
    )(x, sh, sc, gt, *consts, conv0, h0)


def _pad_rows_front(a, rows):
    return jnp.pad(a, ((0, 0), (rows - a.shape[1], 0), (0, 0)))


def kernel(x_prompt, x_sample, state_a_C, state_a_n, state_a_m, cache_b_k, cache_b_v, state_c_conv, state_c_h,
           c_prompt, c_sample, ffn1_norm, ffn1_w_in, ffn1_w_out, mix_norm, ffn2_norm, ffn2_w_in, ffn2_w_out,
           ada_w, ada_b, ab_w_in, ab_gate_bias, a_out_norm, b_q_norm, b_k_norm, b_rel_bias, ab_w_out,
           c_w_in, c_conv_w, c_conv_b, c_gate_w, c_gate_b, c_lambda, c_w_out):
    nbp, seq, d = x_prompt.shape
    nbs, tdec, _ = x_sample.shape
    depth = ada_w.shape[0]
    n_ada = ada_w.shape[2] // d
    _, _, nh, dh, _ = state_a_C.shape
    _, _, w_band, nhb, dhb = cache_b_k.shape
    aw, bw = nh * dh, nhb * dhb
    ncw = c_conv_w.shape[1]
    assert 2 * dhb == LANES and dh == LANES and w_band % CHUNK == 0 and seq % w_band == 0

    ada = _ada_call(jnp.concatenate([c_prompt, c_sample], axis=0), ada_w, ada_b)
    ada = ada.reshape(depth, nbp + nbs, n_ada, 1, d)
    ada_p = [[ada[l, :nbp, k] for k in range(n_ada)] for l in range(depth)]
    ada_s = [[ada[l, nbp:, k] for k in range(n_ada)] for l in range(depth)]

    tl_p = FFN_ROWS
    xp, xs = x_prompt, x_sample
    outs_p, outs_s = {}, {}
    for l in range(depth):
        ap, as_ = ada_p[l], ada_s[l]
        i = l // 2
        g1 = ffn1_norm[l].reshape(1, d)
        gm = mix_norm[l].reshape(1, d)
        g2 = ffn2_norm[l].reshape(1, d)
        mix_p = None
        xp, xs = _ffn_call(xp, ap[0:3], xs, as_[0:3], g1, ffn1_w_in, ffn1_w_out, l, tl_p)
        if l % 2 == 0:
            w_in = ab_w_in[i]
            wab = jnp.concatenate(
                [w_in[:, :4 * aw], w_in[:, 4 * aw + 2 * nh:], w_in[:, 4 * aw:4 * aw + 2 * nh],
                 jnp.zeros((d, LANES - 2 * nh), F32)], axis=1).astype(BF16)
            gbias = jnp.pad(ab_gate_bias[i], (0, LANES - 2 * nh)).reshape(1, LANES)
            qg = jnp.tile(b_q_norm[i], nhb).reshape(1, bw)
            kg = jnp.tile(b_k_norm[i], nhb).reshape(1, bw)
            head = jnp.arange(bw) // dhb
            e = (head[:, None] == head[None, :]).astype(BF16)
            woa = ab_w_out[i][:aw].astype(BF16)
            wob = ab_w_out[i][aw:].astype(BF16)
            gout = a_out_norm[i]
            bias = _relbias_call(b_rel_bias[i], w_band)
            bias2 = bias.reshape(nhb // 2, 2 * CHUNK, w_band + CHUNK)

            ua_s, gts_s, qn_s, kn, vn = _proj_sample_call(
                xs, as_[3], as_[4], gm, wab, gbias, qg, kg, e, nh=nh, dhb=dhb)
            sample_mix = (ua_s, gts_s, state_a_C[i],
                          jnp.broadcast_to(state_a_n[i][..., None], (nbs, nh, dh, LANES)),
                          jnp.broadcast_to(state_a_m[i][..., None], (nbs, nh, LANES)), gout,
                          qn_s, kn, vn, cache_b_k[i].reshape(nbs, w_band, bw), cache_b_v[i].reshape(nbs, w_band, bw),
                          bias[:, :tdec, :w_band + tdec].reshape(nhb // 2, 2 * tdec, w_band + tdec))
            ride = nbs == nbp * (seq // w_band)

            (ua, gts, qn, kpad, vpad, klast, vlast), rode = _proj_prompt_call(
                xp, ap[3], ap[4], gm, wab, gbias, qg, kg, e, nh=nh, dhb=dhb, w=w_band,
                rider=sample_mix if ride else None)
            zc = jnp.zeros((nbp, nh, dh, dh), F32)
            ha, c1, n1, m1, hb = _mixer_prompt_call(ua, gts, zc, zc, zc[:, :, 0], gout, qn, kpad, vpad, bias2,
                                                    w=w_band, nck=MIXER_CHUNKS)
            mix_p = (ha, hb, ap[5], woa, wob)
            outs_p.setdefault('a_C', []).append(c1)
            outs_p.setdefault('a_n', []).append(n1)
            outs_p.setdefault('a_m', []).append(m1[:, :, 0])
            outs_p.setdefault('b_k', []).append(klast.reshape(nbp, w_band, nhb, dhb))
            outs_p.setdefault('b_v', []).append(vlast.reshape(nbp, w_band, nhb, dhb))

            if ride:
                ha, c1, n1, m1, hb = rode
            else:
                ha, c1, n1, m1 = _mlstm_call(*sample_mix[:6], tq=tdec, seg=tdec)
                hb = _band_sample_call(*sample_mix[6:])
            xs = _mixout_call(xs, ha, hb, as_[5], woa, wob, nbs, tdec)
            outs_s.setdefault('a_C', []).append(c1)
            outs_s.setdefault('a_n', []).append(n1)
            outs_s.setdefault('a_m', []).append(m1[:, :, 0])
            outs_s.setdefault('b_k', []).append(kn.reshape(nbs, tdec, nhb, dhb))
            outs_s.setdefault('b_v', []).append(vn.reshape(nbs, tdec, nhb, dhb))
        else:
            r_w = c_lambda.shape[1]
            consts = (gm, c_w_in[i].astype(BF16), c_conv_w[i], c_conv_b[i].reshape(1, r_w), c_gate_w[i].astype(BF16),
                      c_gate_b[i][0].reshape(1, r_w), c_gate_b[i][1].reshape(1, r_w), c_lambda[i].reshape(1, r_w),
                      c_w_out[i].astype(BF16))
            xp, conv_p, h_p = _rglru_prompt_call(
                xp, ap[3], ap[4], ap[5], *consts,
                jnp.zeros((nbp, SUBLANES, r_w), F32), jnp.zeros((nbp, 1, r_w), F32), tq=RGLRU_ROWS)
            xs, conv_s, h_s = _rglru_sample_call(
                xs, as_[3], as_[4], as_[5], *consts,
                _pad_rows_front(state_c_conv[i], SUBLANES), state_c_h[i][:, None, :])
            outs_p.setdefault('c_conv', []).append(conv_p[:, SUBLANES - (ncw - 1):])
            outs_p.setdefault('c_h', []).append(h_p[:, 0])
            outs_s.setdefault('c_conv', []).append(conv_s[:, SUBLANES - (ncw - 1):])
            outs_s.setdefault('c_h', []).append(h_s[:, 0])
        xp, xs = _ffn_call(xp, ap[6:9], xs, as_[6:9], g2, ffn2_w_in, ffn2_w_out, l, tl_p, mix=mix_p)

    names = ('a_C', 'a_n', 'a_m', 'b_k', 'b_v', 'c_conv', 'c_h')
    ps = [jnp.stack(outs_p[n], axis=0) for n in names]
    ss = [jnp.stack(outs_s[n], axis=0) for n in names]
    return (xp, xs, *ps, *ss)
```

```python
import functools

import jax
import jax.numpy as jnp
from jax import lax
from jax.experimental import pallas as pl
from jax.experimental.pallas import tpu as pltpu

F32 = jnp.float32
BF16 = jnp.bfloat16

EPS = 1e-6
CHUNK = 64
LRU_C = 8.0
LANES = 128
SUBLANES = 8
MIB = 1024 * 1024
FFN_ROWS = 512
MIXER_CHUNKS = 4
RGLRU_ROWS = 512


def _cparams(semantics, vmem_mib):
    return pltpu.CompilerParams(dimension_semantics=semantics, vmem_limit_bytes=vmem_mib * MIB)


def _const_spec(shape):
    nd = len(shape)
    return pl.BlockSpec(shape, lambda *_: (0,) * nd, pipeline_mode=pl.Buffered(1))


def _dot(a, b):
    return jnp.dot(a, b, preferred_element_type=F32)


def _dot_nt(a, b):
    return lax.dot_general(a, b, (((1,), (1,)), ((), ())), preferred_element_type=F32)


def _dot_tn(a, b):
    return lax.dot_general(a, b, (((0,), (0,)), ((), ())), preferred_element_type=F32)


def _rms_mod(x, g, shift, scale):
    ms = jnp.mean(x * x, axis=-1, keepdims=True)
    return (x * lax.rsqrt(ms + EPS)) * (g * (1.0 + scale)) + shift


def _softplus(x):
    return jnp.maximum(x, 0.0) + jnp.log1p(jnp.exp(-jnp.abs(x)))


def _gelu_tanh(x):
    c = 0.7978845608028654
    inner = x * ((x * x) * (c * 0.044715) + c)
    return x * (0.5 * jnp.tanh(inner) + 0.5)


def _ada_kernel(c_ref, w_ref, b_ref, o_ref):
    c = c_ref[...].astype(BF16)
    w = w_ref[0].astype(BF16)
    o_ref[0] = _dot(c, w) + b_ref[0]


def _ada_call(c_all, ada_w, ada_b):
    depth, d, n = ada_w.shape
    m = c_all.shape[0]
    tn = d
    return pl.pallas_call(
        _ada_kernel,
        grid=(depth, n // tn),
        in_specs=[pl.BlockSpec((m, d), lambda l, j: (0, 0)),
                  pl.BlockSpec((1, d, tn), lambda l, j: (l, 0, j)),
                  pl.BlockSpec((1, 1, tn), lambda l, j: (l, 0, j))],
        out_specs=pl.BlockSpec((1, m, tn), lambda l, j: (l, 0, j)),
        out_shape=jax.ShapeDtypeStruct((depth, m, n), F32),
        compiler_params=_cparams(("arbitrary", "arbitrary"), 32),
        name="ada_proj",
    )(c_all, ada_w, ada_b.reshape(depth, 1, n))


FFN_TF = 256


def _ffn_kernel(*refs, mixed, n_prompt):
    if mixed:
        ha_ref, hb_ref, gm_ref, woa_ref, wob_ref = refs[:5]
        refs = refs[5:]
    x_ref, sh_ref, sc_ref, gt_ref, xs_ref, adas_ref, g_ref, win_ref, wo_ref, o_ref, os_ref, act_scr = refs
    is_sample = pl.program_id(0) == n_prompt
    _, tl, d = x_ref.shape
    nbs = adas_ref.shape[1]
    dff = wo_ref.shape[0]

    def half_step(x, shift, scale, gate_vec):
        h = _rms_mod(x, g_ref[...], shift, scale).astype(BF16)
        for c0 in range(0, dff, FFN_TF):
            gate = _dot(h, win_ref[:, c0:c0 + FFN_TF].astype(BF16))
            up = _dot(h, win_ref[:, dff + c0:dff + c0 + FFN_TF].astype(BF16))
            act_scr[:, c0:c0 + FFN_TF] = ((gate * jax.nn.sigmoid(gate)) * up).astype(BF16)
        y = _dot(act_scr[...], wo_ref[...].astype(BF16))
        return x + (0.5 * gate_vec) * y

    @pl.when(jnp.logical_not(is_sample))
    def _():
        x = x_ref[0]
        if mixed:
            x = x + gm_ref[0] * (_dot(ha_ref[0], woa_ref[...]) + _dot(hb_ref[0], wob_ref[...]))
        o_ref[0] = half_step(x, sh_ref[0], sc_ref[0], gt_ref[0])

    @pl.when(is_sample)
    def _():
        def rows(k):
            return jnp.broadcast_to(adas_ref[k][:, None, :], (nbs, tl // nbs, d)).reshape(tl, d)
        os_ref[...] = half_step(xs_ref[...], rows(0), rows(1), rows(2))


def _ffn_call(xp, ada_p, xs, ada_s, g, w_in, w_out, layer, tl, mix=None):
    nb, length, d = xp.shape
    nbs, tdec, _ = xs.shape
    dff = w_out.shape[1]
    assert dff % FFN_TF == 0 and nbs * tdec == tl
    nt = length // tl
    n_prompt = nb * nt
    cur = lambda i: jnp.minimum(i, n_prompt - 1)
    tile = lambda n: pl.BlockSpec((1, tl, n), lambda i: (cur(i) // nt, cur(i) % nt, 0))
    ada_spec = pl.BlockSpec((1, 1, d), lambda i: (cur(i) // nt, 0, 0))
    once = lambda shape: pl.BlockSpec(shape, lambda i: (0,) * len(shape), pipeline_mode=pl.Buffered(1))
    layer_spec = lambda shape: pl.BlockSpec((None,) + shape, lambda i: (layer, 0, 0), pipeline_mode=pl.Buffered(1))
    mix_args, mix_specs = [], []
    if mix is not None:
        ha, hb, gm, woa, wob = mix
        mix_args = [ha, hb, gm, woa, wob]
        mix_specs = [tile(ha.shape[-1]), tile(hb.shape[-1]), ada_spec, once(woa.shape), once(wob.shape)]
    adas = jnp.stack([a[:, 0] for a in ada_s])
    op, os = pl.pallas_call(
        functools.partial(_ffn_kernel, mixed=mix is not None, n_prompt=n_prompt),
        grid=(n_prompt + 1,),
        in_specs=mix_specs + [tile(d), ada_spec, ada_spec, ada_spec, once((tl, d)), once(adas.shape), once((1, d)),
                              layer_spec(w_in.shape[1:]), layer_spec(w_out.shape[1:])],
        out_specs=[tile(d), pl.BlockSpec((tl, d), lambda i: (0, 0))],
        out_shape=[jax.ShapeDtypeStruct(xp.shape, F32), jax.ShapeDtypeStruct((tl, d), F32)],
        scratch_shapes=[pltpu.VMEM((tl, dff), BF16)],
        compiler_params=_cparams(("arbitrary",), 60),
        name="ffn",
    )(*mix_args, xp, *ada_p, xs.reshape(tl, d), adas, g, w_in, w_out)
    return op, os.reshape(nbs, tdec, d)


def _head_rmsnorm(q, e, g, dhb):
    ss = _dot((q * q).astype(BF16), e)
    return q * lax.rsqrt(ss * (1.0 / dhb) + EPS) * g


def _proj_body(x, sh, sc, g, w_ref, gb_ref, qg_ref, kg_ref, e_ref, *, nh, bw, dhb):
    bb, tl, d = x.shape
    na = w_ref.shape[1] - 3 * bw - LANES
    h = _rms_mod(x, g, sh, sc).reshape(bb * tl, d).astype(BF16)
    ua = _dot(h, w_ref[:, :na])
    gg = _dot(h, w_ref[:, na + 3 * bw:]) + gb_ref[...]
    lane = lax.broadcasted_iota(jnp.int32, gg.shape, 1)
    gates = jnp.where(lane < nh, gg, -_softplus(-gg))
    ub = _dot(h, w_ref[:, na:na + 3 * bw])
    e = e_ref[...]
    qn = _head_rmsnorm(ub[:, :bw], e, qg_ref[...], dhb) * (dhb ** -0.5)
    kn = _head_rmsnorm(ub[:, bw:2 * bw], e, kg_ref[...], dhb)
    vb = ub[:, 2 * bw:]
    return ua, gates, qn, kn, vb


N_PROJ_IN, N_PROJ_OUT, N_RIDER_IN, N_RIDER_OUT = 9, 7, 12, 5


def _proj_prompt_kernel(*refs, nh, bw, dhb, rider):
    x_ref, sh_ref, sc_ref, g_ref, w_ref, gb_ref, qg_ref, kg_ref, e_ref = refs[:N_PROJ_IN]
    refs = refs[N_PROJ_IN:]
    if rider:
        (uas_ref, gs_ref, c0_ref, n0_ref, m0_ref, go_ref, qs_ref, kns_ref, vns_ref, ck_ref, cv_ref,
         bs_ref) = refs[:N_RIDER_IN]
        refs = refs[N_RIDER_IN:]
    ua_ref, gt_ref, qn_ref, kp_ref, vp_ref, kl_ref, vl_ref = refs[:N_PROJ_OUT]
    refs = refs[N_PROJ_OUT:]
    if rider:
        has_ref, cs_ref, ns_ref, ms_ref, hbs_ref = refs[:N_RIDER_OUT]
        nrep_scr, rep_scr, s_scr, pv_scr, kv_scr, qc_scr = refs[N_RIDER_OUT:]
    t = pl.program_id(1)
    nt = pl.num_programs(1)

    @pl.when(t == 0)
    def _():
        kp_ref[...] = jnp.zeros_like(kp_ref)
        vp_ref[...] = jnp.zeros_like(vp_ref)

    @pl.when(t > 0)
    def _():
        ua, gates, qn, kn, vb = _proj_body(x_ref[...], sh_ref[...], sc_ref[...], g_ref[...], w_ref,
                                           gb_ref, qg_ref, kg_ref, e_ref, nh=nh, bw=bw, dhb=dhb)
        ua_ref[0] = ua
        gt_ref[0] = gates
        qn_ref[0] = qn.astype(BF16)
        kp_ref[0] = kn.astype(BF16)
        vp_ref[0] = vb.astype(BF16)
        if rider:
            cs_ref[...] = c0_ref[...]
            for hh in range(nh):
                nrep_scr[hh] = jnp.broadcast_to(n0_ref[0, hh:hh + 1, :], (LANES, n0_ref.shape[2])).T
            ms_ref[...] = m0_ref[...]
            for steps in _mlstm_tile(uas_ref, gs_ref, go_ref, has_ref, cs_ref, ns_ref, ms_ref, nrep_scr, rep_scr,
                                     s_scr, pv_scr, kv_scr, qc_scr, seg=uas_ref.shape[1], nh=nh, dh=go_ref.shape[1]):
                _run_passes(steps)
            _band_sample_kernel(qs_ref, kns_ref, vns_ref, ck_ref, cv_ref, bs_ref, hbs_ref, npair=bs_ref.shape[0])

        @pl.when(t == nt - 1)
        def _():
            kl_ref[0] = kn
            vl_ref[0] = vb


def _proj_prompt_call(x, sh, sc, g, wab, gbias, qg, kg, e, *, nh, dhb, w, rider=None):
    nb, length, d = x.shape
    tl = w
    nt = length // tl
    bw = e.shape[0]
    aw4 = wab.shape[1] - 3 * bw - LANES
    prev = lambda b, t: (b, jnp.maximum(t - 1, 0), 0)
    ada_spec = pl.BlockSpec((1, 1, d), lambda b, t: (b, 0, 0))
    in_specs = [pl.BlockSpec((1, tl, d), prev), ada_spec, ada_spec, _const_spec((1, d)),
                _const_spec(wab.shape), _const_spec(gbias.shape),
                _const_spec(qg.shape), _const_spec(kg.shape), _const_spec(e.shape)]
    out_specs = [pl.BlockSpec((1, tl, aw4), prev),
                 pl.BlockSpec((1, tl, LANES), prev),
                 pl.BlockSpec((1, tl, bw), prev),
                 pl.BlockSpec((1, tl, bw), lambda b, t: (b, t, 0)),
                 pl.BlockSpec((1, tl, bw), lambda b, t: (b, t, 0)),
                 pl.BlockSpec((1, tl, bw), lambda b, t: (b, 0, 0)),
                 pl.BlockSpec((1, tl, bw), lambda b, t: (b, 0, 0))]
    out_shape = [jax.ShapeDtypeStruct((nb, length, aw4), F32),
                 jax.ShapeDtypeStruct((nb, length, LANES), F32),
                 jax.ShapeDtypeStruct((nb, length, bw), BF16),
                 jax.ShapeDtypeStruct((nb, length + w, bw), BF16),
                 jax.ShapeDtypeStruct((nb, length + w, bw), BF16),
                 jax.ShapeDtypeStruct((nb, w, bw), F32),
                 jax.ShapeDtypeStruct((nb, w, bw), F32)]
    args = [x, sh, sc, g, wab, gbias, qg, kg, e]
    scratch = []
    if rider is not None:
        ua_s, g_s, c0, n0rep, m0rep, gout, q_s, kn_s, vn_s, ck, cv, bias2_s = rider
        nbs, tdec, _ = ua_s.shape
        _, nha, dh, _ = c0.shape
        assert nbs == nb * nt and nha == nh
        per = lambda a: pl.BlockSpec((1,) + a.shape[1:],
                                     lambda b, t: (b * nt + jnp.maximum(t - 1, 0),) + (0,) * (a.ndim - 1))
        streams = [ua_s, g_s, c0, n0rep, m0rep]
        in_specs += [per(a) for a in streams] + [_const_spec(gout.shape)]
        in_specs += [per(a) for a in (q_s, kn_s, vn_s, ck, cv)] + [_const_spec(bias2_s.shape)]
        args += streams + [gout, q_s, kn_s, vn_s, ck, cv, bias2_s]
        r_shapes = [jax.ShapeDtypeStruct((nbs, tdec, nh * dh), BF16), jax.ShapeDtypeStruct((nbs, nh, dh, dh), F32),
                    jax.ShapeDtypeStruct((nbs, nh, dh), F32), jax.ShapeDtypeStruct((nbs, nh, LANES), F32),
                    jax.ShapeDtypeStruct((nbs, tdec, bw), BF16)]
        out_shape += r_shapes
        out_specs += [per(a) for a in r_shapes]
        scratch = [pltpu.VMEM((nh, dh, LANES), F32), pltpu.VMEM((3 * nh, tdec, LANES), F32),
                   pltpu.VMEM((nh, tdec, tdec), F32), pltpu.VMEM((nh, tdec, dh + LANES), F32),
                   pltpu.VMEM((nh, dh, dh + LANES), F32), pltpu.VMEM((nh, tdec, dh + LANES), F32)]
    outs = pl.pallas_call(
        functools.partial(_proj_prompt_kernel, nh=nh, bw=bw, dhb=dhb, rider=rider is not None),
        grid=(nb, nt + 1),
        in_specs=in_specs,
        out_specs=out_specs,
        out_shape=out_shape,
        scratch_shapes=scratch,
        compiler_params=_cparams(("arbitrary", "arbitrary"), 48),
        name="proj_prompt",
    )(*args)
    return outs[:N_PROJ_OUT], outs[N_PROJ_OUT:]


def _proj_sample_kernel(x_ref, sh_ref, sc_ref, g_ref, w_ref, gb_ref, qg_ref, kg_ref, e_ref,
                        ua_ref, gt_ref, qn_ref, kn_ref, vb_ref, *, nh, bw, dhb):
    bb, tl, _ = x_ref.shape
    ua, gates, qn, kn, vb = _proj_body(x_ref[...], sh_ref[...], sc_ref[...], g_ref[...], w_ref,
                                       gb_ref, qg_ref, kg_ref, e_ref, nh=nh, bw=bw, dhb=dhb)
    ua_ref[...] = ua.reshape(bb, tl, -1)
    gt_ref[...] = gates.reshape(bb, tl, -1)
    qn_ref[...] = qn.reshape(bb, tl, -1).astype(BF16)
    kn_ref[...] = kn.reshape(bb, tl, -1)
    vb_ref[...] = vb.reshape(bb, tl, -1)


def _proj_sample_call(x, sh, sc, g, wab, gbias, qg, kg, e, *, nh, dhb):
    nb, length, d = x.shape
    bw = e.shape[0]
    aw4 = wab.shape[1] - 3 * bw - LANES
    full = lambda n: pl.BlockSpec((nb, length, n), lambda i: (0, 0, 0))
    ada_spec = pl.BlockSpec((nb, 1, d), lambda i: (0, 0, 0))
    return pl.pallas_call(
        functools.partial(_proj_sample_kernel, nh=nh, bw=bw, dhb=dhb),
        grid=(1,),
        in_specs=[full(d), ada_spec, ada_spec, _const_spec((1, d)),
                  _const_spec(wab.shape), _const_spec(gbias.shape),
                  _const_spec(qg.shape), _const_spec(kg.shape), _const_spec(e.shape)],
        out_specs=[full(aw4), full(LANES), full(bw), full(bw), full(bw)],
        out_shape=[jax.ShapeDtypeStruct((nb, length, aw4), F32),
                   jax.ShapeDtypeStruct((nb, length, LANES), F32),
                   jax.ShapeDtypeStruct((nb, length, bw), BF16),
                   jax.ShapeDtypeStruct((nb, length, bw), F32),
                   jax.ShapeDtypeStruct((nb, length, bw), F32)],
        compiler_params=_cparams(("arbitrary",), 48),
        name="proj_sample",
    )(x, sh, sc, g, wab, gbias, qg, kg, e)


def _mlstm_init(c0_ref, n0_ref, m0_ref, c_ref, m_ref, nrep_scr):
    @pl.when(pl.program_id(1) == 0)
    def _():
        c_ref[...] = c0_ref[...]
        nrep_scr[...] = n0_ref[0]
        m_ref[...] = m0_ref[...]


def _mlstm_kernel(ua_ref, g_ref, c0_ref, n0_ref, m0_ref, go_ref, ha_ref, c_ref, n_ref, m_ref,
                  nrep_scr, rep_scr, s_scr, pv_scr, kv_scr, qc_scr, *, seg, nh, dh):
    _mlstm_init(c0_ref, n0_ref, m0_ref, c_ref, m_ref, nrep_scr)
    for steps in _mlstm_tile(ua_ref, g_ref, go_ref, ha_ref, c_ref, n_ref, m_ref,
                             nrep_scr, rep_scr, s_scr, pv_scr, kv_scr, qc_scr, seg=seg, nh=nh, dh=dh):
        _run_passes(steps)


def _mlstm_tile(ua_ref, g_ref, go_ref, ha_ref, c_ref, n_ref, m_ref,
                nrep_scr, rep_scr, s_scr, pv_scr, kv_scr, qc_scr, *, seg, nh, dh):
    tq = ua_ref.shape[1]
    nck = tq // seg
    aw = nh * dh
    gates = g_ref[0]
    pos = lax.broadcasted_iota(jnp.int32, gates.shape, 0) % seg
    bt = gates
    s = 1
    while s < seg:
        bt = bt + jnp.where(pos >= s, pltpu.roll(bt, s, 0), 0.0)
        s *= 2
    dmb = pltpu.roll(gates, nh, 1) - bt
    pm = dmb
    s = 1
    while s < seg:
        pm = jnp.maximum(pm, jnp.where(pos >= s, pltpu.roll(pm, s, 0), -jnp.inf))
        s *= 2
    if tq % LANES:
        dsq = jnp.concatenate([dmb, jnp.zeros((LANES - tq % LANES, LANES), F32)], axis=0)
    else:
        dsq = dmb
    dtr = dsq.T
    ri = lax.broadcasted_iota(jnp.int32, (seg, seg), 0)
    ci = lax.broadcasted_iota(jnp.int32, (seg, seg), 1)
    causal = ri >= ci
    ones = jnp.ones((seg, LANES), BF16)
    ones_dh = jnp.ones((dh, LANES), BF16)
    for h in range(nh):
        ln = slice(nh + h, nh + h + 1)
        for j, arr in enumerate((bt, dmb, pm)):
            rep_scr[3 * h + j] = jnp.broadcast_to(arr[:, ln], (tq, LANES))

    def cols(jc, h):
        rows = slice(jc * seg, (jc + 1) * seg)
        return rows, rep_scr[3 * h, rows, :], rep_scr[3 * h + 1, rows, :], rep_scr[3 * h + 2, rows, :]

    def last(jc, h, j):
        r = (jc + 1) * seg - 1
        return rep_scr[3 * h + j, r:r + 1, :]

    groups = [(jc, h) for jc in range(nck) for h in range(nh)]

    def qkv(jc, h, which):
        rows = slice(jc * seg, (jc + 1) * seg)
        return ua_ref[0, rows, which * aw + h * dh:which * aw + (h + 1) * dh]

    state = []
    before = []

    def score(g, jc, h):
        k = qkv(jc, h, 1) * (dh ** -0.5)
        s_scr[g] = _dot_nt(qkv(jc, h, 0).astype(BF16), k.astype(BF16))

    def local(g, jc, h):
        rows, _, _, p_col = cols(jc, h)
        d_row = dtr[nh + h:nh + h + 1, rows]
        dloc = jnp.exp(jnp.where(causal, d_row - p_col[:, :seg], -jnp.inf))
        sl = (s_scr[g] * dloc).astype(BF16)
        v = qkv(jc, h, 2).astype(BF16)
        pv_scr[g] = _dot(sl, jnp.concatenate([v, ones], axis=1))

    def contrib(g, jc, h):
        _, _, d_col, _ = cols(jc, h)
        kw = (qkv(jc, h, 1) * (dh ** -0.5)) * jnp.exp(d_col - last(jc, h, 2))
        vx = jnp.concatenate([qkv(jc, h, 2).astype(BF16), ones], axis=1)
        kv_scr[g] = _dot_tn(kw.astype(BF16), vx)

    def carry(g, jc, h):
        if not state:
            state.extend((c_ref[0, hh], nrep_scr[hh], m_ref[0, hh:hh + 1, :]) for hh in range(nh))
        c_mem, n_rep, m = state[h]
        cn = jnp.concatenate([c_mem.astype(BF16), n_rep.astype(BF16)], axis=1)
        qc_scr[g] = _dot(qkv(jc, h, 0).astype(BF16), cn)
        before.append(m)
        p_last = last(jc, h, 2)
        mml = jnp.maximum(m, p_last)
        w_prev = jnp.exp(m - mml)
        f_new = jnp.exp(p_last - mml)
        kvx = kv_scr[g]
        state[h] = (w_prev * c_mem + f_new * kvx[:, :dh],
                    w_prev * n_rep + f_new * kvx[:, dh:],
                    last(jc, h, 0) + mml)
        if g == len(groups) - 1:
            for hh in range(nh):
                c_ref[0, hh], nrep_scr[hh], m_ref[0, hh:hh + 1, :] = state[hh]
                n_ref[0, hh:hh + 1, :] = state[hh][1].T[0:1, :]

    def combine(g, jc, h):
        rows, b_col, _, p_col = cols(jc, h)
        m = before[g]
        mm = jnp.maximum(m, p_col)
        iw = jnp.exp(m - mm)
        fl = jnp.exp(p_col - mm)
        pv = pv_scr[g]
        qc = qc_scr[g]
        num = iw * qc[:, :dh] + fl * pv[:, :dh]
        den = iw * qc[:, dh:] + fl * pv[:, dh:]
        hh = num / jnp.maximum(jnp.abs(den), jnp.exp(-(b_col + mm)))
        h2 = hh * hh
        hi = h2.astype(BF16)
        lo = (h2 - hi.astype(F32)).astype(BF16)
        ms = (_dot(hi, ones_dh) + _dot(lo, ones_dh)) * (1.0 / dh)
        hn = (hh * lax.rsqrt(ms + EPS) * go_ref[h:h + 1, :]) * jax.nn.sigmoid(qkv(jc, h, 3))
        ha_ref[0, rows, h * dh:(h + 1) * dh] = hn.astype(BF16)

    return [[functools.partial(fn, g, jc, h) for g, (jc, h) in enumerate(groups)]
            for fn in (score, local, contrib, carry, combine)]


def _run_passes(*pass_lists):
    longest = max(map(len, pass_lists))
    for slot in range(longest):
        for steps in pass_lists:
            for k, step in enumerate(steps):
                if k * longest // len(steps) == slot:
                    step()


def _mlstm_call(ua, gates, c0, n0rep, m0rep, gout, *, tq, seg):
    nb, length, aw4 = ua.shape
    _, nh, dh, _ = c0.shape
    assert dh == LANES
    groups = (tq // seg) * nh
    st = lambda shape: pl.BlockSpec((1,) + shape, lambda b, t: (b,) + (0,) * len(shape))
    tile = lambda n: pl.BlockSpec((1, tq, n), lambda b, t: (b, t, 0))
    return pl.pallas_call(
        functools.partial(_mlstm_kernel, seg=seg, nh=nh, dh=dh),
        grid=(nb, length // tq),
        in_specs=[tile(aw4), tile(LANES), st((nh, dh, dh)), st((nh, dh, LANES)), st((nh, LANES)),
                  _const_spec(gout.shape)],
        out_specs=[tile(nh * dh), st((nh, dh, dh)), st((nh, dh)), st((nh, LANES))],
        out_shape=[jax.ShapeDtypeStruct((nb, length, nh * dh), BF16),
                   jax.ShapeDtypeStruct((nb, nh, dh, dh), F32),
                   jax.ShapeDtypeStruct((nb, nh, dh), F32),
                   jax.ShapeDtypeStruct((nb, nh, LANES), F32)],
        scratch_shapes=[pltpu.VMEM((nh, dh, LANES), F32), pltpu.VMEM((3 * nh, tq, LANES), F32),
                        pltpu.VMEM((groups, seg, seg), F32), pltpu.VMEM((groups, seg, dh + LANES), F32),
                        pltpu.VMEM((groups, dh, dh + LANES), F32), pltpu.VMEM((groups, seg, dh + LANES), F32)],
        compiler_params=_cparams(("arbitrary", "arbitrary"), 32),
        name="mlstm",
    )(ua, gates, c0, n0rep, m0rep, gout)


def _relbias_kernel(b0_ref, o_ref):
    nhb, nq, nk = o_ref.shape
    for h in range(nhb):
        x = jnp.broadcast_to(b0_ref[h:h + 1, :], (nq, b0_ref.shape[1]))
        o_ref[h] = pltpu.roll(x, 0, 1, stride=1, stride_axis=0)[:, :nk]


def _relbias_call(table, w):
    nhb = table.shape[0]
    max_rel = (table.shape[1] - 1) // 2
    assert CHUNK - 1 <= max_rel <= w
    first = jnp.broadcast_to(table[:, :1], (nhb, w - max_rel))
    wrap = jnp.broadcast_to(table[:, :1], (nhb, CHUNK))
    b0 = jnp.concatenate([first, table[:, :max_rel + CHUNK], wrap], axis=1).astype(F32)
    return pl.pallas_call(
        _relbias_kernel,
        out_shape=jax.ShapeDtypeStruct((nhb, CHUNK, w + CHUNK), F32),
        name="rel_bias",
    )(b0)


def _band_tile(q_ref, k_ref, v_ref, bias_ref, o_ref, s_scr, m_scr, e_scr, *, masked, npair, w, nck):
    c4 = pl.program_id(1)
    nk = w + CHUNK
    lane = lax.broadcasted_iota(jnp.int32, (CHUNK, LANES), 1)
    low = lane < LANES // 2
    zero = jnp.zeros((CHUNK, LANES), BF16)
    ones = jnp.ones((nk, LANES), BF16)

    starts = [pl.multiple_of((c4 * nck + jc) * CHUNK, CHUNK) for jc in range(nck)]
    groups = [(jc, p) for jc in range(nck) for p in range(npair)]

    def score(g, jc, p):
        sl = slice(p * LANES, (p + 1) * LANES)
        qp = q_ref[0, jc * CHUNK:(jc + 1) * CHUNK, sl]
        q2 = jnp.concatenate([jnp.where(low, qp, zero), jnp.where(low, zero, qp)], axis=0)
        s = _dot_nt(q2, k_ref[0, pl.ds(starts[jc], nk), sl]) + bias_ref[p]
        if masked:
            col = lax.broadcasted_iota(jnp.int32, s.shape, 1)
            s = jnp.where(col + starts[jc] >= w, s, -jnp.inf)
        s_scr[g] = s
        m_scr[g] = jnp.max(s, axis=-1, keepdims=True)

    def expo(g, jc, p):
        e_scr[g] = jnp.exp(s_scr[g] - m_scr[g]).astype(BF16)

    def value(g, jc, p):
        sl = slice(p * LANES, (p + 1) * LANES)
        vx = jnp.concatenate([v_ref[0, pl.ds(starts[jc], nk), sl], ones], axis=1)
        r = _dot(e_scr[g], vx)
        o_lo = r[:CHUNK, :LANES] / r[:CHUNK, LANES:]
        o_hi = r[CHUNK:, :LANES] / r[CHUNK:, LANES:]
        o_ref[0, jc * CHUNK:(jc + 1) * CHUNK, sl] = jnp.where(low, o_lo, o_hi).astype(BF16)

    return [[functools.partial(fn, g, jc, p) for g, (jc, p) in enumerate(groups)] for fn in (score, expo, value)]


def _mixer_prompt_kernel(ua_ref, g_ref, c0_ref, n0_ref, m0_ref, go_ref, q_ref, k_ref, v_ref, bias_ref,
                         ha_ref, c_ref, n_ref, m_ref, hb_ref,
                         nrep_scr, rep_scr, sa_scr, pv_scr, kv_scr, qc_scr, sb_scr, mb_scr, eb_scr,
                         *, seg, nh, dh, npair, w, nck):
    _mlstm_init(c0_ref, n0_ref, m0_ref, c_ref, m_ref, nrep_scr)
    first_full = w // (CHUNK * nck)

    def tile(masked):
        b_score, b_exp, b_value = _band_tile(q_ref, k_ref, v_ref, bias_ref, hb_ref, sb_scr, mb_scr, eb_scr,
                                             masked=masked, npair=npair, w=w, nck=nck)
        a_score, a_local, a_contrib, a_carry, a_combine = _mlstm_tile(
            ua_ref, g_ref, go_ref, ha_ref, c_ref, n_ref, m_ref,
            nrep_scr, rep_scr, sa_scr, pv_scr, kv_scr, qc_scr, seg=seg, nh=nh, dh=dh)
        _run_passes(b_score, a_score)
        _run_passes(b_exp, a_local)
        _run_passes(a_contrib)
        half = len(b_value) // 2
        _run_passes(b_value[:half], a_carry)
        _run_passes(b_value[half:], a_combine)

    @pl.when(pl.program_id(1) < first_full)
    def _():
        tile(True)

    @pl.when(pl.program_id(1) >= first_full)
    def _():
        tile(False)


def _mixer_prompt_call(ua, gates, c0, n0rep, m0rep, gout, qs, kpad, vpad, bias2, *, w, nck):
    nb, length, aw4 = ua.shape
    _, nh, dh, _ = c0.shape
    bw = qs.shape[2]
    npair = bias2.shape[0]
    lp = kpad.shape[1]
    tq = nck * CHUNK
    nk = w + CHUNK
    ga = nck * nh
    gb = nck * npair
    assert w % tq == 0 and dh == LANES
    st = lambda shape: pl.BlockSpec((1,) + shape, lambda b, t: (b,) + (0,) * len(shape))
    tile = lambda n: pl.BlockSpec((1, tq, n), lambda b, t: (b, t, 0))
    whole = pl.BlockSpec((1, lp, bw), lambda b, t: (b, 0, 0))
    return pl.pallas_call(
        functools.partial(_mixer_prompt_kernel, seg=CHUNK, nh=nh, dh=dh, npair=npair, w=w, nck=nck),
        grid=(nb, length // tq),
        in_specs=[tile(aw4), tile(LANES), st((nh, dh, dh)), st((nh, dh, LANES)), st((nh, LANES)),
                  _const_spec(gout.shape), tile(bw), whole, whole, _const_spec(bias2.shape)],
        out_specs=[tile(nh * dh), st((nh, dh, dh)), st((nh, dh)), st((nh, LANES)), tile(bw)],
        out_shape=[jax.ShapeDtypeStruct((nb, length, nh * dh), BF16),
                   jax.ShapeDtypeStruct((nb, nh, dh, dh), F32),
                   jax.ShapeDtypeStruct((nb, nh, dh), F32),
                   jax.ShapeDtypeStruct((nb, nh, LANES), F32),
                   jax.ShapeDtypeStruct((nb, length, bw), BF16)],
        scratch_shapes=[pltpu.VMEM((nh, dh, LANES), F32), pltpu.VMEM((3 * nh, tq, LANES), F32),
                        pltpu.VMEM((ga, CHUNK, CHUNK), F32), pltpu.VMEM((ga, CHUNK, dh + LANES), F32),
                        pltpu.VMEM((ga, dh, dh + LANES), F32), pltpu.VMEM((ga, CHUNK, dh + LANES), F32),
                        pltpu.VMEM((gb, 2 * CHUNK, nk), F32), pltpu.VMEM((gb, 2 * CHUNK, 1), F32),
                        pltpu.VMEM((gb, 2 * CHUNK, nk), BF16)],
        compiler_params=_cparams(("arbitrary", "arbitrary"), 48),
        name="mixer_prompt",
    )(ua, gates, c0, n0rep, m0rep, gout, qs, kpad, vpad, bias2)


def _band_sample_kernel(q_ref, kn_ref, vn_ref, ck_ref, cv_ref, bias_ref, o_ref, *, npair):
    tq = q_ref.shape[1]
    nk = ck_ref.shape[1] + tq
    lane = lax.broadcasted_iota(jnp.int32, (tq, LANES), 1)
    low = lane < LANES // 2
    zero = jnp.zeros((tq, LANES), BF16)
    ones = jnp.ones((nk, LANES), BF16)
    scores = []
    for p in range(npair):
        sl = slice(p * LANES, (p + 1) * LANES)
        qp = q_ref[0, :, sl]
        q2 = jnp.concatenate([jnp.where(low, qp, zero), jnp.where(low, zero, qp)], axis=0)
        kx = jnp.concatenate([ck_ref[0, :, sl].astype(BF16), kn_ref[0, :, sl].astype(BF16)], axis=0)
        scores.append(_dot_nt(q2, kx) + bias_ref[p])
    probs = [jnp.exp(s - jnp.max(s, axis=-1, keepdims=True)).astype(BF16) for s in scores]
    for p in range(npair):
        sl = slice(p * LANES, (p + 1) * LANES)
        vx = jnp.concatenate([cv_ref[0, :, sl].astype(BF16), vn_ref[0, :, sl].astype(BF16)], axis=0)
        r = _dot(probs[p], jnp.concatenate([vx, ones], axis=1))
        o_lo = r[:tq, :LANES] / r[:tq, LANES:]
        o_hi = r[tq:, :LANES] / r[tq:, LANES:]
        o_ref[0, :, sl] = jnp.where(low, o_lo, o_hi).astype(BF16)


def _band_sample_call(qn, kn, vn, ck, cv, bias2):
    nb, tq, bw = qn.shape
    w = ck.shape[1]
    npair = bias2.shape[0]
    new = pl.BlockSpec((1, tq, bw), lambda b: (b, 0, 0))
    cache = pl.BlockSpec((1, w, bw), lambda b: (b, 0, 0))
    return pl.pallas_call(
        functools.partial(_band_sample_kernel, npair=npair),
        grid=(nb,),
        in_specs=[new, new, new, cache, cache, _const_spec(bias2.shape)],
        out_specs=new,
        out_shape=jax.ShapeDtypeStruct((nb, tq, bw), BF16),
        compiler_params=_cparams(("arbitrary",), 32),
        name="band_sample",
    )(qn, kn, vn, ck, cv, bias2)


def _mixout_kernel(x_ref, ha_ref, hb_ref, gt_ref, woa_ref, wob_ref, o_ref):
    bb, tl, d = x_ref.shape
    ha = ha_ref[...].reshape(bb * tl, -1)
    hb = hb_ref[...].reshape(bb * tl, -1)
    y = _dot(ha, woa_ref[...]) + _dot(hb, wob_ref[...])
    o_ref[...] = x_ref[...] + gt_ref[...] * y.reshape(bb, tl, d)


def _mixout_call(x, ha, hb, gt, woa, wob, bb, tl):
    nb, length, d = x.shape
    tile = lambda n: pl.BlockSpec((bb, tl, n), lambda i, t: (i, t, 0))
    return pl.pallas_call(
        _mixout_kernel,
        grid=(nb // bb, length // tl),
        in_specs=[tile(d), tile(ha.shape[-1]), tile(hb.shape[-1]),
                  pl.BlockSpec((bb, 1, d), lambda i, t: (i, 0, 0)),
                  _const_spec(woa.shape), _const_spec(wob.shape)],
        out_specs=tile(d),
        out_shape=jax.ShapeDtypeStruct(x.shape, F32),
        compiler_params=_cparams(("arbitrary", "arbitrary"), 32),
        name="mix_out",
    )(x, ha, hb, gt, woa, wob)


def _rglru_gates(xc, gw_ref, rb, ib, lam, nblk):
    bwc = xc.shape[1] // nblk
    r_parts, i_parts = [], []
    for n in range(nblk):
        gn = _dot(xc[:, n * bwc:(n + 1) * bwc].astype(BF16), gw_ref[n])
        r_parts.append(gn[:, :bwc])
        i_parts.append(gn[:, bwc:])
    r = jax.nn.sigmoid(jnp.concatenate(r_parts, axis=1) + rb)
    ii = jax.nn.sigmoid(jnp.concatenate(i_parts, axis=1) + ib)
    log_a = r * (-LRU_C * _softplus(-lam))
    a = jnp.exp(log_a)
    th = jnp.tanh(log_a)
    v = -2.0 * th / (1.0 - th)
    root = jnp.where(v > 0.0, v * lax.rsqrt(v), 0.0)
    return a, root * (ii * xc)


def _rglru_prompt_kernel(x_ref, sh_ref, sc_ref, gt_ref, g_ref, win_ref, cw_ref, cb_ref, gw_ref, rb_ref, ib_ref,
                         lam_ref, wout_ref, conv0_ref, h0_ref, o_ref, conv_ref, hl_ref,
                         u_scr, x_scr, xp_scr, a_scr, h_scr, *, nblk, tiles_per_seq, proj_chunks):
    i = pl.program_id(0)
    tq, d = x_ref.shape[1], x_ref.shape[2]
    r_w = lam_ref.shape[1]
    ncw = cw_ref.shape[0]

    @pl.when(i == 0)
    def _():
        u_scr[1] = jnp.zeros(u_scr.shape[1:], F32)
        x_scr[1] = jnp.zeros(x_scr.shape[1:], F32)
        xp_scr[0:SUBLANES, :] = jnp.zeros((SUBLANES, r_w), F32)
        h_scr[...] = jnp.zeros_like(h_scr)

    @pl.when((i >= 1) & ((i - 1) % tiles_per_seq == 0))
    def _():
        xp_scr[0:SUBLANES, :] = conv0_ref[0]
        h_scr[...] = h0_ref[0]

    row8 = lax.broadcasted_iota(jnp.int32, (SUBLANES, r_w), 0)

    def body(slot):
        prev = 1 - slot
        x = x_ref[0]
        x_scr[slot] = x
        hm = _rms_mod(x, g_ref[...], sh_ref[0], sc_ref[0]).astype(BF16)
        wcols = 2 * r_w // proj_chunks

        def project(c):
            cols = slice(c * wcols, (c + 1) * wcols)
            uc = _dot(hm, win_ref[:, cols])
            u_scr[slot, :, cols] = uc
            bits = pltpu.bitcast(uc[0:SUBLANES, 0:LANES], jnp.uint32)
            zero = pltpu.bitcast(lax.shift_right_logical(bits, jnp.uint32(32)), F32)
            return zero[0:1, :]

        def wide(z, n):
            return 0.0 if z is None else jnp.tile(z, (1, n // LANES))

        pending = list(range(proj_chunks))

        def issue(n):
            z = None
            for _ in range(n):
                if pending:
                    zc = project(pending.pop(0))
                    z = zc if z is None else z + zc
            return z

        xp_scr[SUBLANES:SUBLANES + tq, :] = u_scr[prev, :, r_w:]
        xc = cb_ref[...]
        for j in range(ncw):
            off = SUBLANES - (ncw - 1 - j)
            xc = xc + xp_scr[off:off + tq, :] * cw_ref[j:j + 1, :]
        conv_ref[0] = xp_scr[tq:tq + SUBLANES, :]
        xp_scr[0:SUBLANES, :] = xp_scr[tq:tq + SUBLANES, :]
        a, upd = _rglru_gates(xc, gw_ref, rb_ref[...] + wide(issue(3), r_w), ib_ref[...], lam_ref[...], nblk)
        h = h_scr[...] + wide(issue(3), r_w)
        groups = tq // SUBLANES
        for gi in range(groups):
            r0 = gi * SUBLANES
            ai = a[r0:r0 + SUBLANES, :]
            bi = upd[r0:r0 + SUBLANES, :]
            s = 1
            while s < SUBLANES:
                m = row8 >= s
                bi = jnp.where(m, ai * pltpu.roll(bi, s, 0) + bi, bi)
                ai = jnp.where(m, ai * pltpu.roll(ai, s, 0), ai)
                s *= 2
            hs = ai * h + bi
            a_scr[r0:r0 + SUBLANES, :] = hs
            h = hs[SUBLANES - 1:SUBLANES, :]
            if gi == groups // 2 - 1:
                h = h + wide(issue(1), r_w)
        h_scr[...] = h
        hl_ref[0] = h
        gate_vec = gt_ref[0] + wide(issue(proj_chunks), d)
        y = _dot((_gelu_tanh(u_scr[prev, :, :r_w]) * a_scr[...]).astype(BF16), wout_ref[...])
        o_ref[0] = x_scr[prev] + gate_vec * y

    @pl.when(i % 2 == 0)
    def _():
        body(0)

    @pl.when(i % 2 == 1)
    def _():
        body(1)


def _rglru_prompt_call(x, sh, sc, gt, g, win, cw, cb, gw, rb, ib, lam, wout, conv0, h0, *, tq):
    nb, length, d = x.shape
    r_w = lam.shape[1]
    nblk = gw.shape[0]
    nt = length // tq
    ntiles = nb * nt
    cur = lambda i: jnp.minimum(i, ntiles - 1)
    prv = lambda i: jnp.maximum(i - 1, 0)
    consts = [g, win, cw, cb, gw, rb, ib, lam, wout]
    ada_cur = pl.BlockSpec((1, 1, d), lambda i: (cur(i) // nt, 0, 0))
    ada_prv = pl.BlockSpec((1, 1, d), lambda i: (prv(i) // nt, 0, 0))
    conv_spec = pl.BlockSpec((1, SUBLANES, r_w), lambda i: (prv(i) // nt, 0, 0))
    h_spec = pl.BlockSpec((1, 1, r_w), lambda i: (prv(i) // nt, 0, 0))
    return pl.pallas_call(
        functools.partial(_rglru_prompt_kernel, nblk=nblk, tiles_per_seq=nt, proj_chunks=8),
        grid=(ntiles + 1,),
        in_specs=[pl.BlockSpec((1, tq, d), lambda i: (cur(i) // nt, cur(i) % nt, 0)), ada_cur, ada_cur, ada_prv]
                 + [_const_spec(a.shape) for a in consts] + [conv_spec, h_spec],
        out_specs=[pl.BlockSpec((1, tq, d), lambda i: (prv(i) // nt, prv(i) % nt, 0)), conv_spec, h_spec],
        out_shape=[jax.ShapeDtypeStruct(x.shape, F32),
                   jax.ShapeDtypeStruct((nb, SUBLANES, r_w), F32),
                   jax.ShapeDtypeStruct((nb, 1, r_w), F32)],
        scratch_shapes=[pltpu.VMEM((2, tq, 2 * r_w), F32), pltpu.VMEM((2, tq, d), F32),
                        pltpu.VMEM((tq + SUBLANES, r_w), F32), pltpu.VMEM((tq, r_w), F32), pltpu.VMEM((1, r_w), F32)],
        compiler_params=_cparams(("arbitrary",), 56),
        name="rglru_prompt",
    )(x, sh, sc, gt, *consts, conv0, h0)


def _rglru_sample_kernel(x_ref, sh_ref, sc_ref, gt_ref, g_ref, win_ref, cw_ref, cb_ref, gw_ref, rb_ref, ib_ref,
                         lam_ref, wout_ref, conv0_ref, h0_ref, o_ref, conv_ref, hl_ref, xp_scr, *, nblk):
    bb, tl, d = x_ref.shape
    tm = bb * tl
    r_w = lam_ref.shape[1]
    ncw = cw_ref.shape[0]
    x = x_ref[...]
    hm = _rms_mod(x, g_ref[...], sh_ref[...], sc_ref[...]).reshape(tm, d).astype(BF16)
    u = _dot(hm, win_ref[...])
    gb = u[:, :r_w]
    xp_scr[:, 0:SUBLANES, :] = conv0_ref[...]
    xp_scr[:, SUBLANES:SUBLANES + tl, :] = u[:, r_w:].reshape(bb, tl, r_w)
    xc = jnp.broadcast_to(cb_ref[...], (bb, tl, r_w))
    for j in range(ncw):
        off = SUBLANES - (ncw - 1 - j)
        xc = xc + xp_scr[:, off:off + tl, :] * cw_ref[j:j + 1, :]
    conv_ref[...] = xp_scr[:, tl:tl + SUBLANES, :]
    a, b = _rglru_gates(xc.reshape(tm, r_w), gw_ref, rb_ref[...], ib_ref[...], lam_ref[...], nblk)
    pos = lax.broadcasted_iota(jnp.int32, (tm, r_w), 0) % tl
    s = 1
    while s < tl:
        m = pos >= s
        b = jnp.where(m, a * pltpu.roll(b, s, 0) + b, b)
        a = jnp.where(m, a * pltpu.roll(a, s, 0), a)
        s *= 2
    hs = a.reshape(bb, tl, r_w) * h0_ref[...] + b.reshape(bb, tl, r_w)
    hl_ref[...] = hs[:, tl - 1:tl, :]
    y = _dot((_gelu_tanh(gb) * hs.reshape(tm, r_w)).astype(BF16), wout_ref[...])
    o_ref[...] = x + gt_ref[...] * y.reshape(bb, tl, d)


def _rglru_sample_call(x, sh, sc, gt, g, win, cw, cb, gw, rb, ib, lam, wout, conv0, h0):
    nb, tl, d = x.shape
    r_w = lam.shape[1]
    nblk = gw.shape[0]
    full = lambda a, b: pl.BlockSpec((nb, a, b), lambda i: (0, 0, 0))
    consts = [g, win, cw, cb, gw, rb, ib, lam, wout]
    return pl.pallas_call(
        functools.partial(_rglru_sample_kernel, nblk=nblk),
        grid=(1,),
        in_specs=[full(tl, d), full(1, d), full(1, d), full(1, d)] + [_const_spec(a.shape) for a in consts]
                 + [full(SUBLANES, r_w), full(1, r_w)],
        out_specs=[full(tl, d), full(SUBLANES, r_w), full(1, r_w)],
        out_shape=[jax.ShapeDtypeStruct(x.shape, F32),
                   jax.ShapeDtypeStruct((nb, SUBLANES, r_w), F32),
                   jax.ShapeDtypeStruct((nb, 1, r_w), F32)],
        scratch_shapes=[pltpu.VMEM((nb, tl + SUBLANES, r_w), F32)],
        compiler_params=_cparams(("arbitrary",), 48),
        name="rglru_sample",
    )(x, sh, sc, gt, *consts, conv0, h0)


def _pad_rows_front(a, rows):
    return jnp.pad(a, ((0, 0), (rows - a.shape[1], 0), (0, 0)))


def kernel(x_prompt, x_sample, state_a_C, state_a_n, state_a_m, cache_b_k, cache_b_v, state_c_conv, state_c_h,
           c_prompt, c_sample, ffn1_norm, ffn1_w_in, ffn1_w_out, mix_norm, ffn2_norm, ffn2_w_in, ffn2_w_out,
           ada_w, ada_b, ab_w_in, ab_gate_bias, a_out_norm, b_q_norm, b_k_norm, b_rel_bias, ab_w_out,
           c_w_in, c_conv_w, c_conv_b, c_gate_w, c_gate_b, c_lambda, c_w_out):
    nbp, seq, d = x_prompt.shape
    nbs, tdec, _ = x_sample.shape
    depth = ada_w.shape[0]
    n_ada = ada_w.shape[2] // d
    _, _, nh, dh, _ = state_a_C.shape
    _, _, w_band, nhb, dhb = cache_b_k.shape
    aw, bw = nh * dh, nhb * dhb
    ncw = c_conv_w.shape[1]
    assert 2 * dhb == LANES and dh == LANES and w_band % CHUNK == 0 and seq % w_band == 0

    ada = _ada_call(jnp.concatenate([c_prompt, c_sample], axis=0), ada_w, ada_b)
    ada = ada.reshape(depth, nbp + nbs, n_ada, 1, d)
    ada_p = [[ada[l, :nbp, k] for k in range(n_ada)] for l in range(depth)]
    ada_s = [[ada[l, nbp:, k] for k in range(n_ada)] for l in range(depth)]

    tl_p = FFN_ROWS
    xp, xs = x_prompt, x_sample
    outs_p, outs_s = {}, {}
    for l in range(depth):
        ap, as_ = ada_p[l], ada_s[l]
        i = l // 2
        g1 = ffn1_norm[l].reshape(1, d)
        gm = mix_norm[l].reshape(1, d)
        g2 = ffn2_norm[l].reshape(1, d)
        mix_p = None
        xp, xs = _ffn_call(xp, ap[0:3], xs, as_[0:3], g1, ffn1_w_in, ffn1_w_out, l, tl_p)
        if l % 2 == 0:
            w_in = ab_w_in[i]
            wab = jnp.concatenate(
                [w_in[:, :4 * aw], w_in[:, 4 * aw + 2 * nh:], w_in[:, 4 * aw:4 * aw + 2 * nh],
                 jnp.zeros((d, LANES - 2 * nh), F32)], axis=1).astype(BF16)
            gbias = jnp.pad(ab_gate_bias[i], (0, LANES - 2 * nh)).reshape(1, LANES)
            qg = jnp.tile(b_q_norm[i], nhb).reshape(1, bw)
            kg = jnp.tile(b_k_norm[i], nhb).reshape(1, bw)
            head = jnp.arange(bw) // dhb
            e = (head[:, None] == head[None, :]).astype(BF16)
            woa = ab_w_out[i][:aw].astype(BF16)
            wob = ab_w_out[i][aw:].astype(BF16)
            gout = a_out_norm[i]
            bias = _relbias_call(b_rel_bias[i], w_band)
            bias2 = bias.reshape(nhb // 2, 2 * CHUNK, w_band + CHUNK)

            ua_s, gts_s, qn_s, kn, vn = _proj_sample_call(
                xs, as_[3], as_[4], gm, wab, gbias, qg, kg, e, nh=nh, dhb=dhb)
            sample_mix = (ua_s, gts_s, state_a_C[i],
                          jnp.broadcast_to(state_a_n[i][..., None], (nbs, nh, dh, LANES)),
                          jnp.broadcast_to(state_a_m[i][..., None], (nbs, nh, LANES)), gout,
                          qn_s, kn, vn, cache_b_k[i].reshape(nbs, w_band, bw), cache_b_v[i].reshape(nbs, w_band, bw),
                          bias[:, :tdec, :w_band + tdec].reshape(nhb // 2, 2 * tdec, w_band + tdec))
            ride = nbs == nbp * (seq // w_band)

            (ua, gts, qn, kpad, vpad, klast, vlast), rode = _proj_prompt_call(
                xp, ap[3], ap[4], gm, wab, gbias, qg, kg, e, nh=nh, dhb=dhb, w=w_band,
                rider=sample_mix[:3] + (state_a_n[i],) + sample_mix[4:] if ride else None)
            zc = jnp.zeros((nbp, nh, dh, dh), F32)
            ha, c1, n1, m1, hb = _mixer_prompt_call(ua, gts, zc, zc, zc[:, :, 0], gout, qn, kpad, vpad, bias2,
                                                    w=w_band, nck=MIXER_CHUNKS)
            mix_p = (ha, hb, ap[5], woa, wob)
            outs_p.setdefault('a_C', []).append(c1)
            outs_p.setdefault('a_n', []).append(n1)
            outs_p.setdefault('a_m', []).append(m1[:, :, 0])
            outs_p.setdefault('b_k', []).append(klast.reshape(nbp, w_band, nhb, dhb))
            outs_p.setdefault('b_v', []).append(vlast.reshape(nbp, w_band, nhb, dhb))

            if ride:
                ha, c1, n1, m1, hb = rode
            else:
                ha, c1, n1, m1 = _mlstm_call(*sample_mix[:6], tq=tdec, seg=tdec)
                hb = _band_sample_call(*sample_mix[6:])
            xs = _mixout_call(xs, ha, hb, as_[5], woa, wob, nbs, tdec)
            outs_s.setdefault('a_C', []).append(c1)
            outs_s.setdefault('a_n', []).append(n1)
            outs_s.setdefault('a_m', []).append(m1[:, :, 0])
            outs_s.setdefault('b_k', []).append(kn.reshape(nbs, tdec, nhb, dhb))
            outs_s.setdefault('b_v', []).append(vn.reshape(nbs, tdec, nhb, dhb))
        else:
            r_w = c_lambda.shape[1]
            consts = (gm, c_w_in[i].astype(BF16), c_conv_w[i], c_conv_b[i].reshape(1, r_w), c_gate_w[i].astype(BF16),
                      c_gate_b[i][0].reshape(1, r_w), c_gate_b[i][1].reshape(1, r_w), c_lambda[i].reshape(1, r_w),
                      c_w_out[i].astype(BF16))
            xp, conv_p, h_p = _rglru_prompt_call(
                xp, ap[3], ap[4], ap[5], *consts,
                jnp.zeros((nbp, SUBLANES, r_w), F32), jnp.zeros((nbp, 1, r_w), F32), tq=RGLRU_ROWS)
            xs, conv_s, h_s = _rglru_sample_call(
                xs, as_[3], as_[4], as_[5], *consts,
                _pad_rows_front(state_c_conv[i], SUBLANES), state_c_h[i][:, None, :])
            outs_p.setdefault('c_conv', []).append(conv_p[:, SUBLANES - (ncw - 1):])
            outs_p.setdefault('c_h', []).append(h_p[:, 0])
            outs_s.setdefault('c_conv', []).append(conv_s[:, SUBLANES - (ncw - 1):])
            outs_s.setdefault('c_h', []).append(h_s[:, 0])
        xp, xs = _ffn_call(xp, ap[6:9], xs, as_[6:9], g2, ffn2_w_in, ffn2_w_out, l, tl_p, mix=mix_p)

    names = ('a_C', 'a_n', 'a_m', 'b_k', 'b_v', 'c_conv', 'c_h')
    ps = [jnp.stack(outs_p[n], axis=0) for n in names]
    ss = [jnp.stack(outs_s[n], axis=0) for n in names]
    return (xp, xs, *ps, *ss)
```
